```python
import jax, jax.numpy as jnp
from jax import lax
import numpy as np

D_MODEL = 2048
BATCH = 8
SEQ = 4096
DEPTH = 1

CHUNK = 64
D_MIX = D_MODEL
D_ATTN = D_MIX // 2
D_POOL = D_MIX - D_ATTN
HEAD_DIM = 128
N_HEADS = D_ATTN // HEAD_DIM
POOL_WINDOWS = (2, 4, 8, 16)
N_POOL_GROUPS = len(POOL_WINDOWS)
POOL_GROUP_DIM = D_POOL // N_POOL_GROUPS
D_FF = ((8 * D_MODEL // 3 + 127) // 128) * 128
Q_BLOCK = 128
N_MOD = 9
D_IN_PROJ = 3 * D_ATTN + N_HEADS + D_POOL
EPS = 1e-6

kernel_name = "hybrid_fox_pool_macaron_block"


def _rmsnorm(x, g):
    xf = x.astype(jnp.float32)
    xf = xf * lax.rsqrt(jnp.mean(xf * xf, axis=-1, keepdims=True) + EPS)
    return xf.astype(x.dtype) * g


def _modulate(h, shift, scale):
    return h * (1.0 + scale[:, None, :]) + shift[:, None, :]


def _swiglu(h, w_in, w_out):
    a, b = jnp.split(h @ w_in, 2, axis=-1)
    return (jax.nn.silu(a) * b) @ w_out


def _forgetting_attention(q, k, v, log_f):
    S = q.shape[2]
    scale = HEAD_DIM ** -0.5
    F = jnp.cumsum(log_f, axis=-1)
    outs = []
    for i in range(S // Q_BLOCK):
        qs, qe = i * Q_BLOCK, (i + 1) * Q_BLOCK
        qb = q[:, :, qs:qe]
        kb = k[:, :, :qe]
        vb = v[:, :, :qe]
        logits = jnp.einsum('bhqd,bhkd->bhqk', qb, kb).astype(jnp.float32) * scale
        logits = logits + (F[:, :, qs:qe, None] - F[:, :, None, :qe])
        causal = (qs + jnp.arange(Q_BLOCK))[:, None] >= jnp.arange(qe)[None, :]
        logits = jnp.where(causal[None, None], logits, -jnp.inf)
        p = jax.nn.softmax(logits, axis=-1)
        outs.append(jnp.einsum('bhqk,bhkd->bhqd', p.astype(vb.dtype), vb))
    return jnp.concatenate(outs, axis=2)


def _multiscale_pool(u, pool_w, pool_scale):
    B, S, _ = u.shape
    ug = u.reshape(B, S, N_POOL_GROUPS, POOL_GROUP_DIM)
    pos = jnp.arange(S)
    pooled = []
    for g, w in enumerate(POOL_WINDOWS):
        xg = ug[:, :, g].astype(jnp.float32)
        cs0 = jnp.pad(jnp.cumsum(xg, axis=1), ((0, 0), (1, 0), (0, 0)))
        lag = jnp.pad(cs0, ((0, 0), (w - 1, 0), (0, 0)))[:, :S]
        count = jnp.minimum(pos + 1, w).astype(jnp.float32)[None, :, None]
        mean = (cs0[:, 1:] - lag) / count
        pooled.append((mean - xg).astype(u.dtype))
    p = jnp.stack(pooled, axis=2)
    p = jnp.einsum('bsgc,gcd->bsgd', p, pool_w)
    return p.reshape(B, S, D_POOL) * pool_scale


def _hybrid_mixer(h, w_in, b_forget, q_norm_g, k_norm_g, pool_w, pool_scale, w_out):
    B, S, _ = h.shape
    proj = h @ w_in
    q, k, v, f_logit, u = jnp.split(
        proj, [D_ATTN, 2 * D_ATTN, 3 * D_ATTN, 3 * D_ATTN + N_HEADS], axis=-1)

    def heads(t):
        return t.reshape(B, S, N_HEADS, HEAD_DIM).transpose(0, 2, 1, 3)

    q = _rmsnorm(heads(q), q_norm_g)
    k = _rmsnorm(heads(k), k_norm_g)
    v = heads(v)
    log_f = jax.nn.log_sigmoid((f_logit + b_forget).astype(jnp.float32)).transpose(0, 2, 1)
    attn = _forgetting_attention(q, k, v, log_f)
    attn = attn.transpose(0, 2, 1, 3).reshape(B, S, D_ATTN)
    pool = _multiscale_pool(u, pool_w, pool_scale)
    return jnp.concatenate([attn, pool], axis=-1) @ w_out


def _nrm(k, shape, scale):
    return jax.random.normal(k, shape, jnp.float32) * scale


def _fwd_setup_inputs(seed: int = 0) -> dict:
    key = jax.random.key(seed)
    ks = jax.random.split(key, 20)
    L = DEPTH
    return {
        "x": _nrm(ks[0], (BATCH, SEQ, D_MODEL), 1.0),
        "c": _nrm(ks[1], (BATCH, D_MODEL), 1.0),
        "w_ada": _nrm(ks[2], (L, D_MODEL, N_MOD * D_MODEL), 0.5 * D_MODEL ** -0.5),
        "b_ada": _nrm(ks[3], (L, N_MOD * D_MODEL), 0.02),
        "ffn1_norm_g": 1.0 + _nrm(ks[4], (L, D_MODEL), 0.05),
        "ffn1_w_in": _nrm(ks[5], (L, D_MODEL, 2 * D_FF), D_MODEL ** -0.5),
        "ffn1_w_out": _nrm(ks[6], (L, D_FF, D_MODEL), D_FF ** -0.5),
        "mix_norm_g": 1.0 + _nrm(ks[7], (L, D_MODEL), 0.05),
        "w_in": _nrm(ks[8], (L, D_MODEL, D_IN_PROJ), D_MODEL ** -0.5),
        "b_forget": jax.random.uniform(ks[9], (L, N_HEADS), jnp.float32, 1.0, 4.0),
        "q_norm_g": 1.0 + _nrm(ks[10], (L, HEAD_DIM), 0.05),
        "k_norm_g": 1.0 + _nrm(ks[11], (L, HEAD_DIM), 0.05),
        "pool_w": _nrm(ks[12], (L, N_POOL_GROUPS, POOL_GROUP_DIM, POOL_GROUP_DIM), POOL_GROUP_DIM ** -0.5),
        "pool_scale": 1.0 + _nrm(ks[13], (L, D_POOL), 0.1),
        "w_out": _nrm(ks[14], (L, D_MIX, D_MODEL), D_MIX ** -0.5),
        "ffn2_norm_g": 1.0 + _nrm(ks[15], (L, D_MODEL), 0.05),
        "ffn2_w_in": _nrm(ks[16], (L, D_MODEL, 2 * D_FF), D_MODEL ** -0.5),
        "ffn2_w_out": _nrm(ks[17], (L, D_FF, D_MODEL), D_FF ** -0.5),
        "final_norm_g": 1.0 + _nrm(ks[18], (D_MODEL,), 0.05),
    }


def _fwd_reference(x, c, w_ada, b_ada, ffn1_norm_g, ffn1_w_in, ffn1_w_out, mix_norm_g,
              w_in, b_forget, q_norm_g, k_norm_g, pool_w, pool_scale, w_out,
              ffn2_norm_g, ffn2_w_in, ffn2_w_out, final_norm_g):
    c_act = jax.nn.silu(c)
    for l in range(DEPTH):
        mod = c_act @ w_ada[l] + b_ada[l]
        sh1, sc1, g1, sh2, sc2, g2, sh3, sc3, g3 = jnp.split(mod, N_MOD, axis=-1)
        h = _modulate(_rmsnorm(x, ffn1_norm_g[l]), sh1, sc1)
        x = x + 0.5 * g1[:, None, :] * _swiglu(h, ffn1_w_in[l], ffn1_w_out[l])
        h = _modulate(_rmsnorm(x, mix_norm_g[l]), sh2, sc2)
        x = x + g2[:, None, :] * _hybrid_mixer(h, w_in[l], b_forget[l], q_norm_g[l], k_norm_g[l],
                                              pool_w[l], pool_scale[l], w_out[l])
        h = _modulate(_rmsnorm(x, ffn2_norm_g[l]), sh3, sc3)
        x = x + 0.5 * g3[:, None, :] * _swiglu(h, ffn2_w_in[l], ffn2_w_out[l])
    return _rmsnorm(x, final_norm_g)


import jax as _jax
import jax.numpy as _jnp

TWIN_FORMAT = 'train_step'
FWD_PARAMS = ['x', 'c', 'w_ada', 'b_ada', 'ffn1_norm_g', 'ffn1_w_in', 'ffn1_w_out', 'mix_norm_g', 'w_in', 'b_forget', 'q_norm_g', 'k_norm_g', 'pool_w', 'pool_scale', 'w_out', 'ffn2_norm_g', 'ffn2_w_in', 'ffn2_w_out', 'final_norm_g']
TWIN_WEIGHTS = ['w_ada', 'b_ada', 'ffn1_norm_g', 'ffn1_w_in', 'ffn1_w_out', 'mix_norm_g', 'w_in', 'b_forget', 'q_norm_g', 'k_norm_g', 'pool_w', 'pool_scale', 'w_out', 'ffn2_norm_g', 'ffn2_w_in', 'ffn2_w_out', 'final_norm_g']
TWIN_DIFF_INPUT = 'x'
TWIN_INPUTS = ['x', 'c', 'w_ada', 'b_ada', 'ffn1_norm_g', 'ffn1_w_in', 'ffn1_w_out', 'mix_norm_g', 'w_in', 'b_forget', 'q_norm_g', 'k_norm_g', 'pool_w', 'pool_scale', 'w_out', 'ffn2_norm_g', 'ffn2_w_in', 'ffn2_w_out', 'final_norm_g', 'loss_target', 'm_w_ada', 'm_b_ada', 'm_ffn1_norm_g', 'm_ffn1_w_in', 'm_ffn1_w_out', 'm_mix_norm_g', 'm_w_in', 'm_b_forget', 'm_q_norm_g', 'm_k_norm_g', 'm_pool_w', 'm_pool_scale', 'm_w_out', 'm_ffn2_norm_g', 'm_ffn2_w_in', 'm_ffn2_w_out', 'm_final_norm_g', 'v_w_ada', 'v_b_ada', 'v_ffn1_norm_g', 'v_ffn1_w_in', 'v_ffn1_w_out', 'v_mix_norm_g', 'v_w_in', 'v_b_forget', 'v_q_norm_g', 'v_k_norm_g', 'v_pool_w', 'v_pool_scale', 'v_w_out', 'v_ffn2_norm_g', 'v_ffn2_w_in', 'v_ffn2_w_out', 'v_final_norm_g']
TWIN_OUTPUTS = ['loss', 'grad_x', 'grad_w_ada', 'grad_b_ada', 'grad_ffn1_norm_g', 'grad_ffn1_w_in', 'grad_ffn1_w_out', 'grad_mix_norm_g', 'grad_w_in', 'grad_b_forget', 'grad_q_norm_g', 'grad_k_norm_g', 'grad_pool_w', 'grad_pool_scale', 'grad_w_out', 'grad_ffn2_norm_g', 'grad_ffn2_w_in', 'grad_ffn2_w_out', 'grad_final_norm_g', 'delta_w_ada', 'delta_b_ada', 'delta_ffn1_norm_g', 'delta_ffn1_w_in', 'delta_ffn1_w_out', 'delta_mix_norm_g', 'delta_w_in', 'delta_b_forget', 'delta_q_norm_g', 'delta_k_norm_g', 'delta_pool_w', 'delta_pool_scale', 'delta_w_out', 'delta_ffn2_norm_g', 'delta_ffn2_w_in', 'delta_ffn2_w_out', 'delta_final_norm_g', 'new_m_w_ada', 'new_m_b_ada', 'new_m_ffn1_norm_g', 'new_m_ffn1_w_in', 'new_m_ffn1_w_out', 'new_m_mix_norm_g', 'new_m_w_in', 'new_m_b_forget', 'new_m_q_norm_g', 'new_m_k_norm_g', 'new_m_pool_w', 'new_m_pool_scale', 'new_m_w_out', 'new_m_ffn2_norm_g', 'new_m_ffn2_w_in', 'new_m_ffn2_w_out', 'new_m_final_norm_g', 'new_v_w_ada', 'new_v_b_ada', 'new_v_ffn1_norm_g', 'new_v_ffn1_w_in', 'new_v_ffn1_w_out', 'new_v_mix_norm_g', 'new_v_w_in', 'new_v_b_forget', 'new_v_q_norm_g', 'new_v_k_norm_g', 'new_v_pool_w', 'new_v_pool_scale', 'new_v_w_out', 'new_v_ffn2_norm_g', 'new_v_ffn2_w_in', 'new_v_ffn2_w_out', 'new_v_final_norm_g']
TWIN_LEAF_KINDS = {'loss': 'loss', 'grad_x': 'grad_x', 'grad_w_ada': 'grad_w', 'grad_b_ada': 'grad_w', 'grad_ffn1_norm_g': 'grad_w', 'grad_ffn1_w_in': 'grad_w', 'grad_ffn1_w_out': 'grad_w', 'grad_mix_norm_g': 'grad_w', 'grad_w_in': 'grad_w', 'grad_b_forget': 'grad_w', 'grad_q_norm_g': 'grad_w', 'grad_k_norm_g': 'grad_w', 'grad_pool_w': 'grad_w', 'grad_pool_scale': 'grad_w', 'grad_w_out': 'grad_w', 'grad_ffn2_norm_g': 'grad_w', 'grad_ffn2_w_in': 'grad_w', 'grad_ffn2_w_out': 'grad_w', 'grad_final_norm_g': 'grad_w', 'delta_w_ada': 'delta_w', 'delta_b_ada': 'delta_w', 'delta_ffn1_norm_g': 'delta_w', 'delta_ffn1_w_in': 'delta_w', 'delta_ffn1_w_out': 'delta_w', 'delta_mix_norm_g': 'delta_w', 'delta_w_in': 'delta_w', 'delta_b_forget': 'delta_w', 'delta_q_norm_g': 'delta_w', 'delta_k_norm_g': 'delta_w', 'delta_pool_w': 'delta_w', 'delta_pool_scale': 'delta_w', 'delta_w_out': 'delta_w', 'delta_ffn2_norm_g': 'delta_w', 'delta_ffn2_w_in': 'delta_w', 'delta_ffn2_w_out': 'delta_w', 'delta_final_norm_g': 'delta_w', 'new_m_w_ada': 'new_m', 'new_m_b_ada': 'new_m', 'new_m_ffn1_norm_g': 'new_m', 'new_m_ffn1_w_in': 'new_m', 'new_m_ffn1_w_out': 'new_m', 'new_m_mix_norm_g': 'new_m', 'new_m_w_in': 'new_m', 'new_m_b_forget': 'new_m', 'new_m_q_norm_g': 'new_m', 'new_m_k_norm_g': 'new_m', 'new_m_pool_w': 'new_m', 'new_m_pool_scale': 'new_m', 'new_m_w_out': 'new_m', 'new_m_ffn2_norm_g': 'new_m', 'new_m_ffn2_w_in': 'new_m', 'new_m_ffn2_w_out': 'new_m', 'new_m_final_norm_g': 'new_m', 'new_v_w_ada': 'new_v', 'new_v_b_ada': 'new_v', 'new_v_ffn1_norm_g': 'new_v', 'new_v_ffn1_w_in': 'new_v', 'new_v_ffn1_w_out': 'new_v', 'new_v_mix_norm_g': 'new_v', 'new_v_w_in': 'new_v', 'new_v_b_forget': 'new_v', 'new_v_q_norm_g': 'new_v', 'new_v_k_norm_g': 'new_v', 'new_v_pool_w': 'new_v', 'new_v_pool_scale': 'new_v', 'new_v_w_out': 'new_v', 'new_v_ffn2_norm_g': 'new_v', 'new_v_ffn2_w_in': 'new_v', 'new_v_ffn2_w_out': 'new_v', 'new_v_final_norm_g': 'new_v'}


def _forward(args):
    return _fwd_reference(*[args[k] for k in FWD_PARAMS])


def _output_shape():
    def fwd():
        inp = _fwd_setup_inputs(0)
        return _fwd_reference(*[inp[k] for k in FWD_PARAMS])
    out = _jax.eval_shape(fwd)
    return out.shape, out.dtype

N_MICROBATCH = 1
ADAM_LR = 0.001
ADAM_B1 = 0.9
ADAM_B2 = 0.999
ADAM_EPS = 1e-08
ADAM_WD = 0.01
ADAM_STEP = 10
PER_EXAMPLE_BATCH_AXIS = {'x': 0, 'c': 0, 'loss_target': 0}
SHARED_INPUTS = []
_WEIGHT_DTYPES = {'w_ada': _jnp.float32, 'b_ada': _jnp.float32, 'ffn1_norm_g': _jnp.float32, 'ffn1_w_in': _jnp.float32, 'ffn1_w_out': _jnp.float32, 'mix_norm_g': _jnp.float32, 'w_in': _jnp.float32, 'b_forget': _jnp.float32, 'q_norm_g': _jnp.float32, 'k_norm_g': _jnp.float32, 'pool_w': _jnp.float32, 'pool_scale': _jnp.float32, 'w_out': _jnp.float32, 'ffn2_norm_g': _jnp.float32, 'ffn2_w_in': _jnp.float32, 'ffn2_w_out': _jnp.float32, 'final_norm_g': _jnp.float32}
MOMENT_SCALE = {'w_ada': 2.182713e-02, 'b_ada': 4.356448e-02, 'ffn1_norm_g': 1.359641e-02, 'ffn1_w_in': 5.975722e-03, 'ffn1_w_out': 9.659757e-03, 'mix_norm_g': 2.010242e-02, 'w_in': 1.470739e-02, 'b_forget': 8.206414e-02, 'q_norm_g': 2.156651e-02, 'k_norm_g': 2.139439e-02, 'pool_w': 2.450219e-02, 'pool_scale': 2.449802e-02, 'w_out': 1.959740e-02, 'ffn2_norm_g': 1.289489e-02, 'ffn2_w_in': 5.758484e-03, 'ffn2_w_out': 9.295037e-03, 'final_norm_g': 1.603709e+01}


def _to_microbatches(a, axis):
    t = _jnp.moveaxis(a, axis, 0)
    t = t.reshape((N_MICROBATCH, t.shape[0] // N_MICROBATCH) + t.shape[1:])
    return _jnp.moveaxis(t, 1, axis + 1)


def setup_inputs(seed: int = 0) -> dict:
    inp = _fwd_setup_inputs(seed)
    key = _jax.random.fold_in(_jax.random.key(seed), 7919)
    shape, _ = _output_shape()
    out = dict(inp)
    out["loss_target"] = _jax.random.normal(_jax.random.fold_in(key, 0), shape, _jnp.float32)
    for i, name in enumerate(TWIN_WEIGHTS):
        w = inp[name].astype(_jnp.float32)
        if MOMENT_SCALE is None:
            s = _jnp.sqrt(_jnp.mean(_jnp.square(w)) + 1e-30)
        else:
            s = MOMENT_SCALE[name]
        km, kv = _jax.random.split(_jax.random.fold_in(key, i + 1))
        out[name] = w
        out["m_" + name] = s * _jax.random.normal(km, w.shape, _jnp.float32)
        out["v_" + name] = (s * s) * _jax.random.uniform(kv, w.shape, _jnp.float32, 0.5, 1.5)
    if N_MICROBATCH > 1:
        for name, axis in PER_EXAMPLE_BATCH_AXIS.items():
            out[name] = _to_microbatches(out[name], axis)
    return {'x': out['x'], 'c': out['c'], 'w_ada': out['w_ada'], 'b_ada': out['b_ada'], 'ffn1_norm_g': out['ffn1_norm_g'], 'ffn1_w_in': out['ffn1_w_in'], 'ffn1_w_out': out['ffn1_w_out'], 'mix_norm_g': out['mix_norm_g'], 'w_in': out['w_in'], 'b_forget': out['b_forget'], 'q_norm_g': out['q_norm_g'], 'k_norm_g': out['k_norm_g'], 'pool_w': out['pool_w'], 'pool_scale': out['pool_scale'], 'w_out': out['w_out'], 'ffn2_norm_g': out['ffn2_norm_g'], 'ffn2_w_in': out['ffn2_w_in'], 'ffn2_w_out': out['ffn2_w_out'], 'final_norm_g': out['final_norm_g'], 'loss_target': out['loss_target'], 'm_w_ada': out['m_w_ada'], 'm_b_ada': out['m_b_ada'], 'm_ffn1_norm_g': out['m_ffn1_norm_g'], 'm_ffn1_w_in': out['m_ffn1_w_in'], 'm_ffn1_w_out': out['m_ffn1_w_out'], 'm_mix_norm_g': out['m_mix_norm_g'], 'm_w_in': out['m_w_in'], 'm_b_forget': out['m_b_forget'], 'm_q_norm_g': out['m_q_norm_g'], 'm_k_norm_g': out['m_k_norm_g'], 'm_pool_w': out['m_pool_w'], 'm_pool_scale': out['m_pool_scale'], 'm_w_out': out['m_w_out'], 'm_ffn2_norm_g': out['m_ffn2_norm_g'], 'm_ffn2_w_in': out['m_ffn2_w_in'], 'm_ffn2_w_out': out['m_ffn2_w_out'], 'm_final_norm_g': out['m_final_norm_g'], 'v_w_ada': out['v_w_ada'], 'v_b_ada': out['v_b_ada'], 'v_ffn1_norm_g': out['v_ffn1_norm_g'], 'v_ffn1_w_in': out['v_ffn1_w_in'], 'v_ffn1_w_out': out['v_ffn1_w_out'], 'v_mix_norm_g': out['v_mix_norm_g'], 'v_w_in': out['v_w_in'], 'v_b_forget': out['v_b_forget'], 'v_q_norm_g': out['v_q_norm_g'], 'v_k_norm_g': out['v_k_norm_g'], 'v_pool_w': out['v_pool_w'], 'v_pool_scale': out['v_pool_scale'], 'v_w_out': out['v_w_out'], 'v_ffn2_norm_g': out['v_ffn2_norm_g'], 'v_ffn2_w_in': out['v_ffn2_w_in'], 'v_ffn2_w_out': out['v_ffn2_w_out'], 'v_final_norm_g': out['v_final_norm_g']}


def _loss(weights, diff, rest, loss_target):
    with _jax.named_scope("forward"):
        args = {**rest, TWIN_DIFF_INPUT: diff, **{k: w.astype(_WEIGHT_DTYPES[k]) for k, w in weights.items()}}
        y = _forward(args)
    with _jax.named_scope("loss_head"):
        err = _jnp.square(y.astype(_jnp.float32) - loss_target)
        return 0.5 * _jnp.sum(_jnp.mean(err, axis=-1)) if err.ndim else 0.5 * err


def _adamw(w, g, m, v):
    m = ADAM_B1 * m + (1.0 - ADAM_B1) * g
    v = ADAM_B2 * v + (1.0 - ADAM_B2) * _jnp.square(g)
    m_hat = m / (1.0 - ADAM_B1 ** ADAM_STEP)
    v_hat = v / (1.0 - ADAM_B2 ** ADAM_STEP)
    delta = -ADAM_LR * (m_hat / (_jnp.sqrt(v_hat) + ADAM_EPS) + ADAM_WD * w)
    return delta, m, v


def reference(x, c, w_ada, b_ada, ffn1_norm_g, ffn1_w_in, ffn1_w_out, mix_norm_g, w_in, b_forget, q_norm_g, k_norm_g, pool_w, pool_scale, w_out, ffn2_norm_g, ffn2_w_in, ffn2_w_out, final_norm_g, loss_target, m_w_ada, m_b_ada, m_ffn1_norm_g, m_ffn1_w_in, m_ffn1_w_out, m_mix_norm_g, m_w_in, m_b_forget, m_q_norm_g, m_k_norm_g, m_pool_w, m_pool_scale, m_w_out, m_ffn2_norm_g, m_ffn2_w_in, m_ffn2_w_out, m_final_norm_g, v_w_ada, v_b_ada, v_ffn1_norm_g, v_ffn1_w_in, v_ffn1_w_out, v_mix_norm_g, v_w_in, v_b_forget, v_q_norm_g, v_k_norm_g, v_pool_w, v_pool_scale, v_w_out, v_ffn2_norm_g, v_ffn2_w_in, v_ffn2_w_out, v_final_norm_g):
    given = dict(x=x, c=c, w_ada=w_ada, b_ada=b_ada, ffn1_norm_g=ffn1_norm_g, ffn1_w_in=ffn1_w_in, ffn1_w_out=ffn1_w_out, mix_norm_g=mix_norm_g, w_in=w_in, b_forget=b_forget, q_norm_g=q_norm_g, k_norm_g=k_norm_g, pool_w=pool_w, pool_scale=pool_scale, w_out=w_out, ffn2_norm_g=ffn2_norm_g, ffn2_w_in=ffn2_w_in, ffn2_w_out=ffn2_w_out, final_norm_g=final_norm_g, loss_target=loss_target, m_w_ada=m_w_ada, m_b_ada=m_b_ada, m_ffn1_norm_g=m_ffn1_norm_g, m_ffn1_w_in=m_ffn1_w_in, m_ffn1_w_out=m_ffn1_w_out, m_mix_norm_g=m_mix_norm_g, m_w_in=m_w_in, m_b_forget=m_b_forget, m_q_norm_g=m_q_norm_g, m_k_norm_g=m_k_norm_g, m_pool_w=m_pool_w, m_pool_scale=m_pool_scale, m_w_out=m_w_out, m_ffn2_norm_g=m_ffn2_norm_g, m_ffn2_w_in=m_ffn2_w_in, m_ffn2_w_out=m_ffn2_w_out, m_final_norm_g=m_final_norm_g, v_w_ada=v_w_ada, v_b_ada=v_b_ada, v_ffn1_norm_g=v_ffn1_norm_g, v_ffn1_w_in=v_ffn1_w_in, v_ffn1_w_out=v_ffn1_w_out, v_mix_norm_g=v_mix_norm_g, v_w_in=v_w_in, v_b_forget=v_b_forget, v_q_norm_g=v_q_norm_g, v_k_norm_g=v_k_norm_g, v_pool_w=v_pool_w, v_pool_scale=v_pool_scale, v_w_out=v_w_out, v_ffn2_norm_g=v_ffn2_norm_g, v_ffn2_w_in=v_ffn2_w_in, v_ffn2_w_out=v_ffn2_w_out, v_final_norm_g=v_final_norm_g)
    weights = {n: given[n] for n in TWIN_WEIGHTS}
    shared = {n: given[n] for n in SHARED_INPUTS}
    per_example = {n: given[n] for n in ['x', 'c']}
    grad_fn = _jax.value_and_grad(_loss, argnums=(0, 1))

    def one_microbatch(ex, loss_target):
        ex = dict(ex)
        diff = ex.pop(TWIN_DIFF_INPUT)
        return grad_fn(weights, diff, {**shared, **ex}, loss_target)

    if N_MICROBATCH == 1:
        loss, (grad_w, grad_x) = one_microbatch(per_example, given["loss_target"])
    else:
        def body(carry, xs):
            loss_sum, grad_sum = carry
            l_k, (gw_k, gx_k) = one_microbatch(xs[0], xs[1])
            with _jax.named_scope("update"):
                return (loss_sum + l_k, _jax.tree.map(_jnp.add, grad_sum, gw_k)), gx_k

        init = (_jnp.zeros((), _jnp.float32), _jax.tree.map(_jnp.zeros_like, weights))
        (loss, grad_w), grad_x = _jax.lax.scan(body, init, (per_example, given["loss_target"]))
    with _jax.named_scope("update"):
        delta_w, new_m, new_v = {}, {}, {}
        for n in TWIN_WEIGHTS:
            delta_w[n], new_m[n], new_v[n] = _adamw(weights[n], grad_w[n], given["m_" + n], given["v_" + n])
    return (loss, grad_x, *[grad_w[n] for n in TWIN_WEIGHTS], *[delta_w[n] for n in TWIN_WEIGHTS],
            *[new_m[n] for n in TWIN_WEIGHTS], *[new_v[n] for n in TWIN_WEIGHTS])
```

```python
import jax
import jax.numpy as jnp
from jax import lax
from jax.experimental import pallas as pl
from jax.experimental.pallas import tpu as pltpu

F32 = jnp.float32
BF16 = jnp.bfloat16
MESH = pl.DeviceIdType.MESH
ANY = pl.BlockSpec(memory_space=pl.ANY)

N_DEV = 8
EPS = 1e-6
HEAD_DIM = 128
N_HEADS = 8
POOL_WINDOWS = (2, 4, 8, 16)
POOL_GROUP_DIM = 256
N_MOD = 9
ADAM_LR = 0.001
ADAM_B1 = 0.9
ADAM_B2 = 0.999
ADAM_EPS = 1e-08
ADAM_WD = 0.01
ADAM_STEP = 10
NEG = -1e30
VMEM_LIMIT_V7X = 56 * 1024 * 1024


def _params():
    return pltpu.CompilerParams(vmem_limit_bytes=VMEM_LIMIT_V7X)


def _sigmoid(z):
    return 1.0 / (1.0 + jnp.exp(-z))


def _rstd(x):
    return lax.rsqrt(jnp.mean(x * x, axis=-1, keepdims=True) + EPS)


def _mesh_pos():
    return lax.axis_index("x"), lax.axis_index("y"), lax.axis_index("c")


def _flat(px, py, pc):
    return 4 * px + 2 * py + pc


def _all_gather(name, arrs):
    n = len(arrs)

    def body(*refs):
        ins, outs = refs[:n], refs[n:2 * n]
        send_sems, recv_sems, local_sems = refs[2 * n:]
        x, y, c = _mesh_pos()
        me, sibling = (x, y, c), (x, y, 1 - c)
        chips = [(1 - x, y), (x, 1 - y), (1 - x, 1 - y)]

        def copy(a, k, block, to, src=None):
            dst = outs[a].at[_flat(*block)]
            return pltpu.make_async_remote_copy(
                src_ref=dst if src is None else src, dst_ref=dst,
                send_sem=send_sems.at[a, k], recv_sem=recv_sems.at[a, k],
                device_id=to, device_id_type=MESH)

        mine, first, passed = [], [], []
        for a in range(n):
            cp = pltpu.make_async_copy(ins[a], outs[a].at[_flat(*me)], local_sems.at[a])
            cp.start()
            mine.append(cp)
            f = [copy(a, 0, me, sibling, src=ins[a])]
            f += [copy(a, 1 + j, me, (*chip, c), src=ins[a]) for j, chip in enumerate(chips)]
            for cp in f:
                cp.start()
            first += f
        for a in range(n):
            for j, chip in enumerate(chips):
                copy(a, 1 + j, (*chip, c), me).wait_recv()
                fwd = copy(a, 4 + j, (*chip, c), sibling)
                fwd.start()
                passed.append(fwd)
        for a in range(n):
            copy(a, 0, sibling, me).wait_recv()
            for j, chip in enumerate(chips):
                copy(a, 4 + j, (*chip, 1 - c), me).wait_recv()
        for cp in first + passed:
            cp.wait_send()
        for cp in mine:
            cp.wait()

    return pl.pallas_call(
        body, name=name,
        out_shape=[jax.ShapeDtypeStruct((N_DEV,) + a.shape, a.dtype) for a in arrs],
        in_specs=[ANY] * n, out_specs=[ANY] * n,
        scratch_shapes=[pltpu.SemaphoreType.DMA((n, 7)), pltpu.SemaphoreType.DMA((n, 7)),
                        pltpu.SemaphoreType.DMA((n,))],
    )(*arrs)


def _exchange_shards(name, arrs):
    n = len(arrs)

    def body(*refs):
        ins, outs = refs[:n], refs[n:2 * n]
        send_sems, recv_sems, local_sems = refs[2 * n:]
        x, y, c = _mesh_pos()
        me = _flat(x, y, c)
        peers = [(x ^ ((k >> 2) & 1), y ^ ((k >> 1) & 1), c ^ (k & 1)) for k in range(1, 8)]

        local, sends = [], []
        for a in range(n):
            cp = pltpu.make_async_copy(ins[a].at[me], outs[a].at[me], local_sems.at[a])
            cp.start()
            local.append(cp)
            for k, peer in enumerate(peers):
                cp = pltpu.make_async_remote_copy(
                    src_ref=ins[a].at[_flat(*peer)], dst_ref=outs[a].at[me],
                    send_sem=send_sems.at[a, k], recv_sem=recv_sems.at[a, k],
                    device_id=peer, device_id_type=MESH)
                cp.start()
                sends.append(cp)
        for a in range(n):
            for k, peer in enumerate(peers):
                slab = outs[a].at[_flat(*peer)]
                pltpu.make_async_remote_copy(
                    src_ref=slab, dst_ref=slab,
                    send_sem=send_sems.at[a, k], recv_sem=recv_sems.at[a, k],
                    device_id=peer, device_id_type=MESH).wait_recv()
        for cp in sends:
            cp.wait_send()
        for cp in local:
            cp.wait()

    return pl.pallas_call(
        body, name=name,
        out_shape=[jax.ShapeDtypeStruct(a.shape, a.dtype) for a in arrs],
        in_specs=[ANY] * n, out_specs=[ANY] * n,
        scratch_shapes=[pltpu.SemaphoreType.DMA((n, 7)), pltpu.SemaphoreType.DMA((n, 7)),
                        pltpu.SemaphoreType.DMA((n,))],
    )(*arrs)


def _rowwise(name, body, T, tb, rows, vecs, out_rows, out_accs):
    n_in = len(rows) + len(vecs)
    n_o, n_a = len(out_rows), len(out_accs)

    def kern(*refs):
        i = pl.program_id(0)
        res = body(*[r[...] for r in refs[:n_in]])
        if not isinstance(res, (tuple, list)):
            res = (res,)
        outs = refs[n_in:]
        for k in range(n_o):
            outs[k][...] = res[k].astype(outs[k].dtype)

        def accumulate(ref, val):
            @pl.when(i == 0)
            def _():
                ref[...] = val

            @pl.when(i > 0)
            def _():
                ref[...] += val

        for k in range(n_a):
            accumulate(outs[n_o + k], res[n_o + k])

    in_specs = [pl.BlockSpec((tb, w), lambda i, cb=cb: (i, cb)) for (_, w, cb) in rows]
    in_specs += [pl.BlockSpec((1, v.shape[1]), lambda i: (0, 0)) for v in vecs]
    out_specs = [pl.BlockSpec((tb, w), lambda i: (i, 0)) for (w, _) in out_rows]
    out_specs += [pl.BlockSpec((1, w), lambda i: (0, 0)) for w in out_accs]
    out_shape = [jax.ShapeDtypeStruct((T, w), dt) for (w, dt) in out_rows]
    out_shape += [jax.ShapeDtypeStruct((1, w), F32) for w in out_accs]
    res = pl.pallas_call(
        kern, name=name, grid=(T // tb,), in_specs=in_specs, out_specs=out_specs,
        out_shape=out_shape, compiler_params=_params(),
    )(*[r[0] for r in rows], *vecs)
    return res


def _dot(a, b, mode):
    dims = {"NN": ((1,), (0,)), "NT": ((1,), (1,)), "TN": ((0,), (0,))}[mode]
    return lax.dot_general(a.astype(BF16), b.astype(BF16), (dims, ((), ())),
                           preferred_element_type=F32)


def _mm(name, a, b, mode, out_dtype, tm, tn, tk, ga=False, gb=False, gmode=None):
    G = (a.shape[0] if ga else b.shape[0]) if gmode else 1
    a2, b2 = a.shape[-2:], b.shape[-2:]
    if mode == "NN":
        (M, K), (_, N) = a2, b2
    elif mode == "NT":
        (M, K), (N, _) = a2, b2
    else:
        (K, M), (_, N) = a2, b2
    tm, tn, tk = min(tm, M), min(tn, N), min(tk, K)
    assert M % tm == 0 and N % tn == 0 and K % tk == 0, (name, M, N, K, tm, tn, tk)
    batch = gmode == "batch"
    n_gb, n_gs = (G if batch else 1), (G if gmode == "sum" else 1)
    nk = K // tk
    n_red = n_gs * nk

    def grp(g_b, g_s):
        return g_b if batch else g_s

    if mode == "TN":
        a_blk, a_idx = (tk, tm), lambda g_b, mi, ni, g_s, ki: (ki, mi)
    else:
        a_blk, a_idx = (tm, tk), lambda g_b, mi, ni, g_s, ki: (mi, ki)
    if mode == "NT":
        b_blk, b_idx = (tn, tk), lambda g_b, mi, ni, g_s, ki: (ni, ki)
    else:
        b_blk, b_idx = (tk, tn), lambda g_b, mi, ni, g_s, ki: (ki, ni)

    def with_group(blk, idx, has_group):
        if not has_group:
            return pl.BlockSpec(blk, idx)
        return pl.BlockSpec((None,) + blk, lambda g_b, mi, ni, g_s, ki: (grp(g_b, g_s),) + idx(g_b, mi, ni, g_s, ki))

    o_blk, o_idx = (tm, tn), lambda g_b, mi, ni, g_s, ki: (mi, ni)
    o_spec = with_group(o_blk, o_idx, batch)
    o_shape = ((G,) if batch else ()) + (M, N)

    def kern(a_ref, b_ref, o_ref, *scratch):
        part = _dot(a_ref[...], b_ref[...], mode)
        if n_red == 1:
            o_ref[...] = part.astype(o_ref.dtype)
            return
        acc = scratch[0]
        step = pl.program_id(3) * nk + pl.program_id(4)

        @pl.when(step == 0)
        def _():
            acc[...] = part

        @pl.when(step > 0)
        def _():
            acc[...] += part

        @pl.when(step == n_red - 1)
        def _():
            o_ref[...] = acc[...].astype(o_ref.dtype)

    return pl.pallas_call(
        kern, name=name, grid=(n_gb, M // tm, N // tn, n_gs, nk),
        in_specs=[with_group(a_blk, a_idx, ga), with_group(b_blk, b_idx, gb)],
        out_specs=o_spec, out_shape=jax.ShapeDtypeStruct(o_shape, out_dtype),
        scratch_shapes=[] if n_red == 1 else [pltpu.VMEM((tm, tn), F32)],
        compiler_params=_params(),
    )(a, b)


def _adamw(name, parts, w, m, v, tr):
    G, R, C = parts.shape
    assert R % tr == 0
    bc1 = 1.0 - ADAM_B1 ** ADAM_STEP
    bc2 = 1.0 - ADAM_B2 ** ADAM_STEP

    def kern(p_ref, w_ref, m_ref, v_ref, g_out, d_out, m_out, v_out):
        g = p_ref[0].astype(F32)
        for s in range(1, G):
            g = g + p_ref[s].astype(F32)
        m2 = ADAM_B1 * m_ref[...] + (1.0 - ADAM_B1) * g
        v2 = ADAM_B2 * v_ref[...] + (1.0 - ADAM_B2) * (g * g)
        m_hat = m2 / bc1
        v_hat = v2 / bc2
        g_out[...] = g
        d_out[...] = -ADAM_LR * (m_hat / (jnp.sqrt(v_hat) + ADAM_EPS) + ADAM_WD * w_ref[...])
        m_out[...] = m2
        v_out[...] = v2

    blk = pl.BlockSpec((tr, C), lambda i: (i, 0))
    return pl.pallas_call(
        kern, name=name, grid=(R // tr,),
        in_specs=[pl.BlockSpec((G, tr, C), lambda i: (0, i, 0)), blk, blk, blk],
        out_specs=[blk] * 4, out_shape=[jax.ShapeDtypeStruct((R, C), F32)] * 4,
        compiler_params=_params(),
    )(parts, w, m, v)


def _ada_fwd(c_all, w_loc, b_loc, tn):
    B, D = c_all.shape
    N = w_loc.shape[1]

    def kern(c_ref, w_ref, b_ref, o_ref):
        cc = c_ref[...]
        act = cc * _sigmoid(cc)
        o_ref[...] = _dot(act, w_ref[...], "NN") + b_ref[...]

    return pl.pallas_call(
        kern, name="ada_fwd", grid=(N // tn,),
        in_specs=[pl.BlockSpec((B, D), lambda j: (0, 0)), pl.BlockSpec((D, tn), lambda j: (0, j)),
                  pl.BlockSpec((1, tn), lambda j: (0, j))],
        out_specs=pl.BlockSpec((B, tn), lambda j: (0, j)),
        out_shape=jax.ShapeDtypeStruct((B, N), F32), compiler_params=_params(),
    )(c_all, w_loc, b_loc)


def _ada_bwd(c_all, dmod_loc, tn):
    B, D = c_all.shape
    N = dmod_loc.shape[1]

    def kern(c_ref, d_ref, o_ref):
        cc = c_ref[...]
        act = cc * _sigmoid(cc)
        o_ref[...] = _dot(act, d_ref[...], "TN")

    return pl.pallas_call(
        kern, name="ada_bwd", grid=(N // tn,),
        in_specs=[pl.BlockSpec((B, D), lambda j: (0, 0)), pl.BlockSpec((B, tn), lambda j: (0, j))],
        out_specs=pl.BlockSpec((D, tn), lambda j: (0, j)),
        out_shape=jax.ShapeDtypeStruct((D, N), F32), compiler_params=_params(),
    )(c_all, dmod_loc)


def _norm_mod_fwd(name, x, g, sc, sh, T, tb):
    D = x.shape[1]

    def body(xb, gb, scb, shb):
        n = (xb * _rstd(xb)) * gb
        return n * (1.0 + scb) + shb

    return _rowwise(name, body, T, tb, [(x, D, 0)], [g, sc, sh], [(D, BF16)], [])[0]


def _norm_mod_bwd(name, x, dhm, dres, g, sc, T, tb):
    D = x.shape[1]

    def body(xb, db, rb, gb, scb):
        r = _rstd(xb)
        xh = xb * r
        n = xh * gb
        dn = db * (1.0 + scb)
        dxh = dn * gb
        dx = rb + r * (dxh - xh * jnp.mean(dxh * xh, axis=-1, keepdims=True))
        return (dx, jnp.sum(db, axis=0, keepdims=True), jnp.sum(db * n, axis=0, keepdims=True),
                jnp.sum(dn * xh, axis=0, keepdims=True))

    return _rowwise(name, body, T, tb, [(x, D, 0), (dhm, D, 0), (dres, D, 0)], [g, sc],
                    [(D, F32)], [D, D, D])


def _residual(name, x, f, gate, coef, T, tb):
    D = x.shape[1]

    def body(xb, fb, gb):
        return xb + (coef * gb) * fb

    return _rowwise(name, body, T, tb, [(x, D, 0), (f, D, 0)], [gate], [(D, F32)], [])[0]


def _residual_bwd(name, dx, f, gate, coef, T, tb):
    D = dx.shape[1]

    def body(db, fb, gb):
        return (coef * gb) * db, jnp.sum((coef * fb) * db, axis=0, keepdims=True)

    return _rowwise(name, body, T, tb, [(dx, D, 0), (f, D, 0)], [gate], [(D, BF16)], [D])


def _final_loss(x, tgt, g, T, tb):
    D = x.shape[1]

    def body(xb, tb_, gb):
        r = _rstd(xb)
        xh = xb * r
        err = xh * gb - tb_
        loss = 0.5 * jnp.sum(jnp.mean(err * err, axis=-1, keepdims=True), axis=0, keepdims=True)
        dy = err * (1.0 / D)
        dxh = dy * gb
        dx = r * (dxh - xh * jnp.mean(dxh * xh, axis=-1, keepdims=True))
        return dx, jnp.sum(dy * xh, axis=0, keepdims=True), jnp.broadcast_to(loss, (1, 128))

    return _rowwise("final_loss", body, T, tb, [(x, D, 0), (tgt, D, 0)], [g], [(D, F32)], [D, 128])


def _ffn_up(name, hm, wi, T, tm):
    D = hm.shape[1]
    Ws = wi.shape[2]
    half = wi.shape[0] // 2

    def kern(h_ref, wa_ref, wb_ref, a_ref, b_ref, hid_ref):
        h = h_ref[...]
        a = _dot(h, wa_ref[...], "NN")
        b = _dot(h, wb_ref[...], "NN")
        a_ref[...] = a
        b_ref[...] = b
        hid_ref[...] = ((a * _sigmoid(a)) * b).astype(BF16)

    o_spec = pl.BlockSpec((None, tm, Ws), lambda g, i: (g, i, 0))
    return pl.pallas_call(
        kern, name=name, grid=(half, T // tm),
        in_specs=[pl.BlockSpec((tm, D), lambda g, i: (i, 0)),
                  pl.BlockSpec((None, D, Ws), lambda g, i: (g, 0, 0)),
                  pl.BlockSpec((None, D, Ws), lambda g, i: (g + half, 0, 0))],
        out_specs=[o_spec] * 3,
        out_shape=[jax.ShapeDtypeStruct((half, T, Ws), F32)] * 2 + [jax.ShapeDtypeStruct((half, T, Ws), BF16)],
        compiler_params=_params(),
    )(hm, wi, wi)


def _ffn_down_bwd(name, df, wo, a, b, T, tm):
    D = df.shape[1]
    half, _, Ws = a.shape

    def kern(df_ref, wo_ref, a_ref, b_ref, dp_ref):
        dhid = _dot(df_ref[...], wo_ref[...], "NT")
        av = a_ref[...]
        s = _sigmoid(av)
        silu = av * s
        dp_ref[0] = (dhid * b_ref[...] * (s + silu * (1.0 - s))).astype(BF16)
        dp_ref[1] = (dhid * silu).astype(BF16)

    act = pl.BlockSpec((None, tm, Ws), lambda g, i: (g, i, 0))
    return pl.pallas_call(
        kern, name=name, grid=(half, T // tm),
        in_specs=[pl.BlockSpec((tm, D), lambda g, i: (i, 0)),
                  pl.BlockSpec((None, Ws, D), lambda g, i: (g, 0, 0)), act, act],
        out_specs=pl.BlockSpec((2, None, tm, Ws), lambda g, i: (0, g, i, 0)),
        out_shape=jax.ShapeDtypeStruct((2, half, T, Ws), BF16),
        compiler_params=_params(),
    )(df, wo, a, b)


def _ffn_fwd(tag, x, norm_g, sh, sc, gate, wi, wo, T):
    tb = min(256, T)
    hm = _norm_mod_fwd(tag + "_norm_fwd", x, norm_g, sc, sh, T, tb)
    a, b, hid = _ffn_up(tag + "_up", hm, wi, T, min(512, T))
    f = _mm(tag + "_down", hid, wo, "NN", F32, 512, 2048, 2048, ga=True, gb=True, gmode="sum")
    x_out = _residual(tag + "_res", x, f, gate, 0.5, T, tb)
    return x_out, (x, hm, a, b, hid, f)


def _ffn_bwd(tag, dx_out, saved, norm_g, sc, gate, wi, wo, T):
    x, hm, a, b, hid, f = saved
    tb = min(256, T)
    df, dgate = _residual_bwd(tag + "_res_bwd", dx_out, f, gate, 0.5, T, tb)
    dproj = _ffn_down_bwd(tag + "_down_bwd", df, wo, a, b, T, min(512, T))
    dproj = dproj.reshape((2 * dproj.shape[1],) + dproj.shape[2:])
    dwo = _mm(tag + "_dwo", hid, df, "TN", BF16, 2048, 2048, 512, ga=True, gmode="batch")
    dwi = _mm(tag + "_dwi", hm, dproj, "TN", BF16, 2048, 2048, 512, gb=True, gmode="batch")
    dhm = _mm(tag + "_dhm", dproj, wi, "NT", F32, 512, 2048, 2048, ga=True, gb=True, gmode="sum")
    dx, dsh, dsc, dng = _norm_mod_bwd(tag + "_norm_bwd", x, dhm, dx_out, norm_g, sc, T, tb)
    return dx, (dsh, dsc, dgate, dng), dwi, dwo


def _heads(fn, *arrs):
    outs = [fn(*[a[:, h * HEAD_DIM:(h + 1) * HEAD_DIM] for a in arrs]) for h in range(N_HEADS)]
    return outs


def _qknorm_fwd(proj, gq, gk, T, tb):
    W = N_HEADS * HEAD_DIM

    def body(q, k, v, gqb, gkb):
        qn = jnp.concatenate(_heads(lambda t: (t * _rstd(t)) * gqb, q), axis=1)
        kn = jnp.concatenate(_heads(lambda t: (t * _rstd(t)) * gkb, k), axis=1)
        return qn, kn, v

    return _rowwise("qknorm_fwd", body, T, tb, [(proj, W, 0), (proj, W, 1), (proj, W, 2)], [gq, gk],
                    [(W, BF16)] * 3, [])


def _qknorm_bwd(proj, dqn, dkn, gq, gk, T, tb):
    W = N_HEADS * HEAD_DIM

    def one(t, dt, g):
        r = _rstd(t)
        th = t * r
        dth = dt * g
        d = r * (dth - th * jnp.mean(dth * th, axis=-1, keepdims=True))
        return d, jnp.sum(dt * th, axis=0, keepdims=True)

    def body(q, k, dq, dk, gqb, gkb):
        rq = _heads(lambda t, dt: one(t, dt, gqb), q, dq)
        rk = _heads(lambda t, dt: one(t, dt, gkb), k, dk)
        return (jnp.concatenate([r[0] for r in rq], axis=1), jnp.concatenate([r[0] for r in rk], axis=1),
                sum(r[1] for r in rq), sum(r[1] for r in rk))

    return _rowwise("qknorm_bwd", body, T, tb, [(proj, W, 0), (proj, W, 1), (dqn, W, 0), (dkn, W, 0)],
                    [gq, gk], [(W, BF16)] * 2, [HEAD_DIM, HEAD_DIM])


def _log_sigmoid(z):
    return jnp.minimum(z, 0.0) - jnp.log(1.0 + jnp.exp(-jnp.abs(z)))


def _fgate_fwd(proj, fcol, b_pad, T):
    nblk = T // 128

    def kern(f_ref, b_ref, o_ref):
        r = lax.broadcasted_iota(jnp.int32, (128, 128), 0)
        c = lax.broadcasted_iota(jnp.int32, (128, 128), 1)
        tri = (r >= c).astype(F32)
        carry = jnp.zeros((1, 128), F32)
        for k in range(nblk):
            rows = pl.ds(k * 128, 128)
            lf = _log_sigmoid(f_ref[rows, :] + b_ref[...])
            o_ref[rows, :] = jnp.dot(tri, lf, precision=lax.Precision.HIGHEST, preferred_element_type=F32) + carry
            carry = carry + jnp.sum(lf, axis=0, keepdims=True)

    return pl.pallas_call(
        kern, name="fgate_fwd", grid=(1,),
        in_specs=[pl.BlockSpec((T, 128), lambda i: (0, fcol)), pl.BlockSpec((1, 128), lambda i: (0, 0))],
        out_specs=pl.BlockSpec((T, 128), lambda i: (0, 0)),
        out_shape=jax.ShapeDtypeStruct((T, 128), F32), compiler_params=_params(),
    )(proj, b_pad)


def _fgate_bwd(proj, fcol, b_pad, dF, T):
    nblk = T // 128

    def kern(f_ref, b_ref, d_ref, o_ref, db_ref):
        r = lax.broadcasted_iota(jnp.int32, (128, 128), 0)
        c = lax.broadcasted_iota(jnp.int32, (128, 128), 1)
        tri = (c >= r).astype(F32)
        carry = jnp.zeros((1, 128), F32)
        db = jnp.zeros((1, 128), F32)
        for k in reversed(range(nblk)):
            rows = pl.ds(k * 128, 128)
            dblk = d_ref[rows, :]
            rc = jnp.dot(tri, dblk, precision=lax.Precision.HIGHEST, preferred_element_type=F32) + carry
            carry = carry + jnp.sum(dblk, axis=0, keepdims=True)
            z = f_ref[rows, :] + b_ref[...]
            dz = rc * (1.0 / (1.0 + jnp.exp(z)))
            o_ref[rows, :] = dz
            db = db + jnp.sum(dz, axis=0, keepdims=True)
        db_ref[...] = db

    return pl.pallas_call(
        kern, name="fgate_bwd", grid=(1,),
        in_specs=[pl.BlockSpec((T, 128), lambda i: (0, fcol)), pl.BlockSpec((1, 128), lambda i: (0, 0)),
                  pl.BlockSpec((T, 128), lambda i: (0, 0))],
        out_specs=[pl.BlockSpec((T, 128), lambda i: (0, 0)), pl.BlockSpec((1, 128), lambda i: (0, 0))],
        out_shape=[jax.ShapeDtypeStruct((T, 128), F32), jax.ShapeDtypeStruct((1, 128), F32)],
        compiler_params=_params(),
    )(proj, b_pad, dF)


def _gate_bias(ft, fh, h):
    lane = lax.broadcasted_iota(jnp.int32, ft.shape, 1)
    fq = jnp.sum(jnp.where(lane == h, ft, 0.0), axis=1, keepdims=True)
    sub = lax.broadcasted_iota(jnp.int32, fh.shape, 0)
    fk = jnp.sum(jnp.where(sub == h, fh, 0.0), axis=0, keepdims=True)
    return fq - fk


def _causal(i, j, blk):
    row = i * blk + lax.broadcasted_iota(jnp.int32, (blk, blk), 0)
    col = j * blk + lax.broadcasted_iota(jnp.int32, (blk, blk), 1)
    return row >= col


def _attn_fwd(qn, kn, vb, f_tm, f_hm, T, blk):
    nb = T // blk
    scale = HEAD_DIM ** -0.5
    W = N_HEADS * HEAD_DIM

    def kern(q_ref, k_ref, v_ref, ft_ref, fh_ref, o_ref, lse_ref, m_scr, l_scr, acc_scr):
        h, i, j = pl.program_id(0), pl.program_id(1), pl.program_id(2)

        @pl.when(j == 0)
        def _():
            m_scr[...] = jnp.full_like(m_scr, NEG)
            l_scr[...] = jnp.zeros_like(l_scr)
            acc_scr[...] = jnp.zeros_like(acc_scr)

        @pl.when(j <= i)
        def _():
            s = _dot(q_ref[...], k_ref[...], "NT") * scale + _gate_bias(ft_ref[...], fh_ref[...], h)
            s = jnp.where(_causal(i, j, blk), s, NEG)
            m_prev = m_scr[...]
            m_new = jnp.maximum(m_prev, jnp.max(s, axis=1, keepdims=True))
            alpha = jnp.exp(m_prev - m_new)
            p = jnp.exp(s - m_new)
            l_scr[...] = alpha * l_scr[...] + jnp.sum(p, axis=1, keepdims=True)
            acc_scr[...] = alpha * acc_scr[...] + _dot(p, v_ref[...], "NN")
            m_scr[...] = m_new

        @pl.when(j == i)
        def _():
            l = l_scr[...]
            o_ref[...] = acc_scr[...] / l
            lse_ref[...] = jnp.broadcast_to(m_scr[...] + jnp.log(l), (blk, HEAD_DIM))

    qspec = pl.BlockSpec((blk, HEAD_DIM), lambda h, i, j: (i, h))
    kspec = pl.BlockSpec((blk, HEAD_DIM), lambda h, i, j: (jnp.minimum(j, i), h))
    return pl.pallas_call(
        kern, name="attn_fwd", grid=(N_HEADS, nb, nb),
        in_specs=[qspec, kspec, kspec,
                  pl.BlockSpec((blk, 128), lambda h, i, j: (i, 0)),
                  pl.BlockSpec((N_HEADS, blk), lambda h, i, j: (0, jnp.minimum(j, i)))],
        out_specs=[qspec, qspec],
        out_shape=[jax.ShapeDtypeStruct((T, W), F32)] * 2,
        scratch_shapes=[pltpu.VMEM((blk, 1), F32), pltpu.VMEM((blk, 1), F32), pltpu.VMEM((blk, HEAD_DIM), F32)],
        compiler_params=_params(),
    )(qn, kn, vb, f_tm, f_hm)


def _attn_bwd(qn, kn, vb, do, lse, delta, f_tm, f_hm, T, blk):
    nb = T // blk
    scale = HEAD_DIM ** -0.5
    W = N_HEADS * HEAD_DIM

    def kern(q_ref, k_ref, v_ref, do_ref, lse_ref, dl_ref, ft_ref, fh_ref,
             dq_ref, dfq_ref, dk_ref, dv_ref, df_ref, dq_scr, dfq_scr, dk_scr, dv_scr, df_scr):
        h, j, i = pl.program_id(0), pl.program_id(1), pl.program_id(2)

        @pl.when((j == 0) & (i == 0))
        def _():
            dq_scr[...] = jnp.zeros_like(dq_scr)
            dfq_scr[...] = jnp.zeros_like(dfq_scr)

        @pl.when(i == 0)
        def _():
            dk_scr[...] = jnp.zeros_like(dk_scr)
            dv_scr[...] = jnp.zeros_like(dv_scr)
            df_scr[...] = jnp.zeros_like(df_scr)

        @pl.when(i >= j)
        def _():
            q, k, v = q_ref[...], k_ref[...], v_ref[...]
            dob = do_ref[...].astype(BF16)
            s = _dot(q, k, "NT") * scale + _gate_bias(ft_ref[...], fh_ref[...], h)
            p = jnp.where(_causal(i, j, blk), jnp.exp(s - lse_ref[:, 0:1]), 0.0)
            dv_scr[...] += _dot(p, dob, "TN")
            dp = _dot(dob, v, "NT")
            ds = p * (dp - dl_ref[:, 0:1])
            df_scr[...] += jnp.sum(ds, axis=0, keepdims=True)
            dsb = ds.astype(BF16)
            dk_scr[...] += _dot(dsb, q, "TN") * scale
            rows = pl.ds(pl.multiple_of(i * blk, blk), blk)
            dq_scr[rows, :] += _dot(dsb, k, "NN") * scale
            dfq_scr[rows, :] += jnp.sum(ds, axis=1, keepdims=True)

        @pl.when(i == nb - 1)
        def _():
            dk_ref[...] = dk_scr[...]
            dv_ref[...] = dv_scr[...]
            df_ref[...] = -df_scr[...]

        @pl.when((j == nb - 1) & (i == nb - 1))
        def _():
            dq_ref[...] = dq_scr[...]
            dfq_ref[...] = jnp.broadcast_to(dfq_scr[...], (T, HEAD_DIM))

    qspec = pl.BlockSpec((blk, HEAD_DIM), lambda h, j, i: (jnp.maximum(i, j), h))
    full = pl.BlockSpec((T, HEAD_DIM), lambda h, j, i: (0, h))
    kspec = pl.BlockSpec((blk, HEAD_DIM), lambda h, j, i: (j, h))
    return pl.pallas_call(
        kern, name="attn_bwd", grid=(N_HEADS, nb, nb),
        in_specs=[qspec, kspec, kspec, qspec, qspec, qspec,
                  pl.BlockSpec((blk, 128), lambda h, j, i: (jnp.maximum(i, j), 0)),
                  pl.BlockSpec((N_HEADS, blk), lambda h, j, i: (0, j))],
        out_specs=[full, full, kspec, kspec, pl.BlockSpec((None, 1, blk), lambda h, j, i: (h, 0, j))],
        out_shape=[jax.ShapeDtypeStruct((T, W), F32)] * 4 + [jax.ShapeDtypeStruct((N_HEADS, 1, T), F32)],
        scratch_shapes=[pltpu.VMEM((T, HEAD_DIM), F32), pltpu.VMEM((T, 1), F32), pltpu.VMEM((blk, HEAD_DIM), F32),
                        pltpu.VMEM((blk, HEAD_DIM), F32), pltpu.VMEM((1, blk), F32)],
        compiler_params=_params(),
    )(qn, kn, vb, do, lse, delta, f_tm, f_hm)


def _attn_delta(o, do, T, tb):
    W = N_HEADS * HEAD_DIM

    def body(ob, dob):
        return jnp.concatenate(
            _heads(lambda a, b: jnp.broadcast_to(jnp.sum(a * b, axis=1, keepdims=True), a.shape), ob, dob), axis=1)

    return _rowwise("attn_delta", body, T, tb, [(o, W, 0), (do, W, 0)], [], [(W, F32)], [])[0]


def _window_select(s, g, shift):
    picks = []
    for k in (1, 2, 4, 8):
        s = s + shift(s, k)
        picks.append(s)
    return jnp.where(g == 0, picks[0], jnp.where(g == 1, picks[1], jnp.where(g == 2, picks[2], picks[3])))


def _group_window(g):
    return jnp.where(g == 0, POOL_WINDOWS[0], jnp.where(g == 1, POOL_WINDOWS[1],
                     jnp.where(g == 2, POOL_WINDOWS[2], POOL_WINDOWS[3])))


def _pool_fwd(proj, ucol, pw, ps, T, tb):
    C = POOL_GROUP_DIM
    n_g = len(POOL_WINDOWS)

    def kern(uc_ref, up_ref, pw_ref, ps_ref, pooled_ref, out_ref):
        g, i = pl.program_id(0), pl.program_id(1)
        uc = uc_ref[...]
        t2 = (i - 1) * tb + lax.broadcasted_iota(jnp.int32, (2 * tb, C), 0)
        u2 = jnp.where(t2 >= 0, jnp.concatenate([up_ref[...], uc], axis=0), 0.0)
        sums = _window_select(u2, g, lambda s, k: pltpu.roll(s, k, 0))[tb:, :]
        count = jnp.minimum(t2[tb:, :] + 1, _group_window(g)).astype(F32)
        pooled = sums / count - uc
        pooled_ref[...] = pooled.astype(BF16)
        out_ref[...] = _dot(pooled, pw_ref[...], "NN") * ps_ref[...]

    ospec = pl.BlockSpec((tb, C), lambda g, i: (i, g))
    return pl.pallas_call(
        kern, name="pool_fwd", grid=(n_g, T // tb),
        in_specs=[pl.BlockSpec((tb, C), lambda g, i: (i, ucol + g)),
                  pl.BlockSpec((tb, C), lambda g, i: (jnp.maximum(i - 1, 0), ucol + g)),
                  pl.BlockSpec((None, C, C), lambda g, i: (g, 0, 0)),
                  pl.BlockSpec((1, C), lambda g, i: (0, g))],
        out_specs=[ospec, ospec],
        out_shape=[jax.ShapeDtypeStruct((T, n_g * C), BF16), jax.ShapeDtypeStruct((T, n_g * C), F32)],
        compiler_params=_params(),
    )(proj, proj, pw, ps)


def _pool_bwd(dmix_in, dcol, pooled, pw, ps, T, tb):
    C = POOL_GROUP_DIM
    n_g = len(POOL_WINDOWS)
    nb = T // tb

    def kern(dc_ref, dn_ref, pooled_ref, pw_ref, ps_ref, du_ref, dpw_ref, dps_ref):
        g, i = pl.program_id(0), pl.program_id(1)
        dc = dc_ref[...]
        scale = ps_ref[...]
        t2 = i * tb + lax.broadcasted_iota(jnp.int32, (2 * tb, C), 0)
        d2 = jnp.where(t2 < T, jnp.concatenate([dc, dn_ref[...]], axis=0) * scale, 0.0)
        dpooled2 = _dot(d2, pw_ref[...], "NT")
        count = jnp.minimum(t2 + 1, _group_window(g)).astype(F32)
        sums = _window_select(dpooled2 / count, g, lambda s, k: pltpu.roll(s, 2 * tb - k, 0))
        du_ref[...] = (sums[:tb, :] - dpooled2[:tb, :]).astype(BF16)
        pooled = pooled_ref[...]
        p = _dot(pooled, pw_ref[...], "NN")
        dps = jnp.sum(dc * p, axis=0, keepdims=True)
        dpw = _dot(pooled, d2[:tb, :], "TN")

        @pl.when(i == 0)
        def _():
            dps_ref[...] = dps
            dpw_ref[...] = dpw

        @pl.when(i > 0)
        def _():
            dps_ref[...] += dps
            dpw_ref[...] += dpw

    return pl.pallas_call(
        kern, name="pool_bwd", grid=(n_g, nb),
        in_specs=[pl.BlockSpec((tb, C), lambda g, i: (i, dcol + g)),
                  pl.BlockSpec((tb, C), lambda g, i: (jnp.minimum(i + 1, nb - 1), dcol + g)),
                  pl.BlockSpec((tb, C), lambda g, i: (i, g)),
                  pl.BlockSpec((None, C, C), lambda g, i: (g, 0, 0)),
                  pl.BlockSpec((1, C), lambda g, i: (0, g))],
        out_specs=[pl.BlockSpec((tb, C), lambda g, i: (i, g)),
                   pl.BlockSpec((None, C, C), lambda g, i: (g, 0, 0)),
                   pl.BlockSpec((1, C), lambda g, i: (0, g))],
        out_shape=[jax.ShapeDtypeStruct((T, n_g * C), BF16), jax.ShapeDtypeStruct((n_g, C, C), F32),
                   jax.ShapeDtypeStruct((1, n_g * C), F32)],
        compiler_params=_params(),
    )(dmix_in, dmix_in, pooled, pw, ps)


D_QKV = 3 * N_HEADS * HEAD_DIM
D_U = len(POOL_WINDOWS) * POOL_GROUP_DIM
F_PAD = 128
D_PROJ = D_QKV + D_U + F_PAD


def _perm_w_in(w):
    pad = jnp.zeros((w.shape[0], F_PAD - N_HEADS), w.dtype)
    return jnp.concatenate([w[:, :D_QKV], w[:, D_QKV + N_HEADS:], w[:, D_QKV:D_QKV + N_HEADS], pad], axis=1)


def _unperm_w_in(w):
    return jnp.concatenate([w[:, :D_QKV], w[:, D_QKV + D_U:D_QKV + D_U + N_HEADS], w[:, D_QKV:D_QKV + D_U]], axis=1)


def _mixer_fwd(x, norm_g, sh, sc, gate, w_in_p, b_pad, gq, gk, pw, ps, w_out, T):
    tb = min(256, T)
    blk = min(512, T)
    hm = _norm_mod_fwd("mix_norm_fwd", x, norm_g, sc, sh, T, tb)
    proj = _mm("mix_proj", hm, w_in_p, "NN", F32, 512, D_PROJ // 3, 2048)
    qn, kn, vb = _qknorm_fwd(proj, gq, gk, T, tb)
    fcol = (D_QKV + D_U) // 128
    f_tm = _fgate_fwd(proj, fcol, b_pad, T)
    f_hm = f_tm[:, :N_HEADS].T
    o, lse = _attn_fwd(qn, kn, vb, f_tm, f_hm, T, blk)
    pooled, pool_o = _pool_fwd(proj, D_QKV // POOL_GROUP_DIM, pw, ps, T, tb)
    mix_in = jnp.concatenate([o.astype(BF16), pool_o.astype(BF16)], axis=1)
    mix = _mm("mix_out", mix_in, w_out, "NN", F32, 512, 2048, 2048)
    x_out = _residual("mix_res", x, mix, gate, 1.0, T, tb)
    return x_out, (x, hm, proj, qn, kn, vb, f_tm, f_hm, o, lse, pooled, mix_in, mix)


def _mixer_bwd(dx_out, saved, norm_g, sc, gate, w_in_p, b_pad, gq, gk, pw, ps, w_out, T):
    x, hm, proj, qn, kn, vb, f_tm, f_hm, o, lse, pooled, mix_in, mix = saved
    tb = min(256, T)
    blk = min(512, T)
    W = N_HEADS * HEAD_DIM
    dmix, dgate = _residual_bwd("mix_res_bwd", dx_out, mix, gate, 1.0, T, tb)
    dmix_in = _mm("mix_out_bwd", dmix, w_out, "NT", F32, 512, 2048, 2048)
    dw_out = _mm("mix_dw_out", mix_in, dmix, "TN", BF16, 2048, 1024, 512)
    delta = _attn_delta(o, dmix_in, T, tb)
    dqn, dfq, dkn, dv, dfk = _attn_bwd(qn, kn, vb, dmix_in, lse, delta, f_tm, f_hm, T, blk)
    dq, dk, dgq, dgk = _qknorm_bwd(proj, dqn, dkn, gq, gk, T, tb)
    dF = jnp.pad(dfq[:, ::HEAD_DIM] + dfk.reshape(N_HEADS, T).T, ((0, 0), (0, F_PAD - N_HEADS)))
    fcol = (D_QKV + D_U) // 128
    dfl, dbf = _fgate_bwd(proj, fcol, b_pad, dF, T)
    du, dpw, dps = _pool_bwd(dmix_in, W // POOL_GROUP_DIM, pooled, pw, ps, T, tb)
    dproj = jnp.concatenate([dq, dk, dv.astype(BF16), du, dfl.astype(BF16)], axis=1)
    dw_in_p = _mm("mix_dw_in", hm, dproj, "TN", BF16, 2048, D_PROJ // 3, 512)
    dhm = _mm("mix_proj_bwd", dproj, w_in_p, "NT", F32, 512, 2048, D_PROJ // 3)
    dx, dsh, dsc, dng = _norm_mod_bwd("mix_norm_bwd", x, dhm, dx_out, norm_g, sc, T, tb)
    return dx, (dsh, dsc, dgate, dng), dw_in_p, dw_out, dpw, dps, dgq, dgk, dbf


def kernel(x, c, w_ada, b_ada, ffn1_norm_g, ffn1_w_in, ffn1_w_out, mix_norm_g, w_in, b_forget, q_norm_g, k_norm_g, pool_w, pool_scale, w_out, ffn2_norm_g, ffn2_w_in, ffn2_w_out, final_norm_g, loss_target, m_w_ada, m_b_ada, m_ffn1_norm_g, m_ffn1_w_in, m_ffn1_w_out, m_mix_norm_g, m_w_in, m_b_forget, m_q_norm_g, m_k_norm_g, m_pool_w, m_pool_scale, m_w_out, m_ffn2_norm_g, m_ffn2_w_in, m_ffn2_w_out, m_final_norm_g, v_w_ada, v_b_ada, v_ffn1_norm_g, v_ffn1_w_in, v_ffn1_w_out, v_mix_norm_g, v_w_in, v_b_forget, v_q_norm_g, v_k_norm_g, v_pool_w, v_pool_scale, v_w_out, v_ffn2_norm_g, v_ffn2_w_in, v_ffn2_w_out, v_final_norm_g):
    T, D = x.shape[1], x.shape[2]
    mx, my, mc = _mesh_pos()
    me = _flat(mx, my, mc)
    x0 = x[0]
    tgt = loss_target[0]
    tb = min(256, T)

    sharded = dict(ffn1_w_in=ffn1_w_in[0], ffn1_w_out=ffn1_w_out[0], w_in=w_in[0],
                   pool_w=pool_w[0].reshape(-1, POOL_GROUP_DIM), w_out=w_out[0],
                   ffn2_w_in=ffn2_w_in[0], ffn2_w_out=ffn2_w_out[0])
    names = list(sharded)
    gathered = dict(zip(names, _all_gather("gather_weights", [sharded[k].astype(BF16) for k in names])))
    wi1, wi2 = gathered["ffn1_w_in"], gathered["ffn2_w_in"]
    half = N_DEV // 2
    wo1 = gathered["ffn1_w_out"].reshape(half, -1, D)
    wo2 = gathered["ffn2_w_out"].reshape(half, -1, D)
    w_in_full = jnp.transpose(gathered["w_in"], (1, 0, 2)).reshape(D, -1)
    w_in_p = _perm_w_in(w_in_full)
    n_g = len(POOL_WINDOWS)
    pw_rows = POOL_GROUP_DIM // N_DEV
    pw_full = jnp.transpose(gathered["pool_w"].reshape(N_DEV, n_g, pw_rows, POOL_GROUP_DIM),
                            (1, 0, 2, 3)).reshape(n_g, POOL_GROUP_DIM, POOL_GROUP_DIM)
    w_out_full = gathered["w_out"].reshape(-1, D)

    n_loc = w_ada.shape[2]
    c_all = _all_gather("gather_c", [c.reshape(8, D // 8)])[0].reshape(N_DEV, D)
    b_loc = lax.dynamic_slice_in_dim(b_ada, me * n_loc, n_loc, axis=1)
    mod_loc = _ada_fwd(c_all, w_ada[0], b_loc, n_loc // 3)
    mod_all = _all_gather("gather_mod", [mod_loc])[0]
    mod = lax.dynamic_index_in_dim(mod_all, me, axis=1, keepdims=False).reshape(N_MOD, 1, D)
    sh1, sc1, g1, sh2, sc2, g2, sh3, sc3, g3 = [mod[k] for k in range(N_MOD)]

    b_pad = jnp.pad(b_forget, ((0, 0), (0, F_PAD - N_HEADS)))
    ps = pool_scale

    x1, sv1 = _ffn_fwd("ffn1", x0, ffn1_norm_g, sh1, sc1, g1, wi1, wo1, T)
    x2, svm = _mixer_fwd(x1, mix_norm_g, sh2, sc2, g2, w_in_p, b_pad, q_norm_g, k_norm_g, pw_full, ps,
                         w_out_full, T)
    x3, sv2 = _ffn_fwd("ffn2", x2, ffn2_norm_g, sh3, sc3, g3, wi2, wo2, T)
    dx3, dgf, loss_l = _final_loss(x3, tgt, final_norm_g.reshape(1, D), T, tb)
    loss = lax.psum(loss_l[0, 0], ("x", "y", "c"))

    dx2, (dsh3, dsc3, dg3, dn3), dwi2, dwo2 = _ffn_bwd("ffn2", dx3, sv2, ffn2_norm_g, sc3, g3, wi2, wo2, T)
    dx1, (dsh2, dsc2, dg2, dn2), dw_in_p, dw_out, dpw, dps, dgq, dgk, dbf = _mixer_bwd(
        dx2, svm, mix_norm_g, sc2, g2, w_in_p, b_pad, q_norm_g, k_norm_g, pw_full, ps, w_out_full, T)
    dx0, (dsh1, dsc1, dg1, dn1), dwi1, dwo1 = _ffn_bwd("ffn1", dx1, sv1, ffn1_norm_g, sc1, g1, wi1, wo1, T)

    rows_o = ffn1_w_out.shape[1]
    dw_in_full = _unperm_w_in(dw_in_p)
    slabs = dict(
        ffn1_w_in=dwi1, ffn1_w_out=dwo1.reshape(N_DEV, rows_o, D),
        w_in=jnp.transpose(dw_in_full.reshape(D, N_DEV, -1), (1, 0, 2)),
        pool_w=jnp.transpose(dpw.astype(BF16).reshape(n_g, N_DEV, pw_rows, POOL_GROUP_DIM),
                             (1, 0, 2, 3)).reshape(N_DEV, n_g * pw_rows, POOL_GROUP_DIM),
        w_out=dw_out.reshape(N_DEV, -1, D), ffn2_w_in=dwi2, ffn2_w_out=dwo2.reshape(N_DEV, rows_o, D))
    received = dict(zip(names, _exchange_shards("exchange_grads", [slabs[k] for k in names])))

    moments = dict(ffn1_w_in=(m_ffn1_w_in, v_ffn1_w_in), ffn1_w_out=(m_ffn1_w_out, v_ffn1_w_out),
                   w_in=(m_w_in, v_w_in), pool_w=(m_pool_w, v_pool_w), w_out=(m_w_out, v_w_out),
                   ffn2_w_in=(m_ffn2_w_in, v_ffn2_w_in), ffn2_w_out=(m_ffn2_w_out, v_ffn2_w_out))
    weights = dict(ffn1_w_in=ffn1_w_in, ffn1_w_out=ffn1_w_out, w_in=w_in, pool_w=pool_w, w_out=w_out,
                   ffn2_w_in=ffn2_w_in, ffn2_w_out=ffn2_w_out)
    row_tiles = dict(ffn1_w_in=128, ffn1_w_out=16, w_in=128, pool_w=128, w_out=64, ffn2_w_in=128, ffn2_w_out=16)
    results = {}
    for k in names:
        shape = weights[k].shape
        two_d = sharded[k].shape
        mk, vk = moments[k]
        outs = _adamw("adamw_" + k, received[k], weights[k].reshape(two_d), mk.reshape(two_d),
                      vk.reshape(two_d), row_tiles[k])
        results[k] = [o.reshape(shape) for o in outs]

    dmod = jnp.concatenate([dsh1, dsc1, dg1, dsh2, dsc2, dg2, dsh3, dsc3, dg3], axis=1)
    small_names = ["b_ada", "ffn1_norm_g", "mix_norm_g", "ffn2_norm_g", "final_norm_g", "b_forget",
                   "q_norm_g", "k_norm_g", "pool_scale"]
    small_w = dict(b_ada=b_ada, ffn1_norm_g=ffn1_norm_g, mix_norm_g=mix_norm_g, ffn2_norm_g=ffn2_norm_g,
                   final_norm_g=final_norm_g, b_forget=b_forget, q_norm_g=q_norm_g, k_norm_g=k_norm_g,
                   pool_scale=pool_scale)
    small_m = dict(b_ada=m_b_ada, ffn1_norm_g=m_ffn1_norm_g, mix_norm_g=m_mix_norm_g, ffn2_norm_g=m_ffn2_norm_g,
                   final_norm_g=m_final_norm_g, b_forget=m_b_forget, q_norm_g=m_q_norm_g, k_norm_g=m_k_norm_g,
                   pool_scale=m_pool_scale)
    small_v = dict(b_ada=v_b_ada, ffn1_norm_g=v_ffn1_norm_g, mix_norm_g=v_mix_norm_g, ffn2_norm_g=v_ffn2_norm_g,
                   final_norm_g=v_final_norm_g, b_forget=v_b_forget, q_norm_g=v_q_norm_g, k_norm_g=v_k_norm_g,
                   pool_scale=v_pool_scale)
    small_g = dict(b_ada=dmod, ffn1_norm_g=dn1, mix_norm_g=dn2, ffn2_norm_g=dn3, final_norm_g=dgf,
                   b_forget=dbf[:, :N_HEADS], q_norm_g=dgq, k_norm_g=dgk, pool_scale=dps)
    sizes = [small_w[k].size for k in small_names]
    total = sum(sizes)
    lanes = 8 * 128
    padded = -(-total // lanes) * lanes

    def pack(d):
        flat = jnp.concatenate([d[k].reshape(-1) for k in small_names])
        return jnp.pad(flat, (0, padded - total)).reshape(8, padded // 8)

    small_parts = _all_gather("gather_small_grads", [pack(small_g)])[0]
    s_outs = _adamw("adamw_small", small_parts, pack(small_w), pack(small_m), pack(small_v), 8)
    offs = [0]
    for s in sizes:
        offs.append(offs[-1] + s)
    for idx, k in enumerate(small_names):
        results[k] = [o.reshape(-1)[offs[idx]:offs[idx + 1]].reshape(small_w[k].shape) for o in s_outs]

    dmod_all = small_parts.reshape(N_DEV, padded)[:, :N_MOD * D]
    dmod_loc = lax.dynamic_slice_in_dim(dmod_all, me * n_loc, n_loc, axis=1)
    g_ada = _ada_bwd(c_all, dmod_loc, n_loc // 3)
    a_outs = _adamw("adamw_w_ada", g_ada[None], w_ada[0], m_w_ada[0], v_w_ada[0], 128)
    results["w_ada"] = [o.reshape(w_ada.shape) for o in a_outs]

    order = ["w_ada", "b_ada", "ffn1_norm_g", "ffn1_w_in", "ffn1_w_out", "mix_norm_g", "w_in", "b_forget",
             "q_norm_g", "k_norm_g", "pool_w", "pool_scale", "w_out", "ffn2_norm_g", "ffn2_w_in", "ffn2_w_out",
             "final_norm_g"]
    out = [loss, dx0[None]]
    for part in range(4):
        out += [results[k][part] for k in order]
    return tuple(out)
```

```python
import jax
import jax.numpy as jnp
from jax import lax
from jax.experimental import pallas as pl
from jax.experimental.pallas import tpu as pltpu

F32 = jnp.float32
BF16 = jnp.bfloat16
MESH = pl.DeviceIdType.MESH
ANY = pl.BlockSpec(memory_space=pl.ANY)

N_DEV = 8
EPS = 1e-6
HEAD_DIM = 128
N_HEADS = 8
POOL_WINDOWS = (2, 4, 8, 16)
POOL_GROUP_DIM = 256
N_MOD = 9
ADAM_LR = 0.001
ADAM_B1 = 0.9
ADAM_B2 = 0.999
ADAM_EPS = 1e-08
ADAM_WD = 0.01
ADAM_STEP = 10
NEG = -1e30
VMEM_LIMIT_V7X = 56 * 1024 * 1024


def _params():
    return pltpu.CompilerParams(vmem_limit_bytes=VMEM_LIMIT_V7X)


def _sigmoid(z):
    return 1.0 / (1.0 + jnp.exp(-z))


def _rstd(x):
    return lax.rsqrt(jnp.mean(x * x, axis=-1, keepdims=True) + EPS)


def _mesh_pos():
    return lax.axis_index("x"), lax.axis_index("y"), lax.axis_index("c")


def _flat(px, py, pc):
    return 4 * px + 2 * py + pc


class _Comm:
    def __init__(self, ins, outs, sems, phases):
        self.ins, self.outs, self.sems, self.phases = list(ins), list(outs), list(sems), list(phases)


def _pallas(kern, *, comm=None, **kw):
    if comm is None:
        return pl.pallas_call(kern, **kw)
    grid = tuple(kw["grid"])
    single = not isinstance(kw["out_shape"], (list, tuple))
    out_shape = [kw["out_shape"]] if single else list(kw["out_shape"])
    out_specs = [kw["out_specs"]] if single else list(kw["out_specs"])
    in_specs = list(kw["in_specs"])
    scratch = list(kw.get("scratch_shapes", ()))
    n_in, n_out, n_scr = len(in_specs), len(out_shape), len(scratch)
    n_ci, n_co = len(comm.ins), len(comm.outs)
    strides, n_steps = [], 1
    for g in reversed(grid):
        strides.insert(0, n_steps)
        n_steps *= g

    def wrapped(*refs):
        ins, cins = refs[:n_in], refs[n_in:n_in + n_ci]
        base = n_in + n_ci
        outs, couts = refs[base:base + n_out], refs[base + n_out:base + n_out + n_co]
        base += n_out + n_co
        scr, sems = refs[base:base + n_scr], refs[base + n_scr:]
        step = sum(pl.program_id(d) * strides[d] for d in range(len(grid)))
        for frac, fn in comm.phases:
            if frac < 1.0:
                pl.when(step == int(round(frac * (n_steps - 1))))(lambda fn=fn: fn(cins, couts, sems))
        kern(*ins, *outs, *scr)
        for frac, fn in comm.phases:
            if frac >= 1.0:
                pl.when(step == n_steps - 1)(lambda fn=fn: fn(cins, couts, sems))

    kw = dict(kw, in_specs=in_specs + [ANY] * n_ci, out_specs=out_specs + [ANY] * n_co,
              out_shape=out_shape + comm.outs, scratch_shapes=scratch + comm.sems)
    call = pl.pallas_call(wrapped, **kw)

    def run(*args):
        res = call(*args, *comm.ins)
        main = res[0] if single else list(res[:n_out])
        return main, list(res[n_out:])

    return run


def _hosted(comm, res):
    return res if comm is not None else (res, [])


def _standalone(name, comm):
    def kern():
        pass

    return _pallas(kern, comm=comm, name=name, grid=(1,), in_specs=[], out_specs=[], out_shape=[])()[1]


def _dma_sems(*shapes):
    return [pltpu.SemaphoreType.DMA(s) for s in shapes]


def _gather_comm(arrs):
    n = len(arrs)

    def setup(outs, sems):
        send_sems, recv_sems, _ = sems
        x, y, c = _mesh_pos()
        chips = [(1 - x, y), (x, 1 - y), (1 - x, 1 - y)]

        def copy(a, k, block, to, src=None):
            dst = outs[a].at[_flat(*block)]
            return pltpu.make_async_remote_copy(
                src_ref=dst if src is None else src, dst_ref=dst,
                send_sem=send_sems.at[a, k], recv_sem=recv_sems.at[a, k],
                device_id=to, device_id_type=MESH)

        return (x, y, c), (x, y, 1 - c), chips, copy

    def local(ins, outs, sems, a, me):
        return pltpu.make_async_copy(ins[a], outs[a].at[_flat(*me)], sems[2].at[a])

    def send_own(ins, outs, sems):
        me, sibling, chips, copy = setup(outs, sems)
        for a in range(n):
            local(ins, outs, sems, a, me).start()
            copy(a, 0, me, sibling, src=ins[a]).start()
            for j, chip in enumerate(chips):
                copy(a, 1 + j, me, (*chip, me[2]), src=ins[a]).start()

    def forward(ins, outs, sems):
        me, sibling, chips, copy = setup(outs, sems)
        for a in range(n):
            for j, chip in enumerate(chips):
                copy(a, 1 + j, (*chip, me[2]), me).wait_recv()
                copy(a, 4 + j, (*chip, me[2]), sibling).start()

    def finish(ins, outs, sems):
        me, sibling, chips, copy = setup(outs, sems)
        for a in range(n):
            copy(a, 0, sibling, me).wait_recv()
            for j, chip in enumerate(chips):
                copy(a, 4 + j, (*chip, 1 - me[2]), me).wait_recv()
        for a in range(n):
            copy(a, 0, me, sibling, src=ins[a]).wait_send()
            for j, chip in enumerate(chips):
                copy(a, 1 + j, me, (*chip, me[2]), src=ins[a]).wait_send()
                copy(a, 4 + j, (*chip, me[2]), sibling).wait_send()
            local(ins, outs, sems, a, me).wait()

    return _Comm(arrs, [jax.ShapeDtypeStruct((N_DEV,) + a.shape, a.dtype) for a in arrs],
                 _dma_sems((n, 7), (n, 7), (n,)), [(0.0, send_own), (0.5, forward), (1.0, finish)])


CHIPS = [(0, 0), (0, 1), (1, 0), (1, 1)]


def _sibling_comm(parts):
    n = len(parts)

    def copies(ins, outs, sems):
        x, y, c = _mesh_pos()
        return [pltpu.make_async_remote_copy(
                    src_ref=ins[a].at[_flat(qx, qy, 1 - c)], dst_ref=outs[a].at[q],
                    send_sem=sems[0].at[a, q], recv_sem=sems[1].at[a, q],
                    device_id=(x, y, 1 - c), device_id_type=MESH)
                for a in range(n) for q, (qx, qy) in enumerate(CHIPS)]

    def start(ins, outs, sems):
        for cp in copies(ins, outs, sems):
            cp.start()

    def finish(ins, outs, sems):
        for cp in copies(ins, outs, sems):
            cp.wait_recv()
        for cp in copies(ins, outs, sems):
            cp.wait_send()

    return _Comm(parts, [jax.ShapeDtypeStruct((4,) + p.shape[1:], p.dtype) for p in parts],
                 _dma_sems((n, 4), (n, 4)), [(0.0, start), (1.0, finish)])


def _chip_comm(sums):
    n = len(sums)
    flips = [(1, 0), (0, 1), (1, 1)]

    def own(ins, outs, sems):
        mine = 2 * lax.axis_index("x") + lax.axis_index("y")
        return [pltpu.make_async_copy(ins[a].at[mine], outs[a].at[mine], sems[2].at[a]) for a in range(n)]

    def copies(ins, outs, sems, arriving=False):
        x, y, c = _mesh_pos()
        mine = 2 * x + y
        remote = []
        for a in range(n):
            for k, (fx, fy) in enumerate(flips):
                qx, qy = x ^ fx, y ^ fy
                q = 2 * qx + qy
                remote.append(pltpu.make_async_remote_copy(
                    src_ref=ins[a].at[q], dst_ref=outs[a].at[q if arriving else mine],
                    send_sem=sems[0].at[a, k], recv_sem=sems[1].at[a, k],
                    device_id=(qx, qy, c), device_id_type=MESH))
        return remote

    def start(ins, outs, sems):
        for cp in own(ins, outs, sems) + copies(ins, outs, sems):
            cp.start()

    def finish(ins, outs, sems):
        for cp in copies(ins, outs, sems, arriving=True):
            cp.wait_recv()
        for cp in copies(ins, outs, sems):
            cp.wait_send()
        for cp in own(ins, outs, sems):
            cp.wait()

    return _Comm(sums, [jax.ShapeDtypeStruct(s.shape, s.dtype) for s in sums],
                 _dma_sems((n, 3), (n, 3), (n,)), [(0.0, start), (1.0, finish)])


def _pair_add(name, parts, got, core, tr):
    _, R, C = parts.shape
    assert R % tr == 0

    def kern(c_ref, p_ref, g_ref, o_ref):
        o_ref[...] = (p_ref[...].astype(F32) + g_ref[...].astype(F32)).astype(o_ref.dtype)

    blk = pl.BlockSpec((None, tr, C), lambda q, i, c_ref: (q, i, 0))
    return pl.pallas_call(
        kern, name=name,
        grid_spec=pltpu.PrefetchScalarGridSpec(
            num_scalar_prefetch=1, grid=(4, R // tr),
            in_specs=[pl.BlockSpec((None, tr, C), lambda q, i, c_ref: (2 * q + c_ref[0], i, 0)), blk],
            out_specs=blk),
        out_shape=jax.ShapeDtypeStruct((4, R, C), parts.dtype), compiler_params=_params(),
    )(core, parts, got)


def _rowwise(name, body, T, tb, rows, vecs, out_rows, out_accs):
    n_in = len(rows) + len(vecs)
    n_o, n_a = len(out_rows), len(out_accs)

    def kern(*refs):
        i = pl.program_id(0)
        res = body(*[r[...] for r in refs[:n_in]])
        if not isinstance(res, (tuple, list)):
            res = (res,)
        outs = refs[n_in:]
        for k in range(n_o):
            outs[k][...] = res[k].astype(outs[k].dtype)

        def accumulate(ref, val):
            @pl.when(i == 0)
            def _():
                ref[...] = val

            @pl.when(i > 0)
            def _():
                ref[...] += val

        for k in range(n_a):
            accumulate(outs[n_o + k], res[n_o + k])

    in_specs = [pl.BlockSpec((tb, w), lambda i, cb=cb: (i, cb)) for (_, w, cb) in rows]
    in_specs += [pl.BlockSpec((1, v.shape[1]), lambda i: (0, 0)) for v in vecs]
    out_specs = [pl.BlockSpec((tb, w), lambda i: (i, 0)) for (w, _) in out_rows]
    out_specs += [pl.BlockSpec((1, w), lambda i: (0, 0)) for w in out_accs]
    out_shape = [jax.ShapeDtypeStruct((T, w), dt) for (w, dt) in out_rows]
    out_shape += [jax.ShapeDtypeStruct((1, w), F32) for w in out_accs]
    res = pl.pallas_call(
        kern, name=name, grid=(T // tb,), in_specs=in_specs, out_specs=out_specs,
        out_shape=out_shape, compiler_params=_params(),
    )(*[r[0] for r in rows], *vecs)
    return res


def _dot(a, b, mode):
    dims = {"NN": ((1,), (0,)), "NT": ((1,), (1,)), "TN": ((0,), (0,))}[mode]
    return lax.dot_general(a.astype(BF16), b.astype(BF16), (dims, ((), ())),
                           preferred_element_type=F32)


def _mm(name, a, b, mode, out_dtype, tm, tn, tk, ga=False, gb=False, gmode=None, comm=None):
    G = (a.shape[0] if ga else b.shape[0]) if gmode else 1
    a2, b2 = a.shape[-2:], b.shape[-2:]
    if mode == "NN":
        (M, K), (_, N) = a2, b2
    elif mode == "NT":
        (M, K), (N, _) = a2, b2
    else:
        (K, M), (_, N) = a2, b2
    tm, tn, tk = min(tm, M), min(tn, N), min(tk, K)
    assert M % tm == 0 and N % tn == 0 and K % tk == 0, (name, M, N, K, tm, tn, tk)
    batch = gmode == "batch"
    n_gb, n_gs = (G if batch else 1), (G if gmode == "sum" else 1)
    nk = K // tk
    n_red = n_gs * nk

    def grp(g_b, g_s):
        return g_b if batch else g_s

    if mode == "TN":
        a_blk, a_idx = (tk, tm), lambda g_b, mi, ni, g_s, ki: (ki, mi)
    else:
        a_blk, a_idx = (tm, tk), lambda g_b, mi, ni, g_s, ki: (mi, ki)
    if mode == "NT":
        b_blk, b_idx = (tn, tk), lambda g_b, mi, ni, g_s, ki: (ni, ki)
    else:
        b_blk, b_idx = (tk, tn), lambda g_b, mi, ni, g_s, ki: (ki, ni)

    def with_group(blk, idx, has_group):
        if not has_group:
            return pl.BlockSpec(blk, idx)
        return pl.BlockSpec((None,) + blk, lambda g_b, mi, ni, g_s, ki: (grp(g_b, g_s),) + idx(g_b, mi, ni, g_s, ki))

    o_blk, o_idx = (tm, tn), lambda g_b, mi, ni, g_s, ki: (mi, ni)
    o_spec = with_group(o_blk, o_idx, batch)
    o_shape = ((G,) if batch else ()) + (M, N)

    def kern(a_ref, b_ref, o_ref, *scratch):
        part = _dot(a_ref[...], b_ref[...], mode)
        if n_red == 1:
            o_ref[...] = part.astype(o_ref.dtype)
            return
        acc = scratch[0]
        step = pl.program_id(3) * nk + pl.program_id(4)

        @pl.when(step == 0)
        def _():
            acc[...] = part

        @pl.when(step > 0)
        def _():
            acc[...] += part

        @pl.when(step == n_red - 1)
        def _():
            o_ref[...] = acc[...].astype(o_ref.dtype)

    return _pallas(
        kern, comm=comm, name=name, grid=(n_gb, M // tm, N // tn, n_gs, nk),
        in_specs=[with_group(a_blk, a_idx, ga), with_group(b_blk, b_idx, gb)],
        out_specs=o_spec, out_shape=jax.ShapeDtypeStruct(o_shape, out_dtype),
        scratch_shapes=[] if n_red == 1 else [pltpu.VMEM((tm, tn), F32)],
        compiler_params=_params(),
    )(a, b)


def _adamw(name, parts, w, m, v, tr):
    G, R, C = parts.shape
    assert R % tr == 0
    bc1 = 1.0 - ADAM_B1 ** ADAM_STEP
    bc2 = 1.0 - ADAM_B2 ** ADAM_STEP

    def kern(p_ref, w_ref, m_ref, v_ref, g_out, d_out, m_out, v_out):
        g = p_ref[0].astype(F32)
        for s in range(1, G):
            g = g + p_ref[s].astype(F32)
        m2 = ADAM_B1 * m_ref[...] + (1.0 - ADAM_B1) * g
        v2 = ADAM_B2 * v_ref[...] + (1.0 - ADAM_B2) * (g * g)
        m_hat = m2 / bc1
        v_hat = v2 / bc2
        g_out[...] = g
        d_out[...] = -ADAM_LR * (m_hat / (jnp.sqrt(v_hat) + ADAM_EPS) + ADAM_WD * w_ref[...])
        m_out[...] = m2
        v_out[...] = v2

    blk = pl.BlockSpec((tr, C), lambda i: (i, 0))
    return pl.pallas_call(
        kern, name=name, grid=(R // tr,),
        in_specs=[pl.BlockSpec((G, tr, C), lambda i: (0, i, 0)), blk, blk, blk],
        out_specs=[blk] * 4, out_shape=[jax.ShapeDtypeStruct((R, C), F32)] * 4,
        compiler_params=_params(),
    )(parts, w, m, v)


def _ada_fwd(c_all, w_loc, b_loc, tn):
    B, D = c_all.shape
    N = w_loc.shape[1]

    def kern(c_ref, w_ref, b_ref, o_ref):
        cc = c_ref[...]
        act = cc * _sigmoid(cc)
        o_ref[...] = _dot(act, w_ref[...], "NN") + b_ref[...]

    return pl.pallas_call(
        kern, name="ada_fwd", grid=(N // tn,),
        in_specs=[pl.BlockSpec((B, D), lambda j: (0, 0)), pl.BlockSpec((D, tn), lambda j: (0, j)),
                  pl.BlockSpec((1, tn), lambda j: (0, j))],
        out_specs=pl.BlockSpec((B, tn), lambda j: (0, j)),
        out_shape=jax.ShapeDtypeStruct((B, N), F32), compiler_params=_params(),
    )(c_all, w_loc, b_loc)


def _ada_bwd(c_all, dmod_loc, tn):
    B, D = c_all.shape
    N = dmod_loc.shape[1]

    def kern(c_ref, d_ref, o_ref):
        cc = c_ref[...]
        act = cc * _sigmoid(cc)
        o_ref[...] = _dot(act, d_ref[...], "TN")

    return pl.pallas_call(
        kern, name="ada_bwd", grid=(N // tn,),
        in_specs=[pl.BlockSpec((B, D), lambda j: (0, 0)), pl.BlockSpec((B, tn), lambda j: (0, j))],
        out_specs=pl.BlockSpec((D, tn), lambda j: (0, j)),
        out_shape=jax.ShapeDtypeStruct((D, N), F32), compiler_params=_params(),
    )(c_all, dmod_loc)


def _norm_mod_fwd(name, x, g, sc, sh, T, tb):
    D = x.shape[1]

    def body(xb, gb, scb, shb):
        n = (xb * _rstd(xb)) * gb
        return n * (1.0 + scb) + shb

    return _rowwise(name, body, T, tb, [(x, D, 0)], [g, sc, sh], [(D, BF16)], [])[0]


def _norm_mod_bwd(name, x, dhm, dres, g, sc, T, tb):
    D = x.shape[1]

    def body(xb, db, rb, gb, scb):
        r = _rstd(xb)
        xh = xb * r
        n = xh * gb
        dn = db * (1.0 + scb)
        dxh = dn * gb
        dx = rb + r * (dxh - xh * jnp.mean(dxh * xh, axis=-1, keepdims=True))
        return (dx, jnp.sum(db, axis=0, keepdims=True), jnp.sum(db * n, axis=0, keepdims=True),
                jnp.sum(dn * xh, axis=0, keepdims=True))

    return _rowwise(name, body, T, tb, [(x, D, 0), (dhm, D, 0), (dres, D, 0)], [g, sc],
                    [(D, F32)], [D, D, D])


def _residual(name, x, f, gate, coef, T, tb):
    D = x.shape[1]

    def body(xb, fb, gb):
        return xb + (coef * gb) * fb

    return _rowwise(name, body, T, tb, [(x, D, 0), (f, D, 0)], [gate], [(D, F32)], [])[0]


def _residual_bwd(name, dx, f, gate, coef, T, tb):
    D = dx.shape[1]

    def body(db, fb, gb):
        return (coef * gb) * db, jnp.sum((coef * fb) * db, axis=0, keepdims=True)

    return _rowwise(name, body, T, tb, [(dx, D, 0), (f, D, 0)], [gate], [(D, BF16)], [D])


def _final_loss(x, tgt, g, T, tb):
    D = x.shape[1]

    def body(xb, tb_, gb):
        r = _rstd(xb)
        xh = xb * r
        err = xh * gb - tb_
        loss = 0.5 * jnp.sum(jnp.mean(err * err, axis=-1, keepdims=True), axis=0, keepdims=True)
        dy = err * (1.0 / D)
        dxh = dy * gb
        dx = r * (dxh - xh * jnp.mean(dxh * xh, axis=-1, keepdims=True))
        return dx, jnp.sum(dy * xh, axis=0, keepdims=True), jnp.broadcast_to(loss, (1, 128))

    return _rowwise("final_loss", body, T, tb, [(x, D, 0), (tgt, D, 0)], [g], [(D, F32)], [D, 128])


def _ffn_up(name, hm, wi, T, tm, comm=None):
    D = hm.shape[1]
    Ws = wi.shape[2]
    half = wi.shape[0] // 2

    def kern(h_ref, wa_ref, wb_ref, a_ref, b_ref, hid_ref):
        h = h_ref[...]
        a = _dot(h, wa_ref[...], "NN")
        b = _dot(h, wb_ref[...], "NN")
        a_ref[...] = a
        b_ref[...] = b
        hid_ref[...] = ((a * _sigmoid(a)) * b).astype(BF16)

    o_spec = pl.BlockSpec((None, tm, Ws), lambda g, i: (g, i, 0))
    return _pallas(
        kern, comm=comm, name=name, grid=(half, T // tm),
        in_specs=[pl.BlockSpec((tm, D), lambda g, i: (i, 0)),
                  pl.BlockSpec((None, D, Ws), lambda g, i: (g, 0, 0)),
                  pl.BlockSpec((None, D, Ws), lambda g, i: (g + half, 0, 0))],
        out_specs=[o_spec] * 3,
        out_shape=[jax.ShapeDtypeStruct((half, T, Ws), F32)] * 2 + [jax.ShapeDtypeStruct((half, T, Ws), BF16)],
        compiler_params=_params(),
    )(hm, wi, wi)


def _ffn_down_bwd(name, df, wo, a, b, T, tm, comm=None):
    D = df.shape[1]
    half, _, Ws = a.shape

    def kern(df_ref, wo_ref, a_ref, b_ref, dp_ref):
        dhid = _dot(df_ref[...], wo_ref[...], "NT")
        av = a_ref[...]
        s = _sigmoid(av)
        silu = av * s
        dp_ref[0] = (dhid * b_ref[...] * (s + silu * (1.0 - s))).astype(BF16)
        dp_ref[1] = (dhid * silu).astype(BF16)

    act = pl.BlockSpec((None, tm, Ws), lambda g, i: (g, i, 0))
    return _pallas(
        kern, comm=comm, name=name, grid=(half, T // tm),
        in_specs=[pl.BlockSpec((tm, D), lambda g, i: (i, 0)),
                  pl.BlockSpec((None, Ws, D), lambda g, i: (g, 0, 0)), act, act],
        out_specs=pl.BlockSpec((2, None, tm, Ws), lambda g, i: (0, g, i, 0)),
        out_shape=jax.ShapeDtypeStruct((2, half, T, Ws), BF16),
        compiler_params=_params(),
    )(df, wo, a, b)


def _ffn_fwd(tag, x, norm_g, sh, sc, gate, wi, wo_of, T, up_comm=None, down_comm=None):
    tb = min(256, T)
    hm = _norm_mod_fwd(tag + "_norm_fwd", x, norm_g, sc, sh, T, tb)
    (a, b, hid), got_up = _hosted(up_comm, _ffn_up(tag + "_up", hm, wi, T, min(512, T), comm=up_comm))
    wo = wo_of(got_up)
    f, got_down = _hosted(down_comm, _mm(tag + "_down", hid, wo, "NN", F32, 512, 2048, 2048, ga=True, gb=True,
                                         gmode="sum", comm=down_comm))
    x_out = _residual(tag + "_res", x, f, gate, 0.5, T, tb)
    return x_out, (x, hm, a, b, hid, f), wo, got_down


TILE_W_IN = 128
TILE_W_OUT = 16
TILE_MIX_OUT = 64
TILE_POOL = 128


def _reduce_level1(tag, parts, core, tile, host=None):
    comm = _sibling_comm(parts)
    if host is None:
        res, got = None, _standalone(tag + "_sibling", comm)
    else:
        res, got = host(comm)
    sums = [_pair_add("%s_pair_add%d" % (tag, k), p, g, core, t) for k, (p, g, t) in enumerate(zip(parts, got, tile))]
    return res, sums


def _ffn_bwd(tag, dx_out, saved, norm_g, sc, gate, wi, wo, T, core):
    x, hm, a, b, hid, f = saved
    tb = min(256, T)
    D = x.shape[1]
    df, dgate = _residual_bwd(tag + "_res_bwd", dx_out, f, gate, 0.5, T, tb)
    dwo = _mm(tag + "_dwo", hid, df, "TN", BF16, 2048, 2048, 512, ga=True, gmode="batch")
    dwo = dwo.reshape(N_DEV, -1, D)
    dproj, (dwo_sum,) = _reduce_level1(
        tag + "_dwo", [dwo], core, [TILE_W_OUT],
        host=lambda comm: _ffn_down_bwd(tag + "_down_bwd", df, wo, a, b, T, min(512, T), comm=comm))
    dproj = dproj.reshape((2 * dproj.shape[1],) + dproj.shape[2:])
    dwi, (dwo_got,) = _mm(tag + "_dwi", hm, dproj, "TN", BF16, 2048, 2048, 512, gb=True, gmode="batch",
                          comm=_chip_comm([dwo_sum]))
    _, (dwi_sum,) = _reduce_level1(tag + "_dwi", [dwi], core, [TILE_W_IN])
    dhm, (dwi_got,) = _mm(tag + "_dhm", dproj, wi, "NT", F32, 512, 2048, 2048, ga=True, gb=True, gmode="sum",
                          comm=_chip_comm([dwi_sum]))
    dx, dsh, dsc, dng = _norm_mod_bwd(tag + "_norm_bwd", x, dhm, dx_out, norm_g, sc, T, tb)
    return dx, (dsh, dsc, dgate, dng), dwi_got, dwo_got


def _heads(fn, *arrs):
    outs = [fn(*[a[:, h * HEAD_DIM:(h + 1) * HEAD_DIM] for a in arrs]) for h in range(N_HEADS)]
    return outs


def _qknorm_fwd(proj, gq, gk, T, tb):
    W = N_HEADS * HEAD_DIM

    def body(q, k, v, gqb, gkb):
        qn = jnp.concatenate(_heads(lambda t: (t * _rstd(t)) * gqb, q), axis=1)
        kn = jnp.concatenate(_heads(lambda t: (t * _rstd(t)) * gkb, k), axis=1)
        return qn, kn, v

    return _rowwise("qknorm_fwd", body, T, tb, [(proj, W, 0), (proj, W, 1), (proj, W, 2)], [gq, gk],
                    [(W, BF16)] * 3, [])


def _qknorm_bwd(proj, dqn, dkn, gq, gk, T, tb):
    W = N_HEADS * HEAD_DIM

    def one(t, dt, g):
        r = _rstd(t)
        th = t * r
        dth = dt * g
        d = r * (dth - th * jnp.mean(dth * th, axis=-1, keepdims=True))
        return d, jnp.sum(dt * th, axis=0, keepdims=True)

    def body(q, k, dq, dk, gqb, gkb):
        rq = _heads(lambda t, dt: one(t, dt, gqb), q, dq)
        rk = _heads(lambda t, dt: one(t, dt, gkb), k, dk)
        return (jnp.concatenate([r[0] for r in rq], axis=1), jnp.concatenate([r[0] for r in rk], axis=1),
                sum(r[1] for r in rq), sum(r[1] for r in rk))

    return _rowwise("qknorm_bwd", body, T, tb, [(proj, W, 0), (proj, W, 1), (dqn, W, 0), (dkn, W, 0)],
                    [gq, gk], [(W, BF16)] * 2, [HEAD_DIM, HEAD_DIM])


def _log_sigmoid(z):
    return jnp.minimum(z, 0.0) - jnp.log(1.0 + jnp.exp(-jnp.abs(z)))


def _fgate_fwd(proj, fcol, b_pad, T):
    nblk = T // 128

    def kern(f_ref, b_ref, o_ref):
        r = lax.broadcasted_iota(jnp.int32, (128, 128), 0)
        c = lax.broadcasted_iota(jnp.int32, (128, 128), 1)
        tri = (r >= c).astype(F32)
        carry = jnp.zeros((1, 128), F32)
        for k in range(nblk):
            rows = pl.ds(k * 128, 128)
            lf = _log_sigmoid(f_ref[rows, :] + b_ref[...])
            o_ref[rows, :] = jnp.dot(tri, lf, precision=lax.Precision.HIGHEST, preferred_element_type=F32) + carry
            carry = carry + jnp.sum(lf, axis=0, keepdims=True)

    return pl.pallas_call(
        kern, name="fgate_fwd", grid=(1,),
        in_specs=[pl.BlockSpec((T, 128), lambda i: (0, fcol)), pl.BlockSpec((1, 128), lambda i: (0, 0))],
        out_specs=pl.BlockSpec((T, 128), lambda i: (0, 0)),
        out_shape=jax.ShapeDtypeStruct((T, 128), F32), compiler_params=_params(),
    )(proj, b_pad)


def _fgate_bwd(proj, fcol, b_pad, dF, T):
    nblk = T // 128

    def kern(f_ref, b_ref, d_ref, o_ref, db_ref):
        r = lax.broadcasted_iota(jnp.int32, (128, 128), 0)
        c = lax.broadcasted_iota(jnp.int32, (128, 128), 1)
        tri = (c >= r).astype(F32)
        carry = jnp.zeros((1, 128), F32)
        db = jnp.zeros((1, 128), F32)
        for k in reversed(range(nblk)):
            rows = pl.ds(k * 128, 128)
            dblk = d_ref[rows, :]
            rc = jnp.dot(tri, dblk, precision=lax.Precision.HIGHEST, preferred_element_type=F32) + carry
            carry = carry + jnp.sum(dblk, axis=0, keepdims=True)
            z = f_ref[rows, :] + b_ref[...]
            dz = rc * (1.0 / (1.0 + jnp.exp(z)))
            o_ref[rows, :] = dz
            db = db + jnp.sum(dz, axis=0, keepdims=True)
        db_ref[...] = db

    return pl.pallas_call(
        kern, name="fgate_bwd", grid=(1,),
        in_specs=[pl.BlockSpec((T, 128), lambda i: (0, fcol)), pl.BlockSpec((1, 128), lambda i: (0, 0)),
                  pl.BlockSpec((T, 128), lambda i: (0, 0))],
        out_specs=[pl.BlockSpec((T, 128), lambda i: (0, 0)), pl.BlockSpec((1, 128), lambda i: (0, 0))],
        out_shape=[jax.ShapeDtypeStruct((T, 128), F32), jax.ShapeDtypeStruct((1, 128), F32)],
        compiler_params=_params(),
    )(proj, b_pad, dF)


def _gate_bias(ft, fh, h):
    lane = lax.broadcasted_iota(jnp.int32, ft.shape, 1)
    fq = jnp.sum(jnp.where(lane == h, ft, 0.0), axis=1, keepdims=True)
    sub = lax.broadcasted_iota(jnp.int32, fh.shape, 0)
    fk = jnp.sum(jnp.where(sub == h, fh, 0.0), axis=0, keepdims=True)
    return fq - fk


def _causal(i, j, blk):
    row = i * blk + lax.broadcasted_iota(jnp.int32, (blk, blk), 0)
    col = j * blk + lax.broadcasted_iota(jnp.int32, (blk, blk), 1)
    return row >= col


def _attn_fwd(qn, kn, vb, f_tm, f_hm, T, blk, comm=None):
    nb = T // blk
    scale = HEAD_DIM ** -0.5
    W = N_HEADS * HEAD_DIM

    def kern(q_ref, k_ref, v_ref, ft_ref, fh_ref, o_ref, lse_ref, m_scr, l_scr, acc_scr):
        h, i, j = pl.program_id(0), pl.program_id(1), pl.program_id(2)

        @pl.when(j == 0)
        def _():
            m_scr[...] = jnp.full_like(m_scr, NEG)
            l_scr[...] = jnp.zeros_like(l_scr)
            acc_scr[...] = jnp.zeros_like(acc_scr)

        @pl.when(j <= i)
        def _():
            s = _dot(q_ref[...], k_ref[...], "NT") * scale + _gate_bias(ft_ref[...], fh_ref[...], h)
            s = jnp.where(_causal(i, j, blk), s, NEG)
            m_prev = m_scr[...]
            m_new = jnp.maximum(m_prev, jnp.max(s, axis=1, keepdims=True))
            alpha = jnp.exp(m_prev - m_new)
            p = jnp.exp(s - m_new)
            l_scr[...] = alpha * l_scr[...] + jnp.sum(p, axis=1, keepdims=True)
            acc_scr[...] = alpha * acc_scr[...] + _dot(p, v_ref[...], "NN")
            m_scr[...] = m_new

        @pl.when(j == i)
        def _():
            l = l_scr[...]
            o_ref[...] = acc_scr[...] / l
            lse_ref[...] = jnp.broadcast_to(m_scr[...] + jnp.log(l), (blk, HEAD_DIM))

    qspec = pl.BlockSpec((blk, HEAD_DIM), lambda h, i, j: (i, h))
    kspec = pl.BlockSpec((blk, HEAD_DIM), lambda h, i, j: (jnp.minimum(j, i), h))
    return _pallas(
        kern, comm=comm, name="attn_fwd", grid=(N_HEADS, nb, nb),
        in_specs=[qspec, kspec, kspec,
                  pl.BlockSpec((blk, 128), lambda h, i, j: (i, 0)),
                  pl.BlockSpec((N_HEADS, blk), lambda h, i, j: (0, jnp.minimum(j, i)))],
        out_specs=[qspec, qspec],
        out_shape=[jax.ShapeDtypeStruct((T, W), F32)] * 2,
        scratch_shapes=[pltpu.VMEM((blk, 1), F32), pltpu.VMEM((blk, 1), F32), pltpu.VMEM((blk, HEAD_DIM), F32)],
        compiler_params=_params(),
    )(qn, kn, vb, f_tm, f_hm)


def _attn_bwd(qn, kn, vb, do, lse, delta, f_tm, f_hm, T, blk, comm=None):
    nb = T // blk
    scale = HEAD_DIM ** -0.5
    W = N_HEADS * HEAD_DIM

    def kern(q_ref, k_ref, v_ref, do_ref, lse_ref, dl_ref, ft_ref, fh_ref,
             dq_ref, dfq_ref, dk_ref, dv_ref, df_ref, dq_scr, dfq_scr, dk_scr, dv_scr, df_scr):
        h, j, i = pl.program_id(0), pl.program_id(1), pl.program_id(2)

        @pl.when((j == 0) & (i == 0))
        def _():
            dq_scr[...] = jnp.zeros_like(dq_scr)
            dfq_scr[...] = jnp.zeros_like(dfq_scr)

        @pl.when(i == 0)
        def _():
            dk_scr[...] = jnp.zeros_like(dk_scr)
            dv_scr[...] = jnp.zeros_like(dv_scr)
            df_scr[...] = jnp.zeros_like(df_scr)

        @pl.when(i >= j)
        def _():
            q, k, v = q_ref[...], k_ref[...], v_ref[...]
            dob = do_ref[...].astype(BF16)
            s = _dot(q, k, "NT") * scale + _gate_bias(ft_ref[...], fh_ref[...], h)
            p = jnp.where(_causal(i, j, blk), jnp.exp(s - lse_ref[:, 0:1]), 0.0)
            dv_scr[...] += _dot(p, dob, "TN")
            dp = _dot(dob, v, "NT")
            ds = p * (dp - dl_ref[:, 0:1])
            df_scr[...] += jnp.sum(ds, axis=0, keepdims=True)
            dsb = ds.astype(BF16)
            dk_scr[...] += _dot(dsb, q, "TN") * scale
            rows = pl.ds(pl.multiple_of(i * blk, blk), blk)
            dq_scr[rows, :] += _dot(dsb, k, "NN") * scale
            dfq_scr[rows, :] += jnp.sum(ds, axis=1, keepdims=True)

        @pl.when(i == nb - 1)
        def _():
            dk_ref[...] = dk_scr[...]
            dv_ref[...] = dv_scr[...]
            df_ref[...] = -df_scr[...]

        @pl.when((j == nb - 1) & (i == nb - 1))
        def _():
            dq_ref[...] = dq_scr[...]
            dfq_ref[...] = jnp.broadcast_to(dfq_scr[...], (T, HEAD_DIM))

    qspec = pl.BlockSpec((blk, HEAD_DIM), lambda h, j, i: (jnp.maximum(i, j), h))
    full = pl.BlockSpec((T, HEAD_DIM), lambda h, j, i: (0, h))
    kspec = pl.BlockSpec((blk, HEAD_DIM), lambda h, j, i: (j, h))
    return _pallas(
        kern, comm=comm, name="attn_bwd", grid=(N_HEADS, nb, nb),
        in_specs=[qspec, kspec, kspec, qspec, qspec, qspec,
                  pl.BlockSpec((blk, 128), lambda h, j, i: (jnp.maximum(i, j), 0)),
                  pl.BlockSpec((N_HEADS, blk), lambda h, j, i: (0, j))],
        out_specs=[full, full, kspec, kspec, pl.BlockSpec((None, 1, blk), lambda h, j, i: (h, 0, j))],
        out_shape=[jax.ShapeDtypeStruct((T, W), F32)] * 4 + [jax.ShapeDtypeStruct((N_HEADS, 1, T), F32)],
        scratch_shapes=[pltpu.VMEM((T, HEAD_DIM), F32), pltpu.VMEM((T, 1), F32), pltpu.VMEM((blk, HEAD_DIM), F32),
                        pltpu.VMEM((blk, HEAD_DIM), F32), pltpu.VMEM((1, blk), F32)],
        compiler_params=_params(),
    )(qn, kn, vb, do, lse, delta, f_tm, f_hm)


def _attn_delta(o, do, T, tb):
    W = N_HEADS * HEAD_DIM

    def body(ob, dob):
        return jnp.concatenate(
            _heads(lambda a, b: jnp.broadcast_to(jnp.sum(a * b, axis=1, keepdims=True), a.shape), ob, dob), axis=1)

    return _rowwise("attn_delta", body, T, tb, [(o, W, 0), (do, W, 0)], [], [(W, F32)], [])[0]


def _window_select(s, g, shift):
    picks = []
    for k in (1, 2, 4, 8):
        s = s + shift(s, k)
        picks.append(s)
    return jnp.where(g == 0, picks[0], jnp.where(g == 1, picks[1], jnp.where(g == 2, picks[2], picks[3])))


def _group_window(g):
    return jnp.where(g == 0, POOL_WINDOWS[0], jnp.where(g == 1, POOL_WINDOWS[1],
                     jnp.where(g == 2, POOL_WINDOWS[2], POOL_WINDOWS[3])))


def _pool_fwd(proj, ucol, pw, ps, T, tb):
    C = POOL_GROUP_DIM
    n_g = len(POOL_WINDOWS)

    def kern(uc_ref, up_ref, pw_ref, ps_ref, pooled_ref, out_ref):
        g, i = pl.program_id(0), pl.program_id(1)
        uc = uc_ref[...]
        t2 = (i - 1) * tb + lax.broadcasted_iota(jnp.int32, (2 * tb, C), 0)
        u2 = jnp.where(t2 >= 0, jnp.concatenate([up_ref[...], uc], axis=0), 0.0)
        sums = _window_select(u2, g, lambda s, k: pltpu.roll(s, k, 0))[tb:, :]
        count = jnp.minimum(t2[tb:, :] + 1, _group_window(g)).astype(F32)
        pooled = sums / count - uc
        pooled_ref[...] = pooled.astype(BF16)
        out_ref[...] = _dot(pooled, pw_ref[...], "NN") * ps_ref[...]

    ospec = pl.BlockSpec((tb, C), lambda g, i: (i, g))
    return pl.pallas_call(
        kern, name="pool_fwd", grid=(n_g, T // tb),
        in_specs=[pl.BlockSpec((tb, C), lambda g, i: (i, ucol + g)),
                  pl.BlockSpec((tb, C), lambda g, i: (jnp.maximum(i - 1, 0), ucol + g)),
                  pl.BlockSpec((None, C, C), lambda g, i: (g, 0, 0)),
                  pl.BlockSpec((1, C), lambda g, i: (0, g))],
        out_specs=[ospec, ospec],
        out_shape=[jax.ShapeDtypeStruct((T, n_g * C), BF16), jax.ShapeDtypeStruct((T, n_g * C), F32)],
        compiler_params=_params(),
    )(proj, proj, pw, ps)


def _pool_bwd(dmix_in, dcol, pooled, pw, ps, T, tb):
    C = POOL_GROUP_DIM
    n_g = len(POOL_WINDOWS)
    nb = T // tb

    def kern(dc_ref, dn_ref, pooled_ref, pw_ref, ps_ref, du_ref, dpw_ref, dps_ref):
        g, i = pl.program_id(0), pl.program_id(1)
        dc = dc_ref[...]
        scale = ps_ref[...]
        t2 = i * tb + lax.broadcasted_iota(jnp.int32, (2 * tb, C), 0)
        d2 = jnp.where(t2 < T, jnp.concatenate([dc, dn_ref[...]], axis=0) * scale, 0.0)
        dpooled2 = _dot(d2, pw_ref[...], "NT")
        count = jnp.minimum(t2 + 1, _group_window(g)).astype(F32)
        sums = _window_select(dpooled2 / count, g, lambda s, k: pltpu.roll(s, 2 * tb - k, 0))
        du_ref[...] = (sums[:tb, :] - dpooled2[:tb, :]).astype(BF16)
        pooled = pooled_ref[...]
        p = _dot(pooled, pw_ref[...], "NN")
        dps = jnp.sum(dc * p, axis=0, keepdims=True)
        dpw = _dot(pooled, d2[:tb, :], "TN")

        @pl.when(i == 0)
        def _():
            dps_ref[...] = dps
            dpw_ref[...] = dpw

        @pl.when(i > 0)
        def _():
            dps_ref[...] += dps
            dpw_ref[...] += dpw

    return pl.pallas_call(
        kern, name="pool_bwd", grid=(n_g, nb),
        in_specs=[pl.BlockSpec((tb, C), lambda g, i: (i, dcol + g)),
                  pl.BlockSpec((tb, C), lambda g, i: (jnp.minimum(i + 1, nb - 1), dcol + g)),
                  pl.BlockSpec((tb, C), lambda g, i: (i, g)),
                  pl.BlockSpec((None, C, C), lambda g, i: (g, 0, 0)),
                  pl.BlockSpec((1, C), lambda g, i: (0, g))],
        out_specs=[pl.BlockSpec((tb, C), lambda g, i: (i, g)),
                   pl.BlockSpec((None, C, C), lambda g, i: (g, 0, 0)),
                   pl.BlockSpec((1, C), lambda g, i: (0, g))],
        out_shape=[jax.ShapeDtypeStruct((T, n_g * C), BF16), jax.ShapeDtypeStruct((n_g, C, C), F32),
                   jax.ShapeDtypeStruct((1, n_g * C), F32)],
        compiler_params=_params(),
    )(dmix_in, dmix_in, pooled, pw, ps)


D_QKV = 3 * N_HEADS * HEAD_DIM
D_U = len(POOL_WINDOWS) * POOL_GROUP_DIM
F_PAD = 128
D_PROJ = D_QKV + D_U + F_PAD


def _perm_w_in(w):
    pad = jnp.zeros((w.shape[0], F_PAD - N_HEADS), w.dtype)
    return jnp.concatenate([w[:, :D_QKV], w[:, D_QKV + N_HEADS:], w[:, D_QKV:D_QKV + N_HEADS], pad], axis=1)


def _unperm_w_in(w):
    return jnp.concatenate([w[:, :D_QKV], w[:, D_QKV + D_U:D_QKV + D_U + N_HEADS], w[:, D_QKV:D_QKV + D_U]], axis=1)


def _mixer_fwd(x, norm_g, sh, sc, gate, w_in_p, b_pad, gq, gk, pw, ps, w_out, T, attn_comm=None):
    tb = min(256, T)
    blk = min(512, T)
    hm = _norm_mod_fwd("mix_norm_fwd", x, norm_g, sc, sh, T, tb)
    proj = _mm("mix_proj", hm, w_in_p, "NN", F32, 512, D_PROJ // 3, 2048)
    qn, kn, vb = _qknorm_fwd(proj, gq, gk, T, tb)
    fcol = (D_QKV + D_U) // 128
    f_tm = _fgate_fwd(proj, fcol, b_pad, T)
    f_hm = f_tm[:, :N_HEADS].T
    (o, lse), got = _hosted(attn_comm, _attn_fwd(qn, kn, vb, f_tm, f_hm, T, blk, comm=attn_comm))
    pooled, pool_o = _pool_fwd(proj, D_QKV // POOL_GROUP_DIM, pw, ps, T, tb)
    mix_in = jnp.concatenate([o.astype(BF16), pool_o.astype(BF16)], axis=1)
    mix = _mm("mix_out", mix_in, w_out, "NN", F32, 512, 2048, 2048)
    x_out = _residual("mix_res", x, mix, gate, 1.0, T, tb)
    return x_out, (x, hm, proj, qn, kn, vb, f_tm, f_hm, o, lse, pooled, mix_in, mix), got


def _mixer_bwd(dx_out, saved, norm_g, sc, gate, w_in_p, b_pad, gq, gk, pw, ps, w_out, T, core):
    x, hm, proj, qn, kn, vb, f_tm, f_hm, o, lse, pooled, mix_in, mix = saved
    tb = min(256, T)
    blk = min(512, T)
    W = N_HEADS * HEAD_DIM
    D = x.shape[1]
    n_g = len(POOL_WINDOWS)
    dmix, dgate = _residual_bwd("mix_res_bwd", dx_out, mix, gate, 1.0, T, tb)
    dmix_in = _mm("mix_out_bwd", dmix, w_out, "NT", F32, 512, 2048, 2048)
    dw_out = _mm("mix_dw_out", mix_in, dmix, "TN", BF16, 2048, 1024, 512)
    delta = _attn_delta(o, dmix_in, T, tb)
    dqn, dfq, dkn, dv, dfk = _attn_bwd(qn, kn, vb, dmix_in, lse, delta, f_tm, f_hm, T, blk)
    dq, dk, dgq, dgk = _qknorm_bwd(proj, dqn, dkn, gq, gk, T, tb)
    dF = jnp.pad(dfq[:, ::HEAD_DIM] + dfk.reshape(N_HEADS, T).T, ((0, 0), (0, F_PAD - N_HEADS)))
    fcol = (D_QKV + D_U) // 128
    dfl, dbf = _fgate_bwd(proj, fcol, b_pad, dF, T)
    du, dpw, dps = _pool_bwd(dmix_in, W // POOL_GROUP_DIM, pooled, pw, ps, T, tb)
    dproj = jnp.concatenate([dq, dk, dv.astype(BF16), du, dfl.astype(BF16)], axis=1)
    dw_in_p = _mm("mix_dw_in", hm, dproj, "TN", BF16, 2048, D_PROJ // 3, 512)
    pw_rows = POOL_GROUP_DIM // N_DEV
    slabs = [jnp.transpose(_unperm_w_in(dw_in_p).reshape(D, N_DEV, -1), (1, 0, 2)),
             jnp.transpose(dpw.astype(BF16).reshape(n_g, N_DEV, pw_rows, POOL_GROUP_DIM),
                           (1, 0, 2, 3)).reshape(N_DEV, n_g * pw_rows, POOL_GROUP_DIM),
             dw_out.reshape(N_DEV, -1, D)]
    _, sums = _reduce_level1("mix", slabs, core, [TILE_W_IN, TILE_POOL, TILE_MIX_OUT])
    dhm, got_w = _mm("mix_proj_bwd", dproj, w_in_p, "NT", F32, 512, 2048, D_PROJ // 3, comm=_chip_comm(sums))
    dx, dsh, dsc, dng = _norm_mod_bwd("mix_norm_bwd", x, dhm, dx_out, norm_g, sc, T, tb)
    return dx, (dsh, dsc, dgate, dng), got_w, dps, dgq, dgk, dbf


def kernel(x, c, w_ada, b_ada, ffn1_norm_g, ffn1_w_in, ffn1_w_out, mix_norm_g, w_in, b_forget, q_norm_g, k_norm_g, pool_w, pool_scale, w_out, ffn2_norm_g, ffn2_w_in, ffn2_w_out, final_norm_g, loss_target, m_w_ada, m_b_ada, m_ffn1_norm_g, m_ffn1_w_in, m_ffn1_w_out, m_mix_norm_g, m_w_in, m_b_forget, m_q_norm_g, m_k_norm_g, m_pool_w, m_pool_scale, m_w_out, m_ffn2_norm_g, m_ffn2_w_in, m_ffn2_w_out, m_final_norm_g, v_w_ada, v_b_ada, v_ffn1_norm_g, v_ffn1_w_in, v_ffn1_w_out, v_mix_norm_g, v_w_in, v_b_forget, v_q_norm_g, v_k_norm_g, v_pool_w, v_pool_scale, v_w_out, v_ffn2_norm_g, v_ffn2_w_in, v_ffn2_w_out, v_final_norm_g):
    T, D = x.shape[1], x.shape[2]
    mx, my, mc = _mesh_pos()
    me = _flat(mx, my, mc)
    x0 = x[0]
    tgt = loss_target[0]
    tb = min(256, T)

    core = jnp.reshape(mc, (1,)).astype(jnp.int32)
    half = N_DEV // 2
    n_g = len(POOL_WINDOWS)
    pw_rows = POOL_GROUP_DIM // N_DEV

    def bf(w):
        return w.astype(BF16)

    n_loc = w_ada.shape[2]
    c_all = _standalone("gather_c", _gather_comm([c.reshape(8, D // 8)]))[0].reshape(N_DEV, D)
    b_loc = lax.dynamic_slice_in_dim(b_ada, me * n_loc, n_loc, axis=1)
    mod_loc = _ada_fwd(c_all, w_ada[0], b_loc, n_loc // 3)
    mod_all = _standalone("gather_mod", _gather_comm([mod_loc]))[0]
    mod = lax.dynamic_index_in_dim(mod_all, me, axis=1, keepdims=False).reshape(N_MOD, 1, D)
    sh1, sc1, g1, sh2, sc2, g2, sh3, sc3, g3 = [mod[k] for k in range(N_MOD)]
    b_pad = jnp.pad(b_forget, ((0, 0), (0, F_PAD - N_HEADS)))
    ps = pool_scale

    wi1 = _standalone("gather_ffn1_w_in", _gather_comm([bf(ffn1_w_in[0])]))[0]
    mixer_shards = [bf(w_in[0]), bf(pool_w[0].reshape(-1, POOL_GROUP_DIM)), bf(w_out[0])]
    x1, sv1, wo1, (w_in_g, pool_g, w_out_g) = _ffn_fwd(
        "ffn1", x0, ffn1_norm_g, sh1, sc1, g1, wi1, lambda got: got[0].reshape(half, -1, D), T,
        up_comm=_gather_comm([bf(ffn1_w_out[0])]), down_comm=_gather_comm(mixer_shards))
    w_in_p = _perm_w_in(jnp.transpose(w_in_g, (1, 0, 2)).reshape(D, -1))
    pw_full = jnp.transpose(pool_g.reshape(N_DEV, n_g, pw_rows, POOL_GROUP_DIM),
                            (1, 0, 2, 3)).reshape(n_g, POOL_GROUP_DIM, POOL_GROUP_DIM)
    w_out_full = w_out_g.reshape(-1, D)
    x2, svm, (wi2, wo2_g) = _mixer_fwd(
        x1, mix_norm_g, sh2, sc2, g2, w_in_p, b_pad, q_norm_g, k_norm_g, pw_full, ps, w_out_full, T,
        attn_comm=_gather_comm([bf(ffn2_w_in[0]), bf(ffn2_w_out[0])]))
    x3, sv2, wo2, _ = _ffn_fwd("ffn2", x2, ffn2_norm_g, sh3, sc3, g3, wi2,
                               lambda got: wo2_g.reshape(half, -1, D), T)
    dx3, dgf, loss_l = _final_loss(x3, tgt, final_norm_g.reshape(1, D), T, tb)
    loss = lax.psum(loss_l[0, 0], ("x", "y", "c"))

    dx2, (dsh3, dsc3, dg3, dn3), dwi2, dwo2 = _ffn_bwd("ffn2", dx3, sv2, ffn2_norm_g, sc3, g3, wi2, wo2, T, core)
    dx1, (dsh2, dsc2, dg2, dn2), (dw_in_r, dpw_r, dw_out_r), dps, dgq, dgk, dbf = _mixer_bwd(
        dx2, svm, mix_norm_g, sc2, g2, w_in_p, b_pad, q_norm_g, k_norm_g, pw_full, ps, w_out_full, T, core)
    dx0, (dsh1, dsc1, dg1, dn1), dwi1, dwo1 = _ffn_bwd("ffn1", dx1, sv1, ffn1_norm_g, sc1, g1, wi1, wo1, T, core)

    received = dict(ffn1_w_in=dwi1, ffn1_w_out=dwo1, w_in=dw_in_r, pool_w=dpw_r, w_out=dw_out_r,
                    ffn2_w_in=dwi2, ffn2_w_out=dwo2)
    moments = dict(ffn1_w_in=(m_ffn1_w_in, v_ffn1_w_in), ffn1_w_out=(m_ffn1_w_out, v_ffn1_w_out),
                   w_in=(m_w_in, v_w_in), pool_w=(m_pool_w, v_pool_w), w_out=(m_w_out, v_w_out),
                   ffn2_w_in=(m_ffn2_w_in, v_ffn2_w_in), ffn2_w_out=(m_ffn2_w_out, v_ffn2_w_out))
    weights = dict(ffn1_w_in=ffn1_w_in, ffn1_w_out=ffn1_w_out, w_in=w_in, pool_w=pool_w, w_out=w_out,
                   ffn2_w_in=ffn2_w_in, ffn2_w_out=ffn2_w_out)
    row_tiles = dict(ffn1_w_in=TILE_W_IN, ffn1_w_out=TILE_W_OUT, w_in=TILE_W_IN, pool_w=TILE_POOL,
                     w_out=TILE_MIX_OUT, ffn2_w_in=TILE_W_IN, ffn2_w_out=TILE_W_OUT)
    results = {}
    for k in received:
        shape = weights[k].shape
        two_d = received[k].shape[1:]
        mk, vk = moments[k]
        outs = _adamw("adamw_" + k, received[k], weights[k].reshape(two_d), mk.reshape(two_d),
                      vk.reshape(two_d), row_tiles[k])
        results[k] = [o.reshape(shape) for o in outs]

    dmod = jnp.concatenate([dsh1, dsc1, dg1, dsh2, dsc2, dg2, dsh3, dsc3, dg3], axis=1)
    small_names = ["b_ada", "ffn1_norm_g", "mix_norm_g", "ffn2_norm_g", "final_norm_g", "b_forget",
                   "q_norm_g", "k_norm_g", "pool_scale"]
    small_w = dict(b_ada=b_ada, ffn1_norm_g=ffn1_norm_g, mix_norm_g=mix_norm_g, ffn2_norm_g=ffn2_norm_g,
                   final_norm_g=final_norm_g, b_forget=b_forget, q_norm_g=q_norm_g, k_norm_g=k_norm_g,
                   pool_scale=pool_scale)
    small_m = dict(b_ada=m_b_ada, ffn1_norm_g=m_ffn1_norm_g, mix_norm_g=m_mix_norm_g, ffn2_norm_g=m_ffn2_norm_g,
                   final_norm_g=m_final_norm_g, b_forget=m_b_forget, q_norm_g=m_q_norm_g, k_norm_g=m_k_norm_g,
                   pool_scale=m_pool_scale)
    small_v = dict(b_ada=v_b_ada, ffn1_norm_g=v_ffn1_norm_g, mix_norm_g=v_mix_norm_g, ffn2_norm_g=v_ffn2_norm_g,
                   final_norm_g=v_final_norm_g, b_forget=v_b_forget, q_norm_g=v_q_norm_g, k_norm_g=v_k_norm_g,
                   pool_scale=v_pool_scale)
    small_g = dict(b_ada=dmod, ffn1_norm_g=dn1, mix_norm_g=dn2, ffn2_norm_g=dn3, final_norm_g=dgf,
                   b_forget=dbf[:, :N_HEADS], q_norm_g=dgq, k_norm_g=dgk, pool_scale=dps)
    sizes = [small_w[k].size for k in small_names]
    total = sum(sizes)
    lanes = 8 * 128
    padded = -(-total // lanes) * lanes

    def pack(d):
        flat = jnp.concatenate([d[k].reshape(-1) for k in small_names])
        return jnp.pad(flat, (0, padded - total)).reshape(8, padded // 8)

    small_parts = _standalone("gather_small_grads", _gather_comm([pack(small_g)]))[0]
    s_outs = _adamw("adamw_small", small_parts, pack(small_w), pack(small_m), pack(small_v), 8)
    offs = [0]
    for s in sizes:
        offs.append(offs[-1] + s)
    for idx, k in enumerate(small_names):
        results[k] = [o.reshape(-1)[offs[idx]:offs[idx + 1]].reshape(small_w[k].shape) for o in s_outs]

    dmod_all = small_parts.reshape(N_DEV, padded)[:, :N_MOD * D]
    dmod_loc = lax.dynamic_slice_in_dim(dmod_all, me * n_loc, n_loc, axis=1)
    g_ada = _ada_bwd(c_all, dmod_loc, n_loc // 3)
    a_outs = _adamw("adamw_w_ada", g_ada[None], w_ada[0], m_w_ada[0], v_w_ada[0], 128)
    results["w_ada"] = [o.reshape(w_ada.shape) for o in a_outs]

    order = ["w_ada", "b_ada", "ffn1_norm_g", "ffn1_w_in", "ffn1_w_out", "mix_norm_g", "w_in", "b_forget",
             "q_norm_g", "k_norm_g", "pool_w", "pool_scale", "w_out", "ffn2_norm_g", "ffn2_w_in", "ffn2_w_out",
             "final_norm_g"]
    out = [loss, dx0[None]]
    for part in range(4):
        out += [results[k][part] for k in order]
    return tuple(out)
```

```python
import jax
import jax.numpy as jnp
from jax import lax
from jax.experimental import pallas as pl
from jax.experimental.pallas import tpu as pltpu

F32 = jnp.float32
BF16 = jnp.bfloat16
MESH = pl.DeviceIdType.MESH
ANY = pl.BlockSpec(memory_space=pl.ANY)

N_DEV = 8
EPS = 1e-6
HEAD_DIM = 128
N_HEADS = 8
POOL_WINDOWS = (2, 4, 8, 16)
POOL_GROUP_DIM = 256
N_MOD = 9
ADAM_LR = 0.001
ADAM_B1 = 0.9
ADAM_B2 = 0.999
ADAM_EPS = 1e-08
ADAM_WD = 0.01
ADAM_STEP = 10
NEG = -1e30
VMEM_LIMIT_V7X = 56 * 1024 * 1024


def _params():
    return pltpu.CompilerParams(vmem_limit_bytes=VMEM_LIMIT_V7X)


def _sigmoid(z):
    return 1.0 / (1.0 + jnp.exp(-z))


def _rstd(x):
    return lax.rsqrt(jnp.mean(x * x, axis=-1, keepdims=True) + EPS)


def _mesh_pos():
    return lax.axis_index("x"), lax.axis_index("y"), lax.axis_index("c")


def _flat(px, py, pc):
    return 4 * px + 2 * py + pc


class _Comm:
    def __init__(self, ins, outs, sems, phases):
        self.ins, self.outs, self.sems, self.phases = list(ins), list(outs), list(sems), list(phases)


def _pallas(kern, *, comm=None, **kw):
    if comm is None:
        return pl.pallas_call(kern, **kw)
    grid = tuple(kw["grid"])
    single = not isinstance(kw["out_shape"], (list, tuple))
    out_shape = [kw["out_shape"]] if single else list(kw["out_shape"])
    out_specs = [kw["out_specs"]] if single else list(kw["out_specs"])
    in_specs = list(kw["in_specs"])
    scratch = list(kw.get("scratch_shapes", ()))
    n_in, n_out, n_scr = len(in_specs), len(out_shape), len(scratch)
    n_ci, n_co = len(comm.ins), len(comm.outs)
    strides, n_steps = [], 1
    for g in reversed(grid):
        strides.insert(0, n_steps)
        n_steps *= g

    def wrapped(*refs):
        ins, cins = refs[:n_in], refs[n_in:n_in + n_ci]
        base = n_in + n_ci
        outs, couts = refs[base:base + n_out], refs[base + n_out:base + n_out + n_co]
        base += n_out + n_co
        scr, sems = refs[base:base + n_scr], refs[base + n_scr:]
        step = sum(pl.program_id(d) * strides[d] for d in range(len(grid)))
        for frac, fn in comm.phases:
            if frac < 1.0:
                pl.when(step == int(round(frac * (n_steps - 1))))(lambda fn=fn: fn(cins, couts, sems))
        kern(*ins, *outs, *scr)
        for frac, fn in comm.phases:
            if frac >= 1.0:
                pl.when(step == n_steps - 1)(lambda fn=fn: fn(cins, couts, sems))

    kw = dict(kw, in_specs=in_specs + [ANY] * n_ci, out_specs=out_specs + [ANY] * n_co,
              out_shape=out_shape + comm.outs, scratch_shapes=scratch + comm.sems)
    call = pl.pallas_call(wrapped, **kw)

    def run(*args):
        res = call(*args, *comm.ins)
        main = res[0] if single else list(res[:n_out])
        return main, list(res[n_out:])

    return run


def _hosted(comm, res):
    return res if comm is not None else (res, [])


def _standalone(name, comm):
    def kern():
        pass

    return _pallas(kern, comm=comm, name=name, grid=(1,), in_specs=[], out_specs=[], out_shape=[])()[1]


def _dma_sems(*shapes):
    return [pltpu.SemaphoreType.DMA(s) for s in shapes]


def _gather_comm(arrs, forward_at=0.5):
    n = len(arrs)

    def setup(outs, sems):
        send_sems, recv_sems, _ = sems
        x, y, c = _mesh_pos()
        chips = [(1 - x, y), (x, 1 - y), (1 - x, 1 - y)]

        def copy(a, k, block, to, src=None):
            dst = outs[a].at[_flat(*block)]
            return pltpu.make_async_remote_copy(
                src_ref=dst if src is None else src, dst_ref=dst,
                send_sem=send_sems.at[a, k], recv_sem=recv_sems.at[a, k],
                device_id=to, device_id_type=MESH)

        return (x, y, c), (x, y, 1 - c), chips, copy

    def local(ins, outs, sems, a, me):
        return pltpu.make_async_copy(ins[a], outs[a].at[_flat(*me)], sems[2].at[a])

    def send_own(ins, outs, sems):
        me, sibling, chips, copy = setup(outs, sems)
        for a in range(n):
            local(ins, outs, sems, a, me).start()
            copy(a, 0, me, sibling, src=ins[a]).start()
            for j, chip in enumerate(chips):
                copy(a, 1 + j, me, (*chip, me[2]), src=ins[a]).start()

    def forward(ins, outs, sems):
        me, sibling, chips, copy = setup(outs, sems)
        for a in range(n):
            for j, chip in enumerate(chips):
                copy(a, 1 + j, (*chip, me[2]), me).wait_recv()
                copy(a, 4 + j, (*chip, me[2]), sibling).start()

    def finish(ins, outs, sems):
        me, sibling, chips, copy = setup(outs, sems)
        for a in range(n):
            copy(a, 0, sibling, me).wait_recv()
            for j, chip in enumerate(chips):
                copy(a, 4 + j, (*chip, 1 - me[2]), me).wait_recv()
        for a in range(n):
            copy(a, 0, me, sibling, src=ins[a]).wait_send()
            for j, chip in enumerate(chips):
                copy(a, 1 + j, me, (*chip, me[2]), src=ins[a]).wait_send()
                copy(a, 4 + j, (*chip, me[2]), sibling).wait_send()
            local(ins, outs, sems, a, me).wait()

    return _Comm(arrs, [jax.ShapeDtypeStruct((N_DEV,) + a.shape, a.dtype) for a in arrs],
                 _dma_sems((n, 7), (n, 7), (n,)), [(0.0, send_own), (forward_at, forward), (1.0, finish)])


CHIPS = [(0, 0), (0, 1), (1, 0), (1, 1)]


def _sibling_comm(parts):
    n = len(parts)

    def copies(ins, outs, sems):
        x, y, c = _mesh_pos()
        return [pltpu.make_async_remote_copy(
                    src_ref=ins[a].at[_flat(qx, qy, 1 - c)], dst_ref=outs[a].at[q],
                    send_sem=sems[0].at[a, q], recv_sem=sems[1].at[a, q],
                    device_id=(x, y, 1 - c), device_id_type=MESH)
                for a in range(n) for q, (qx, qy) in enumerate(CHIPS)]

    def start(ins, outs, sems):
        for cp in copies(ins, outs, sems):
            cp.start()

    def finish(ins, outs, sems):
        for cp in copies(ins, outs, sems):
            cp.wait_recv()
        for cp in copies(ins, outs, sems):
            cp.wait_send()

    return _Comm(parts, [jax.ShapeDtypeStruct((4,) + p.shape[1:], p.dtype) for p in parts],
                 _dma_sems((n, 4), (n, 4)), [(0.0, start), (1.0, finish)])


def _chip_comm(sums):
    n = len(sums)
    flips = [(1, 0), (0, 1), (1, 1)]

    def own(ins, outs, sems):
        mine = 2 * lax.axis_index("x") + lax.axis_index("y")
        return [pltpu.make_async_copy(ins[a].at[mine], outs[a].at[mine], sems[2].at[a]) for a in range(n)]

    def copies(ins, outs, sems, arriving=False):
        x, y, c = _mesh_pos()
        mine = 2 * x + y
        remote = []
        for a in range(n):
            for k, (fx, fy) in enumerate(flips):
                qx, qy = x ^ fx, y ^ fy
                q = 2 * qx + qy
                remote.append(pltpu.make_async_remote_copy(
                    src_ref=ins[a].at[q], dst_ref=outs[a].at[q if arriving else mine],
                    send_sem=sems[0].at[a, k], recv_sem=sems[1].at[a, k],
                    device_id=(qx, qy, c), device_id_type=MESH))
        return remote

    def start(ins, outs, sems):
        for cp in own(ins, outs, sems) + copies(ins, outs, sems):
            cp.start()

    def finish(ins, outs, sems):
        for cp in copies(ins, outs, sems, arriving=True):
            cp.wait_recv()
        for cp in copies(ins, outs, sems):
            cp.wait_send()
        for cp in own(ins, outs, sems):
            cp.wait()

    return _Comm(sums, [jax.ShapeDtypeStruct(s.shape, s.dtype) for s in sums],
                 _dma_sems((n, 3), (n, 3), (n,)), [(0.0, start), (1.0, finish)])


def _pair_add(name, parts, got, core, tr):
    _, R, C = parts.shape
    assert R % tr == 0

    def kern(c_ref, p_ref, g_ref, o_ref):
        o_ref[...] = (p_ref[...].astype(F32) + g_ref[...].astype(F32)).astype(o_ref.dtype)

    blk = pl.BlockSpec((None, tr, C), lambda q, i, c_ref: (q, i, 0))
    return pl.pallas_call(
        kern, name=name,
        grid_spec=pltpu.PrefetchScalarGridSpec(
            num_scalar_prefetch=1, grid=(4, R // tr),
            in_specs=[pl.BlockSpec((None, tr, C), lambda q, i, c_ref: (2 * q + c_ref[0], i, 0)), blk],
            out_specs=blk),
        out_shape=jax.ShapeDtypeStruct((4, R, C), parts.dtype), compiler_params=_params(),
    )(core, parts, got)


def _rowwise(name, body, T, tb, rows, vecs, out_rows, out_accs):
    n_in = len(rows) + len(vecs)
    n_o, n_a = len(out_rows), len(out_accs)

    def kern(*refs):
        i = pl.program_id(0)
        res = body(*[r[...] for r in refs[:n_in]])
        if not isinstance(res, (tuple, list)):
            res = (res,)
        outs = refs[n_in:]
        for k in range(n_o):
            outs[k][...] = res[k].astype(outs[k].dtype)

        def accumulate(ref, val):
            @pl.when(i == 0)
            def _():
                ref[...] = val

            @pl.when(i > 0)
            def _():
                ref[...] += val

        for k in range(n_a):
            accumulate(outs[n_o + k], res[n_o + k])

    in_specs = [pl.BlockSpec((tb, w), lambda i, cb=cb: (i, cb)) for (_, w, cb) in rows]
    in_specs += [pl.BlockSpec((1, v.shape[1]), lambda i: (0, 0)) for v in vecs]
    out_specs = [pl.BlockSpec((tb, w), lambda i: (i, 0)) for (w, _) in out_rows]
    out_specs += [pl.BlockSpec((1, w), lambda i: (0, 0)) for w in out_accs]
    out_shape = [jax.ShapeDtypeStruct((T, w), dt) for (w, dt) in out_rows]
    out_shape += [jax.ShapeDtypeStruct((1, w), F32) for w in out_accs]
    res = pl.pallas_call(
        kern, name=name, grid=(T // tb,), in_specs=in_specs, out_specs=out_specs,
        out_shape=out_shape, compiler_params=_params(),
    )(*[r[0] for r in rows], *vecs)
    return res


def _dot(a, b, mode):
    dims = {"NN": ((1,), (0,)), "NT": ((1,), (1,)), "TN": ((0,), (0,))}[mode]
    return lax.dot_general(a.astype(BF16), b.astype(BF16), (dims, ((), ())),
                           preferred_element_type=F32)


def _mm(name, a, b, mode, out_dtype, tm, tn, tk, ga=False, gb=False, gmode=None, comm=None):
    G = (a.shape[0] if ga else b.shape[0]) if gmode else 1
    a2, b2 = a.shape[-2:], b.shape[-2:]
    if mode == "NN":
        (M, K), (_, N) = a2, b2
    elif mode == "NT":
        (M, K), (N, _) = a2, b2
    else:
        (K, M), (_, N) = a2, b2
    tm, tn, tk = min(tm, M), min(tn, N), min(tk, K)
    assert M % tm == 0 and N % tn == 0 and K % tk == 0, (name, M, N, K, tm, tn, tk)
    batch = gmode == "batch"
    n_gb, n_gs = (G if batch else 1), (G if gmode == "sum" else 1)
    nk = K // tk
    n_red = n_gs * nk

    def grp(g_b, g_s):
        return g_b if batch else g_s

    if mode == "TN":
        a_blk, a_idx = (tk, tm), lambda g_b, mi, ni, g_s, ki: (ki, mi)
    else:
        a_blk, a_idx = (tm, tk), lambda g_b, mi, ni, g_s, ki: (mi, ki)
    if mode == "NT":
        b_blk, b_idx = (tn, tk), lambda g_b, mi, ni, g_s, ki: (ni, ki)
    else:
        b_blk, b_idx = (tk, tn), lambda g_b, mi, ni, g_s, ki: (ki, ni)

    def with_group(blk, idx, has_group):
        if not has_group:
            return pl.BlockSpec(blk, idx)
        return pl.BlockSpec((None,) + blk, lambda g_b, mi, ni, g_s, ki: (grp(g_b, g_s),) + idx(g_b, mi, ni, g_s, ki))

    o_blk, o_idx = (tm, tn), lambda g_b, mi, ni, g_s, ki: (mi, ni)
    o_spec = with_group(o_blk, o_idx, batch)
    o_shape = ((G,) if batch else ()) + (M, N)

    def kern(a_ref, b_ref, o_ref, *scratch):
        part = _dot(a_ref[...], b_ref[...], mode)
        if n_red == 1:
            o_ref[...] = part.astype(o_ref.dtype)
            return
        acc = scratch[0]
        step = pl.program_id(3) * nk + pl.program_id(4)

        @pl.when(step == 0)
        def _():
            acc[...] = part

        @pl.when(step > 0)
        def _():
            acc[...] += part

        @pl.when(step == n_red - 1)
        def _():
            o_ref[...] = acc[...].astype(o_ref.dtype)

    return _pallas(
        kern, comm=comm, name=name, grid=(n_gb, M // tm, N // tn, n_gs, nk),
        in_specs=[with_group(a_blk, a_idx, ga), with_group(b_blk, b_idx, gb)],
        out_specs=o_spec, out_shape=jax.ShapeDtypeStruct(o_shape, out_dtype),
        scratch_shapes=[] if n_red == 1 else [pltpu.VMEM((tm, tn), F32)],
        compiler_params=_params(),
    )(a, b)


def _adamw(name, parts, w, m, v, tr):
    G, R, C = parts.shape
    assert R % tr == 0
    bc1 = 1.0 - ADAM_B1 ** ADAM_STEP
    bc2 = 1.0 - ADAM_B2 ** ADAM_STEP

    def kern(p_ref, w_ref, m_ref, v_ref, g_out, d_out, m_out, v_out):
        g = p_ref[0].astype(F32)
        for s in range(1, G):
            g = g + p_ref[s].astype(F32)
        m2 = ADAM_B1 * m_ref[...] + (1.0 - ADAM_B1) * g
        v2 = ADAM_B2 * v_ref[...] + (1.0 - ADAM_B2) * (g * g)
        m_hat = m2 / bc1
        v_hat = v2 / bc2
        g_out[...] = g
        d_out[...] = -ADAM_LR * (m_hat / (jnp.sqrt(v_hat) + ADAM_EPS) + ADAM_WD * w_ref[...])
        m_out[...] = m2
        v_out[...] = v2

    blk = pl.BlockSpec((tr, C), lambda i: (i, 0))
    return pl.pallas_call(
        kern, name=name, grid=(R // tr,),
        in_specs=[pl.BlockSpec((G, tr, C), lambda i: (0, i, 0)), blk, blk, blk],
        out_specs=[blk] * 4, out_shape=[jax.ShapeDtypeStruct((R, C), F32)] * 4,
        compiler_params=_params(),
    )(parts, w, m, v)


def _ada_fwd(c_all, w_loc, b_loc, tn):
    B, D = c_all.shape
    N = w_loc.shape[1]

    def kern(c_ref, w_ref, b_ref, o_ref):
        cc = c_ref[...]
        act = cc * _sigmoid(cc)
        o_ref[...] = _dot(act, w_ref[...], "NN") + b_ref[...]

    return pl.pallas_call(
        kern, name="ada_fwd", grid=(N // tn,),
        in_specs=[pl.BlockSpec((B, D), lambda j: (0, 0)), pl.BlockSpec((D, tn), lambda j: (0, j)),
                  pl.BlockSpec((1, tn), lambda j: (0, j))],
        out_specs=pl.BlockSpec((B, tn), lambda j: (0, j)),
        out_shape=jax.ShapeDtypeStruct((B, N), F32), compiler_params=_params(),
    )(c_all, w_loc, b_loc)


def _ada_bwd(c_all, dmod_loc, tn):
    B, D = c_all.shape
    N = dmod_loc.shape[1]

    def kern(c_ref, d_ref, o_ref):
        cc = c_ref[...]
        act = cc * _sigmoid(cc)
        o_ref[...] = _dot(act, d_ref[...], "TN")

    return pl.pallas_call(
        kern, name="ada_bwd", grid=(N // tn,),
        in_specs=[pl.BlockSpec((B, D), lambda j: (0, 0)), pl.BlockSpec((B, tn), lambda j: (0, j))],
        out_specs=pl.BlockSpec((D, tn), lambda j: (0, j)),
        out_shape=jax.ShapeDtypeStruct((D, N), F32), compiler_params=_params(),
    )(c_all, dmod_loc)


def _norm_mod_fwd(name, x, g, sc, sh, T, tb):
    D = x.shape[1]

    def body(xb, gb, scb, shb):
        n = (xb * _rstd(xb)) * gb
        return n * (1.0 + scb) + shb

    return _rowwise(name, body, T, tb, [(x, D, 0)], [g, sc, sh], [(D, BF16)], [])[0]


def _norm_mod_bwd(name, x, dhm, dres, g, sc, T, tb):
    D = x.shape[1]

    def body(xb, db, rb, gb, scb):
        r = _rstd(xb)
        xh = xb * r
        n = xh * gb
        dn = db * (1.0 + scb)
        dxh = dn * gb
        dx = rb + r * (dxh - xh * jnp.mean(dxh * xh, axis=-1, keepdims=True))
        return (dx, jnp.sum(db, axis=0, keepdims=True), jnp.sum(db * n, axis=0, keepdims=True),
                jnp.sum(dn * xh, axis=0, keepdims=True))

    return _rowwise(name, body, T, tb, [(x, D, 0), (dhm, D, 0), (dres, D, 0)], [g, sc],
                    [(D, F32)], [D, D, D])


def _residual(name, x, f, gate, coef, T, tb):
    D = x.shape[1]

    def body(xb, fb, gb):
        return xb + (coef * gb) * fb

    return _rowwise(name, body, T, tb, [(x, D, 0), (f, D, 0)], [gate], [(D, F32)], [])[0]


def _residual_bwd(name, dx, f, gate, coef, T, tb):
    D = dx.shape[1]

    def body(db, fb, gb):
        return (coef * gb) * db, jnp.sum((coef * fb) * db, axis=0, keepdims=True)

    return _rowwise(name, body, T, tb, [(dx, D, 0), (f, D, 0)], [gate], [(D, BF16)], [D])


def _final_loss(x, tgt, g, T, tb):
    D = x.shape[1]

    def body(xb, tb_, gb):
        r = _rstd(xb)
        xh = xb * r
        err = xh * gb - tb_
        loss = 0.5 * jnp.sum(jnp.mean(err * err, axis=-1, keepdims=True), axis=0, keepdims=True)
        dy = err * (1.0 / D)
        dxh = dy * gb
        dx = r * (dxh - xh * jnp.mean(dxh * xh, axis=-1, keepdims=True))
        return dx, jnp.sum(dy * xh, axis=0, keepdims=True), jnp.broadcast_to(loss, (1, 128))

    return _rowwise("final_loss", body, T, tb, [(x, D, 0), (tgt, D, 0)], [g], [(D, F32)], [D, 128])


def _ffn_up(name, hm, wi, T, tm, comm=None):
    D = hm.shape[1]
    Ws = wi.shape[2]
    half = wi.shape[0] // 2

    def kern(h_ref, wa_ref, wb_ref, a_ref, b_ref, hid_ref):
        h = h_ref[...]
        a = _dot(h, wa_ref[...], "NN")
        b = _dot(h, wb_ref[...], "NN")
        a_ref[...] = a
        b_ref[...] = b
        hid_ref[...] = ((a * _sigmoid(a)) * b).astype(BF16)

    o_spec = pl.BlockSpec((None, tm, Ws), lambda g, i: (g, i, 0))
    return _pallas(
        kern, comm=comm, name=name, grid=(half, T // tm),
        in_specs=[pl.BlockSpec((tm, D), lambda g, i: (i, 0)),
                  pl.BlockSpec((None, D, Ws), lambda g, i: (g, 0, 0)),
                  pl.BlockSpec((None, D, Ws), lambda g, i: (g + half, 0, 0))],
        out_specs=[o_spec] * 3,
        out_shape=[jax.ShapeDtypeStruct((half, T, Ws), F32)] * 2 + [jax.ShapeDtypeStruct((half, T, Ws), BF16)],
        compiler_params=_params(),
    )(hm, wi, wi)


def _ffn_down_bwd(name, df, wo, a, b, T, tm, comm=None):
    D = df.shape[1]
    half, _, Ws = a.shape

    def kern(df_ref, wo_ref, a_ref, b_ref, dp_ref):
        dhid = _dot(df_ref[...], wo_ref[...], "NT")
        av = a_ref[...]
        s = _sigmoid(av)
        silu = av * s
        dp_ref[0] = (dhid * b_ref[...] * (s + silu * (1.0 - s))).astype(BF16)
        dp_ref[1] = (dhid * silu).astype(BF16)

    act = pl.BlockSpec((None, tm, Ws), lambda g, i: (g, i, 0))
    return _pallas(
        kern, comm=comm, name=name, grid=(half, T // tm),
        in_specs=[pl.BlockSpec((tm, D), lambda g, i: (i, 0)),
                  pl.BlockSpec((None, Ws, D), lambda g, i: (g, 0, 0)), act, act],
        out_specs=pl.BlockSpec((2, None, tm, Ws), lambda g, i: (0, g, i, 0)),
        out_shape=jax.ShapeDtypeStruct((2, half, T, Ws), BF16),
        compiler_params=_params(),
    )(df, wo, a, b)


def _ffn_fwd(tag, x, norm_g, sh, sc, gate, wi, wo_of, T, up_comm=None, down_comm=None):
    tb = min(256, T)
    hm = _norm_mod_fwd(tag + "_norm_fwd", x, norm_g, sc, sh, T, tb)
    (a, b, hid), got_up = _hosted(up_comm, _ffn_up(tag + "_up", hm, wi, T, min(512, T), comm=up_comm))
    wo = wo_of(got_up)
    f, got_down = _hosted(down_comm, _mm(tag + "_down", hid, wo, "NN", F32, 512, 2048, 2048, ga=True, gb=True,
                                         gmode="sum", comm=down_comm))
    x_out = _residual(tag + "_res", x, f, gate, 0.5, T, tb)
    return x_out, (x, hm, a, b, hid, f), wo, got_down


TILE_W_IN = (128, 512)
TILE_W_OUT = (16, 688)
TILE_MIX_OUT = (64, 256)
TILE_POOL = (128, 128)


def _reduce_level1(tag, parts, core, tiles, host=None):
    comm = _sibling_comm(parts)
    if host is None:
        res, got = None, _standalone(tag + "_sibling", comm)
    else:
        res, got = host(comm)
    sums = [_pair_add("%s_pair_add%d" % (tag, k), p, g, core, min(t[1], p.shape[1]))
            for k, (p, g, t) in enumerate(zip(parts, got, tiles))]
    return res, sums


def _ffn_bwd(tag, dx_out, saved, norm_g, sc, gate, wi, wo, T, core, ride_sums=None, defer_dwi=False):
    x, hm, a, b, hid, f = saved
    tb = min(256, T)
    D = x.shape[1]
    df, dgate = _residual_bwd(tag + "_res_bwd", dx_out, f, gate, 0.5, T, tb)
    ride = None if ride_sums is None else _chip_comm(ride_sums)
    dwo, ride_got = _hosted(ride, _mm(tag + "_dwo", hid, df, "TN", BF16, 2048, 2048, 512, ga=True, gmode="batch",
                                      comm=ride))
    dwo = dwo.reshape(N_DEV, -1, D)
    dproj, (dwo_sum,) = _reduce_level1(
        tag + "_dwo", [dwo], core, [TILE_W_OUT],
        host=lambda comm: _ffn_down_bwd(tag + "_down_bwd", df, wo, a, b, T, min(512, T), comm=comm))
    dproj = dproj.reshape((2 * dproj.shape[1],) + dproj.shape[2:])
    dwi, (dwo_got,) = _mm(tag + "_dwi", hm, dproj, "TN", BF16, 2048, 2048, 512, gb=True, gmode="batch",
                          comm=_chip_comm([dwo_sum]))

    def dhm_call(comm):
        return _mm(tag + "_dhm", dproj, wi, "NT", F32, 512, 2048, 2048, ga=True, gb=True, gmode="sum", comm=comm)

    if defer_dwi:
        dhm, (dwi_out,) = _reduce_level1(tag + "_dwi", [dwi], core, [TILE_W_IN], host=dhm_call)
    else:
        _, (dwi_sum,) = _reduce_level1(tag + "_dwi", [dwi], core, [TILE_W_IN])
        dhm, (dwi_out,) = dhm_call(_chip_comm([dwi_sum]))
    dx, dsh, dsc, dng = _norm_mod_bwd(tag + "_norm_bwd", x, dhm, dx_out, norm_g, sc, T, tb)
    return dx, (dsh, dsc, dgate, dng), dwi_out, dwo_got, ride_got


def _heads(fn, *arrs):
    outs = [fn(*[a[:, h * HEAD_DIM:(h + 1) * HEAD_DIM] for a in arrs]) for h in range(N_HEADS)]
    return outs


def _qknorm_fwd(proj, gq, gk, T, tb):
    W = N_HEADS * HEAD_DIM

    def body(q, k, v, gqb, gkb):
        qn = jnp.concatenate(_heads(lambda t: (t * _rstd(t)) * gqb, q), axis=1)
        kn = jnp.concatenate(_heads(lambda t: (t * _rstd(t)) * gkb, k), axis=1)
        return qn, kn, v

    return _rowwise("qknorm_fwd", body, T, tb, [(proj, W, 0), (proj, W, 1), (proj, W, 2)], [gq, gk],
                    [(W, BF16)] * 3, [])


def _qknorm_bwd(proj, dqn, dkn, gq, gk, T, tb):
    W = N_HEADS * HEAD_DIM

    def one(t, dt, g):
        r = _rstd(t)
        th = t * r
        dth = dt * g
        d = r * (dth - th * jnp.mean(dth * th, axis=-1, keepdims=True))
        return d, jnp.sum(dt * th, axis=0, keepdims=True)

    def body(q, k, dq, dk, gqb, gkb):
        rq = _heads(lambda t, dt: one(t, dt, gqb), q, dq)
        rk = _heads(lambda t, dt: one(t, dt, gkb), k, dk)
        return (jnp.concatenate([r[0] for r in rq], axis=1), jnp.concatenate([r[0] for r in rk], axis=1),
                sum(r[1] for r in rq), sum(r[1] for r in rk))

    return _rowwise("qknorm_bwd", body, T, tb, [(proj, W, 0), (proj, W, 1), (dqn, W, 0), (dkn, W, 0)],
                    [gq, gk], [(W, BF16)] * 2, [HEAD_DIM, HEAD_DIM])


def _log_sigmoid(z):
    return jnp.minimum(z, 0.0) - jnp.log(1.0 + jnp.exp(-jnp.abs(z)))


def _fgate_fwd(proj, fcol, b_pad, T):
    nblk = T // 128

    def kern(f_ref, b_ref, o_ref):
        r = lax.broadcasted_iota(jnp.int32, (128, 128), 0)
        c = lax.broadcasted_iota(jnp.int32, (128, 128), 1)
        tri = (r >= c).astype(F32)
        carry = jnp.zeros((1, 128), F32)
        for k in range(nblk):
            rows = pl.ds(k * 128, 128)
            lf = _log_sigmoid(f_ref[rows, :] + b_ref[...])
            o_ref[rows, :] = jnp.dot(tri, lf, precision=lax.Precision.HIGHEST, preferred_element_type=F32) + carry
            carry = carry + jnp.sum(lf, axis=0, keepdims=True)

    return pl.pallas_call(
        kern, name="fgate_fwd", grid=(1,),
        in_specs=[pl.BlockSpec((T, 128), lambda i: (0, fcol)), pl.BlockSpec((1, 128), lambda i: (0, 0))],
        out_specs=pl.BlockSpec((T, 128), lambda i: (0, 0)),
        out_shape=jax.ShapeDtypeStruct((T, 128), F32), compiler_params=_params(),
    )(proj, b_pad)


def _fgate_bwd(proj, fcol, b_pad, dF, T):
    nblk = T // 128

    def kern(f_ref, b_ref, d_ref, o_ref, db_ref):
        r = lax.broadcasted_iota(jnp.int32, (128, 128), 0)
        c = lax.broadcasted_iota(jnp.int32, (128, 128), 1)
        tri = (c >= r).astype(F32)
        carry = jnp.zeros((1, 128), F32)
        db = jnp.zeros((1, 128), F32)
        for k in reversed(range(nblk)):
            rows = pl.ds(k * 128, 128)
            dblk = d_ref[rows, :]
            rc = jnp.dot(tri, dblk, precision=lax.Precision.HIGHEST, preferred_element_type=F32) + carry
            carry = carry + jnp.sum(dblk, axis=0, keepdims=True)
            z = f_ref[rows, :] + b_ref[...]
            dz = rc * (1.0 / (1.0 + jnp.exp(z)))
            o_ref[rows, :] = dz
            db = db + jnp.sum(dz, axis=0, keepdims=True)
        db_ref[...] = db

    return pl.pallas_call(
        kern, name="fgate_bwd", grid=(1,),
        in_specs=[pl.BlockSpec((T, 128), lambda i: (0, fcol)), pl.BlockSpec((1, 128), lambda i: (0, 0)),
                  pl.BlockSpec((T, 128), lambda i: (0, 0))],
        out_specs=[pl.BlockSpec((T, 128), lambda i: (0, 0)), pl.BlockSpec((1, 128), lambda i: (0, 0))],
        out_shape=[jax.ShapeDtypeStruct((T, 128), F32), jax.ShapeDtypeStruct((1, 128), F32)],
        compiler_params=_params(),
    )(proj, b_pad, dF)


def _gate_bias(ft, fh, h):
    lane = lax.broadcasted_iota(jnp.int32, ft.shape, 1)
    fq = jnp.sum(jnp.where(lane == h, ft, 0.0), axis=1, keepdims=True)
    sub = lax.broadcasted_iota(jnp.int32, fh.shape, 0)
    fk = jnp.sum(jnp.where(sub == h, fh, 0.0), axis=0, keepdims=True)
    return fq - fk


def _causal(i, j, blk):
    row = i * blk + lax.broadcasted_iota(jnp.int32, (blk, blk), 0)
    col = j * blk + lax.broadcasted_iota(jnp.int32, (blk, blk), 1)
    return row >= col


def _attn_fwd(qn, kn, vb, f_tm, f_hm, T, blk, comm=None):
    nb = T // blk
    scale = HEAD_DIM ** -0.5
    W = N_HEADS * HEAD_DIM

    def kern(q_ref, k_ref, v_ref, ft_ref, fh_ref, o_ref, lse_ref, m_scr, l_scr, acc_scr):
        h, i, j = pl.program_id(0), pl.program_id(1), pl.program_id(2)

        @pl.when(j == 0)
        def _():
            m_scr[...] = jnp.full_like(m_scr, NEG)
            l_scr[...] = jnp.zeros_like(l_scr)
            acc_scr[...] = jnp.zeros_like(acc_scr)

        @pl.when(j <= i)
        def _():
            s = _dot(q_ref[...], k_ref[...], "NT") * scale + _gate_bias(ft_ref[...], fh_ref[...], h)
            s = jnp.where(_causal(i, j, blk), s, NEG)
            m_prev = m_scr[...]
            m_new = jnp.maximum(m_prev, jnp.max(s, axis=1, keepdims=True))
            alpha = jnp.exp(m_prev - m_new)
            p = jnp.exp(s - m_new)
            l_scr[...] = alpha * l_scr[...] + jnp.sum(p, axis=1, keepdims=True)
            acc_scr[...] = alpha * acc_scr[...] + _dot(p, v_ref[...], "NN")
            m_scr[...] = m_new

        @pl.when(j == i)
        def _():
            l = l_scr[...]
            o_ref[...] = acc_scr[...] / l
            lse_ref[...] = jnp.broadcast_to(m_scr[...] + jnp.log(l), (blk, HEAD_DIM))

    qspec = pl.BlockSpec((blk, HEAD_DIM), lambda h, i, j: (i, h))
    kspec = pl.BlockSpec((blk, HEAD_DIM), lambda h, i, j: (jnp.minimum(j, i), h))
    return _pallas(
        kern, comm=comm, name="attn_fwd", grid=(N_HEADS, nb, nb),
        in_specs=[qspec, kspec, kspec,
                  pl.BlockSpec((blk, 128), lambda h, i, j: (i, 0)),
                  pl.BlockSpec((N_HEADS, blk), lambda h, i, j: (0, jnp.minimum(j, i)))],
        out_specs=[qspec, qspec],
        out_shape=[jax.ShapeDtypeStruct((T, W), F32)] * 2,
        scratch_shapes=[pltpu.VMEM((blk, 1), F32), pltpu.VMEM((blk, 1), F32), pltpu.VMEM((blk, HEAD_DIM), F32)],
        compiler_params=_params(),
    )(qn, kn, vb, f_tm, f_hm)


def _attn_bwd(qn, kn, vb, do, lse, delta, f_tm, f_hm, T, blk, comm=None):
    nb = T // blk
    scale = HEAD_DIM ** -0.5
    W = N_HEADS * HEAD_DIM

    def kern(q_ref, k_ref, v_ref, do_ref, lse_ref, dl_ref, ft_ref, fh_ref,
             dq_ref, dfq_ref, dk_ref, dv_ref, df_ref, dq_scr, dfq_scr, dk_scr, dv_scr, df_scr):
        h, j, i = pl.program_id(0), pl.program_id(1), pl.program_id(2)

        @pl.when((j == 0) & (i == 0))
        def _():
            dq_scr[...] = jnp.zeros_like(dq_scr)
            dfq_scr[...] = jnp.zeros_like(dfq_scr)

        @pl.when(i == 0)
        def _():
            dk_scr[...] = jnp.zeros_like(dk_scr)
            dv_scr[...] = jnp.zeros_like(dv_scr)
            df_scr[...] = jnp.zeros_like(df_scr)

        @pl.when(i >= j)
        def _():
            q, k, v = q_ref[...], k_ref[...], v_ref[...]
            dob = do_ref[...].astype(BF16)
            s = _dot(q, k, "NT") * scale + _gate_bias(ft_ref[...], fh_ref[...], h)
            p = jnp.where(_causal(i, j, blk), jnp.exp(s - lse_ref[:, 0:1]), 0.0)
            dv_scr[...] += _dot(p, dob, "TN")
            dp = _dot(dob, v, "NT")
            ds = p * (dp - dl_ref[:, 0:1])
            df_scr[...] += jnp.sum(ds, axis=0, keepdims=True)
            dsb = ds.astype(BF16)
            dk_scr[...] += _dot(dsb, q, "TN") * scale
            rows = pl.ds(pl.multiple_of(i * blk, blk), blk)
            dq_scr[rows, :] += _dot(dsb, k, "NN") * scale
            dfq_scr[rows, :] += jnp.sum(ds, axis=1, keepdims=True)

        @pl.when(i == nb - 1)
        def _():
            dk_ref[...] = dk_scr[...]
            dv_ref[...] = dv_scr[...]
            df_ref[...] = -df_scr[...]

        @pl.when((j == nb - 1) & (i == nb - 1))
        def _():
            dq_ref[...] = dq_scr[...]
            dfq_ref[...] = jnp.broadcast_to(dfq_scr[...], (T, HEAD_DIM))

    qspec = pl.BlockSpec((blk, HEAD_DIM), lambda h, j, i: (jnp.maximum(i, j), h))
    full = pl.BlockSpec((T, HEAD_DIM), lambda h, j, i: (0, h))
    kspec = pl.BlockSpec((blk, HEAD_DIM), lambda h, j, i: (j, h))
    return _pallas(
        kern, comm=comm, name="attn_bwd", grid=(N_HEADS, nb, nb),
        in_specs=[qspec, kspec, kspec, qspec, qspec, qspec,
                  pl.BlockSpec((blk, 128), lambda h, j, i: (jnp.maximum(i, j), 0)),
                  pl.BlockSpec((N_HEADS, blk), lambda h, j, i: (0, j))],
        out_specs=[full, full, kspec, kspec, pl.BlockSpec((None, 1, blk), lambda h, j, i: (h, 0, j))],
        out_shape=[jax.ShapeDtypeStruct((T, W), F32)] * 4 + [jax.ShapeDtypeStruct((N_HEADS, 1, T), F32)],
        scratch_shapes=[pltpu.VMEM((T, HEAD_DIM), F32), pltpu.VMEM((T, 1), F32), pltpu.VMEM((blk, HEAD_DIM), F32),
                        pltpu.VMEM((blk, HEAD_DIM), F32), pltpu.VMEM((1, blk), F32)],
        compiler_params=_params(),
    )(qn, kn, vb, do, lse, delta, f_tm, f_hm)


def _attn_delta(o, do, T, tb):
    W = N_HEADS * HEAD_DIM

    def body(ob, dob):
        return jnp.concatenate(
            _heads(lambda a, b: jnp.broadcast_to(jnp.sum(a * b, axis=1, keepdims=True), a.shape), ob, dob), axis=1)

    return _rowwise("attn_delta", body, T, tb, [(o, W, 0), (do, W, 0)], [], [(W, F32)], [])[0]


def _window_select(s, g, shift):
    picks = []
    for k in (1, 2, 4, 8):
        s = s + shift(s, k)
        picks.append(s)
    return jnp.where(g == 0, picks[0], jnp.where(g == 1, picks[1], jnp.where(g == 2, picks[2], picks[3])))


def _group_window(g):
    return jnp.where(g == 0, POOL_WINDOWS[0], jnp.where(g == 1, POOL_WINDOWS[1],
                     jnp.where(g == 2, POOL_WINDOWS[2], POOL_WINDOWS[3])))


def _pool_fwd(proj, ucol, pw, ps, T, tb):
    C = POOL_GROUP_DIM
    n_g = len(POOL_WINDOWS)

    def kern(uc_ref, up_ref, pw_ref, ps_ref, pooled_ref, out_ref):
        g, i = pl.program_id(0), pl.program_id(1)
        uc = uc_ref[...]
        t2 = (i - 1) * tb + lax.broadcasted_iota(jnp.int32, (2 * tb, C), 0)
        u2 = jnp.where(t2 >= 0, jnp.concatenate([up_ref[...], uc], axis=0), 0.0)
        sums = _window_select(u2, g, lambda s, k: pltpu.roll(s, k, 0))[tb:, :]
        count = jnp.minimum(t2[tb:, :] + 1, _group_window(g)).astype(F32)
        pooled = sums / count - uc
        pooled_ref[...] = pooled.astype(BF16)
        out_ref[...] = _dot(pooled, pw_ref[...], "NN") * ps_ref[...]

    ospec = pl.BlockSpec((tb, C), lambda g, i: (i, g))
    return pl.pallas_call(
        kern, name="pool_fwd", grid=(n_g, T // tb),
        in_specs=[pl.BlockSpec((tb, C), lambda g, i: (i, ucol + g)),
                  pl.BlockSpec((tb, C), lambda g, i: (jnp.maximum(i - 1, 0), ucol + g)),
                  pl.BlockSpec((None, C, C), lambda g, i: (g, 0, 0)),
                  pl.BlockSpec((1, C), lambda g, i: (0, g))],
        out_specs=[ospec, ospec],
        out_shape=[jax.ShapeDtypeStruct((T, n_g * C), BF16), jax.ShapeDtypeStruct((T, n_g * C), F32)],
        compiler_params=_params(),
    )(proj, proj, pw, ps)


def _pool_bwd(dmix_in, dcol, pooled, pw, ps, T, tb):
    C = POOL_GROUP_DIM
    n_g = len(POOL_WINDOWS)
    nb = T // tb

    def kern(dc_ref, dn_ref, pooled_ref, pw_ref, ps_ref, du_ref, dpw_ref, dps_ref):
        g, i = pl.program_id(0), pl.program_id(1)
        dc = dc_ref[...]
        scale = ps_ref[...]
        t2 = i * tb + lax.broadcasted_iota(jnp.int32, (2 * tb, C), 0)
        d2 = jnp.where(t2 < T, jnp.concatenate([dc, dn_ref[...]], axis=0) * scale, 0.0)
        dpooled2 = _dot(d2, pw_ref[...], "NT")
        count = jnp.minimum(t2 + 1, _group_window(g)).astype(F32)
        sums = _window_select(dpooled2 / count, g, lambda s, k: pltpu.roll(s, 2 * tb - k, 0))
        du_ref[...] = (sums[:tb, :] - dpooled2[:tb, :]).astype(BF16)
        pooled = pooled_ref[...]
        p = _dot(pooled, pw_ref[...], "NN")
        dps = jnp.sum(dc * p, axis=0, keepdims=True)
        dpw = _dot(pooled, d2[:tb, :], "TN")

        @pl.when(i == 0)
        def _():
            dps_ref[...] = dps
            dpw_ref[...] = dpw

        @pl.when(i > 0)
        def _():
            dps_ref[...] += dps
            dpw_ref[...] += dpw

    return pl.pallas_call(
        kern, name="pool_bwd", grid=(n_g, nb),
        in_specs=[pl.BlockSpec((tb, C), lambda g, i: (i, dcol + g)),
                  pl.BlockSpec((tb, C), lambda g, i: (jnp.minimum(i + 1, nb - 1), dcol + g)),
                  pl.BlockSpec((tb, C), lambda g, i: (i, g)),
                  pl.BlockSpec((None, C, C), lambda g, i: (g, 0, 0)),
                  pl.BlockSpec((1, C), lambda g, i: (0, g))],
        out_specs=[pl.BlockSpec((tb, C), lambda g, i: (i, g)),
                   pl.BlockSpec((None, C, C), lambda g, i: (g, 0, 0)),
                   pl.BlockSpec((1, C), lambda g, i: (0, g))],
        out_shape=[jax.ShapeDtypeStruct((T, n_g * C), BF16), jax.ShapeDtypeStruct((n_g, C, C), F32),
                   jax.ShapeDtypeStruct((1, n_g * C), F32)],
        compiler_params=_params(),
    )(dmix_in, dmix_in, pooled, pw, ps)


D_QKV = 3 * N_HEADS * HEAD_DIM
D_U = len(POOL_WINDOWS) * POOL_GROUP_DIM
F_PAD = 128
D_PROJ = D_QKV + D_U + F_PAD


def _perm_w_in(w):
    pad = jnp.zeros((w.shape[0], F_PAD - N_HEADS), w.dtype)
    return jnp.concatenate([w[:, :D_QKV], w[:, D_QKV + N_HEADS:], w[:, D_QKV:D_QKV + N_HEADS], pad], axis=1)


def _unperm_w_in(w):
    return jnp.concatenate([w[:, :D_QKV], w[:, D_QKV + D_U:D_QKV + D_U + N_HEADS], w[:, D_QKV:D_QKV + D_U]], axis=1)


def _mixer_fwd(x, norm_g, sh, sc, gate, w_in_p, b_pad, gq, gk, late_weights, ps, T, proj_comm, attn_comm):
    tb = min(256, T)
    blk = min(512, T)
    hm = _norm_mod_fwd("mix_norm_fwd", x, norm_g, sc, sh, T, tb)
    proj, got_proj = _mm("mix_proj", hm, w_in_p, "NN", F32, 512, D_PROJ // 3, 2048, comm=proj_comm)
    pw, w_out = late_weights(got_proj)
    qn, kn, vb = _qknorm_fwd(proj, gq, gk, T, tb)
    fcol = (D_QKV + D_U) // 128
    f_tm = _fgate_fwd(proj, fcol, b_pad, T)
    f_hm = f_tm[:, :N_HEADS].T
    (o, lse), got = _attn_fwd(qn, kn, vb, f_tm, f_hm, T, blk, comm=attn_comm)
    pooled, pool_o = _pool_fwd(proj, D_QKV // POOL_GROUP_DIM, pw, ps, T, tb)
    mix_in = jnp.concatenate([o.astype(BF16), pool_o.astype(BF16)], axis=1)
    mix = _mm("mix_out", mix_in, w_out, "NN", F32, 512, 2048, 2048)
    x_out = _residual("mix_res", x, mix, gate, 1.0, T, tb)
    return x_out, (x, hm, proj, qn, kn, vb, f_tm, f_hm, o, lse, pooled, mix_in, mix), pw, w_out, got


def _mixer_bwd(dx_out, saved, norm_g, sc, gate, w_in_p, b_pad, gq, gk, pw, ps, w_out, T, core, ride_sums):
    x, hm, proj, qn, kn, vb, f_tm, f_hm, o, lse, pooled, mix_in, mix = saved
    tb = min(256, T)
    blk = min(512, T)
    W = N_HEADS * HEAD_DIM
    D = x.shape[1]
    n_g = len(POOL_WINDOWS)
    dmix, dgate = _residual_bwd("mix_res_bwd", dx_out, mix, gate, 1.0, T, tb)
    dmix_in = _mm("mix_out_bwd", dmix, w_out, "NT", F32, 512, 2048, 2048)
    dw_out = _mm("mix_dw_out", mix_in, dmix, "TN", BF16, 2048, 1024, 512)
    delta = _attn_delta(o, dmix_in, T, tb)
    (dqn, dfq, dkn, dv, dfk), ride_got = _attn_bwd(qn, kn, vb, dmix_in, lse, delta, f_tm, f_hm, T, blk,
                                                   comm=_chip_comm(ride_sums))
    dq, dk, dgq, dgk = _qknorm_bwd(proj, dqn, dkn, gq, gk, T, tb)
    dF = jnp.pad(dfq[:, ::HEAD_DIM] + dfk.reshape(N_HEADS, T).T, ((0, 0), (0, F_PAD - N_HEADS)))
    fcol = (D_QKV + D_U) // 128
    dfl, dbf = _fgate_bwd(proj, fcol, b_pad, dF, T)
    du, dpw, dps = _pool_bwd(dmix_in, W // POOL_GROUP_DIM, pooled, pw, ps, T, tb)
    dproj = jnp.concatenate([dq, dk, dv.astype(BF16), du, dfl.astype(BF16)], axis=1)
    dw_in_p = _mm("mix_dw_in", hm, dproj, "TN", BF16, 2048, D_PROJ // 3, 512)
    pw_rows = POOL_GROUP_DIM // N_DEV
    slabs = [jnp.transpose(_unperm_w_in(dw_in_p).reshape(D, N_DEV, -1), (1, 0, 2)),
             jnp.transpose(dpw.astype(BF16).reshape(n_g, N_DEV, pw_rows, POOL_GROUP_DIM),
                           (1, 0, 2, 3)).reshape(N_DEV, n_g * pw_rows, POOL_GROUP_DIM),
             dw_out.reshape(N_DEV, -1, D)]
    _, sums = _reduce_level1("mix", slabs, core, [TILE_W_IN, TILE_POOL, TILE_MIX_OUT])
    dhm = _mm("mix_proj_bwd", dproj, w_in_p, "NT", F32, 512, 2048, D_PROJ // 3)
    dx, dsh, dsc, dng = _norm_mod_bwd("mix_norm_bwd", x, dhm, dx_out, norm_g, sc, T, tb)
    return dx, (dsh, dsc, dgate, dng), sums, dps, dgq, dgk, dbf, ride_got


def kernel(x, c, w_ada, b_ada, ffn1_norm_g, ffn1_w_in, ffn1_w_out, mix_norm_g, w_in, b_forget, q_norm_g, k_norm_g, pool_w, pool_scale, w_out, ffn2_norm_g, ffn2_w_in, ffn2_w_out, final_norm_g, loss_target, m_w_ada, m_b_ada, m_ffn1_norm_g, m_ffn1_w_in, m_ffn1_w_out, m_mix_norm_g, m_w_in, m_b_forget, m_q_norm_g, m_k_norm_g, m_pool_w, m_pool_scale, m_w_out, m_ffn2_norm_g, m_ffn2_w_in, m_ffn2_w_out, m_final_norm_g, v_w_ada, v_b_ada, v_ffn1_norm_g, v_ffn1_w_in, v_ffn1_w_out, v_mix_norm_g, v_w_in, v_b_forget, v_q_norm_g, v_k_norm_g, v_pool_w, v_pool_scale, v_w_out, v_ffn2_norm_g, v_ffn2_w_in, v_ffn2_w_out, v_final_norm_g):
    T, D = x.shape[1], x.shape[2]
    mx, my, mc = _mesh_pos()
    me = _flat(mx, my, mc)
    x0 = x[0]
    tgt = loss_target[0]
    tb = min(256, T)

    core = jnp.reshape(mc, (1,)).astype(jnp.int32)
    half = N_DEV // 2
    n_g = len(POOL_WINDOWS)
    pw_rows = POOL_GROUP_DIM // N_DEV

    def bf(w):
        return w.astype(BF16)

    n_loc = w_ada.shape[2]
    c_all = _standalone("gather_c", _gather_comm([c.reshape(8, D // 8)]))[0].reshape(N_DEV, D)
    b_loc = lax.dynamic_slice_in_dim(b_ada, me * n_loc, n_loc, axis=1)
    mod_loc = _ada_fwd(c_all, w_ada[0], b_loc, n_loc // 3)
    mod_all = _standalone("gather_mod", _gather_comm([mod_loc]))[0]
    mod = lax.dynamic_index_in_dim(mod_all, me, axis=1, keepdims=False).reshape(N_MOD, 1, D)
    sh1, sc1, g1, sh2, sc2, g2, sh3, sc3, g3 = [mod[k] for k in range(N_MOD)]
    b_pad = jnp.pad(b_forget, ((0, 0), (0, F_PAD - N_HEADS)))
    ps = pool_scale

    wi1 = _standalone("gather_ffn1_w_in", _gather_comm([bf(ffn1_w_in[0])]))[0]
    x1, sv1, wo1, (w_in_g,) = _ffn_fwd(
        "ffn1", x0, ffn1_norm_g, sh1, sc1, g1, wi1, lambda got: got[0].reshape(half, -1, D), T,
        up_comm=_gather_comm([bf(ffn1_w_out[0])], forward_at=0.7), down_comm=_gather_comm([bf(w_in[0])], forward_at=0.8))
    w_in_p = _perm_w_in(jnp.transpose(w_in_g, (1, 0, 2)).reshape(D, -1))

    def late_weights(got):
        pool_g, w_out_g = got
        pw = jnp.transpose(pool_g.reshape(N_DEV, n_g, pw_rows, POOL_GROUP_DIM),
                           (1, 0, 2, 3)).reshape(n_g, POOL_GROUP_DIM, POOL_GROUP_DIM)
        return pw, w_out_g.reshape(-1, D)

    x2, svm, pw_full, w_out_full, (wi2, wo2_g) = _mixer_fwd(
        x1, mix_norm_g, sh2, sc2, g2, w_in_p, b_pad, q_norm_g, k_norm_g, late_weights, ps, T,
        proj_comm=_gather_comm([bf(pool_w[0].reshape(-1, POOL_GROUP_DIM)), bf(w_out[0])], forward_at=0.6),
        attn_comm=_gather_comm([bf(ffn2_w_in[0]), bf(ffn2_w_out[0])], forward_at=0.8))
    x3, sv2, wo2, _ = _ffn_fwd("ffn2", x2, ffn2_norm_g, sh3, sc3, g3, wi2,
                               lambda got: wo2_g.reshape(half, -1, D), T)
    dx3, dgf, loss_l = _final_loss(x3, tgt, final_norm_g.reshape(1, D), T, tb)
    loss = lax.psum(loss_l[0, 0], ("x", "y", "c"))

    dx2, (dsh3, dsc3, dg3, dn3), dwi2_sum, dwo2, _ = _ffn_bwd(
        "ffn2", dx3, sv2, ffn2_norm_g, sc3, g3, wi2, wo2, T, core, defer_dwi=True)
    dx1, (dsh2, dsc2, dg2, dn2), mix_sums, dps, dgq, dgk, dbf, (dwi2,) = _mixer_bwd(
        dx2, svm, mix_norm_g, sc2, g2, w_in_p, b_pad, q_norm_g, k_norm_g, pw_full, ps, w_out_full, T, core,
        ride_sums=[dwi2_sum])
    dx0, (dsh1, dsc1, dg1, dn1), dwi1, dwo1, (dw_in_r, dpw_r, dw_out_r) = _ffn_bwd(
        "ffn1", dx1, sv1, ffn1_norm_g, sc1, g1, wi1, wo1, T, core, ride_sums=mix_sums)

    received = dict(ffn1_w_in=dwi1, ffn1_w_out=dwo1, w_in=dw_in_r, pool_w=dpw_r, w_out=dw_out_r,
                    ffn2_w_in=dwi2, ffn2_w_out=dwo2)
    moments = dict(ffn1_w_in=(m_ffn1_w_in, v_ffn1_w_in), ffn1_w_out=(m_ffn1_w_out, v_ffn1_w_out),
                   w_in=(m_w_in, v_w_in), pool_w=(m_pool_w, v_pool_w), w_out=(m_w_out, v_w_out),
                   ffn2_w_in=(m_ffn2_w_in, v_ffn2_w_in), ffn2_w_out=(m_ffn2_w_out, v_ffn2_w_out))
    weights = dict(ffn1_w_in=ffn1_w_in, ffn1_w_out=ffn1_w_out, w_in=w_in, pool_w=pool_w, w_out=w_out,
                   ffn2_w_in=ffn2_w_in, ffn2_w_out=ffn2_w_out)
    row_tiles = dict(ffn1_w_in=TILE_W_IN, ffn1_w_out=TILE_W_OUT, w_in=TILE_W_IN, pool_w=TILE_POOL,
                     w_out=TILE_MIX_OUT, ffn2_w_in=TILE_W_IN, ffn2_w_out=TILE_W_OUT)
    results = {}
    for k in received:
        shape = weights[k].shape
        two_d = received[k].shape[1:]
        mk, vk = moments[k]
        outs = _adamw("adamw_" + k, received[k], weights[k].reshape(two_d), mk.reshape(two_d),
                      vk.reshape(two_d), row_tiles[k][0])
        results[k] = [o.reshape(shape) for o in outs]

    dmod = jnp.concatenate([dsh1, dsc1, dg1, dsh2, dsc2, dg2, dsh3, dsc3, dg3], axis=1)
    small_names = ["b_ada", "ffn1_norm_g", "mix_norm_g", "ffn2_norm_g", "final_norm_g", "b_forget",
                   "q_norm_g", "k_norm_g", "pool_scale"]
    small_w = dict(b_ada=b_ada, ffn1_norm_g=ffn1_norm_g, mix_norm_g=mix_norm_g, ffn2_norm_g=ffn2_norm_g,
                   final_norm_g=final_norm_g, b_forget=b_forget, q_norm_g=q_norm_g, k_norm_g=k_norm_g,
                   pool_scale=pool_scale)
    small_m = dict(b_ada=m_b_ada, ffn1_norm_g=m_ffn1_norm_g, mix_norm_g=m_mix_norm_g, ffn2_norm_g=m_ffn2_norm_g,
                   final_norm_g=m_final_norm_g, b_forget=m_b_forget, q_norm_g=m_q_norm_g, k_norm_g=m_k_norm_g,
                   pool_scale=m_pool_scale)
    small_v = dict(b_ada=v_b_ada, ffn1_norm_g=v_ffn1_norm_g, mix_norm_g=v_mix_norm_g, ffn2_norm_g=v_ffn2_norm_g,
                   final_norm_g=v_final_norm_g, b_forget=v_b_forget, q_norm_g=v_q_norm_g, k_norm_g=v_k_norm_g,
                   pool_scale=v_pool_scale)
    small_g = dict(b_ada=dmod, ffn1_norm_g=dn1, mix_norm_g=dn2, ffn2_norm_g=dn3, final_norm_g=dgf,
                   b_forget=dbf[:, :N_HEADS], q_norm_g=dgq, k_norm_g=dgk, pool_scale=dps)
    sizes = [small_w[k].size for k in small_names]
    total = sum(sizes)
    lanes = 8 * 128
    padded = -(-total // lanes) * lanes

    def pack(d):
        flat = jnp.concatenate([d[k].reshape(-1) for k in small_names])
        return jnp.pad(flat, (0, padded - total)).reshape(8, padded // 8)

    small_parts = _standalone("gather_small_grads", _gather_comm([pack(small_g)]))[0]
    s_outs = _adamw("adamw_small", small_parts, pack(small_w), pack(small_m), pack(small_v), 8)
    offs = [0]
    for s in sizes:
        offs.append(offs[-1] + s)
    for idx, k in enumerate(small_names):
        results[k] = [o.reshape(-1)[offs[idx]:offs[idx + 1]].reshape(small_w[k].shape) for o in s_outs]

    dmod_all = small_parts.reshape(N_DEV, padded)[:, :N_MOD * D]
    dmod_loc = lax.dynamic_slice_in_dim(dmod_all, me * n_loc, n_loc, axis=1)
    g_ada = _ada_bwd(c_all, dmod_loc, n_loc // 3)
    a_outs = _adamw("adamw_w_ada", g_ada[None], w_ada[0], m_w_ada[0], v_w_ada[0], 128)
    results["w_ada"] = [o.reshape(w_ada.shape) for o in a_outs]

    order = ["w_ada", "b_ada", "ffn1_norm_g", "ffn1_w_in", "ffn1_w_out", "mix_norm_g", "w_in", "b_forget",
             "q_norm_g", "k_norm_g", "pool_w", "pool_scale", "w_out", "ffn2_norm_g", "ffn2_w_in", "ffn2_w_out",
             "final_norm_g"]
    out = [loss, dx0[None]]
    for part in range(4):
        out += [results[k][part] for k in order]
    return tuple(out)
```

```python
import jax
import jax.numpy as jnp
from jax import lax
from jax.experimental import pallas as pl
from jax.experimental.pallas import tpu as pltpu

F32 = jnp.float32
BF16 = jnp.bfloat16
MESH = pl.DeviceIdType.MESH
ANY = pl.BlockSpec(memory_space=pl.ANY)

N_DEV = 8
EPS = 1e-6
HEAD_DIM = 128
N_HEADS = 8
POOL_WINDOWS = (2, 4, 8, 16)
POOL_GROUP_DIM = 256
N_MOD = 9
ADAM_LR = 0.001
ADAM_B1 = 0.9
ADAM_B2 = 0.999
ADAM_EPS = 1e-08
ADAM_WD = 0.01
ADAM_STEP = 10
NEG = -1e30
VMEM_LIMIT_V7X = 56 * 1024 * 1024


def _params():
    return pltpu.CompilerParams(vmem_limit_bytes=VMEM_LIMIT_V7X)


def _sigmoid(z):
    return 1.0 / (1.0 + jnp.exp(-z))


def _rstd(x):
    return lax.rsqrt(jnp.mean(x * x, axis=-1, keepdims=True) + EPS)


def _mesh_pos():
    return lax.axis_index("x"), lax.axis_index("y"), lax.axis_index("c")


def _flat(px, py, pc):
    return 4 * px + 2 * py + pc


class _Comm:
    def __init__(self, ins, outs, sems, phases):
        self.ins, self.outs, self.sems, self.phases = list(ins), list(outs), list(sems), list(phases)


def _pallas(kern, *, comm=None, **kw):
    if comm is None:
        return pl.pallas_call(kern, **kw)
    grid = tuple(kw["grid"])
    single = not isinstance(kw["out_shape"], (list, tuple))
    out_shape = [kw["out_shape"]] if single else list(kw["out_shape"])
    out_specs = [kw["out_specs"]] if single else list(kw["out_specs"])
    in_specs = list(kw["in_specs"])
    scratch = list(kw.get("scratch_shapes", ()))
    n_in, n_out, n_scr = len(in_specs), len(out_shape), len(scratch)
    n_ci, n_co = len(comm.ins), len(comm.outs)
    strides, n_steps = [], 1
    for g in reversed(grid):
        strides.insert(0, n_steps)
        n_steps *= g

    def wrapped(*refs):
        ins, cins = refs[:n_in], refs[n_in:n_in + n_ci]
        base = n_in + n_ci
        outs, couts = refs[base:base + n_out], refs[base + n_out:base + n_out + n_co]
        base += n_out + n_co
        scr, sems = refs[base:base + n_scr], refs[base + n_scr:]
        step = sum(pl.program_id(d) * strides[d] for d in range(len(grid)))
        for frac, fn in comm.phases:
            if frac < 1.0:
                pl.when(step == int(round(frac * (n_steps - 1))))(lambda fn=fn: fn(cins, couts, sems))
        kern(*ins, *outs, *scr)
        for frac, fn in comm.phases:
            if frac >= 1.0:
                pl.when(step == n_steps - 1)(lambda fn=fn: fn(cins, couts, sems))

    kw = dict(kw, in_specs=in_specs + [ANY] * n_ci, out_specs=out_specs + [ANY] * n_co,
              out_shape=out_shape + comm.outs, scratch_shapes=scratch + comm.sems)
    call = pl.pallas_call(wrapped, **kw)

    def run(*args):
        res = call(*args, *comm.ins)
        main = res[0] if single else list(res[:n_out])
        return main, list(res[n_out:])

    return run


def _hosted(comm, res):
    return res if comm is not None else (res, [])


def _standalone(name, comm):
    def kern():
        pass

    return _pallas(kern, comm=comm, name=name, grid=(1,), in_specs=[], out_specs=[], out_shape=[])()[1]


def _dma_sems(*shapes):
    return [pltpu.SemaphoreType.DMA(s) for s in shapes]


def _gather_comm(arrs, forward_at=0.5):
    n = len(arrs)

    def setup(outs, sems):
        send_sems, recv_sems, _ = sems
        x, y, c = _mesh_pos()
        chips = [(1 - x, y), (x, 1 - y), (1 - x, 1 - y)]

        def copy(a, k, block, to, src=None):
            dst = outs[a].at[_flat(*block)]
            return pltpu.make_async_remote_copy(
                src_ref=dst if src is None else src, dst_ref=dst,
                send_sem=send_sems.at[a, k], recv_sem=recv_sems.at[a, k],
                device_id=to, device_id_type=MESH)

        return (x, y, c), (x, y, 1 - c), chips, copy

    def local(ins, outs, sems, a, me):
        return pltpu.make_async_copy(ins[a], outs[a].at[_flat(*me)], sems[2].at[a])

    def send_own(ins, outs, sems):
        me, sibling, chips, copy = setup(outs, sems)
        for a in range(n):
            local(ins, outs, sems, a, me).start()
            copy(a, 0, me, sibling, src=ins[a]).start()
            for j, chip in enumerate(chips):
                copy(a, 1 + j, me, (*chip, me[2]), src=ins[a]).start()

    def forward(ins, outs, sems):
        me, sibling, chips, copy = setup(outs, sems)
        for a in range(n):
            for j, chip in enumerate(chips):
                copy(a, 1 + j, (*chip, me[2]), me).wait_recv()
                copy(a, 4 + j, (*chip, me[2]), sibling).start()

    def finish(ins, outs, sems):
        me, sibling, chips, copy = setup(outs, sems)
        for a in range(n):
            copy(a, 0, sibling, me).wait_recv()
            for j, chip in enumerate(chips):
                copy(a, 4 + j, (*chip, 1 - me[2]), me).wait_recv()
        for a in range(n):
            copy(a, 0, me, sibling, src=ins[a]).wait_send()
            for j, chip in enumerate(chips):
                copy(a, 1 + j, me, (*chip, me[2]), src=ins[a]).wait_send()
                copy(a, 4 + j, (*chip, me[2]), sibling).wait_send()
            local(ins, outs, sems, a, me).wait()

    return _Comm(arrs, [jax.ShapeDtypeStruct((N_DEV,) + a.shape, a.dtype) for a in arrs],
                 _dma_sems((n, 7), (n, 7), (n,)), [(0.0, send_own), (forward_at, forward), (1.0, finish)])


CHIPS = [(0, 0), (0, 1), (1, 0), (1, 1)]


def _sibling_comm(parts):
    n = len(parts)

    def copies(ins, outs, sems):
        x, y, c = _mesh_pos()
        return [pltpu.make_async_remote_copy(
                    src_ref=ins[a].at[_flat(qx, qy, 1 - c)], dst_ref=outs[a].at[q],
                    send_sem=sems[0].at[a, q], recv_sem=sems[1].at[a, q],
                    device_id=(x, y, 1 - c), device_id_type=MESH)
                for a in range(n) for q, (qx, qy) in enumerate(CHIPS)]

    def start(ins, outs, sems):
        for cp in copies(ins, outs, sems):
            cp.start()

    def finish(ins, outs, sems):
        for cp in copies(ins, outs, sems):
            cp.wait_recv()
        for cp in copies(ins, outs, sems):
            cp.wait_send()

    return _Comm(parts, [jax.ShapeDtypeStruct((4,) + p.shape[1:], p.dtype) for p in parts],
                 _dma_sems((n, 4), (n, 4)), [(0.0, start), (1.0, finish)])


def _chip_comm(sums):
    n = len(sums)
    flips = [(1, 0), (0, 1), (1, 1)]

    def own(ins, outs, sems):
        mine = 2 * lax.axis_index("x") + lax.axis_index("y")
        return [pltpu.make_async_copy(ins[a].at[mine], outs[a].at[mine], sems[2].at[a]) for a in range(n)]

    def copies(ins, outs, sems, arriving=False):
        x, y, c = _mesh_pos()
        mine = 2 * x + y
        remote = []
        for a in range(n):
            for k, (fx, fy) in enumerate(flips):
                qx, qy = x ^ fx, y ^ fy
                q = 2 * qx + qy
                remote.append(pltpu.make_async_remote_copy(
                    src_ref=ins[a].at[q], dst_ref=outs[a].at[q if arriving else mine],
                    send_sem=sems[0].at[a, k], recv_sem=sems[1].at[a, k],
                    device_id=(qx, qy, c), device_id_type=MESH))
        return remote

    def start(ins, outs, sems):
        for cp in own(ins, outs, sems) + copies(ins, outs, sems):
            cp.start()

    def finish(ins, outs, sems):
        for cp in copies(ins, outs, sems, arriving=True):
            cp.wait_recv()
        for cp in copies(ins, outs, sems):
            cp.wait_send()
        for cp in own(ins, outs, sems):
            cp.wait()

    return _Comm(sums, [jax.ShapeDtypeStruct(s.shape, s.dtype) for s in sums],
                 _dma_sems((n, 3), (n, 3), (n,)), [(0.0, start), (1.0, finish)])


def _pair_add(name, parts, got, core, tr):
    _, R, C = parts.shape
    assert R % tr == 0

    def kern(c_ref, p_ref, g_ref, o_ref):
        o_ref[...] = (p_ref[...].astype(F32) + g_ref[...].astype(F32)).astype(o_ref.dtype)

    blk = pl.BlockSpec((None, tr, C), lambda q, i, c_ref: (q, i, 0))
    return pl.pallas_call(
        kern, name=name,
        grid_spec=pltpu.PrefetchScalarGridSpec(
            num_scalar_prefetch=1, grid=(4, R // tr),
            in_specs=[pl.BlockSpec((None, tr, C), lambda q, i, c_ref: (2 * q + c_ref[0], i, 0)), blk],
            out_specs=blk),
        out_shape=jax.ShapeDtypeStruct((4, R, C), parts.dtype), compiler_params=_params(),
    )(core, parts, got)


def _rowwise(name, body, T, tb, rows, vecs, out_rows, out_accs):
    n_in = len(rows) + len(vecs)
    n_o, n_a = len(out_rows), len(out_accs)

    def kern(*refs):
        i = pl.program_id(0)
        res = body(*[r[...] for r in refs[:n_in]])
        if not isinstance(res, (tuple, list)):
            res = (res,)
        outs = refs[n_in:]
        for k in range(n_o):
            outs[k][...] = res[k].astype(outs[k].dtype)

        def accumulate(ref, val):
            @pl.when(i == 0)
            def _():
                ref[...] = val

            @pl.when(i > 0)
            def _():
                ref[...] += val

        for k in range(n_a):
            accumulate(outs[n_o + k], res[n_o + k])

    in_specs = [pl.BlockSpec((tb, w), lambda i, cb=cb: (i, cb)) for (_, w, cb) in rows]
    in_specs += [pl.BlockSpec((1, v.shape[1]), lambda i: (0, 0)) for v in vecs]
    out_specs = [pl.BlockSpec((tb, w), lambda i: (i, 0)) for (w, _) in out_rows]
    out_specs += [pl.BlockSpec((1, w), lambda i: (0, 0)) for w in out_accs]
    out_shape = [jax.ShapeDtypeStruct((T, w), dt) for (w, dt) in out_rows]
    out_shape += [jax.ShapeDtypeStruct((1, w), F32) for w in out_accs]
    res = pl.pallas_call(
        kern, name=name, grid=(T // tb,), in_specs=in_specs, out_specs=out_specs,
        out_shape=out_shape, compiler_params=_params(),
    )(*[r[0] for r in rows], *vecs)
    return res


def _dot(a, b, mode):
    dims = {"NN": ((1,), (0,)), "NT": ((1,), (1,)), "TN": ((0,), (0,))}[mode]
    return lax.dot_general(a.astype(BF16), b.astype(BF16), (dims, ((), ())),
                           preferred_element_type=F32)


def _mm(name, a, b, mode, out_dtype, tm, tn, tk, ga=False, gb=False, gmode=None, comm=None):
    G = (a.shape[0] if ga else b.shape[0]) if gmode else 1
    a2, b2 = a.shape[-2:], b.shape[-2:]
    if mode == "NN":
        (M, K), (_, N) = a2, b2
    elif mode == "NT":
        (M, K), (N, _) = a2, b2
    else:
        (K, M), (_, N) = a2, b2
    tm, tn, tk = min(tm, M), min(tn, N), min(tk, K)
    assert M % tm == 0 and N % tn == 0 and K % tk == 0, (name, M, N, K, tm, tn, tk)
    batch = gmode == "batch"
    n_gb, n_gs = (G if batch else 1), (G if gmode == "sum" else 1)
    nk = K // tk
    n_red = n_gs * nk

    def grp(g_b, g_s):
        return g_b if batch else g_s

    if mode == "TN":
        a_blk, a_idx = (tk, tm), lambda g_b, mi, ni, g_s, ki: (ki, mi)
    else:
        a_blk, a_idx = (tm, tk), lambda g_b, mi, ni, g_s, ki: (mi, ki)
    if mode == "NT":
        b_blk, b_idx = (tn, tk), lambda g_b, mi, ni, g_s, ki: (ni, ki)
    else:
        b_blk, b_idx = (tk, tn), lambda g_b, mi, ni, g_s, ki: (ki, ni)

    def with_group(blk, idx, has_group):
        if not has_group:
            return pl.BlockSpec(blk, idx)
        return pl.BlockSpec((None,) + blk, lambda g_b, mi, ni, g_s, ki: (grp(g_b, g_s),) + idx(g_b, mi, ni, g_s, ki))

    o_blk, o_idx = (tm, tn), lambda g_b, mi, ni, g_s, ki: (mi, ni)
    o_spec = with_group(o_blk, o_idx, batch)
    o_shape = ((G,) if batch else ()) + (M, N)

    def kern(a_ref, b_ref, o_ref, *scratch):
        part = _dot(a_ref[...], b_ref[...], mode)
        if n_red == 1:
            o_ref[...] = part.astype(o_ref.dtype)
            return
        acc = scratch[0]
        step = pl.program_id(3) * nk + pl.program_id(4)

        @pl.when(step == 0)
        def _():
            acc[...] = part

        @pl.when(step > 0)
        def _():
            acc[...] += part

        @pl.when(step == n_red - 1)
        def _():
            o_ref[...] = acc[...].astype(o_ref.dtype)

    return _pallas(
        kern, comm=comm, name=name, grid=(n_gb, M // tm, N // tn, n_gs, nk),
        in_specs=[with_group(a_blk, a_idx, ga), with_group(b_blk, b_idx, gb)],
        out_specs=o_spec, out_shape=jax.ShapeDtypeStruct(o_shape, out_dtype),
        scratch_shapes=[] if n_red == 1 else [pltpu.VMEM((tm, tn), F32)],
        compiler_params=_params(),
    )(a, b)


def _adamw(name, parts, w, m, v, tr):
    G, R, C = parts.shape
    assert R % tr == 0
    bc1 = 1.0 - ADAM_B1 ** ADAM_STEP
    bc2 = 1.0 - ADAM_B2 ** ADAM_STEP

    def kern(p_ref, w_ref, m_ref, v_ref, g_out, d_out, m_out, v_out):
        g = p_ref[0].astype(F32)
        for s in range(1, G):
            g = g + p_ref[s].astype(F32)
        m2 = ADAM_B1 * m_ref[...] + (1.0 - ADAM_B1) * g
        v2 = ADAM_B2 * v_ref[...] + (1.0 - ADAM_B2) * (g * g)
        m_hat = m2 / bc1
        v_hat = v2 / bc2
        g_out[...] = g
        d_out[...] = -ADAM_LR * (m_hat / (jnp.sqrt(v_hat) + ADAM_EPS) + ADAM_WD * w_ref[...])
        m_out[...] = m2
        v_out[...] = v2

    blk = pl.BlockSpec((tr, C), lambda i: (i, 0))
    return pl.pallas_call(
        kern, name=name, grid=(R // tr,),
        in_specs=[pl.BlockSpec((G, tr, C), lambda i: (0, i, 0)), blk, blk, blk],
        out_specs=[blk] * 4, out_shape=[jax.ShapeDtypeStruct((R, C), F32)] * 4,
        compiler_params=_params(),
    )(parts, w, m, v)


def _ada_fwd(c_all, w_loc, b_loc, tn):
    B, D = c_all.shape
    N = w_loc.shape[1]

    def kern(c_ref, w_ref, b_ref, o_ref):
        cc = c_ref[...]
        act = cc * _sigmoid(cc)
        o_ref[...] = _dot(act, w_ref[...], "NN") + b_ref[...]

    return pl.pallas_call(
        kern, name="ada_fwd", grid=(N // tn,),
        in_specs=[pl.BlockSpec((B, D), lambda j: (0, 0)), pl.BlockSpec((D, tn), lambda j: (0, j)),
                  pl.BlockSpec((1, tn), lambda j: (0, j))],
        out_specs=pl.BlockSpec((B, tn), lambda j: (0, j)),
        out_shape=jax.ShapeDtypeStruct((B, N), F32), compiler_params=_params(),
    )(c_all, w_loc, b_loc)


def _ada_bwd(c_all, dmod_loc, tn):
    B, D = c_all.shape
    N = dmod_loc.shape[1]

    def kern(c_ref, d_ref, o_ref):
        cc = c_ref[...]
        act = cc * _sigmoid(cc)
        o_ref[...] = _dot(act, d_ref[...], "TN")

    return pl.pallas_call(
        kern, name="ada_bwd", grid=(N // tn,),
        in_specs=[pl.BlockSpec((B, D), lambda j: (0, 0)), pl.BlockSpec((B, tn), lambda j: (0, j))],
        out_specs=pl.BlockSpec((D, tn), lambda j: (0, j)),
        out_shape=jax.ShapeDtypeStruct((D, N), F32), compiler_params=_params(),
    )(c_all, dmod_loc)


def _norm_mod_fwd(name, x, g, sc, sh, T, tb):
    D = x.shape[1]

    def body(xb, gb, scb, shb):
        n = (xb * _rstd(xb)) * gb
        return n * (1.0 + scb) + shb

    return _rowwise(name, body, T, tb, [(x, D, 0)], [g, sc, sh], [(D, BF16)], [])[0]


def _norm_mod_bwd(name, x, dhm, dres, g, sc, T, tb):
    D = x.shape[1]

    def body(xb, db, rb, gb, scb):
        r = _rstd(xb)
        xh = xb * r
        n = xh * gb
        dn = db * (1.0 + scb)
        dxh = dn * gb
        dx = rb + r * (dxh - xh * jnp.mean(dxh * xh, axis=-1, keepdims=True))
        return (dx, jnp.sum(db, axis=0, keepdims=True), jnp.sum(db * n, axis=0, keepdims=True),
                jnp.sum(dn * xh, axis=0, keepdims=True))

    return _rowwise(name, body, T, tb, [(x, D, 0), (dhm, D, 0), (dres, D, 0)], [g, sc],
                    [(D, F32)], [D, D, D])


def _residual(name, x, f, gate, coef, T, tb):
    D = x.shape[1]

    def body(xb, fb, gb):
        return xb + (coef * gb) * fb

    return _rowwise(name, body, T, tb, [(x, D, 0), (f, D, 0)], [gate], [(D, F32)], [])[0]


def _residual_bwd(name, dx, f, gate, coef, T, tb):
    D = dx.shape[1]

    def body(db, fb, gb):
        return (coef * gb) * db, jnp.sum((coef * fb) * db, axis=0, keepdims=True)

    return _rowwise(name, body, T, tb, [(dx, D, 0), (f, D, 0)], [gate], [(D, BF16)], [D])


def _final_loss(x, tgt, g, T, tb):
    D = x.shape[1]

    def body(xb, tb_, gb):
        r = _rstd(xb)
        xh = xb * r
        err = xh * gb - tb_
        loss = 0.5 * jnp.sum(jnp.mean(err * err, axis=-1, keepdims=True), axis=0, keepdims=True)
        dy = err * (1.0 / D)
        dxh = dy * gb
        dx = r * (dxh - xh * jnp.mean(dxh * xh, axis=-1, keepdims=True))
        return dx, jnp.sum(dy * xh, axis=0, keepdims=True), jnp.broadcast_to(loss, (1, 128))

    return _rowwise("final_loss", body, T, tb, [(x, D, 0), (tgt, D, 0)], [g], [(D, F32)], [D, 128])


def _ffn_up(name, hm, wi, T, tm, comm=None):
    D = hm.shape[1]
    Ws = wi.shape[2]
    half = wi.shape[0] // 2

    def kern(h_ref, wa_ref, wb_ref, a_ref, b_ref, hid_ref):
        h = h_ref[...]
        a = _dot(h, wa_ref[...], "NN")
        b = _dot(h, wb_ref[...], "NN")
        a_ref[...] = a
        b_ref[...] = b
        hid_ref[...] = ((a * _sigmoid(a)) * b).astype(BF16)

    o_spec = pl.BlockSpec((None, tm, Ws), lambda g, i: (g, i, 0))
    return _pallas(
        kern, comm=comm, name=name, grid=(half, T // tm),
        in_specs=[pl.BlockSpec((tm, D), lambda g, i: (i, 0)),
                  pl.BlockSpec((None, D, Ws), lambda g, i: (g, 0, 0)),
                  pl.BlockSpec((None, D, Ws), lambda g, i: (g + half, 0, 0))],
        out_specs=[o_spec] * 3,
        out_shape=[jax.ShapeDtypeStruct((half, T, Ws), F32)] * 2 + [jax.ShapeDtypeStruct((half, T, Ws), BF16)],
        compiler_params=_params(),
    )(hm, wi, wi)


def _ffn_down_bwd(name, df, wo, a, b, T, tm, comm=None):
    D = df.shape[1]
    half, _, Ws = a.shape

    def kern(df_ref, wo_ref, a_ref, b_ref, dp_ref):
        dhid = _dot(df_ref[...], wo_ref[...], "NT")
        av = a_ref[...]
        s = _sigmoid(av)
        silu = av * s
        dp_ref[0] = (dhid * b_ref[...] * (s + silu * (1.0 - s))).astype(BF16)
        dp_ref[1] = (dhid * silu).astype(BF16)

    act = pl.BlockSpec((None, tm, Ws), lambda g, i: (g, i, 0))
    return _pallas(
        kern, comm=comm, name=name, grid=(half, T // tm),
        in_specs=[pl.BlockSpec((tm, D), lambda g, i: (i, 0)),
                  pl.BlockSpec((None, Ws, D), lambda g, i: (g, 0, 0)), act, act],
        out_specs=pl.BlockSpec((2, None, tm, Ws), lambda g, i: (0, g, i, 0)),
        out_shape=jax.ShapeDtypeStruct((2, half, T, Ws), BF16),
        compiler_params=_params(),
    )(df, wo, a, b)


def _ffn_fwd(tag, x, norm_g, sh, sc, gate, wi, wo_of, T, up_comm=None, down_comm=None):
    tb = min(256, T)
    hm = _norm_mod_fwd(tag + "_norm_fwd", x, norm_g, sc, sh, T, tb)
    (a, b, hid), got_up = _hosted(up_comm, _ffn_up(tag + "_up", hm, wi, T, min(512, T), comm=up_comm))
    wo = wo_of(got_up)
    f, got_down = _hosted(down_comm, _mm(tag + "_down", hid, wo, "NN", F32, 512, 2048, 2048, ga=True, gb=True,
                                         gmode="sum", comm=down_comm))
    x_out = _residual(tag + "_res", x, f, gate, 0.5, T, tb)
    return x_out, (x, hm, a, b, hid, f), wo, got_down


TILE_W_IN = (128, 512)
TILE_W_OUT = (16, 688)
TILE_MIX_OUT = (64, 256)
TILE_POOL = (128, 128)


def _reduce_level1(tag, parts, core, tiles, host=None):
    comm = _sibling_comm(parts)
    if host is None:
        res, got = None, _standalone(tag + "_sibling", comm)
    else:
        res, got = host(comm)
    sums = [_pair_add("%s_pair_add%d" % (tag, k), p, g, core, min(t[1], p.shape[1]))
            for k, (p, g, t) in enumerate(zip(parts, got, tiles))]
    return res, sums


def _ffn_bwd(tag, dx_out, saved, norm_g, sc, gate, wi, wo, T, core, ride_sums=None, defer_dwi=False):
    x, hm, a, b, hid, f = saved
    tb = min(256, T)
    D = x.shape[1]
    df, dgate = _residual_bwd(tag + "_res_bwd", dx_out, f, gate, 0.5, T, tb)
    ride = None if ride_sums is None else _chip_comm(ride_sums)
    dwo, ride_got = _hosted(ride, _mm(tag + "_dwo", hid, df, "TN", BF16, 2048, 512, T, ga=True, gmode="batch",
                                      comm=ride))
    dwo = dwo.reshape(N_DEV, -1, D)
    dproj, (dwo_sum,) = _reduce_level1(
        tag + "_dwo", [dwo], core, [TILE_W_OUT],
        host=lambda comm: _ffn_down_bwd(tag + "_down_bwd", df, wo, a, b, T, min(512, T), comm=comm))
    dproj = dproj.reshape((2 * dproj.shape[1],) + dproj.shape[2:])
    dwi, (dwo_got,) = _mm(tag + "_dwi", hm, dproj, "TN", BF16, 512, 2048, T, gb=True, gmode="batch",
                          comm=_chip_comm([dwo_sum]))

    def dhm_call(comm):
        return _mm(tag + "_dhm", dproj, wi, "NT", F32, 512, 2048, 2048, ga=True, gb=True, gmode="sum", comm=comm)

    if defer_dwi:
        dhm, (dwi_out,) = _reduce_level1(tag + "_dwi", [dwi], core, [TILE_W_IN], host=dhm_call)
    else:
        _, (dwi_sum,) = _reduce_level1(tag + "_dwi", [dwi], core, [TILE_W_IN])
        dhm, (dwi_out,) = dhm_call(_chip_comm([dwi_sum]))
    dx, dsh, dsc, dng = _norm_mod_bwd(tag + "_norm_bwd", x, dhm, dx_out, norm_g, sc, T, tb)
    return dx, (dsh, dsc, dgate, dng), dwi_out, dwo_got, ride_got


def _heads(fn, *arrs):
    outs = [fn(*[a[:, h * HEAD_DIM:(h + 1) * HEAD_DIM] for a in arrs]) for h in range(N_HEADS)]
    return outs


def _qknorm_fwd(proj, gq, gk, T, tb):
    W = N_HEADS * HEAD_DIM

    def body(q, k, v, gqb, gkb):
        qn = jnp.concatenate(_heads(lambda t: (t * _rstd(t)) * gqb, q), axis=1)
        kn = jnp.concatenate(_heads(lambda t: (t * _rstd(t)) * gkb, k), axis=1)
        return qn, kn, v

    return _rowwise("qknorm_fwd", body, T, tb, [(proj, W, 0), (proj, W, 1), (proj, W, 2)], [gq, gk],
                    [(W, BF16)] * 3, [])


def _qknorm_bwd(proj, dqn, dkn, gq, gk, T, tb):
    W = N_HEADS * HEAD_DIM

    def one(t, dt, g):
        r = _rstd(t)
        th = t * r
        dth = dt * g
        d = r * (dth - th * jnp.mean(dth * th, axis=-1, keepdims=True))
        return d, jnp.sum(dt * th, axis=0, keepdims=True)

    def body(q, k, dq, dk, gqb, gkb):
        rq = _heads(lambda t, dt: one(t, dt, gqb), q, dq)
        rk = _heads(lambda t, dt: one(t, dt, gkb), k, dk)
        return (jnp.concatenate([r[0] for r in rq], axis=1), jnp.concatenate([r[0] for r in rk], axis=1),
                sum(r[1] for r in rq), sum(r[1] for r in rk))

    return _rowwise("qknorm_bwd", body, T, tb, [(proj, W, 0), (proj, W, 1), (dqn, W, 0), (dkn, W, 0)],
                    [gq, gk], [(W, BF16)] * 2, [HEAD_DIM, HEAD_DIM])


def _log_sigmoid(z):
    return jnp.minimum(z, 0.0) - jnp.log(1.0 + jnp.exp(-jnp.abs(z)))


def _fgate_fwd(proj, fcol, b_pad, T):
    nblk = T // 128

    def kern(f_ref, b_ref, o_ref):
        r = lax.broadcasted_iota(jnp.int32, (128, 128), 0)
        c = lax.broadcasted_iota(jnp.int32, (128, 128), 1)
        tri = (r >= c).astype(F32)
        carry = jnp.zeros((1, 128), F32)
        for k in range(nblk):
            rows = pl.ds(k * 128, 128)
            lf = _log_sigmoid(f_ref[rows, :] + b_ref[...])
            o_ref[rows, :] = jnp.dot(tri, lf, precision=lax.Precision.HIGHEST, preferred_element_type=F32) + carry
            carry = carry + jnp.sum(lf, axis=0, keepdims=True)

    return pl.pallas_call(
        kern, name="fgate_fwd", grid=(1,),
        in_specs=[pl.BlockSpec((T, 128), lambda i: (0, fcol)), pl.BlockSpec((1, 128), lambda i: (0, 0))],
        out_specs=pl.BlockSpec((T, 128), lambda i: (0, 0)),
        out_shape=jax.ShapeDtypeStruct((T, 128), F32), compiler_params=_params(),
    )(proj, b_pad)


def _fgate_bwd(proj, fcol, b_pad, dF, T):
    nblk = T // 128

    def kern(f_ref, b_ref, d_ref, o_ref, db_ref):
        r = lax.broadcasted_iota(jnp.int32, (128, 128), 0)
        c = lax.broadcasted_iota(jnp.int32, (128, 128), 1)
        tri = (c >= r).astype(F32)
        carry = jnp.zeros((1, 128), F32)
        db = jnp.zeros((1, 128), F32)
        for k in reversed(range(nblk)):
            rows = pl.ds(k * 128, 128)
            dblk = d_ref[rows, :]
            rc = jnp.dot(tri, dblk, precision=lax.Precision.HIGHEST, preferred_element_type=F32) + carry
            carry = carry + jnp.sum(dblk, axis=0, keepdims=True)
            z = f_ref[rows, :] + b_ref[...]
            dz = rc * (1.0 / (1.0 + jnp.exp(z)))
            o_ref[rows, :] = dz
            db = db + jnp.sum(dz, axis=0, keepdims=True)
        db_ref[...] = db

    return pl.pallas_call(
        kern, name="fgate_bwd", grid=(1,),
        in_specs=[pl.BlockSpec((T, 128), lambda i: (0, fcol)), pl.BlockSpec((1, 128), lambda i: (0, 0)),
                  pl.BlockSpec((T, 128), lambda i: (0, 0))],
        out_specs=[pl.BlockSpec((T, 128), lambda i: (0, 0)), pl.BlockSpec((1, 128), lambda i: (0, 0))],
        out_shape=[jax.ShapeDtypeStruct((T, 128), F32), jax.ShapeDtypeStruct((1, 128), F32)],
        compiler_params=_params(),
    )(proj, b_pad, dF)


def _gate_bias(ft, fh, h):
    lane = lax.broadcasted_iota(jnp.int32, ft.shape, 1)
    fq = jnp.sum(jnp.where(lane == h, ft, 0.0), axis=1, keepdims=True)
    sub = lax.broadcasted_iota(jnp.int32, fh.shape, 0)
    fk = jnp.sum(jnp.where(sub == h, fh, 0.0), axis=0, keepdims=True)
    return fq - fk


HEADS_PER_STEP = 2


def _causal(i, j, blk):
    row = i * blk + lax.broadcasted_iota(jnp.int32, (blk, blk), 0)
    col = j * blk + lax.broadcasted_iota(jnp.int32, (blk, blk), 1)
    return row >= col


def _attn_fwd(qn, kn, vb, f_tm, f_hm, T, blk, comm=None):
    nb = T // blk
    scale = HEAD_DIM ** -0.5
    W = N_HEADS * HEAD_DIM
    G = HEADS_PER_STEP
    lanes = [slice(g * HEAD_DIM, (g + 1) * HEAD_DIM) for g in range(G)]

    def kern(q_ref, k_ref, v_ref, ft_ref, fh_ref, o_ref, lse_ref, m_scr, l_scr, acc_scr):
        hp, i, j = pl.program_id(0), pl.program_id(1), pl.program_id(2)

        @pl.when(j == 0)
        def _():
            m_scr[...] = jnp.full_like(m_scr, NEG)
            l_scr[...] = jnp.zeros_like(l_scr)
            acc_scr[...] = jnp.zeros_like(acc_scr)

        @pl.when(j <= i)
        def _():
            mask = _causal(i, j, blk)
            ft, fh = ft_ref[...], fh_ref[...]
            s = [_dot(q_ref[:, sl], k_ref[:, sl], "NT") * scale + _gate_bias(ft, fh, hp * G + g)
                 for g, sl in enumerate(lanes)]
            s = [jnp.where(mask, sg, NEG) for sg in s]
            m_prev = [m_scr[g] for g in range(G)]
            m_new = [jnp.maximum(mp, jnp.max(sg, axis=1, keepdims=True)) for mp, sg in zip(m_prev, s)]
            alpha = [jnp.exp(mp - mn) for mp, mn in zip(m_prev, m_new)]
            p = [jnp.exp(sg - mn) for sg, mn in zip(s, m_new)]
            for g, sl in enumerate(lanes):
                l_scr[g] = alpha[g] * l_scr[g] + jnp.sum(p[g], axis=1, keepdims=True)
                acc_scr[:, sl] = alpha[g] * acc_scr[:, sl] + _dot(p[g], v_ref[:, sl], "NN")
                m_scr[g] = m_new[g]

        @pl.when(j == i)
        def _():
            for g, sl in enumerate(lanes):
                l = l_scr[g]
                o_ref[:, sl] = acc_scr[:, sl] / l
                lse_ref[:, sl] = jnp.broadcast_to(m_scr[g] + jnp.log(l), (blk, HEAD_DIM))

    qspec = pl.BlockSpec((blk, G * HEAD_DIM), lambda h, i, j: (i, h))
    kspec = pl.BlockSpec((blk, G * HEAD_DIM), lambda h, i, j: (jnp.minimum(j, i), h))
    return _pallas(
        kern, comm=comm, name="attn_fwd", grid=(N_HEADS // G, nb, nb),
        in_specs=[qspec, kspec, kspec,
                  pl.BlockSpec((blk, 128), lambda h, i, j: (i, 0)),
                  pl.BlockSpec((N_HEADS, blk), lambda h, i, j: (0, jnp.minimum(j, i)))],
        out_specs=[qspec, qspec],
        out_shape=[jax.ShapeDtypeStruct((T, W), F32)] * 2,
        scratch_shapes=[pltpu.VMEM((G, blk, 1), F32), pltpu.VMEM((G, blk, 1), F32),
                        pltpu.VMEM((blk, G * HEAD_DIM), F32)],
        compiler_params=_params(),
    )(qn, kn, vb, f_tm, f_hm)


def _attn_bwd(qn, kn, vb, do, lse, delta, f_tm, f_hm, T, blk, comm=None):
    nb = T // blk
    scale = HEAD_DIM ** -0.5
    W = N_HEADS * HEAD_DIM
    G = HEADS_PER_STEP
    lanes = [slice(g * HEAD_DIM, (g + 1) * HEAD_DIM) for g in range(G)]

    def kern(q_ref, k_ref, v_ref, do_ref, lse_ref, dl_ref, ft_ref, fh_ref,
             dq_ref, dfq_ref, dk_ref, dv_ref, df_ref, dq_scr, dfq_scr, dk_scr, dv_scr, df_scr):
        hp, j, i = pl.program_id(0), pl.program_id(1), pl.program_id(2)

        @pl.when((j == 0) & (i == 0))
        def _():
            dq_scr[...] = jnp.zeros_like(dq_scr)
            dfq_scr[...] = jnp.zeros_like(dfq_scr)

        @pl.when(i == 0)
        def _():
            dk_scr[...] = jnp.zeros_like(dk_scr)
            dv_scr[...] = jnp.zeros_like(dv_scr)
            df_scr[...] = jnp.zeros_like(df_scr)

        @pl.when(i >= j)
        def _():
            mask = _causal(i, j, blk)
            ft, fh = ft_ref[...], fh_ref[...]
            rows = pl.ds(pl.multiple_of(i * blk, blk), blk)
            q = [q_ref[:, sl] for sl in lanes]
            k = [k_ref[:, sl] for sl in lanes]
            dob = [do_ref[:, sl].astype(BF16) for sl in lanes]
            s = [_dot(q[g], k[g], "NT") * scale + _gate_bias(ft, fh, hp * G + g) for g in range(G)]
            p = [jnp.where(mask, jnp.exp(s[g] - lse_ref[:, sl.start:sl.start + 1]), 0.0) for g, sl in enumerate(lanes)]
            dp = [_dot(dob[g], v_ref[:, sl], "NT") for g, sl in enumerate(lanes)]
            ds = [p[g] * (dp[g] - dl_ref[:, sl.start:sl.start + 1]) for g, sl in enumerate(lanes)]
            dsb = [d.astype(BF16) for d in ds]
            for g, sl in enumerate(lanes):
                dv_scr[:, sl] += _dot(p[g], dob[g], "TN")
                dk_scr[:, sl] += _dot(dsb[g], q[g], "TN") * scale
                dq_scr[rows, sl] += _dot(dsb[g], k[g], "NN") * scale
                df_scr[g] += jnp.sum(ds[g], axis=0, keepdims=True)
                dfq_scr[g, rows, :] += jnp.sum(ds[g], axis=1, keepdims=True)

        @pl.when(i == nb - 1)
        def _():
            dk_ref[...] = dk_scr[...]
            dv_ref[...] = dv_scr[...]
            df_ref[...] = -df_scr[...]

        @pl.when((j == nb - 1) & (i == nb - 1))
        def _():
            dq_ref[...] = dq_scr[...]
            for g, sl in enumerate(lanes):
                dfq_ref[:, sl] = jnp.broadcast_to(dfq_scr[g], (T, HEAD_DIM))

    qspec = pl.BlockSpec((blk, G * HEAD_DIM), lambda h, j, i: (jnp.maximum(i, j), h))
    full = pl.BlockSpec((T, G * HEAD_DIM), lambda h, j, i: (0, h))
    kspec = pl.BlockSpec((blk, G * HEAD_DIM), lambda h, j, i: (j, h))
    return _pallas(
        kern, comm=comm, name="attn_bwd", grid=(N_HEADS // G, nb, nb),
        in_specs=[qspec, kspec, kspec, qspec, qspec, qspec,
                  pl.BlockSpec((blk, 128), lambda h, j, i: (jnp.maximum(i, j), 0)),
                  pl.BlockSpec((N_HEADS, blk), lambda h, j, i: (0, j))],
        out_specs=[full, full, kspec, kspec, pl.BlockSpec((G, 1, blk), lambda h, j, i: (h, 0, j))],
        out_shape=[jax.ShapeDtypeStruct((T, W), F32)] * 4 + [jax.ShapeDtypeStruct((N_HEADS, 1, T), F32)],
        scratch_shapes=[pltpu.VMEM((T, G * HEAD_DIM), F32), pltpu.VMEM((G, T, 1), F32),
                        pltpu.VMEM((blk, G * HEAD_DIM), F32), pltpu.VMEM((blk, G * HEAD_DIM), F32),
                        pltpu.VMEM((G, 1, blk), F32)],
        compiler_params=_params(),
    )(qn, kn, vb, do, lse, delta, f_tm, f_hm)


def _attn_delta(o, do, T, tb):
    W = N_HEADS * HEAD_DIM

    def body(ob, dob):
        return jnp.concatenate(
            _heads(lambda a, b: jnp.broadcast_to(jnp.sum(a * b, axis=1, keepdims=True), a.shape), ob, dob), axis=1)

    return _rowwise("attn_delta", body, T, tb, [(o, W, 0), (do, W, 0)], [], [(W, F32)], [])[0]


def _window_select(s, g, shift):
    picks = []
    for k in (1, 2, 4, 8):
        s = s + shift(s, k)
        picks.append(s)
    return jnp.where(g == 0, picks[0], jnp.where(g == 1, picks[1], jnp.where(g == 2, picks[2], picks[3])))


def _group_window(g):
    return jnp.where(g == 0, POOL_WINDOWS[0], jnp.where(g == 1, POOL_WINDOWS[1],
                     jnp.where(g == 2, POOL_WINDOWS[2], POOL_WINDOWS[3])))


def _pool_fwd(proj, ucol, pw, ps, T, tb):
    C = POOL_GROUP_DIM
    n_g = len(POOL_WINDOWS)

    def kern(uc_ref, up_ref, pw_ref, ps_ref, pooled_ref, out_ref):
        g, i = pl.program_id(0), pl.program_id(1)
        uc = uc_ref[...]
        t2 = (i - 1) * tb + lax.broadcasted_iota(jnp.int32, (2 * tb, C), 0)
        u2 = jnp.where(t2 >= 0, jnp.concatenate([up_ref[...], uc], axis=0), 0.0)
        sums = _window_select(u2, g, lambda s, k: pltpu.roll(s, k, 0))[tb:, :]
        count = jnp.minimum(t2[tb:, :] + 1, _group_window(g)).astype(F32)
        pooled = sums / count - uc
        pooled_ref[...] = pooled.astype(BF16)
        out_ref[...] = _dot(pooled, pw_ref[...], "NN") * ps_ref[...]

    ospec = pl.BlockSpec((tb, C), lambda g, i: (i, g))
    return pl.pallas_call(
        kern, name="pool_fwd", grid=(n_g, T // tb),
        in_specs=[pl.BlockSpec((tb, C), lambda g, i: (i, ucol + g)),
                  pl.BlockSpec((tb, C), lambda g, i: (jnp.maximum(i - 1, 0), ucol + g)),
                  pl.BlockSpec((None, C, C), lambda g, i: (g, 0, 0)),
                  pl.BlockSpec((1, C), lambda g, i: (0, g))],
        out_specs=[ospec, ospec],
        out_shape=[jax.ShapeDtypeStruct((T, n_g * C), BF16), jax.ShapeDtypeStruct((T, n_g * C), F32)],
        compiler_params=_params(),
    )(proj, proj, pw, ps)


def _pool_bwd(dmix_in, dcol, pooled, pw, ps, T, tb):
    C = POOL_GROUP_DIM
    n_g = len(POOL_WINDOWS)
    nb = T // tb

    def kern(dc_ref, dn_ref, pooled_ref, pw_ref, ps_ref, du_ref, dpw_ref, dps_ref):
        g, i = pl.program_id(0), pl.program_id(1)
        dc = dc_ref[...]
        scale = ps_ref[...]
        t2 = i * tb + lax.broadcasted_iota(jnp.int32, (2 * tb, C), 0)
        d2 = jnp.where(t2 < T, jnp.concatenate([dc, dn_ref[...]], axis=0) * scale, 0.0)
        dpooled2 = _dot(d2, pw_ref[...], "NT")
        count = jnp.minimum(t2 + 1, _group_window(g)).astype(F32)
        sums = _window_select(dpooled2 / count, g, lambda s, k: pltpu.roll(s, 2 * tb - k, 0))
        du_ref[...] = (sums[:tb, :] - dpooled2[:tb, :]).astype(BF16)
        pooled = pooled_ref[...]
        p = _dot(pooled, pw_ref[...], "NN")
        dps = jnp.sum(dc * p, axis=0, keepdims=True)
        dpw = _dot(pooled, d2[:tb, :], "TN")

        @pl.when(i == 0)
        def _():
            dps_ref[...] = dps
            dpw_ref[...] = dpw

        @pl.when(i > 0)
        def _():
            dps_ref[...] += dps
            dpw_ref[...] += dpw

    return pl.pallas_call(
        kern, name="pool_bwd", grid=(n_g, nb),
        in_specs=[pl.BlockSpec((tb, C), lambda g, i: (i, dcol + g)),
                  pl.BlockSpec((tb, C), lambda g, i: (jnp.minimum(i + 1, nb - 1), dcol + g)),
                  pl.BlockSpec((tb, C), lambda g, i: (i, g)),
                  pl.BlockSpec((None, C, C), lambda g, i: (g, 0, 0)),
                  pl.BlockSpec((1, C), lambda g, i: (0, g))],
        out_specs=[pl.BlockSpec((tb, C), lambda g, i: (i, g)),
                   pl.BlockSpec((None, C, C), lambda g, i: (g, 0, 0)),
                   pl.BlockSpec((1, C), lambda g, i: (0, g))],
        out_shape=[jax.ShapeDtypeStruct((T, n_g * C), BF16), jax.ShapeDtypeStruct((n_g, C, C), F32),
                   jax.ShapeDtypeStruct((1, n_g * C), F32)],
        compiler_params=_params(),
    )(dmix_in, dmix_in, pooled, pw, ps)


D_QKV = 3 * N_HEADS * HEAD_DIM
D_U = len(POOL_WINDOWS) * POOL_GROUP_DIM
F_PAD = 128
D_PROJ = D_QKV + D_U + F_PAD


def _perm_w_in(w):
    pad = jnp.zeros((w.shape[0], F_PAD - N_HEADS), w.dtype)
    return jnp.concatenate([w[:, :D_QKV], w[:, D_QKV + N_HEADS:], w[:, D_QKV:D_QKV + N_HEADS], pad], axis=1)


def _unperm_w_in(w):
    return jnp.concatenate([w[:, :D_QKV], w[:, D_QKV + D_U:D_QKV + D_U + N_HEADS], w[:, D_QKV:D_QKV + D_U]], axis=1)


def _mixer_fwd(x, norm_g, sh, sc, gate, w_in_p, b_pad, gq, gk, late_weights, ps, T, proj_comm, attn_comm):
    tb = min(256, T)
    blk = min(512, T)
    hm = _norm_mod_fwd("mix_norm_fwd", x, norm_g, sc, sh, T, tb)
    proj, got_proj = _mm("mix_proj", hm, w_in_p, "NN", F32, 512, D_PROJ // 3, 2048, comm=proj_comm)
    pw, w_out = late_weights(got_proj)
    qn, kn, vb = _qknorm_fwd(proj, gq, gk, T, tb)
    fcol = (D_QKV + D_U) // 128
    f_tm = _fgate_fwd(proj, fcol, b_pad, T)
    f_hm = f_tm[:, :N_HEADS].T
    (o, lse), got = _attn_fwd(qn, kn, vb, f_tm, f_hm, T, blk, comm=attn_comm)
    pooled, pool_o = _pool_fwd(proj, D_QKV // POOL_GROUP_DIM, pw, ps, T, tb)
    mix_in = jnp.concatenate([o.astype(BF16), pool_o.astype(BF16)], axis=1)
    mix = _mm("mix_out", mix_in, w_out, "NN", F32, 512, 2048, 2048)
    x_out = _residual("mix_res", x, mix, gate, 1.0, T, tb)
    return x_out, (x, hm, proj, qn, kn, vb, f_tm, f_hm, o, lse, pooled, mix_in, mix), pw, w_out, got


def _mixer_bwd(dx_out, saved, norm_g, sc, gate, w_in_p, b_pad, gq, gk, pw, ps, w_out, T, core, ride_sums):
    x, hm, proj, qn, kn, vb, f_tm, f_hm, o, lse, pooled, mix_in, mix = saved
    tb = min(256, T)
    blk = min(512, T)
    W = N_HEADS * HEAD_DIM
    D = x.shape[1]
    n_g = len(POOL_WINDOWS)
    dmix, dgate = _residual_bwd("mix_res_bwd", dx_out, mix, gate, 1.0, T, tb)
    dmix_in = _mm("mix_out_bwd", dmix, w_out, "NT", F32, 512, 2048, 2048)
    dw_out = _mm("mix_dw_out", mix_in, dmix, "TN", BF16, 512, 1024, T)
    delta = _attn_delta(o, dmix_in, T, tb)
    (dqn, dfq, dkn, dv, dfk), ride_got = _attn_bwd(qn, kn, vb, dmix_in, lse, delta, f_tm, f_hm, T, blk,
                                                   comm=_chip_comm(ride_sums))
    dq, dk, dgq, dgk = _qknorm_bwd(proj, dqn, dkn, gq, gk, T, tb)
    dF = jnp.pad(dfq[:, ::HEAD_DIM] + dfk.reshape(N_HEADS, T).T, ((0, 0), (0, F_PAD - N_HEADS)))
    fcol = (D_QKV + D_U) // 128
    dfl, dbf = _fgate_bwd(proj, fcol, b_pad, dF, T)
    du, dpw, dps = _pool_bwd(dmix_in, W // POOL_GROUP_DIM, pooled, pw, ps, T, tb)
    dproj = jnp.concatenate([dq, dk, dv.astype(BF16), du, dfl.astype(BF16)], axis=1)
    dw_in_p = _mm("mix_dw_in", hm, dproj, "TN", BF16, 512, D_PROJ // 3, T)
    pw_rows = POOL_GROUP_DIM // N_DEV
    slabs = [jnp.transpose(_unperm_w_in(dw_in_p).reshape(D, N_DEV, -1), (1, 0, 2)),
             jnp.transpose(dpw.astype(BF16).reshape(n_g, N_DEV, pw_rows, POOL_GROUP_DIM),
                           (1, 0, 2, 3)).reshape(N_DEV, n_g * pw_rows, POOL_GROUP_DIM),
             dw_out.reshape(N_DEV, -1, D)]
    _, sums = _reduce_level1("mix", slabs, core, [TILE_W_IN, TILE_POOL, TILE_MIX_OUT])
    dhm = _mm("mix_proj_bwd", dproj, w_in_p, "NT", F32, 512, 2048, D_PROJ // 3)
    dx, dsh, dsc, dng = _norm_mod_bwd("mix_norm_bwd", x, dhm, dx_out, norm_g, sc, T, tb)
    return dx, (dsh, dsc, dgate, dng), sums, dps, dgq, dgk, dbf, ride_got


def kernel(x, c, w_ada, b_ada, ffn1_norm_g, ffn1_w_in, ffn1_w_out, mix_norm_g, w_in, b_forget, q_norm_g, k_norm_g, pool_w, pool_scale, w_out, ffn2_norm_g, ffn2_w_in, ffn2_w_out, final_norm_g, loss_target, m_w_ada, m_b_ada, m_ffn1_norm_g, m_ffn1_w_in, m_ffn1_w_out, m_mix_norm_g, m_w_in, m_b_forget, m_q_norm_g, m_k_norm_g, m_pool_w, m_pool_scale, m_w_out, m_ffn2_norm_g, m_ffn2_w_in, m_ffn2_w_out, m_final_norm_g, v_w_ada, v_b_ada, v_ffn1_norm_g, v_ffn1_w_in, v_ffn1_w_out, v_mix_norm_g, v_w_in, v_b_forget, v_q_norm_g, v_k_norm_g, v_pool_w, v_pool_scale, v_w_out, v_ffn2_norm_g, v_ffn2_w_in, v_ffn2_w_out, v_final_norm_g):
    T, D = x.shape[1], x.shape[2]
    mx, my, mc = _mesh_pos()
    me = _flat(mx, my, mc)
    x0 = x[0]
    tgt = loss_target[0]
    tb = min(256, T)

    core = jnp.reshape(mc, (1,)).astype(jnp.int32)
    half = N_DEV // 2
    n_g = len(POOL_WINDOWS)
    pw_rows = POOL_GROUP_DIM // N_DEV

    def bf(w):
        return w.astype(BF16)

    n_loc = w_ada.shape[2]
    c_all = _standalone("gather_c", _gather_comm([c.reshape(8, D // 8)]))[0].reshape(N_DEV, D)
    b_loc = lax.dynamic_slice_in_dim(b_ada, me * n_loc, n_loc, axis=1)
    mod_loc = _ada_fwd(c_all, w_ada[0], b_loc, n_loc // 3)
    mod_all = _standalone("gather_mod", _gather_comm([mod_loc]))[0]
    mod = lax.dynamic_index_in_dim(mod_all, me, axis=1, keepdims=False).reshape(N_MOD, 1, D)
    sh1, sc1, g1, sh2, sc2, g2, sh3, sc3, g3 = [mod[k] for k in range(N_MOD)]
    b_pad = jnp.pad(b_forget, ((0, 0), (0, F_PAD - N_HEADS)))
    ps = pool_scale

    wi1 = _standalone("gather_ffn1_w_in", _gather_comm([bf(ffn1_w_in[0])]))[0]
    x1, sv1, wo1, (w_in_g,) = _ffn_fwd(
        "ffn1", x0, ffn1_norm_g, sh1, sc1, g1, wi1, lambda got: got[0].reshape(half, -1, D), T,
        up_comm=_gather_comm([bf(ffn1_w_out[0])], forward_at=0.7), down_comm=_gather_comm([bf(w_in[0])], forward_at=0.8))
    w_in_p = _perm_w_in(jnp.transpose(w_in_g, (1, 0, 2)).reshape(D, -1))

    def late_weights(got):
        pool_g, w_out_g = got
        pw = jnp.transpose(pool_g.reshape(N_DEV, n_g, pw_rows, POOL_GROUP_DIM),
                           (1, 0, 2, 3)).reshape(n_g, POOL_GROUP_DIM, POOL_GROUP_DIM)
        return pw, w_out_g.reshape(-1, D)

    x2, svm, pw_full, w_out_full, (wi2, wo2_g) = _mixer_fwd(
        x1, mix_norm_g, sh2, sc2, g2, w_in_p, b_pad, q_norm_g, k_norm_g, late_weights, ps, T,
        proj_comm=_gather_comm([bf(pool_w[0].reshape(-1, POOL_GROUP_DIM)), bf(w_out[0])], forward_at=0.6),
        attn_comm=_gather_comm([bf(ffn2_w_in[0]), bf(ffn2_w_out[0])], forward_at=0.8))
    x3, sv2, wo2, _ = _ffn_fwd("ffn2", x2, ffn2_norm_g, sh3, sc3, g3, wi2,
                               lambda got: wo2_g.reshape(half, -1, D), T)
    dx3, dgf, loss_l = _final_loss(x3, tgt, final_norm_g.reshape(1, D), T, tb)
    loss = lax.psum(loss_l[0, 0], ("x", "y", "c"))

    dx2, (dsh3, dsc3, dg3, dn3), dwi2_sum, dwo2, _ = _ffn_bwd(
        "ffn2", dx3, sv2, ffn2_norm_g, sc3, g3, wi2, wo2, T, core, defer_dwi=True)
    dx1, (dsh2, dsc2, dg2, dn2), mix_sums, dps, dgq, dgk, dbf, (dwi2,) = _mixer_bwd(
        dx2, svm, mix_norm_g, sc2, g2, w_in_p, b_pad, q_norm_g, k_norm_g, pw_full, ps, w_out_full, T, core,
        ride_sums=[dwi2_sum])
    dx0, (dsh1, dsc1, dg1, dn1), dwi1, dwo1, (dw_in_r, dpw_r, dw_out_r) = _ffn_bwd(
        "ffn1", dx1, sv1, ffn1_norm_g, sc1, g1, wi1, wo1, T, core, ride_sums=mix_sums)

    received = dict(ffn1_w_in=dwi1, ffn1_w_out=dwo1, w_in=dw_in_r, pool_w=dpw_r, w_out=dw_out_r,
                    ffn2_w_in=dwi2, ffn2_w_out=dwo2)
    moments = dict(ffn1_w_in=(m_ffn1_w_in, v_ffn1_w_in), ffn1_w_out=(m_ffn1_w_out, v_ffn1_w_out),
                   w_in=(m_w_in, v_w_in), pool_w=(m_pool_w, v_pool_w), w_out=(m_w_out, v_w_out),
                   ffn2_w_in=(m_ffn2_w_in, v_ffn2_w_in), ffn2_w_out=(m_ffn2_w_out, v_ffn2_w_out))
    weights = dict(ffn1_w_in=ffn1_w_in, ffn1_w_out=ffn1_w_out, w_in=w_in, pool_w=pool_w, w_out=w_out,
                   ffn2_w_in=ffn2_w_in, ffn2_w_out=ffn2_w_out)
    row_tiles = dict(ffn1_w_in=TILE_W_IN, ffn1_w_out=TILE_W_OUT, w_in=TILE_W_IN, pool_w=TILE_POOL,
                     w_out=TILE_MIX_OUT, ffn2_w_in=TILE_W_IN, ffn2_w_out=TILE_W_OUT)
    results = {}
    for k in received:
        shape = weights[k].shape
        two_d = received[k].shape[1:]
        mk, vk = moments[k]
        outs = _adamw("adamw_" + k, received[k], weights[k].reshape(two_d), mk.reshape(two_d),
                      vk.reshape(two_d), row_tiles[k][0])
        results[k] = [o.reshape(shape) for o in outs]

    dmod = jnp.concatenate([dsh1, dsc1, dg1, dsh2, dsc2, dg2, dsh3, dsc3, dg3], axis=1)
    small_names = ["b_ada", "ffn1_norm_g", "mix_norm_g", "ffn2_norm_g", "final_norm_g", "b_forget",
                   "q_norm_g", "k_norm_g", "pool_scale"]
    small_w = dict(b_ada=b_ada, ffn1_norm_g=ffn1_norm_g, mix_norm_g=mix_norm_g, ffn2_norm_g=ffn2_norm_g,
                   final_norm_g=final_norm_g, b_forget=b_forget, q_norm_g=q_norm_g, k_norm_g=k_norm_g,
                   pool_scale=pool_scale)
    small_m = dict(b_ada=m_b_ada, ffn1_norm_g=m_ffn1_norm_g, mix_norm_g=m_mix_norm_g, ffn2_norm_g=m_ffn2_norm_g,
                   final_norm_g=m_final_norm_g, b_forget=m_b_forget, q_norm_g=m_q_norm_g, k_norm_g=m_k_norm_g,
                   pool_scale=m_pool_scale)
    small_v = dict(b_ada=v_b_ada, ffn1_norm_g=v_ffn1_norm_g, mix_norm_g=v_mix_norm_g, ffn2_norm_g=v_ffn2_norm_g,
                   final_norm_g=v_final_norm_g, b_forget=v_b_forget, q_norm_g=v_q_norm_g, k_norm_g=v_k_norm_g,
                   pool_scale=v_pool_scale)
    small_g = dict(b_ada=dmod, ffn1_norm_g=dn1, mix_norm_g=dn2, ffn2_norm_g=dn3, final_norm_g=dgf,
                   b_forget=dbf[:, :N_HEADS], q_norm_g=dgq, k_norm_g=dgk, pool_scale=dps)
    sizes = [small_w[k].size for k in small_names]
    total = sum(sizes)
    lanes = 8 * 128
    padded = -(-total // lanes) * lanes

    def pack(d):
        flat = jnp.concatenate([d[k].reshape(-1) for k in small_names])
        return jnp.pad(flat, (0, padded - total)).reshape(8, padded // 8)

    small_parts = _standalone("gather_small_grads", _gather_comm([pack(small_g)]))[0]
    s_outs = _adamw("adamw_small", small_parts, pack(small_w), pack(small_m), pack(small_v), 8)
    offs = [0]
    for s in sizes:
        offs.append(offs[-1] + s)
    for idx, k in enumerate(small_names):
        results[k] = [o.reshape(-1)[offs[idx]:offs[idx + 1]].reshape(small_w[k].shape) for o in s_outs]

    dmod_all = small_parts.reshape(N_DEV, padded)[:, :N_MOD * D]
    dmod_loc = lax.dynamic_slice_in_dim(dmod_all, me * n_loc, n_loc, axis=1)
    g_ada = _ada_bwd(c_all, dmod_loc, n_loc // 3)
    a_outs = _adamw("adamw_w_ada", g_ada[None], w_ada[0], m_w_ada[0], v_w_ada[0], 128)
    results["w_ada"] = [o.reshape(w_ada.shape) for o in a_outs]

    order = ["w_ada", "b_ada", "ffn1_norm_g", "ffn1_w_in", "ffn1_w_out", "mix_norm_g", "w_in", "b_forget",
             "q_norm_g", "k_norm_g", "pool_w", "pool_scale", "w_out", "ffn2_norm_g", "ffn2_w_in", "ffn2_w_out",
             "final_norm_g"]
    out = [loss, dx0[None]]
    for part in range(4):
        out += [results[k][part] for k in order]
    return tuple(out)
```

```python
import jax
import jax.numpy as jnp
from jax import lax
from jax.experimental import pallas as pl
from jax.experimental.pallas import tpu as pltpu

F32 = jnp.float32
BF16 = jnp.bfloat16
MESH = pl.DeviceIdType.MESH
ANY = pl.BlockSpec(memory_space=pl.ANY)

N_DEV = 8
EPS = 1e-6
HEAD_DIM = 128
N_HEADS = 8
POOL_WINDOWS = (2, 4, 8, 16)
POOL_GROUP_DIM = 256
N_MOD = 9
ADAM_LR = 0.001
ADAM_B1 = 0.9
ADAM_B2 = 0.999
ADAM_EPS = 1e-08
ADAM_WD = 0.01
ADAM_STEP = 10
NEG = -1e30
VMEM_LIMIT_V7X = 56 * 1024 * 1024


def _params():
    return pltpu.CompilerParams(vmem_limit_bytes=VMEM_LIMIT_V7X)


def _sigmoid(z):
    return 1.0 / (1.0 + jnp.exp(-z))


def _rstd(x):
    return lax.rsqrt(jnp.mean(x * x, axis=-1, keepdims=True) + EPS)


def _mesh_pos():
    return lax.axis_index("x"), lax.axis_index("y"), lax.axis_index("c")


def _flat(px, py, pc):
    return 4 * px + 2 * py + pc


class _Comm:
    def __init__(self, ins, outs, sems, phases):
        self.ins, self.outs, self.sems, self.phases = list(ins), list(outs), list(sems), list(phases)


def _pallas(kern, *, comm=None, **kw):
    if comm is None:
        return pl.pallas_call(kern, **kw)
    grid = tuple(kw["grid"])
    single = not isinstance(kw["out_shape"], (list, tuple))
    out_shape = [kw["out_shape"]] if single else list(kw["out_shape"])
    out_specs = [kw["out_specs"]] if single else list(kw["out_specs"])
    in_specs = list(kw["in_specs"])
    scratch = list(kw.get("scratch_shapes", ()))
    n_in, n_out, n_scr = len(in_specs), len(out_shape), len(scratch)
    n_ci, n_co = len(comm.ins), len(comm.outs)
    strides, n_steps = [], 1
    for g in reversed(grid):
        strides.insert(0, n_steps)
        n_steps *= g

    def wrapped(*refs):
        ins, cins = refs[:n_in], refs[n_in:n_in + n_ci]
        base = n_in + n_ci
        outs, couts = refs[base:base + n_out], refs[base + n_out:base + n_out + n_co]
        base += n_out + n_co
        scr, sems = refs[base:base + n_scr], refs[base + n_scr:]
        step = sum(pl.program_id(d) * strides[d] for d in range(len(grid)))
        for frac, fn in comm.phases:
            if frac < 1.0:
                pl.when(step == int(round(frac * (n_steps - 1))))(lambda fn=fn: fn(cins, couts, sems))
        kern(*ins, *outs, *scr)
        for frac, fn in comm.phases:
            if frac >= 1.0:
                pl.when(step == n_steps - 1)(lambda fn=fn: fn(cins, couts, sems))

    kw = dict(kw, in_specs=in_specs + [ANY] * n_ci, out_specs=out_specs + [ANY] * n_co,
              out_shape=out_shape + comm.outs, scratch_shapes=scratch + comm.sems)
    call = pl.pallas_call(wrapped, **kw)

    def run(*args):
        res = call(*args, *comm.ins)
        main = res[0] if single else list(res[:n_out])
        return main, list(res[n_out:])

    return run


def _join(first, second):
    n_i, n_o, n_s = len(first.ins), len(first.outs), len(first.sems)

    def left(fn):
        return lambda ins, outs, sems: fn(ins[:n_i], outs[:n_o], sems[:n_s])

    def right(fn):
        return lambda ins, outs, sems: fn(ins[n_i:], outs[n_o:], sems[n_s:])

    phases = [(f, left(fn)) for f, fn in first.phases] + [(f, right(fn)) for f, fn in second.phases]
    return _Comm(first.ins + second.ins, first.outs + second.outs, first.sems + second.sems, phases)


def _hosted(comm, res):
    return res if comm is not None else (res, [])


def _standalone(name, comm):
    def kern():
        pass

    return _pallas(kern, comm=comm, name=name, grid=(1,), in_specs=[], out_specs=[], out_shape=[])()[1]


def _dma_sems(*shapes):
    return [pltpu.SemaphoreType.DMA(s) for s in shapes]


def _gather_comm(arrs, forward_at=0.5):
    n = len(arrs)

    def setup(outs, sems):
        send_sems, recv_sems, _ = sems
        x, y, c = _mesh_pos()
        chips = [(1 - x, y), (x, 1 - y), (1 - x, 1 - y)]

        def copy(a, k, block, to, src=None):
            dst = outs[a].at[_flat(*block)]
            return pltpu.make_async_remote_copy(
                src_ref=dst if src is None else src, dst_ref=dst,
                send_sem=send_sems.at[a, k], recv_sem=recv_sems.at[a, k],
                device_id=to, device_id_type=MESH)

        return (x, y, c), (x, y, 1 - c), chips, copy

    def local(ins, outs, sems, a, me):
        return pltpu.make_async_copy(ins[a], outs[a].at[_flat(*me)], sems[2].at[a])

    def send_own(ins, outs, sems):
        me, sibling, chips, copy = setup(outs, sems)
        for a in range(n):
            local(ins, outs, sems, a, me).start()
            copy(a, 0, me, sibling, src=ins[a]).start()
            for j, chip in enumerate(chips):
                copy(a, 1 + j, me, (*chip, me[2]), src=ins[a]).start()

    def forward(ins, outs, sems):
        me, sibling, chips, copy = setup(outs, sems)
        for a in range(n):
            for j, chip in enumerate(chips):
                copy(a, 1 + j, (*chip, me[2]), me).wait_recv()
                copy(a, 4 + j, (*chip, me[2]), sibling).start()

    def finish(ins, outs, sems):
        me, sibling, chips, copy = setup(outs, sems)
        for a in range(n):
            copy(a, 0, sibling, me).wait_recv()
            for j, chip in enumerate(chips):
                copy(a, 4 + j, (*chip, 1 - me[2]), me).wait_recv()
        for a in range(n):
            copy(a, 0, me, sibling, src=ins[a]).wait_send()
            for j, chip in enumerate(chips):
                copy(a, 1 + j, me, (*chip, me[2]), src=ins[a]).wait_send()
                copy(a, 4 + j, (*chip, me[2]), sibling).wait_send()
            local(ins, outs, sems, a, me).wait()

    return _Comm(arrs, [jax.ShapeDtypeStruct((N_DEV,) + a.shape, a.dtype) for a in arrs],
                 _dma_sems((n, 7), (n, 7), (n,)), [(0.0, send_own), (forward_at, forward), (1.0, finish)])


CHIPS = [(0, 0), (0, 1), (1, 0), (1, 1)]


def _sibling_comm(parts):
    n = len(parts)

    def copies(ins, outs, sems):
        x, y, c = _mesh_pos()
        return [pltpu.make_async_remote_copy(
                    src_ref=ins[a].at[_flat(qx, qy, 1 - c)], dst_ref=outs[a].at[q],
                    send_sem=sems[0].at[a, q], recv_sem=sems[1].at[a, q],
                    device_id=(x, y, 1 - c), device_id_type=MESH)
                for a in range(n) for q, (qx, qy) in enumerate(CHIPS)]

    def start(ins, outs, sems):
        for cp in copies(ins, outs, sems):
            cp.start()

    def finish(ins, outs, sems):
        for cp in copies(ins, outs, sems):
            cp.wait_recv()
        for cp in copies(ins, outs, sems):
            cp.wait_send()

    return _Comm(parts, [jax.ShapeDtypeStruct((4,) + p.shape[1:], p.dtype) for p in parts],
                 _dma_sems((n, 4), (n, 4)), [(0.0, start), (1.0, finish)])


def _chip_comm(sums):
    n = len(sums)
    flips = [(1, 0), (0, 1), (1, 1)]

    def own(ins, outs, sems):
        mine = 2 * lax.axis_index("x") + lax.axis_index("y")
        return [pltpu.make_async_copy(ins[a].at[mine], outs[a].at[mine], sems[2].at[a]) for a in range(n)]

    def copies(ins, outs, sems, arriving=False):
        x, y, c = _mesh_pos()
        mine = 2 * x + y
        remote = []
        for a in range(n):
            for k, (fx, fy) in enumerate(flips):
                qx, qy = x ^ fx, y ^ fy
                q = 2 * qx + qy
                remote.append(pltpu.make_async_remote_copy(
                    src_ref=ins[a].at[q], dst_ref=outs[a].at[q if arriving else mine],
                    send_sem=sems[0].at[a, k], recv_sem=sems[1].at[a, k],
                    device_id=(qx, qy, c), device_id_type=MESH))
        return remote

    def start(ins, outs, sems):
        for cp in own(ins, outs, sems) + copies(ins, outs, sems):
            cp.start()

    def finish(ins, outs, sems):
        for cp in copies(ins, outs, sems, arriving=True):
            cp.wait_recv()
        for cp in copies(ins, outs, sems):
            cp.wait_send()
        for cp in own(ins, outs, sems):
            cp.wait()

    return _Comm(sums, [jax.ShapeDtypeStruct(s.shape, s.dtype) for s in sums],
                 _dma_sems((n, 3), (n, 3), (n,)), [(0.0, start), (1.0, finish)])


def _pair_add(name, parts, got, core, tr):
    _, R, C = parts.shape
    assert R % tr == 0

    def kern(c_ref, p_ref, g_ref, o_ref):
        o_ref[...] = (p_ref[...].astype(F32) + g_ref[...].astype(F32)).astype(o_ref.dtype)

    blk = pl.BlockSpec((None, tr, C), lambda q, i, c_ref: (q, i, 0))
    return pl.pallas_call(
        kern, name=name,
        grid_spec=pltpu.PrefetchScalarGridSpec(
            num_scalar_prefetch=1, grid=(4, R // tr),
            in_specs=[pl.BlockSpec((None, tr, C), lambda q, i, c_ref: (2 * q + c_ref[0], i, 0)), blk],
            out_specs=blk),
        out_shape=jax.ShapeDtypeStruct((4, R, C), parts.dtype), compiler_params=_params(),
    )(core, parts, got)


def _rowwise(name, body, T, tb, rows, vecs, out_rows, out_accs):
    n_in = len(rows) + len(vecs)
    n_o, n_a = len(out_rows), len(out_accs)

    def kern(*refs):
        i = pl.program_id(0)
        res = body(*[r[...] for r in refs[:n_in]])
        if not isinstance(res, (tuple, list)):
            res = (res,)
        outs = refs[n_in:]
        for k in range(n_o):
            outs[k][...] = res[k].astype(outs[k].dtype)

        def accumulate(ref, val):
            @pl.when(i == 0)
            def _():
                ref[...] = val

            @pl.when(i > 0)
            def _():
                ref[...] += val

        for k in range(n_a):
            accumulate(outs[n_o + k], res[n_o + k])

    in_specs = [pl.BlockSpec((tb, w), lambda i, cb=cb: (i, cb)) for (_, w, cb) in rows]
    in_specs += [pl.BlockSpec((1, v.shape[1]), lambda i: (0, 0)) for v in vecs]
    out_specs = [pl.BlockSpec((tb, w), lambda i: (i, 0)) for (w, _) in out_rows]
    out_specs += [pl.BlockSpec((1, w), lambda i: (0, 0)) for w in out_accs]
    out_shape = [jax.ShapeDtypeStruct((T, w), dt) for (w, dt) in out_rows]
    out_shape += [jax.ShapeDtypeStruct((1, w), F32) for w in out_accs]
    res = pl.pallas_call(
        kern, name=name, grid=(T // tb,), in_specs=in_specs, out_specs=out_specs,
        out_shape=out_shape, compiler_params=_params(),
    )(*[r[0] for r in rows], *vecs)
    return res


def _dot(a, b, mode):
    dims = {"NN": ((1,), (0,)), "NT": ((1,), (1,)), "TN": ((0,), (0,))}[mode]
    return lax.dot_general(a.astype(BF16), b.astype(BF16), (dims, ((), ())),
                           preferred_element_type=F32)


def _mm(name, a, b, mode, out_dtype, tm, tn, tk, ga=False, gb=False, gmode=None, comm=None):
    G = (a.shape[0] if ga else b.shape[0]) if gmode else 1
    a2, b2 = a.shape[-2:], b.shape[-2:]
    if mode == "NN":
        (M, K), (_, N) = a2, b2
    elif mode == "NT":
        (M, K), (N, _) = a2, b2
    else:
        (K, M), (_, N) = a2, b2
    tm, tn, tk = min(tm, M), min(tn, N), min(tk, K)
    assert M % tm == 0 and N % tn == 0 and K % tk == 0, (name, M, N, K, tm, tn, tk)
    batch = gmode == "batch"
    n_gb, n_gs = (G if batch else 1), (G if gmode == "sum" else 1)
    nk = K // tk
    n_red = n_gs * nk

    def grp(g_b, g_s):
        return g_b if batch else g_s

    if mode == "TN":
        a_blk, a_idx = (tk, tm), lambda g_b, mi, ni, g_s, ki: (ki, mi)
    else:
        a_blk, a_idx = (tm, tk), lambda g_b, mi, ni, g_s, ki: (mi, ki)
    if mode == "NT":
        b_blk, b_idx = (tn, tk), lambda g_b, mi, ni, g_s, ki: (ni, ki)
    else:
        b_blk, b_idx = (tk, tn), lambda g_b, mi, ni, g_s, ki: (ki, ni)

    def with_group(blk, idx, has_group):
        if not has_group:
            return pl.BlockSpec(blk, idx)
        return pl.BlockSpec((None,) + blk, lambda g_b, mi, ni, g_s, ki: (grp(g_b, g_s),) + idx(g_b, mi, ni, g_s, ki))

    o_blk, o_idx = (tm, tn), lambda g_b, mi, ni, g_s, ki: (mi, ni)
    o_spec = with_group(o_blk, o_idx, batch)
    o_shape = ((G,) if batch else ()) + (M, N)

    def kern(a_ref, b_ref, o_ref, *scratch):
        part = _dot(a_ref[...], b_ref[...], mode)
        if n_red == 1:
            o_ref[...] = part.astype(o_ref.dtype)
            return
        acc = scratch[0]
        step = pl.program_id(3) * nk + pl.program_id(4)

        @pl.when(step == 0)
        def _():
            acc[...] = part

        @pl.when(step > 0)
        def _():
            acc[...] += part

        @pl.when(step == n_red - 1)
        def _():
            o_ref[...] = acc[...].astype(o_ref.dtype)

    return _pallas(
        kern, comm=comm, name=name, grid=(n_gb, M // tm, N // tn, n_gs, nk),
        in_specs=[with_group(a_blk, a_idx, ga), with_group(b_blk, b_idx, gb)],
        out_specs=o_spec, out_shape=jax.ShapeDtypeStruct(o_shape, out_dtype),
        scratch_shapes=[] if n_red == 1 else [pltpu.VMEM((tm, tn), F32)],
        compiler_params=_params(),
    )(a, b)


def _mm_groups(name, a, b, mode, tm, tn, residual=None, comm=None):
    G, M, K = a.shape
    N = b.shape[2] if mode == "NN" else b.shape[1]
    tm, tn = min(tm, M), min(tn, N)
    assert M % tm == 0 and N % tn == 0

    def kern(a_ref, b_ref, *rest):
        acc = _dot(a_ref[0], b_ref[0], mode)
        for g in range(1, G):
            acc = acc + _dot(a_ref[g], b_ref[g], mode)
        if residual is None:
            rest[0][...] = acc
        else:
            x_ref, g_ref, f_ref, o_ref = rest
            f_ref[...] = acc
            o_ref[...] = x_ref[...] + (residual[2] * g_ref[...]) * acc

    b_spec = (pl.BlockSpec((G, K, tn), lambda ni, mi: (0, 0, ni)) if mode == "NN"
              else pl.BlockSpec((G, tn, K), lambda ni, mi: (0, ni, 0)))
    o_spec = pl.BlockSpec((tm, tn), lambda ni, mi: (mi, ni))
    in_specs = [pl.BlockSpec((G, tm, K), lambda ni, mi: (0, mi, 0)), b_spec]
    args = [a, b]
    out = jax.ShapeDtypeStruct((M, N), F32)
    if residual is not None:
        in_specs += [o_spec, pl.BlockSpec((1, tn), lambda ni, mi: (0, ni))]
        args += [residual[0], residual[1]]
    return _pallas(
        kern, comm=comm, name=name, grid=(N // tn, M // tm), in_specs=in_specs,
        out_specs=o_spec if residual is None else [o_spec, o_spec],
        out_shape=out if residual is None else [out, out], compiler_params=_params(),
    )(*args)


def _adamw(name, parts, w, m, v, tr):
    G, R, C = parts.shape
    assert R % tr == 0
    bc1 = 1.0 - ADAM_B1 ** ADAM_STEP
    bc2 = 1.0 - ADAM_B2 ** ADAM_STEP

    def kern(p_ref, w_ref, m_ref, v_ref, g_out, d_out, m_out, v_out):
        g = p_ref[0].astype(F32)
        for s in range(1, G):
            g = g + p_ref[s].astype(F32)
        m2 = ADAM_B1 * m_ref[...] + (1.0 - ADAM_B1) * g
        v2 = ADAM_B2 * v_ref[...] + (1.0 - ADAM_B2) * (g * g)
        m_hat = m2 / bc1
        v_hat = v2 / bc2
        g_out[...] = g
        d_out[...] = -ADAM_LR * (m_hat / (jnp.sqrt(v_hat) + ADAM_EPS) + ADAM_WD * w_ref[...])
        m_out[...] = m2
        v_out[...] = v2

    blk = pl.BlockSpec((tr, C), lambda i: (i, 0))
    return pl.pallas_call(
        kern, name=name, grid=(R // tr,),
        in_specs=[pl.BlockSpec((G, tr, C), lambda i: (0, i, 0)), blk, blk, blk],
        out_specs=[blk] * 4, out_shape=[jax.ShapeDtypeStruct((R, C), F32)] * 4,
        compiler_params=_params(),
    )(parts, w, m, v)


def _ada_fwd(c_all, w_loc, b_loc, tn):
    B, D = c_all.shape
    N = w_loc.shape[1]

    def kern(c_ref, w_ref, b_ref, o_ref):
        cc = c_ref[...]
        act = cc * _sigmoid(cc)
        o_ref[...] = _dot(act, w_ref[...], "NN") + b_ref[...]

    return pl.pallas_call(
        kern, name="ada_fwd", grid=(N // tn,),
        in_specs=[pl.BlockSpec((B, D), lambda j: (0, 0)), pl.BlockSpec((D, tn), lambda j: (0, j)),
                  pl.BlockSpec((1, tn), lambda j: (0, j))],
        out_specs=pl.BlockSpec((B, tn), lambda j: (0, j)),
        out_shape=jax.ShapeDtypeStruct((B, N), F32), compiler_params=_params(),
    )(c_all, w_loc, b_loc)


def _ada_bwd(c_all, dmod_loc, tn):
    B, D = c_all.shape
    N = dmod_loc.shape[1]

    def kern(c_ref, d_ref, o_ref):
        cc = c_ref[...]
        act = cc * _sigmoid(cc)
        o_ref[...] = _dot(act, d_ref[...], "TN")

    return pl.pallas_call(
        kern, name="ada_bwd", grid=(N // tn,),
        in_specs=[pl.BlockSpec((B, D), lambda j: (0, 0)), pl.BlockSpec((B, tn), lambda j: (0, j))],
        out_specs=pl.BlockSpec((D, tn), lambda j: (0, j)),
        out_shape=jax.ShapeDtypeStruct((D, N), F32), compiler_params=_params(),
    )(c_all, dmod_loc)


def _norm_mod_fwd(name, x, g, sc, sh, T, tb):
    D = x.shape[1]

    def body(xb, gb, scb, shb):
        n = (xb * _rstd(xb)) * gb
        return n * (1.0 + scb) + shb

    return _rowwise(name, body, T, tb, [(x, D, 0)], [g, sc, sh], [(D, BF16)], [])[0]


def _norm_mod_bwd(name, x, dhm, dres, g, sc, T, tb):
    D = x.shape[1]

    def body(xb, db, rb, gb, scb):
        r = _rstd(xb)
        xh = xb * r
        n = xh * gb
        dn = db * (1.0 + scb)
        dxh = dn * gb
        dx = rb + r * (dxh - xh * jnp.mean(dxh * xh, axis=-1, keepdims=True))
        return (dx, jnp.sum(db, axis=0, keepdims=True), jnp.sum(db * n, axis=0, keepdims=True),
                jnp.sum(dn * xh, axis=0, keepdims=True))

    return _rowwise(name, body, T, tb, [(x, D, 0), (dhm, D, 0), (dres, D, 0)], [g, sc],
                    [(D, F32)], [D, D, D])


def _residual(name, x, f, gate, coef, T, tb):
    D = x.shape[1]

    def body(xb, fb, gb):
        return xb + (coef * gb) * fb

    return _rowwise(name, body, T, tb, [(x, D, 0), (f, D, 0)], [gate], [(D, F32)], [])[0]


def _residual_bwd(name, dx, f, gate, coef, T, tb):
    D = dx.shape[1]

    def body(db, fb, gb):
        return (coef * gb) * db, jnp.sum((coef * fb) * db, axis=0, keepdims=True)

    return _rowwise(name, body, T, tb, [(dx, D, 0), (f, D, 0)], [gate], [(D, BF16)], [D])


def _final_loss(x, tgt, g, T, tb):
    D = x.shape[1]

    def body(xb, tb_, gb):
        r = _rstd(xb)
        xh = xb * r
        err = xh * gb - tb_
        loss = 0.5 * jnp.sum(jnp.mean(err * err, axis=-1, keepdims=True), axis=0, keepdims=True)
        dy = err * (1.0 / D)
        dxh = dy * gb
        dx = r * (dxh - xh * jnp.mean(dxh * xh, axis=-1, keepdims=True))
        return dx, jnp.sum(dy * xh, axis=0, keepdims=True), jnp.broadcast_to(loss, (1, 128))

    return _rowwise("final_loss", body, T, tb, [(x, D, 0), (tgt, D, 0)], [g], [(D, F32)], [D, 128])


def _ffn_up(name, hm, wi, T, tm, comm=None):
    D = hm.shape[1]
    Ws = wi.shape[2]
    half = wi.shape[0] // 2

    def kern(h_ref, wa_ref, wb_ref, a_ref, b_ref, hid_ref):
        h = h_ref[...]
        a = _dot(h, wa_ref[...], "NN")
        b = _dot(h, wb_ref[...], "NN")
        a_ref[...] = a
        b_ref[...] = b
        hid_ref[...] = ((a * _sigmoid(a)) * b).astype(BF16)

    o_spec = pl.BlockSpec((None, tm, Ws), lambda g, i: (g, i, 0))
    return _pallas(
        kern, comm=comm, name=name, grid=(half, T // tm),
        in_specs=[pl.BlockSpec((tm, D), lambda g, i: (i, 0)),
                  pl.BlockSpec((None, D, Ws), lambda g, i: (g, 0, 0)),
                  pl.BlockSpec((None, D, Ws), lambda g, i: (g + half, 0, 0))],
        out_specs=[o_spec] * 3,
        out_shape=[jax.ShapeDtypeStruct((half, T, Ws), F32)] * 2 + [jax.ShapeDtypeStruct((half, T, Ws), BF16)],
        compiler_params=_params(),
    )(hm, wi, wi)


def _ffn_down_bwd(name, df, wo, a, b, T, tm, comm=None):
    D = df.shape[1]
    half, _, Ws = a.shape

    n_sub = 2 if tm % 32 == 0 else 1
    subs = [pl.ds(r * (tm // n_sub), tm // n_sub) for r in range(n_sub)]

    def kern(df_ref, wo_ref, a_ref, b_ref, dp_ref):
        wo_blk = wo_ref[...]
        dhid = [_dot(df_ref[rows, :], wo_blk, "NT") for rows in subs]
        for rows, dh in zip(subs, dhid):
            av = a_ref[rows, :]
            s = _sigmoid(av)
            silu = av * s
            dp_ref[0, rows, :] = (dh * b_ref[rows, :] * (s + silu * (1.0 - s))).astype(BF16)
            dp_ref[1, rows, :] = (dh * silu).astype(BF16)

    act = pl.BlockSpec((None, tm, Ws), lambda g, i: (g, i, 0))
    return _pallas(
        kern, comm=comm, name=name, grid=(half, T // tm),
        in_specs=[pl.BlockSpec((tm, D), lambda g, i: (i, 0)),
                  pl.BlockSpec((None, Ws, D), lambda g, i: (g, 0, 0)), act, act],
        out_specs=pl.BlockSpec((2, None, tm, Ws), lambda g, i: (0, g, i, 0)),
        out_shape=jax.ShapeDtypeStruct((2, half, T, Ws), BF16),
        compiler_params=_params(),
    )(df, wo, a, b)


def _ffn_fwd(tag, x, norm_g, sh, sc, gate, wi, wo_of, T, up_comm=None, down_comm=None):
    tb = min(256, T)
    hm = _norm_mod_fwd(tag + "_norm_fwd", x, norm_g, sc, sh, T, tb)
    (a, b, hid), got_up = _hosted(up_comm, _ffn_up(tag + "_up", hm, wi, T, min(512, T), comm=up_comm))
    wo = wo_of(got_up)
    (f, x_out), got_down = _hosted(down_comm, _mm_groups(tag + "_down", hid, wo, "NN", 512, 512,
                                                         residual=(x, gate, 0.5), comm=down_comm))
    return x_out, (x, hm, a, b, hid, f), wo, got_down


TILE_W_IN = (128, 512)
TILE_W_OUT = (16, 688)
TILE_MIX_OUT = (64, 256)
TILE_POOL = (128, 128)


def _reduce_level1(tag, parts, core, tiles, host=None):
    comm = _sibling_comm(parts)
    if host is None:
        res, got = None, _standalone(tag + "_sibling", comm)
    else:
        res, got = host(comm)
    sums = [_pair_add("%s_pair_add%d" % (tag, k), p, g, core, min(t[1], p.shape[1]))
            for k, (p, g, t) in enumerate(zip(parts, got, tiles))]
    return res, sums


def _ffn_bwd(tag, dx_out, saved, norm_g, sc, gate, wi, wo, T, core, ride_sums=None, defer_dwi=False):
    x, hm, a, b, hid, f = saved
    tb = min(256, T)
    D = x.shape[1]
    df, dgate = _residual_bwd(tag + "_res_bwd", dx_out, f, gate, 0.5, T, tb)
    dwo = _mm(tag + "_dwo", hid, df, "TN", BF16, 2048, 512, T, ga=True, gmode="batch").reshape(N_DEV, -1, D)
    n_ride = 0 if ride_sums is None else len(ride_sums)

    def down_bwd_call(comm):
        if n_ride:
            comm = _join(comm, _chip_comm(ride_sums))
        res, got = _ffn_down_bwd(tag + "_down_bwd", df, wo, a, b, T, min(512, T), comm=comm)
        return (res, got[len(got) - n_ride:]), got[:len(got) - n_ride]

    (dproj, ride_got), (dwo_sum,) = _reduce_level1(tag + "_dwo", [dwo], core, [TILE_W_OUT], host=down_bwd_call)
    dproj = dproj.reshape((2 * dproj.shape[1],) + dproj.shape[2:])
    dwi, (dwo_got,) = _mm(tag + "_dwi", hm, dproj, "TN", BF16, 512, 2048, T, gb=True, gmode="batch",
                          comm=_chip_comm([dwo_sum]))

    def dhm_call(comm):
        return _mm_groups(tag + "_dhm", dproj, wi, "NT", 512, 512, comm=comm)

    if defer_dwi:
        dhm, (dwi_out,) = _reduce_level1(tag + "_dwi", [dwi], core, [TILE_W_IN], host=dhm_call)
    else:
        _, (dwi_sum,) = _reduce_level1(tag + "_dwi", [dwi], core, [TILE_W_IN])
        dhm, (dwi_out,) = dhm_call(_chip_comm([dwi_sum]))
    dx, dsh, dsc, dng = _norm_mod_bwd(tag + "_norm_bwd", x, dhm, dx_out, norm_g, sc, T, tb)
    return dx, (dsh, dsc, dgate, dng), dwi_out, dwo_got, ride_got


def _heads(fn, *arrs):
    outs = [fn(*[a[:, h * HEAD_DIM:(h + 1) * HEAD_DIM] for a in arrs]) for h in range(N_HEADS)]
    return outs


def _qknorm_fwd(proj, gq, gk, T, tb):
    W = N_HEADS * HEAD_DIM

    def body(q, k, v, gqb, gkb):
        qn = jnp.concatenate(_heads(lambda t: (t * _rstd(t)) * gqb, q), axis=1)
        kn = jnp.concatenate(_heads(lambda t: (t * _rstd(t)) * gkb, k), axis=1)
        return qn, kn, v

    return _rowwise("qknorm_fwd", body, T, tb, [(proj, W, 0), (proj, W, 1), (proj, W, 2)], [gq, gk],
                    [(W, BF16)] * 3, [])


def _qknorm_bwd(proj, dqn, dkn, gq, gk, T, tb):
    W = N_HEADS * HEAD_DIM

    def one(t, dt, g):
        r = _rstd(t)
        th = t * r
        dth = dt * g
        d = r * (dth - th * jnp.mean(dth * th, axis=-1, keepdims=True))
        return d, jnp.sum(dt * th, axis=0, keepdims=True)

    def body(q, k, dq, dk, gqb, gkb):
        rq = _heads(lambda t, dt: one(t, dt, gqb), q, dq)
        rk = _heads(lambda t, dt: one(t, dt, gkb), k, dk)
        return (jnp.concatenate([r[0] for r in rq], axis=1), jnp.concatenate([r[0] for r in rk], axis=1),
                sum(r[1] for r in rq), sum(r[1] for r in rk))

    return _rowwise("qknorm_bwd", body, T, tb, [(proj, W, 0), (proj, W, 1), (dqn, W, 0), (dkn, W, 0)],
                    [gq, gk], [(W, BF16)] * 2, [HEAD_DIM, HEAD_DIM])


def _log_sigmoid(z):
    return jnp.minimum(z, 0.0) - jnp.log(1.0 + jnp.exp(-jnp.abs(z)))


def _fgate_fwd(proj, fcol, b_pad, T):
    nblk = T // 128

    def kern(f_ref, b_ref, o_ref):
        r = lax.broadcasted_iota(jnp.int32, (128, 128), 0)
        c = lax.broadcasted_iota(jnp.int32, (128, 128), 1)
        tri = (r >= c).astype(F32)
        carry = jnp.zeros((1, 128), F32)
        for k in range(nblk):
            rows = pl.ds(k * 128, 128)
            lf = _log_sigmoid(f_ref[rows, :] + b_ref[...])
            o_ref[rows, :] = jnp.dot(tri, lf, precision=lax.Precision.HIGHEST, preferred_element_type=F32) + carry
            carry = carry + jnp.sum(lf, axis=0, keepdims=True)

    return pl.pallas_call(
        kern, name="fgate_fwd", grid=(1,),
        in_specs=[pl.BlockSpec((T, 128), lambda i: (0, fcol)), pl.BlockSpec((1, 128), lambda i: (0, 0))],
        out_specs=pl.BlockSpec((T, 128), lambda i: (0, 0)),
        out_shape=jax.ShapeDtypeStruct((T, 128), F32), compiler_params=_params(),
    )(proj, b_pad)


def _fgate_bwd(proj, fcol, b_pad, dF, T):
    nblk = T // 128

    def kern(f_ref, b_ref, d_ref, o_ref, db_ref):
        r = lax.broadcasted_iota(jnp.int32, (128, 128), 0)
        c = lax.broadcasted_iota(jnp.int32, (128, 128), 1)
        tri = (c >= r).astype(F32)
        carry = jnp.zeros((1, 128), F32)
        db = jnp.zeros((1, 128), F32)
        for k in reversed(range(nblk)):
            rows = pl.ds(k * 128, 128)
            dblk = d_ref[rows, :]
            rc = jnp.dot(tri, dblk, precision=lax.Precision.HIGHEST, preferred_element_type=F32) + carry
            carry = carry + jnp.sum(dblk, axis=0, keepdims=True)
            z = f_ref[rows, :] + b_ref[...]
            dz = rc * (1.0 / (1.0 + jnp.exp(z)))
            o_ref[rows, :] = dz
            db = db + jnp.sum(dz, axis=0, keepdims=True)
        db_ref[...] = db

    return pl.pallas_call(
        kern, name="fgate_bwd", grid=(1,),
        in_specs=[pl.BlockSpec((T, 128), lambda i: (0, fcol)), pl.BlockSpec((1, 128), lambda i: (0, 0)),
                  pl.BlockSpec((T, 128), lambda i: (0, 0))],
        out_specs=[pl.BlockSpec((T, 128), lambda i: (0, 0)), pl.BlockSpec((1, 128), lambda i: (0, 0))],
        out_shape=[jax.ShapeDtypeStruct((T, 128), F32), jax.ShapeDtypeStruct((1, 128), F32)],
        compiler_params=_params(),
    )(proj, b_pad, dF)


def _gate_bias(ft, fh, h):
    lane = lax.broadcasted_iota(jnp.int32, ft.shape, 1)
    fq = jnp.sum(jnp.where(lane == h, ft, 0.0), axis=1, keepdims=True)
    sub = lax.broadcasted_iota(jnp.int32, fh.shape, 0)
    fk = jnp.sum(jnp.where(sub == h, fh, 0.0), axis=0, keepdims=True)
    return fq - fk


HEADS_PER_STEP = 2


def _causal(i, j, blk):
    row = i * blk + lax.broadcasted_iota(jnp.int32, (blk, blk), 0)
    col = j * blk + lax.broadcasted_iota(jnp.int32, (blk, blk), 1)
    return row >= col


def _attn_fwd(qn, kn, vb, f_tm, f_hm, T, blk, comm=None):
    nb = T // blk
    scale = HEAD_DIM ** -0.5
    W = N_HEADS * HEAD_DIM
    G = HEADS_PER_STEP
    lanes = [slice(g * HEAD_DIM, (g + 1) * HEAD_DIM) for g in range(G)]

    def kern(q_ref, k_ref, v_ref, ft_ref, fh_ref, o_ref, lse_ref, m_scr, l_scr, acc_scr):
        hp, i, j = pl.program_id(0), pl.program_id(1), pl.program_id(2)

        @pl.when(j == 0)
        def _():
            m_scr[...] = jnp.full_like(m_scr, NEG)
            l_scr[...] = jnp.zeros_like(l_scr)
            acc_scr[...] = jnp.zeros_like(acc_scr)

        @pl.when(j <= i)
        def _():
            mask = _causal(i, j, blk)
            ft, fh = ft_ref[...], fh_ref[...]
            s = [_dot(q_ref[:, sl], k_ref[:, sl], "NT") * scale + _gate_bias(ft, fh, hp * G + g)
                 for g, sl in enumerate(lanes)]
            s = [jnp.where(mask, sg, NEG) for sg in s]
            m_prev = [m_scr[g] for g in range(G)]
            m_new = [jnp.maximum(mp, jnp.max(sg, axis=1, keepdims=True)) for mp, sg in zip(m_prev, s)]
            alpha = [jnp.exp(mp - mn) for mp, mn in zip(m_prev, m_new)]
            p = [jnp.exp(sg - mn) for sg, mn in zip(s, m_new)]
            for g, sl in enumerate(lanes):
                l_scr[g] = alpha[g] * l_scr[g] + jnp.sum(p[g], axis=1, keepdims=True)
                acc_scr[:, sl] = alpha[g] * acc_scr[:, sl] + _dot(p[g], v_ref[:, sl], "NN")
                m_scr[g] = m_new[g]

        @pl.when(j == i)
        def _():
            for g, sl in enumerate(lanes):
                l = l_scr[g]
                o_ref[:, sl] = acc_scr[:, sl] / l
                lse_ref[:, sl] = jnp.broadcast_to(m_scr[g] + jnp.log(l), (blk, HEAD_DIM))

    qspec = pl.BlockSpec((blk, G * HEAD_DIM), lambda h, i, j: (i, h))
    kspec = pl.BlockSpec((blk, G * HEAD_DIM), lambda h, i, j: (jnp.minimum(j, i), h))
    return _pallas(
        kern, comm=comm, name="attn_fwd", grid=(N_HEADS // G, nb, nb),
        in_specs=[qspec, kspec, kspec,
                  pl.BlockSpec((blk, 128), lambda h, i, j: (i, 0)),
                  pl.BlockSpec((N_HEADS, blk), lambda h, i, j: (0, jnp.minimum(j, i)))],
        out_specs=[qspec, qspec],
        out_shape=[jax.ShapeDtypeStruct((T, W), F32)] * 2,
        scratch_shapes=[pltpu.VMEM((G, blk, 1), F32), pltpu.VMEM((G, blk, 1), F32),
                        pltpu.VMEM((blk, G * HEAD_DIM), F32)],
        compiler_params=_params(),
    )(qn, kn, vb, f_tm, f_hm)


def _attn_bwd(qn, kn, vb, do, lse, delta, f_tm, f_hm, T, blk, comm=None):
    nb = T // blk
    scale = HEAD_DIM ** -0.5
    W = N_HEADS * HEAD_DIM
    G = HEADS_PER_STEP
    lanes = [slice(g * HEAD_DIM, (g + 1) * HEAD_DIM) for g in range(G)]

    def kern(q_ref, k_ref, v_ref, do_ref, lse_ref, dl_ref, ft_ref, fh_ref,
             dq_ref, dfq_ref, dk_ref, dv_ref, df_ref, dq_scr, dfq_scr, dk_scr, dv_scr, df_scr):
        hp, j, i = pl.program_id(0), pl.program_id(1), pl.program_id(2)

        @pl.when((j == 0) & (i == 0))
        def _():
            dq_scr[...] = jnp.zeros_like(dq_scr)
            dfq_scr[...] = jnp.zeros_like(dfq_scr)

        @pl.when(i == 0)
        def _():
            dk_scr[...] = jnp.zeros_like(dk_scr)
            dv_scr[...] = jnp.zeros_like(dv_scr)
            df_scr[...] = jnp.zeros_like(df_scr)

        @pl.when(i >= j)
        def _():
            mask = _causal(i, j, blk)
            ft, fh = ft_ref[...], fh_ref[...]
            rows = pl.ds(pl.multiple_of(i * blk, blk), blk)
            q = [q_ref[:, sl] for sl in lanes]
            k = [k_ref[:, sl] for sl in lanes]
            dob = [do_ref[:, sl].astype(BF16) for sl in lanes]
            s = [_dot(q[g], k[g], "NT") * scale + _gate_bias(ft, fh, hp * G + g) for g in range(G)]
            p = [jnp.where(mask, jnp.exp(s[g] - lse_ref[:, sl.start:sl.start + 1]), 0.0) for g, sl in enumerate(lanes)]
            dp = [_dot(dob[g], v_ref[:, sl], "NT") for g, sl in enumerate(lanes)]
            ds = [p[g] * (dp[g] - dl_ref[:, sl.start:sl.start + 1]) for g, sl in enumerate(lanes)]
            dsb = [d.astype(BF16) for d in ds]
            for g, sl in enumerate(lanes):
                dv_scr[:, sl] += _dot(p[g], dob[g], "TN")
                dk_scr[:, sl] += _dot(dsb[g], q[g], "TN") * scale
                dq_scr[rows, sl] += _dot(dsb[g], k[g], "NN") * scale
                df_scr[g] += jnp.sum(ds[g], axis=0, keepdims=True)
                dfq_scr[g, rows, :] += jnp.sum(ds[g], axis=1, keepdims=True)

        @pl.when(i == nb - 1)
        def _():
            dk_ref[...] = dk_scr[...]
            dv_ref[...] = dv_scr[...]
            df_ref[...] = -df_scr[...]

        @pl.when((j == nb - 1) & (i == nb - 1))
        def _():
            dq_ref[...] = dq_scr[...]
            for g, sl in enumerate(lanes):
                dfq_ref[:, sl] = jnp.broadcast_to(dfq_scr[g], (T, HEAD_DIM))

    qspec = pl.BlockSpec((blk, G * HEAD_DIM), lambda h, j, i: (jnp.maximum(i, j), h))
    full = pl.BlockSpec((T, G * HEAD_DIM), lambda h, j, i: (0, h))
    kspec = pl.BlockSpec((blk, G * HEAD_DIM), lambda h, j, i: (j, h))
    return _pallas(
        kern, comm=comm, name="attn_bwd", grid=(N_HEADS // G, nb, nb),
        in_specs=[qspec, kspec, kspec, qspec, qspec, qspec,
                  pl.BlockSpec((blk, 128), lambda h, j, i: (jnp.maximum(i, j), 0)),
                  pl.BlockSpec((N_HEADS, blk), lambda h, j, i: (0, j))],
        out_specs=[full, full, kspec, kspec, pl.BlockSpec((G, 1, blk), lambda h, j, i: (h, 0, j))],
        out_shape=[jax.ShapeDtypeStruct((T, W), F32)] * 4 + [jax.ShapeDtypeStruct((N_HEADS, 1, T), F32)],
        scratch_shapes=[pltpu.VMEM((T, G * HEAD_DIM), F32), pltpu.VMEM((G, T, 1), F32),
                        pltpu.VMEM((blk, G * HEAD_DIM), F32), pltpu.VMEM((blk, G * HEAD_DIM), F32),
                        pltpu.VMEM((G, 1, blk), F32)],
        compiler_params=_params(),
    )(qn, kn, vb, do, lse, delta, f_tm, f_hm)


def _attn_delta(o, do, T, tb):
    W = N_HEADS * HEAD_DIM

    def body(ob, dob):
        return jnp.concatenate(
            _heads(lambda a, b: jnp.broadcast_to(jnp.sum(a * b, axis=1, keepdims=True), a.shape), ob, dob), axis=1)

    return _rowwise("attn_delta", body, T, tb, [(o, W, 0), (do, W, 0)], [], [(W, F32)], [])[0]


def _window_select(s, g, shift):
    picks = []
    for k in (1, 2, 4, 8):
        s = s + shift(s, k)
        picks.append(s)
    return jnp.where(g == 0, picks[0], jnp.where(g == 1, picks[1], jnp.where(g == 2, picks[2], picks[3])))


def _group_window(g):
    return jnp.where(g == 0, POOL_WINDOWS[0], jnp.where(g == 1, POOL_WINDOWS[1],
                     jnp.where(g == 2, POOL_WINDOWS[2], POOL_WINDOWS[3])))


def _pool_fwd(proj, ucol, pw, ps, T, tb):
    C = POOL_GROUP_DIM
    n_g = len(POOL_WINDOWS)

    def kern(uc_ref, up_ref, pw_ref, ps_ref, pooled_ref, out_ref):
        g, i = pl.program_id(0), pl.program_id(1)
        uc = uc_ref[...]
        t2 = (i - 1) * tb + lax.broadcasted_iota(jnp.int32, (2 * tb, C), 0)
        u2 = jnp.where(t2 >= 0, jnp.concatenate([up_ref[...], uc], axis=0), 0.0)
        sums = _window_select(u2, g, lambda s, k: pltpu.roll(s, k, 0))[tb:, :]
        count = jnp.minimum(t2[tb:, :] + 1, _group_window(g)).astype(F32)
        pooled = sums / count - uc
        pooled_ref[...] = pooled.astype(BF16)
        out_ref[...] = _dot(pooled, pw_ref[...], "NN") * ps_ref[...]

    ospec = pl.BlockSpec((tb, C), lambda g, i: (i, g))
    return pl.pallas_call(
        kern, name="pool_fwd", grid=(n_g, T // tb),
        in_specs=[pl.BlockSpec((tb, C), lambda g, i: (i, ucol + g)),
                  pl.BlockSpec((tb, C), lambda g, i: (jnp.maximum(i - 1, 0), ucol + g)),
                  pl.BlockSpec((None, C, C), lambda g, i: (g, 0, 0)),
                  pl.BlockSpec((1, C), lambda g, i: (0, g))],
        out_specs=[ospec, ospec],
        out_shape=[jax.ShapeDtypeStruct((T, n_g * C), BF16), jax.ShapeDtypeStruct((T, n_g * C), F32)],
        compiler_params=_params(),
    )(proj, proj, pw, ps)


def _pool_bwd(dmix_in, dcol, pooled, pw, ps, T, tb):
    C = POOL_GROUP_DIM
    n_g = len(POOL_WINDOWS)
    nb = T // tb

    def kern(dc_ref, dn_ref, pooled_ref, pw_ref, ps_ref, du_ref, dpw_ref, dps_ref):
        g, i = pl.program_id(0), pl.program_id(1)
        dc = dc_ref[...]
        scale = ps_ref[...]
        t2 = i * tb + lax.broadcasted_iota(jnp.int32, (2 * tb, C), 0)
        d2 = jnp.where(t2 < T, jnp.concatenate([dc, dn_ref[...]], axis=0) * scale, 0.0)
        dpooled2 = _dot(d2, pw_ref[...], "NT")
        count = jnp.minimum(t2 + 1, _group_window(g)).astype(F32)
        sums = _window_select(dpooled2 / count, g, lambda s, k: pltpu.roll(s, 2 * tb - k, 0))
        du_ref[...] = (sums[:tb, :] - dpooled2[:tb, :]).astype(BF16)
        pooled = pooled_ref[...]
        p = _dot(pooled, pw_ref[...], "NN")
        dps = jnp.sum(dc * p, axis=0, keepdims=True)
        dpw = _dot(pooled, d2[:tb, :], "TN")

        @pl.when(i == 0)
        def _():
            dps_ref[...] = dps
            dpw_ref[...] = dpw

        @pl.when(i > 0)
        def _():
            dps_ref[...] += dps
            dpw_ref[...] += dpw

    return pl.pallas_call(
        kern, name="pool_bwd", grid=(n_g, nb),
        in_specs=[pl.BlockSpec((tb, C), lambda g, i: (i, dcol + g)),
                  pl.BlockSpec((tb, C), lambda g, i: (jnp.minimum(i + 1, nb - 1), dcol + g)),
                  pl.BlockSpec((tb, C), lambda g, i: (i, g)),
                  pl.BlockSpec((None, C, C), lambda g, i: (g, 0, 0)),
                  pl.BlockSpec((1, C), lambda g, i: (0, g))],
        out_specs=[pl.BlockSpec((tb, C), lambda g, i: (i, g)),
                   pl.BlockSpec((None, C, C), lambda g, i: (g, 0, 0)),
                   pl.BlockSpec((1, C), lambda g, i: (0, g))],
        out_shape=[jax.ShapeDtypeStruct((T, n_g * C), BF16), jax.ShapeDtypeStruct((n_g, C, C), F32),
                   jax.ShapeDtypeStruct((1, n_g * C), F32)],
        compiler_params=_params(),
    )(dmix_in, dmix_in, pooled, pw, ps)


D_QKV = 3 * N_HEADS * HEAD_DIM
D_U = len(POOL_WINDOWS) * POOL_GROUP_DIM
F_PAD = 128
D_PROJ = D_QKV + D_U + F_PAD


def _perm_w_in(w):
    pad = jnp.zeros((w.shape[0], F_PAD - N_HEADS), w.dtype)
    return jnp.concatenate([w[:, :D_QKV], w[:, D_QKV + N_HEADS:], w[:, D_QKV:D_QKV + N_HEADS], pad], axis=1)


def _unperm_w_in(w):
    return jnp.concatenate([w[:, :D_QKV], w[:, D_QKV + D_U:D_QKV + D_U + N_HEADS], w[:, D_QKV:D_QKV + D_U]], axis=1)


def _mixer_fwd(x, norm_g, sh, sc, gate, w_in_p, b_pad, gq, gk, late_weights, ps, T, proj_comm, attn_comm):
    tb = min(256, T)
    blk = min(512, T)
    hm = _norm_mod_fwd("mix_norm_fwd", x, norm_g, sc, sh, T, tb)
    proj, got_proj = _mm("mix_proj", hm, w_in_p, "NN", F32, 512, D_PROJ // 3, 2048, comm=proj_comm)
    pw, w_out = late_weights(got_proj)
    qn, kn, vb = _qknorm_fwd(proj, gq, gk, T, tb)
    fcol = (D_QKV + D_U) // 128
    f_tm = _fgate_fwd(proj, fcol, b_pad, T)
    f_hm = f_tm[:, :N_HEADS].T
    (o, lse), got = _attn_fwd(qn, kn, vb, f_tm, f_hm, T, blk, comm=attn_comm)
    pooled, pool_o = _pool_fwd(proj, D_QKV // POOL_GROUP_DIM, pw, ps, T, tb)
    mix_in = jnp.concatenate([o.astype(BF16), pool_o.astype(BF16)], axis=1)
    mix = _mm("mix_out", mix_in, w_out, "NN", F32, 512, 2048, 2048)
    x_out = _residual("mix_res", x, mix, gate, 1.0, T, tb)
    return x_out, (x, hm, proj, qn, kn, vb, f_tm, f_hm, o, lse, pooled, mix_in, mix), pw, w_out, got


def _mixer_bwd(dx_out, saved, norm_g, sc, gate, w_in_p, b_pad, gq, gk, pw, ps, w_out, T, core, ride_sums):
    x, hm, proj, qn, kn, vb, f_tm, f_hm, o, lse, pooled, mix_in, mix = saved
    tb = min(256, T)
    blk = min(512, T)
    W = N_HEADS * HEAD_DIM
    D = x.shape[1]
    n_g = len(POOL_WINDOWS)
    dmix, dgate = _residual_bwd("mix_res_bwd", dx_out, mix, gate, 1.0, T, tb)
    dmix_in = _mm("mix_out_bwd", dmix, w_out, "NT", F32, 512, 2048, 2048)
    dw_out = _mm("mix_dw_out", mix_in, dmix, "TN", BF16, 512, 1024, T)
    delta = _attn_delta(o, dmix_in, T, tb)
    (dqn, dfq, dkn, dv, dfk), ride_got = _attn_bwd(qn, kn, vb, dmix_in, lse, delta, f_tm, f_hm, T, blk,
                                                   comm=_chip_comm(ride_sums))
    dq, dk, dgq, dgk = _qknorm_bwd(proj, dqn, dkn, gq, gk, T, tb)
    dF = jnp.pad(dfq[:, ::HEAD_DIM] + dfk.reshape(N_HEADS, T).T, ((0, 0), (0, F_PAD - N_HEADS)))
    fcol = (D_QKV + D_U) // 128
    dfl, dbf = _fgate_bwd(proj, fcol, b_pad, dF, T)
    du, dpw, dps = _pool_bwd(dmix_in, W // POOL_GROUP_DIM, pooled, pw, ps, T, tb)
    dproj = jnp.concatenate([dq, dk, dv.astype(BF16), du, dfl.astype(BF16)], axis=1)
    dw_in_p = _mm("mix_dw_in", hm, dproj, "TN", BF16, 512, D_PROJ // 3, T)
    pw_rows = POOL_GROUP_DIM // N_DEV
    slabs = [jnp.transpose(_unperm_w_in(dw_in_p).reshape(D, N_DEV, -1), (1, 0, 2)),
             jnp.transpose(dpw.astype(BF16).reshape(n_g, N_DEV, pw_rows, POOL_GROUP_DIM),
                           (1, 0, 2, 3)).reshape(N_DEV, n_g * pw_rows, POOL_GROUP_DIM),
             dw_out.reshape(N_DEV, -1, D)]
    _, sums = _reduce_level1("mix", slabs, core, [TILE_W_IN, TILE_POOL, TILE_MIX_OUT])
    dhm = _mm("mix_proj_bwd", dproj, w_in_p, "NT", F32, 512, 512, D_PROJ)
    dx, dsh, dsc, dng = _norm_mod_bwd("mix_norm_bwd", x, dhm, dx_out, norm_g, sc, T, tb)
    return dx, (dsh, dsc, dgate, dng), sums, dps, dgq, dgk, dbf, ride_got


def kernel(x, c, w_ada, b_ada, ffn1_norm_g, ffn1_w_in, ffn1_w_out, mix_norm_g, w_in, b_forget, q_norm_g, k_norm_g, pool_w, pool_scale, w_out, ffn2_norm_g, ffn2_w_in, ffn2_w_out, final_norm_g, loss_target, m_w_ada, m_b_ada, m_ffn1_norm_g, m_ffn1_w_in, m_ffn1_w_out, m_mix_norm_g, m_w_in, m_b_forget, m_q_norm_g, m_k_norm_g, m_pool_w, m_pool_scale, m_w_out, m_ffn2_norm_g, m_ffn2_w_in, m_ffn2_w_out, m_final_norm_g, v_w_ada, v_b_ada, v_ffn1_norm_g, v_ffn1_w_in, v_ffn1_w_out, v_mix_norm_g, v_w_in, v_b_forget, v_q_norm_g, v_k_norm_g, v_pool_w, v_pool_scale, v_w_out, v_ffn2_norm_g, v_ffn2_w_in, v_ffn2_w_out, v_final_norm_g):
    T, D = x.shape[1], x.shape[2]
    mx, my, mc = _mesh_pos()
    me = _flat(mx, my, mc)
    x0 = x[0]
    tgt = loss_target[0]
    tb = min(256, T)

    core = jnp.reshape(mc, (1,)).astype(jnp.int32)
    half = N_DEV // 2
    n_g = len(POOL_WINDOWS)
    pw_rows = POOL_GROUP_DIM // N_DEV

    def bf(w):
        return w.astype(BF16)

    n_loc = w_ada.shape[2]
    c_all = _standalone("gather_c", _gather_comm([c.reshape(8, D // 8)]))[0].reshape(N_DEV, D)
    b_loc = lax.dynamic_slice_in_dim(b_ada, me * n_loc, n_loc, axis=1)
    mod_loc = _ada_fwd(c_all, w_ada[0], b_loc, n_loc // 3)
    mod_all = _standalone("gather_mod", _gather_comm([mod_loc]))[0]
    mod = lax.dynamic_index_in_dim(mod_all, me, axis=1, keepdims=False).reshape(N_MOD, 1, D)
    sh1, sc1, g1, sh2, sc2, g2, sh3, sc3, g3 = [mod[k] for k in range(N_MOD)]
    b_pad = jnp.pad(b_forget, ((0, 0), (0, F_PAD - N_HEADS)))
    ps = pool_scale

    wi1 = _standalone("gather_ffn1_w_in", _gather_comm([bf(ffn1_w_in[0])]))[0]
    x1, sv1, wo1, (w_in_g,) = _ffn_fwd(
        "ffn1", x0, ffn1_norm_g, sh1, sc1, g1, wi1, lambda got: got[0].reshape(half, -1, D), T,
        up_comm=_gather_comm([bf(ffn1_w_out[0])], forward_at=0.7), down_comm=_gather_comm([bf(w_in[0])], forward_at=0.8))
    w_in_p = _perm_w_in(jnp.transpose(w_in_g, (1, 0, 2)).reshape(D, -1))

    def late_weights(got):
        pool_g, w_out_g = got
        pw = jnp.transpose(pool_g.reshape(N_DEV, n_g, pw_rows, POOL_GROUP_DIM),
                           (1, 0, 2, 3)).reshape(n_g, POOL_GROUP_DIM, POOL_GROUP_DIM)
        return pw, w_out_g.reshape(-1, D)

    x2, svm, pw_full, w_out_full, (wi2, wo2_g) = _mixer_fwd(
        x1, mix_norm_g, sh2, sc2, g2, w_in_p, b_pad, q_norm_g, k_norm_g, late_weights, ps, T,
        proj_comm=_gather_comm([bf(pool_w[0].reshape(-1, POOL_GROUP_DIM)), bf(w_out[0])], forward_at=0.6),
        attn_comm=_gather_comm([bf(ffn2_w_in[0]), bf(ffn2_w_out[0])], forward_at=0.8))
    x3, sv2, wo2, _ = _ffn_fwd("ffn2", x2, ffn2_norm_g, sh3, sc3, g3, wi2,
                               lambda got: wo2_g.reshape(half, -1, D), T)
    dx3, dgf, loss_l = _final_loss(x3, tgt, final_norm_g.reshape(1, D), T, tb)
    loss = lax.psum(loss_l[0, 0], ("x", "y", "c"))

    dx2, (dsh3, dsc3, dg3, dn3), dwi2_sum, dwo2, _ = _ffn_bwd(
        "ffn2", dx3, sv2, ffn2_norm_g, sc3, g3, wi2, wo2, T, core, defer_dwi=True)
    dx1, (dsh2, dsc2, dg2, dn2), mix_sums, dps, dgq, dgk, dbf, (dwi2,) = _mixer_bwd(
        dx2, svm, mix_norm_g, sc2, g2, w_in_p, b_pad, q_norm_g, k_norm_g, pw_full, ps, w_out_full, T, core,
        ride_sums=[dwi2_sum])
    dx0, (dsh1, dsc1, dg1, dn1), dwi1, dwo1, (dw_in_r, dpw_r, dw_out_r) = _ffn_bwd(
        "ffn1", dx1, sv1, ffn1_norm_g, sc1, g1, wi1, wo1, T, core, ride_sums=mix_sums)

    received = dict(ffn1_w_in=dwi1, ffn1_w_out=dwo1, w_in=dw_in_r, pool_w=dpw_r, w_out=dw_out_r,
                    ffn2_w_in=dwi2, ffn2_w_out=dwo2)
    moments = dict(ffn1_w_in=(m_ffn1_w_in, v_ffn1_w_in), ffn1_w_out=(m_ffn1_w_out, v_ffn1_w_out),
                   w_in=(m_w_in, v_w_in), pool_w=(m_pool_w, v_pool_w), w_out=(m_w_out, v_w_out),
                   ffn2_w_in=(m_ffn2_w_in, v_ffn2_w_in), ffn2_w_out=(m_ffn2_w_out, v_ffn2_w_out))
    weights = dict(ffn1_w_in=ffn1_w_in, ffn1_w_out=ffn1_w_out, w_in=w_in, pool_w=pool_w, w_out=w_out,
                   ffn2_w_in=ffn2_w_in, ffn2_w_out=ffn2_w_out)
    row_tiles = dict(ffn1_w_in=TILE_W_IN, ffn1_w_out=TILE_W_OUT, w_in=TILE_W_IN, pool_w=TILE_POOL,
                     w_out=TILE_MIX_OUT, ffn2_w_in=TILE_W_IN, ffn2_w_out=TILE_W_OUT)
    results = {}
    for k in received:
        shape = weights[k].shape
        two_d = received[k].shape[1:]
        mk, vk = moments[k]
        outs = _adamw("adamw_" + k, received[k], weights[k].reshape(two_d), mk.reshape(two_d),
                      vk.reshape(two_d), row_tiles[k][0])
        results[k] = [o.reshape(shape) for o in outs]

    dmod = jnp.concatenate([dsh1, dsc1, dg1, dsh2, dsc2, dg2, dsh3, dsc3, dg3], axis=1)
    small_names = ["b_ada", "ffn1_norm_g", "mix_norm_g", "ffn2_norm_g", "final_norm_g", "b_forget",
                   "q_norm_g", "k_norm_g", "pool_scale"]
    small_w = dict(b_ada=b_ada, ffn1_norm_g=ffn1_norm_g, mix_norm_g=mix_norm_g, ffn2_norm_g=ffn2_norm_g,
                   final_norm_g=final_norm_g, b_forget=b_forget, q_norm_g=q_norm_g, k_norm_g=k_norm_g,
                   pool_scale=pool_scale)
    small_m = dict(b_ada=m_b_ada, ffn1_norm_g=m_ffn1_norm_g, mix_norm_g=m_mix_norm_g, ffn2_norm_g=m_ffn2_norm_g,
                   final_norm_g=m_final_norm_g, b_forget=m_b_forget, q_norm_g=m_q_norm_g, k_norm_g=m_k_norm_g,
                   pool_scale=m_pool_scale)
    small_v = dict(b_ada=v_b_ada, ffn1_norm_g=v_ffn1_norm_g, mix_norm_g=v_mix_norm_g, ffn2_norm_g=v_ffn2_norm_g,
                   final_norm_g=v_final_norm_g, b_forget=v_b_forget, q_norm_g=v_q_norm_g, k_norm_g=v_k_norm_g,
                   pool_scale=v_pool_scale)
    small_g = dict(b_ada=dmod, ffn1_norm_g=dn1, mix_norm_g=dn2, ffn2_norm_g=dn3, final_norm_g=dgf,
                   b_forget=dbf[:, :N_HEADS], q_norm_g=dgq, k_norm_g=dgk, pool_scale=dps)
    sizes = [small_w[k].size for k in small_names]
    total = sum(sizes)
    lanes = 8 * 128
    padded = -(-total // lanes) * lanes

    def pack(d):
        flat = jnp.concatenate([d[k].reshape(-1) for k in small_names])
        return jnp.pad(flat, (0, padded - total)).reshape(8, padded // 8)

    small_parts = _standalone("gather_small_grads", _gather_comm([pack(small_g)]))[0]
    s_outs = _adamw("adamw_small", small_parts, pack(small_w), pack(small_m), pack(small_v), 8)
    offs = [0]
    for s in sizes:
        offs.append(offs[-1] + s)
    for idx, k in enumerate(small_names):
        results[k] = [o.reshape(-1)[offs[idx]:offs[idx + 1]].reshape(small_w[k].shape) for o in s_outs]

    dmod_all = small_parts.reshape(N_DEV, padded)[:, :N_MOD * D]
    dmod_loc = lax.dynamic_slice_in_dim(dmod_all, me * n_loc, n_loc, axis=1)
    g_ada = _ada_bwd(c_all, dmod_loc, n_loc // 3)
    a_outs = _adamw("adamw_w_ada", g_ada[None], w_ada[0], m_w_ada[0], v_w_ada[0], 128)
    results["w_ada"] = [o.reshape(w_ada.shape) for o in a_outs]

    order = ["w_ada", "b_ada", "ffn1_norm_g", "ffn1_w_in", "ffn1_w_out", "mix_norm_g", "w_in", "b_forget",
             "q_norm_g", "k_norm_g", "pool_w", "pool_scale", "w_out", "ffn2_norm_g", "ffn2_w_in", "ffn2_w_out",
             "final_norm_g"]
    out = [loss, dx0[None]]
    for part in range(4):
        out += [results[k][part] for k in order]
    return tuple(out)
```

```python
import jax
import jax.numpy as jnp
from jax import lax
from jax.experimental import pallas as pl
from jax.experimental.pallas import tpu as pltpu

F32 = jnp.float32
BF16 = jnp.bfloat16
MESH = pl.DeviceIdType.MESH
ANY = pl.BlockSpec(memory_space=pl.ANY)

N_DEV = 8
EPS = 1e-6
HEAD_DIM = 128
N_HEADS = 8
POOL_WINDOWS = (2, 4, 8, 16)
POOL_GROUP_DIM = 256
N_MOD = 9
ADAM_LR = 0.001
ADAM_B1 = 0.9
ADAM_B2 = 0.999
ADAM_EPS = 1e-08
ADAM_WD = 0.01
ADAM_STEP = 10
NEG = -1e30
VMEM_LIMIT_V7X = 56 * 1024 * 1024


def _params():
    return pltpu.CompilerParams(vmem_limit_bytes=VMEM_LIMIT_V7X)


def _sigmoid(z):
    return 1.0 / (1.0 + jnp.exp(-z))


def _rstd(x):
    return lax.rsqrt(jnp.mean(x * x, axis=-1, keepdims=True) + EPS)


def _mesh_pos():
    return lax.axis_index("x"), lax.axis_index("y"), lax.axis_index("c")


def _flat(px, py, pc):
    return 4 * px + 2 * py + pc


class _Comm:
    def __init__(self, ins, outs, sems, phases):
        self.ins, self.outs, self.sems, self.phases = list(ins), list(outs), list(sems), list(phases)


def _pallas(kern, *, comm=None, **kw):
    if comm is None:
        return pl.pallas_call(kern, **kw)
    grid = tuple(kw["grid"])
    single = not isinstance(kw["out_shape"], (list, tuple))
    out_shape = [kw["out_shape"]] if single else list(kw["out_shape"])
    out_specs = [kw["out_specs"]] if single else list(kw["out_specs"])
    in_specs = list(kw["in_specs"])
    scratch = list(kw.get("scratch_shapes", ()))
    n_in, n_out, n_scr = len(in_specs), len(out_shape), len(scratch)
    n_ci, n_co = len(comm.ins), len(comm.outs)
    strides, n_steps = [], 1
    for g in reversed(grid):
        strides.insert(0, n_steps)
        n_steps *= g

    def wrapped(*refs):
        ins, cins = refs[:n_in], refs[n_in:n_in + n_ci]
        base = n_in + n_ci
        outs, couts = refs[base:base + n_out], refs[base + n_out:base + n_out + n_co]
        base += n_out + n_co
        scr, sems = refs[base:base + n_scr], refs[base + n_scr:]
        step = sum(pl.program_id(d) * strides[d] for d in range(len(grid)))
        for frac, fn in comm.phases:
            if frac < 1.0:
                pl.when(step == int(round(frac * (n_steps - 1))))(lambda fn=fn: fn(cins, couts, sems))
        kern(*ins, *outs, *scr)
        for frac, fn in comm.phases:
            if frac >= 1.0:
                pl.when(step == n_steps - 1)(lambda fn=fn: fn(cins, couts, sems))

    kw = dict(kw, in_specs=in_specs + [ANY] * n_ci, out_specs=out_specs + [ANY] * n_co,
              out_shape=out_shape + comm.outs, scratch_shapes=scratch + comm.sems)
    call = pl.pallas_call(wrapped, **kw)

    def run(*args):
        res = call(*args, *comm.ins)
        main = res[0] if single else list(res[:n_out])
        return main, list(res[n_out:])

    return run


def _join(first, second):
    n_i, n_o, n_s = len(first.ins), len(first.outs), len(first.sems)

    def left(fn):
        return lambda ins, outs, sems: fn(ins[:n_i], outs[:n_o], sems[:n_s])

    def right(fn):
        return lambda ins, outs, sems: fn(ins[n_i:], outs[n_o:], sems[n_s:])

    phases = [(f, left(fn)) for f, fn in first.phases] + [(f, right(fn)) for f, fn in second.phases]
    return _Comm(first.ins + second.ins, first.outs + second.outs, first.sems + second.sems, phases)


def _hosted(comm, res):
    return res if comm is not None else (res, [])


def _standalone(name, comm):
    def kern():
        pass

    return _pallas(kern, comm=comm, name=name, grid=(1,), in_specs=[], out_specs=[], out_shape=[])()[1]


def _dma_sems(*shapes):
    return [pltpu.SemaphoreType.DMA(s) for s in shapes]


def _gather_comm(arrs, forward_at=0.5):
    n = len(arrs)

    def setup(outs, sems):
        send_sems, recv_sems, _ = sems
        x, y, c = _mesh_pos()
        chips = [(1 - x, y), (x, 1 - y), (1 - x, 1 - y)]

        def copy(a, k, block, to, src=None):
            dst = outs[a].at[_flat(*block)]
            return pltpu.make_async_remote_copy(
                src_ref=dst if src is None else src, dst_ref=dst,
                send_sem=send_sems.at[a, k], recv_sem=recv_sems.at[a, k],
                device_id=to, device_id_type=MESH)

        return (x, y, c), (x, y, 1 - c), chips, copy

    def local(ins, outs, sems, a, me):
        return pltpu.make_async_copy(ins[a], outs[a].at[_flat(*me)], sems[2].at[a])

    def send_own(ins, outs, sems):
        me, sibling, chips, copy = setup(outs, sems)
        for a in range(n):
            local(ins, outs, sems, a, me).start()
            copy(a, 0, me, sibling, src=ins[a]).start()
            for j, chip in enumerate(chips):
                copy(a, 1 + j, me, (*chip, me[2]), src=ins[a]).start()

    def forward(ins, outs, sems):
        me, sibling, chips, copy = setup(outs, sems)
        for a in range(n):
            for j, chip in enumerate(chips):
                copy(a, 1 + j, (*chip, me[2]), me).wait_recv()
                copy(a, 4 + j, (*chip, me[2]), sibling).start()

    def finish(ins, outs, sems):
        me, sibling, chips, copy = setup(outs, sems)
        for a in range(n):
            copy(a, 0, sibling, me).wait_recv()
            for j, chip in enumerate(chips):
                copy(a, 4 + j, (*chip, 1 - me[2]), me).wait_recv()
        for a in range(n):
            copy(a, 0, me, sibling, src=ins[a]).wait_send()
            for j, chip in enumerate(chips):
                copy(a, 1 + j, me, (*chip, me[2]), src=ins[a]).wait_send()
                copy(a, 4 + j, (*chip, me[2]), sibling).wait_send()
            local(ins, outs, sems, a, me).wait()

    return _Comm(arrs, [jax.ShapeDtypeStruct((N_DEV,) + a.shape, a.dtype) for a in arrs],
                 _dma_sems((n, 7), (n, 7), (n,)), [(0.0, send_own), (forward_at, forward), (1.0, finish)])


CHIPS = [(0, 0), (0, 1), (1, 0), (1, 1)]


def _sibling_comm(parts):
    n = len(parts)

    def copies(ins, outs, sems):
        x, y, c = _mesh_pos()
        return [pltpu.make_async_remote_copy(
                    src_ref=ins[a].at[_flat(qx, qy, 1 - c)], dst_ref=outs[a].at[q],
                    send_sem=sems[0].at[a, q], recv_sem=sems[1].at[a, q],
                    device_id=(x, y, 1 - c), device_id_type=MESH)
                for a in range(n) for q, (qx, qy) in enumerate(CHIPS)]

    def start(ins, outs, sems):
        for cp in copies(ins, outs, sems):
            cp.start()

    def finish(ins, outs, sems):
        for cp in copies(ins, outs, sems):
            cp.wait_recv()
        for cp in copies(ins, outs, sems):
            cp.wait_send()

    return _Comm(parts, [jax.ShapeDtypeStruct((4,) + p.shape[1:], p.dtype) for p in parts],
                 _dma_sems((n, 4), (n, 4)), [(0.0, start), (1.0, finish)])


def _chip_comm(sums):
    n = len(sums)
    flips = [(1, 0), (0, 1), (1, 1)]

    def own(ins, outs, sems):
        mine = 2 * lax.axis_index("x") + lax.axis_index("y")
        return [pltpu.make_async_copy(ins[a].at[mine], outs[a].at[mine], sems[2].at[a]) for a in range(n)]

    def copies(ins, outs, sems, arriving=False):
        x, y, c = _mesh_pos()
        mine = 2 * x + y
        remote = []
        for a in range(n):
            for k, (fx, fy) in enumerate(flips):
                qx, qy = x ^ fx, y ^ fy
                q = 2 * qx + qy
                remote.append(pltpu.make_async_remote_copy(
                    src_ref=ins[a].at[q], dst_ref=outs[a].at[q if arriving else mine],
                    send_sem=sems[0].at[a, k], recv_sem=sems[1].at[a, k],
                    device_id=(qx, qy, c), device_id_type=MESH))
        return remote

    def start(ins, outs, sems):
        for cp in own(ins, outs, sems) + copies(ins, outs, sems):
            cp.start()

    def finish(ins, outs, sems):
        for cp in copies(ins, outs, sems, arriving=True):
            cp.wait_recv()
        for cp in copies(ins, outs, sems):
            cp.wait_send()
        for cp in own(ins, outs, sems):
            cp.wait()

    return _Comm(sums, [jax.ShapeDtypeStruct(s.shape, s.dtype) for s in sums],
                 _dma_sems((n, 3), (n, 3), (n,)), [(0.0, start), (1.0, finish)])


def _pair_add(name, parts, got, core, tr):
    _, R, C = parts.shape
    assert R % tr == 0

    def kern(c_ref, p_ref, g_ref, o_ref):
        o_ref[...] = (p_ref[...].astype(F32) + g_ref[...].astype(F32)).astype(o_ref.dtype)

    blk = pl.BlockSpec((None, tr, C), lambda q, i, c_ref: (q, i, 0))
    return pl.pallas_call(
        kern, name=name,
        grid_spec=pltpu.PrefetchScalarGridSpec(
            num_scalar_prefetch=1, grid=(4, R // tr),
            in_specs=[pl.BlockSpec((None, tr, C), lambda q, i, c_ref: (2 * q + c_ref[0], i, 0)), blk],
            out_specs=blk),
        out_shape=jax.ShapeDtypeStruct((4, R, C), parts.dtype), compiler_params=_params(),
    )(core, parts, got)


def _rowwise(name, body, T, tb, rows, vecs, out_rows, out_accs):
    n_in = len(rows) + len(vecs)
    n_o, n_a = len(out_rows), len(out_accs)

    def kern(*refs):
        i = pl.program_id(0)
        res = body(*[r[...] for r in refs[:n_in]])
        if not isinstance(res, (tuple, list)):
            res = (res,)
        outs = refs[n_in:]
        for k in range(n_o):
            outs[k][...] = res[k].astype(outs[k].dtype)

        def accumulate(ref, val):
            @pl.when(i == 0)
            def _():
                ref[...] = val

            @pl.when(i > 0)
            def _():
                ref[...] += val

        for k in range(n_a):
            accumulate(outs[n_o + k], res[n_o + k])

    in_specs = [pl.BlockSpec((tb, w), lambda i, cb=cb: (i, cb)) for (_, w, cb) in rows]
    in_specs += [pl.BlockSpec((1, v.shape[1]), lambda i: (0, 0)) for v in vecs]
    out_specs = [pl.BlockSpec((tb, w), lambda i: (i, 0)) for (w, _) in out_rows]
    out_specs += [pl.BlockSpec((1, w), lambda i: (0, 0)) for w in out_accs]
    out_shape = [jax.ShapeDtypeStruct((T, w), dt) for (w, dt) in out_rows]
    out_shape += [jax.ShapeDtypeStruct((1, w), F32) for w in out_accs]
    res = pl.pallas_call(
        kern, name=name, grid=(T // tb,), in_specs=in_specs, out_specs=out_specs,
        out_shape=out_shape, compiler_params=_params(),
    )(*[r[0] for r in rows], *vecs)
    return res


def _dot(a, b, mode):
    dims = {"NN": ((1,), (0,)), "NT": ((1,), (1,)), "TN": ((0,), (0,))}[mode]
    return lax.dot_general(a.astype(BF16), b.astype(BF16), (dims, ((), ())),
                           preferred_element_type=F32)


def _mm(name, a, b, mode, out_dtype, tm, tn, tk, ga=False, gb=False, gmode=None, comm=None):
    G = (a.shape[0] if ga else b.shape[0]) if gmode else 1
    a2, b2 = a.shape[-2:], b.shape[-2:]
    if mode == "NN":
        (M, K), (_, N) = a2, b2
    elif mode == "NT":
        (M, K), (N, _) = a2, b2
    else:
        (K, M), (_, N) = a2, b2
    tm, tn, tk = min(tm, M), min(tn, N), min(tk, K)
    assert M % tm == 0 and N % tn == 0 and K % tk == 0, (name, M, N, K, tm, tn, tk)
    batch = gmode == "batch"
    n_gb, n_gs = (G if batch else 1), (G if gmode == "sum" else 1)
    nk = K // tk
    n_red = n_gs * nk

    def grp(g_b, g_s):
        return g_b if batch else g_s

    if mode == "TN":
        a_blk, a_idx = (tk, tm), lambda g_b, mi, ni, g_s, ki: (ki, mi)
    else:
        a_blk, a_idx = (tm, tk), lambda g_b, mi, ni, g_s, ki: (mi, ki)
    if mode == "NT":
        b_blk, b_idx = (tn, tk), lambda g_b, mi, ni, g_s, ki: (ni, ki)
    else:
        b_blk, b_idx = (tk, tn), lambda g_b, mi, ni, g_s, ki: (ki, ni)

    def with_group(blk, idx, has_group):
        if not has_group:
            return pl.BlockSpec(blk, idx)
        return pl.BlockSpec((None,) + blk, lambda g_b, mi, ni, g_s, ki: (grp(g_b, g_s),) + idx(g_b, mi, ni, g_s, ki))

    o_blk, o_idx = (tm, tn), lambda g_b, mi, ni, g_s, ki: (mi, ni)
    o_spec = with_group(o_blk, o_idx, batch)
    o_shape = ((G,) if batch else ()) + (M, N)

    def kern(a_ref, b_ref, o_ref, *scratch):
        part = _dot(a_ref[...], b_ref[...], mode)
        if n_red == 1:
            o_ref[...] = part.astype(o_ref.dtype)
            return
        acc = scratch[0]
        step = pl.program_id(3) * nk + pl.program_id(4)

        @pl.when(step == 0)
        def _():
            acc[...] = part

        @pl.when(step > 0)
        def _():
            acc[...] += part

        @pl.when(step == n_red - 1)
        def _():
            o_ref[...] = acc[...].astype(o_ref.dtype)

    return _pallas(
        kern, comm=comm, name=name, grid=(n_gb, M // tm, N // tn, n_gs, nk),
        in_specs=[with_group(a_blk, a_idx, ga), with_group(b_blk, b_idx, gb)],
        out_specs=o_spec, out_shape=jax.ShapeDtypeStruct(o_shape, out_dtype),
        scratch_shapes=[] if n_red == 1 else [pltpu.VMEM((tm, tn), F32)],
        compiler_params=_params(),
    )(a, b)


def _mm_groups(name, a, b, mode, tm, tn, residual=None, comm=None):
    G, M, K = a.shape
    N = b.shape[2] if mode == "NN" else b.shape[1]
    tm, tn = min(tm, M), min(tn, N)
    assert M % tm == 0 and N % tn == 0

    def kern(a_ref, b_ref, *rest):
        acc = _dot(a_ref[0], b_ref[0], mode)
        for g in range(1, G):
            acc = acc + _dot(a_ref[g], b_ref[g], mode)
        if residual is None:
            rest[0][...] = acc
        else:
            x_ref, g_ref, f_ref, o_ref = rest
            f_ref[...] = acc
            o_ref[...] = x_ref[...] + (residual[2] * g_ref[...]) * acc

    b_spec = (pl.BlockSpec((G, K, tn), lambda ni, mi: (0, 0, ni)) if mode == "NN"
              else pl.BlockSpec((G, tn, K), lambda ni, mi: (0, ni, 0)))
    o_spec = pl.BlockSpec((tm, tn), lambda ni, mi: (mi, ni))
    in_specs = [pl.BlockSpec((G, tm, K), lambda ni, mi: (0, mi, 0)), b_spec]
    args = [a, b]
    out = jax.ShapeDtypeStruct((M, N), F32)
    if residual is not None:
        in_specs += [o_spec, pl.BlockSpec((1, tn), lambda ni, mi: (0, ni))]
        args += [residual[0], residual[1]]
    return _pallas(
        kern, comm=comm, name=name, grid=(N // tn, M // tm), in_specs=in_specs,
        out_specs=o_spec if residual is None else [o_spec, o_spec],
        out_shape=out if residual is None else [out, out], compiler_params=_params(),
    )(*args)


def _adamw(name, parts, w, m, v, tr):
    G, R, C = parts.shape
    assert R % tr == 0
    bc1 = 1.0 - ADAM_B1 ** ADAM_STEP
    bc2 = 1.0 - ADAM_B2 ** ADAM_STEP

    def kern(p_ref, w_ref, m_ref, v_ref, g_out, d_out, m_out, v_out):
        g = p_ref[0].astype(F32)
        for s in range(1, G):
            g = g + p_ref[s].astype(F32)
        m2 = ADAM_B1 * m_ref[...] + (1.0 - ADAM_B1) * g
        v2 = ADAM_B2 * v_ref[...] + (1.0 - ADAM_B2) * (g * g)
        m_hat = m2 / bc1
        v_hat = v2 / bc2
        g_out[...] = g
        d_out[...] = -ADAM_LR * (m_hat / (jnp.sqrt(v_hat) + ADAM_EPS) + ADAM_WD * w_ref[...])
        m_out[...] = m2
        v_out[...] = v2

    blk = pl.BlockSpec((tr, C), lambda i: (i, 0))
    return pl.pallas_call(
        kern, name=name, grid=(R // tr,),
        in_specs=[pl.BlockSpec((G, tr, C), lambda i: (0, i, 0)), blk, blk, blk],
        out_specs=[blk] * 4, out_shape=[jax.ShapeDtypeStruct((R, C), F32)] * 4,
        compiler_params=_params(),
    )(parts, w, m, v)


def _ada_fwd(c_all, w_loc, b_loc, tn):
    B, D = c_all.shape
    N = w_loc.shape[1]

    def kern(c_ref, w_ref, b_ref, o_ref):
        cc = c_ref[...]
        act = cc * _sigmoid(cc)
        o_ref[...] = _dot(act, w_ref[...], "NN") + b_ref[...]

    return pl.pallas_call(
        kern, name="ada_fwd", grid=(N // tn,),
        in_specs=[pl.BlockSpec((B, D), lambda j: (0, 0)), pl.BlockSpec((D, tn), lambda j: (0, j)),
                  pl.BlockSpec((1, tn), lambda j: (0, j))],
        out_specs=pl.BlockSpec((B, tn), lambda j: (0, j)),
        out_shape=jax.ShapeDtypeStruct((B, N), F32), compiler_params=_params(),
    )(c_all, w_loc, b_loc)


def _ada_bwd(c_all, dmod_loc, tn):
    B, D = c_all.shape
    N = dmod_loc.shape[1]

    def kern(c_ref, d_ref, o_ref):
        cc = c_ref[...]
        act = cc * _sigmoid(cc)
        o_ref[...] = _dot(act, d_ref[...], "TN")

    return pl.pallas_call(
        kern, name="ada_bwd", grid=(N // tn,),
        in_specs=[pl.BlockSpec((B, D), lambda j: (0, 0)), pl.BlockSpec((B, tn), lambda j: (0, j))],
        out_specs=pl.BlockSpec((D, tn), lambda j: (0, j)),
        out_shape=jax.ShapeDtypeStruct((D, N), F32), compiler_params=_params(),
    )(c_all, dmod_loc)


def _norm_mod_fwd(name, x, g, sc, sh, T, tb):
    D = x.shape[1]

    def body(xb, gb, scb, shb):
        n = (xb * _rstd(xb)) * gb
        return n * (1.0 + scb) + shb

    return _rowwise(name, body, T, tb, [(x, D, 0)], [g, sc, sh], [(D, BF16)], [])[0]


def _norm_mod_bwd(name, x, dhm, dres, g, sc, T, tb):
    D = x.shape[1]

    def body(xb, db, rb, gb, scb):
        r = _rstd(xb)
        xh = xb * r
        n = xh * gb
        dn = db * (1.0 + scb)
        dxh = dn * gb
        dx = rb + r * (dxh - xh * jnp.mean(dxh * xh, axis=-1, keepdims=True))
        return (dx, jnp.sum(db, axis=0, keepdims=True), jnp.sum(db * n, axis=0, keepdims=True),
                jnp.sum(dn * xh, axis=0, keepdims=True))

    return _rowwise(name, body, T, tb, [(x, D, 0), (dhm, D, 0), (dres, D, 0)], [g, sc],
                    [(D, F32)], [D, D, D])


def _residual(name, x, f, gate, coef, T, tb):
    D = x.shape[1]

    def body(xb, fb, gb):
        return xb + (coef * gb) * fb

    return _rowwise(name, body, T, tb, [(x, D, 0), (f, D, 0)], [gate], [(D, F32)], [])[0]


def _residual_bwd(name, dx, f, gate, coef, T, tb):
    D = dx.shape[1]

    def body(db, fb, gb):
        return (coef * gb) * db, jnp.sum((coef * fb) * db, axis=0, keepdims=True)

    return _rowwise(name, body, T, tb, [(dx, D, 0), (f, D, 0)], [gate], [(D, BF16)], [D])


def _final_loss(x, tgt, g, T, tb):
    D = x.shape[1]

    def body(xb, tb_, gb):
        r = _rstd(xb)
        xh = xb * r
        err = xh * gb - tb_
        loss = 0.5 * jnp.sum(jnp.mean(err * err, axis=-1, keepdims=True), axis=0, keepdims=True)
        dy = err * (1.0 / D)
        dxh = dy * gb
        dx = r * (dxh - xh * jnp.mean(dxh * xh, axis=-1, keepdims=True))
        return dx, jnp.sum(dy * xh, axis=0, keepdims=True), jnp.broadcast_to(loss, (1, 128))

    return _rowwise("final_loss", body, T, tb, [(x, D, 0), (tgt, D, 0)], [g], [(D, F32)], [D, 128])


def _ffn_up(name, hm, wi, T, tm, comm=None):
    D = hm.shape[1]
    Ws = wi.shape[2]
    half = wi.shape[0] // 2

    def kern(h_ref, wa_ref, wb_ref, a_ref, b_ref, hid_ref):
        h = h_ref[...]
        a = _dot(h, wa_ref[...], "NN")
        b = _dot(h, wb_ref[...], "NN")
        a_ref[...] = a
        b_ref[...] = b
        hid_ref[...] = ((a * _sigmoid(a)) * b).astype(BF16)

    o_spec = pl.BlockSpec((None, tm, Ws), lambda g, i: (g, i, 0))
    return _pallas(
        kern, comm=comm, name=name, grid=(half, T // tm),
        in_specs=[pl.BlockSpec((tm, D), lambda g, i: (i, 0)),
                  pl.BlockSpec((None, D, Ws), lambda g, i: (g, 0, 0)),
                  pl.BlockSpec((None, D, Ws), lambda g, i: (g + half, 0, 0))],
        out_specs=[o_spec] * 3,
        out_shape=[jax.ShapeDtypeStruct((half, T, Ws), F32)] * 2 + [jax.ShapeDtypeStruct((half, T, Ws), BF16)],
        compiler_params=_params(),
    )(hm, wi, wi)


def _ffn_down_bwd(name, df, wo, a, b, T, tm, comm=None):
    D = df.shape[1]
    half, _, Ws = a.shape

    n_sub = 2 if tm % 32 == 0 else 1
    subs = [pl.ds(r * (tm // n_sub), tm // n_sub) for r in range(n_sub)]

    def kern(df_ref, wo_ref, a_ref, b_ref, dp_ref):
        wo_blk = wo_ref[...]
        dhid = [_dot(df_ref[rows, :], wo_blk, "NT") for rows in subs]
        for rows, dh in zip(subs, dhid):
            av = a_ref[rows, :]
            s = _sigmoid(av)
            silu = av * s
            dp_ref[0, rows, :] = (dh * b_ref[rows, :] * (s + silu * (1.0 - s))).astype(BF16)
            dp_ref[1, rows, :] = (dh * silu).astype(BF16)

    act = pl.BlockSpec((None, tm, Ws), lambda g, i: (g, i, 0))
    return _pallas(
        kern, comm=comm, name=name, grid=(half, T // tm),
        in_specs=[pl.BlockSpec((tm, D), lambda g, i: (i, 0)),
                  pl.BlockSpec((None, Ws, D), lambda g, i: (g, 0, 0)), act, act],
        out_specs=pl.BlockSpec((2, None, tm, Ws), lambda g, i: (0, g, i, 0)),
        out_shape=jax.ShapeDtypeStruct((2, half, T, Ws), BF16),
        compiler_params=_params(),
    )(df, wo, a, b)


def _ffn_fwd(tag, x, norm_g, sh, sc, gate, wi, wo_of, T, up_comm=None, down_comm=None):
    tb = min(256, T)
    hm = _norm_mod_fwd(tag + "_norm_fwd", x, norm_g, sc, sh, T, tb)
    (a, b, hid), got_up = _hosted(up_comm, _ffn_up(tag + "_up", hm, wi, T, min(512, T), comm=up_comm))
    wo = wo_of(got_up)
    (f, x_out), got_down = _hosted(down_comm, _mm_groups(tag + "_down", hid, wo, "NN", 512, 512,
                                                         residual=(x, gate, 0.5), comm=down_comm))
    return x_out, (x, hm, a, b, hid, f), wo, got_down


TILE_W_IN = (128, 512)
TILE_W_OUT = (16, 688)
TILE_MIX_OUT = (64, 256)
TILE_POOL = (128, 128)


def _reduce_level1(tag, parts, core, tiles, host=None):
    comm = _sibling_comm(parts)
    if host is None:
        res, got = None, _standalone(tag + "_sibling", comm)
    else:
        res, got = host(comm)
    sums = [_pair_add("%s_pair_add%d" % (tag, k), p, g, core, min(t[1], p.shape[1]))
            for k, (p, g, t) in enumerate(zip(parts, got, tiles))]
    return res, sums


def _ffn_bwd(tag, dx_out, saved, norm_g, sc, gate, wi, wo, T, core, ride_sums=None, defer_dwi=False):
    x, hm, a, b, hid, f = saved
    tb = min(256, T)
    D = x.shape[1]
    df, dgate = _residual_bwd(tag + "_res_bwd", dx_out, f, gate, 0.5, T, tb)
    dwo = _mm(tag + "_dwo", hid, df, "TN", BF16, 2048, 512, T, ga=True, gmode="batch").reshape(N_DEV, -1, D)
    n_ride = 0 if ride_sums is None else len(ride_sums)

    def down_bwd_call(comm):
        if n_ride:
            comm = _join(comm, _chip_comm(ride_sums))
        res, got = _ffn_down_bwd(tag + "_down_bwd", df, wo, a, b, T, min(512, T), comm=comm)
        return (res, got[len(got) - n_ride:]), got[:len(got) - n_ride]

    (dproj, ride_got), (dwo_sum,) = _reduce_level1(tag + "_dwo", [dwo], core, [TILE_W_OUT], host=down_bwd_call)
    dproj = dproj.reshape((2 * dproj.shape[1],) + dproj.shape[2:])
    dwi, (dwo_got,) = _mm(tag + "_dwi", hm, dproj, "TN", BF16, 512, 2048, T, gb=True, gmode="batch",
                          comm=_chip_comm([dwo_sum]))

    def dhm_call(comm):
        return _mm_groups(tag + "_dhm", dproj, wi, "NT", 512, 512, comm=comm)

    if defer_dwi:
        dhm, (dwi_out,) = _reduce_level1(tag + "_dwi", [dwi], core, [TILE_W_IN], host=dhm_call)
    else:
        _, (dwi_sum,) = _reduce_level1(tag + "_dwi", [dwi], core, [TILE_W_IN])
        dhm, (dwi_out,) = dhm_call(_chip_comm([dwi_sum]))
    dx, dsh, dsc, dng = _norm_mod_bwd(tag + "_norm_bwd", x, dhm, dx_out, norm_g, sc, T, tb)
    return dx, (dsh, dsc, dgate, dng), dwi_out, dwo_got, ride_got


def _heads(fn, *arrs):
    outs = [fn(*[a[:, h * HEAD_DIM:(h + 1) * HEAD_DIM] for a in arrs]) for h in range(N_HEADS)]
    return outs


def _qknorm_fwd(proj, gq, gk, T, tb):
    W = N_HEADS * HEAD_DIM

    def body(q, k, v, gqb, gkb):
        qn = jnp.concatenate(_heads(lambda t: (t * _rstd(t)) * gqb, q), axis=1)
        kn = jnp.concatenate(_heads(lambda t: (t * _rstd(t)) * gkb, k), axis=1)
        return qn, kn, v

    return _rowwise("qknorm_fwd", body, T, tb, [(proj, W, 0), (proj, W, 1), (proj, W, 2)], [gq, gk],
                    [(W, BF16)] * 3, [])


def _qknorm_bwd(proj, dqn, dkn, gq, gk, T, tb):
    W = N_HEADS * HEAD_DIM

    def one(t, dt, g):
        r = _rstd(t)
        th = t * r
        dth = dt * g
        d = r * (dth - th * jnp.mean(dth * th, axis=-1, keepdims=True))
        return d, jnp.sum(dt * th, axis=0, keepdims=True)

    def body(q, k, dq, dk, gqb, gkb):
        rq = _heads(lambda t, dt: one(t, dt, gqb), q, dq)
        rk = _heads(lambda t, dt: one(t, dt, gkb), k, dk)
        return (jnp.concatenate([r[0] for r in rq], axis=1), jnp.concatenate([r[0] for r in rk], axis=1),
                sum(r[1] for r in rq), sum(r[1] for r in rk))

    return _rowwise("qknorm_bwd", body, T, tb, [(proj, W, 0), (proj, W, 1), (dqn, W, 0), (dkn, W, 0)],
                    [gq, gk], [(W, BF16)] * 2, [HEAD_DIM, HEAD_DIM])


def _log_sigmoid(z):
    return jnp.minimum(z, 0.0) - jnp.log(1.0 + jnp.exp(-jnp.abs(z)))


def _fgate_fwd(proj, fcol, b_pad, T):
    nblk = T // 128

    def kern(f_ref, b_ref, o_ref):
        r = lax.broadcasted_iota(jnp.int32, (128, 128), 0)
        c = lax.broadcasted_iota(jnp.int32, (128, 128), 1)
        tri = (r >= c).astype(F32)
        carry = jnp.zeros((1, 128), F32)
        for k in range(nblk):
            rows = pl.ds(k * 128, 128)
            lf = _log_sigmoid(f_ref[rows, :] + b_ref[...])
            o_ref[rows, :] = jnp.dot(tri, lf, precision=lax.Precision.HIGHEST, preferred_element_type=F32) + carry
            carry = carry + jnp.sum(lf, axis=0, keepdims=True)

    return pl.pallas_call(
        kern, name="fgate_fwd", grid=(1,),
        in_specs=[pl.BlockSpec((T, 128), lambda i: (0, fcol)), pl.BlockSpec((1, 128), lambda i: (0, 0))],
        out_specs=pl.BlockSpec((T, 128), lambda i: (0, 0)),
        out_shape=jax.ShapeDtypeStruct((T, 128), F32), compiler_params=_params(),
    )(proj, b_pad)


def _fgate_bwd(proj, fcol, b_pad, dF, T):
    nblk = T // 128

    def kern(f_ref, b_ref, d_ref, o_ref, db_ref):
        r = lax.broadcasted_iota(jnp.int32, (128, 128), 0)
        c = lax.broadcasted_iota(jnp.int32, (128, 128), 1)
        tri = (c >= r).astype(F32)
        carry = jnp.zeros((1, 128), F32)
        db = jnp.zeros((1, 128), F32)
        for k in reversed(range(nblk)):
            rows = pl.ds(k * 128, 128)
            dblk = d_ref[rows, :]
            rc = jnp.dot(tri, dblk, precision=lax.Precision.HIGHEST, preferred_element_type=F32) + carry
            carry = carry + jnp.sum(dblk, axis=0, keepdims=True)
            z = f_ref[rows, :] + b_ref[...]
            dz = rc * (1.0 / (1.0 + jnp.exp(z)))
            o_ref[rows, :] = dz
            db = db + jnp.sum(dz, axis=0, keepdims=True)
        db_ref[...] = db

    return pl.pallas_call(
        kern, name="fgate_bwd", grid=(1,),
        in_specs=[pl.BlockSpec((T, 128), lambda i: (0, fcol)), pl.BlockSpec((1, 128), lambda i: (0, 0)),
                  pl.BlockSpec((T, 128), lambda i: (0, 0))],
        out_specs=[pl.BlockSpec((T, 128), lambda i: (0, 0)), pl.BlockSpec((1, 128), lambda i: (0, 0))],
        out_shape=[jax.ShapeDtypeStruct((T, 128), F32), jax.ShapeDtypeStruct((1, 128), F32)],
        compiler_params=_params(),
    )(proj, b_pad, dF)


LOG2E = 1.4426950408889634


def _gate_bias(ft, fh, h):
    lane = lax.broadcasted_iota(jnp.int32, ft.shape, 1)
    fq = jnp.sum(jnp.where(lane == h, ft, 0.0), axis=1, keepdims=True)
    f0 = jnp.max(fq, axis=0, keepdims=True)
    sub = lax.broadcasted_iota(jnp.int32, fh.shape, 0)
    fk = jnp.sum(jnp.where(sub == h, fh, 0.0), axis=0, keepdims=True)
    return (f0 - fk) * LOG2E


HEADS_PER_STEP = 2


def _causal_bias(blk):
    row = lax.broadcasted_iota(jnp.int32, (blk, blk), 0)
    col = lax.broadcasted_iota(jnp.int32, (blk, blk), 1)
    return jnp.where(row >= col, 0.0, NEG)


def _attn_fwd(qn, kn, vb, f_tm, f_hm, T, blk, comm=None):
    nb = T // blk
    scale = HEAD_DIM ** -0.5
    W = N_HEADS * HEAD_DIM
    G = HEADS_PER_STEP
    lanes = [slice(g * HEAD_DIM, (g + 1) * HEAD_DIM) for g in range(G)]

    def kern(q_ref, k_ref, v_ref, ft_ref, fh_ref, o_ref, lse_ref, m_scr, l_scr, acc_scr):
        hp, i, j = pl.program_id(0), pl.program_id(1), pl.program_id(2)

        @pl.when(j == 0)
        def _():
            m_scr[...] = jnp.full_like(m_scr, NEG)
            l_scr[...] = jnp.zeros_like(l_scr)
            acc_scr[...] = jnp.zeros_like(acc_scr)

        def block(diagonal):
            ft, fh = ft_ref[...], fh_ref[...]
            s = [_dot(q_ref[:, sl], k_ref[:, sl], "NT") * (scale * LOG2E) + _gate_bias(ft, fh, hp * G + g)
                 for g, sl in enumerate(lanes)]
            if diagonal:
                mask = _causal_bias(blk)
                s = [sg + mask for sg in s]
            m_prev = [m_scr[g] for g in range(G)]
            m_new = [jnp.maximum(mp, jnp.max(sg, axis=1, keepdims=True)) for mp, sg in zip(m_prev, s)]
            alpha = [jnp.exp2(mp - mn) for mp, mn in zip(m_prev, m_new)]
            p = [jnp.exp2(sg - mn) for sg, mn in zip(s, m_new)]
            for g, sl in enumerate(lanes):
                l_scr[g] = alpha[g] * l_scr[g] + jnp.sum(p[g], axis=1, keepdims=True)
                acc_scr[:, sl] = alpha[g] * acc_scr[:, sl] + _dot(p[g], v_ref[:, sl], "NN")
                m_scr[g] = m_new[g]

        @pl.when(j < i)
        def _():
            block(False)

        @pl.when(j == i)
        def _():
            block(True)
            for g, sl in enumerate(lanes):
                l = l_scr[g]
                o_ref[:, sl] = acc_scr[:, sl] / l
                lse_ref[:, sl] = jnp.broadcast_to(m_scr[g] + jnp.log2(l), (blk, HEAD_DIM))

    qspec = pl.BlockSpec((blk, G * HEAD_DIM), lambda h, i, j: (i, h))
    kspec = pl.BlockSpec((blk, G * HEAD_DIM), lambda h, i, j: (jnp.minimum(j, i), h))
    return _pallas(
        kern, comm=comm, name="attn_fwd", grid=(N_HEADS // G, nb, nb),
        in_specs=[qspec, kspec, kspec,
                  pl.BlockSpec((blk, 128), lambda h, i, j: (i, 0)),
                  pl.BlockSpec((N_HEADS, blk), lambda h, i, j: (0, jnp.minimum(j, i)))],
        out_specs=[qspec, qspec],
        out_shape=[jax.ShapeDtypeStruct((T, W), F32)] * 2,
        scratch_shapes=[pltpu.VMEM((G, blk, 1), F32), pltpu.VMEM((G, blk, 1), F32),
                        pltpu.VMEM((blk, G * HEAD_DIM), F32)],
        compiler_params=_params(),
    )(qn, kn, vb, f_tm, f_hm)


def _attn_bwd(qn, kn, vb, do, lse, delta, f_tm, f_hm, T, blk, comm=None):
    nb = T // blk
    scale = HEAD_DIM ** -0.5
    W = N_HEADS * HEAD_DIM
    G = HEADS_PER_STEP
    lanes = [slice(g * HEAD_DIM, (g + 1) * HEAD_DIM) for g in range(G)]

    def kern(q_ref, k_ref, v_ref, do_ref, lse_ref, dl_ref, ft_ref, fh_ref,
             dq_ref, dfq_ref, dk_ref, dv_ref, df_ref, dq_scr, dfq_scr, dk_scr, dv_scr, df_scr):
        hp, j, i = pl.program_id(0), pl.program_id(1), pl.program_id(2)

        @pl.when((j == 0) & (i == 0))
        def _():
            dq_scr[...] = jnp.zeros_like(dq_scr)
            dfq_scr[...] = jnp.zeros_like(dfq_scr)

        @pl.when(i == 0)
        def _():
            dk_scr[...] = jnp.zeros_like(dk_scr)
            dv_scr[...] = jnp.zeros_like(dv_scr)
            df_scr[...] = jnp.zeros_like(df_scr)

        def block(diagonal):
            ft, fh = ft_ref[...], fh_ref[...]
            rows = pl.ds(pl.multiple_of(i * blk, blk), blk)
            q = [q_ref[:, sl] for sl in lanes]
            k = [k_ref[:, sl] for sl in lanes]
            dob = [do_ref[:, sl].astype(BF16) for sl in lanes]
            s = [_dot(q[g], k[g], "NT") * (scale * LOG2E) + _gate_bias(ft, fh, hp * G + g) for g in range(G)]
            if diagonal:
                mask = _causal_bias(blk)
                s = [sg + mask for sg in s]
            p = [jnp.exp2(s[g] - lse_ref[:, sl.start:sl.start + 1]) for g, sl in enumerate(lanes)]
            dp = [_dot(dob[g], v_ref[:, sl], "NT") for g, sl in enumerate(lanes)]
            ds = [p[g] * (dp[g] - dl_ref[:, sl.start:sl.start + 1]) for g, sl in enumerate(lanes)]
            dsb = [d.astype(BF16) for d in ds]
            for g, sl in enumerate(lanes):
                dv_scr[:, sl] += _dot(p[g], dob[g], "TN")
                dk_scr[:, sl] += _dot(dsb[g], q[g], "TN") * scale
                dq_scr[rows, sl] += _dot(dsb[g], k[g], "NN") * scale
                df_scr[g] += jnp.sum(ds[g], axis=0, keepdims=True)
                dfq_scr[g, rows, :] += jnp.sum(ds[g], axis=1, keepdims=True)

        @pl.when(i > j)
        def _():
            block(False)

        @pl.when(i == j)
        def _():
            block(True)

        @pl.when(i == nb - 1)
        def _():
            dk_ref[...] = dk_scr[...]
            dv_ref[...] = dv_scr[...]
            df_ref[...] = -df_scr[...]

        @pl.when((j == nb - 1) & (i == nb - 1))
        def _():
            dq_ref[...] = dq_scr[...]
            for g, sl in enumerate(lanes):
                dfq_ref[:, sl] = jnp.broadcast_to(dfq_scr[g], (T, HEAD_DIM))

    qspec = pl.BlockSpec((blk, G * HEAD_DIM), lambda h, j, i: (jnp.maximum(i, j), h))
    full = pl.BlockSpec((T, G * HEAD_DIM), lambda h, j, i: (0, h))
    kspec = pl.BlockSpec((blk, G * HEAD_DIM), lambda h, j, i: (j, h))
    return _pallas(
        kern, comm=comm, name="attn_bwd", grid=(N_HEADS // G, nb, nb),
        in_specs=[qspec, kspec, kspec, qspec, qspec, qspec,
                  pl.BlockSpec((blk, 128), lambda h, j, i: (jnp.maximum(i, j), 0)),
                  pl.BlockSpec((N_HEADS, blk), lambda h, j, i: (0, j))],
        out_specs=[full, full, kspec, kspec, pl.BlockSpec((G, 1, blk), lambda h, j, i: (h, 0, j))],
        out_shape=[jax.ShapeDtypeStruct((T, W), F32)] * 4 + [jax.ShapeDtypeStruct((N_HEADS, 1, T), F32)],
        scratch_shapes=[pltpu.VMEM((T, G * HEAD_DIM), F32), pltpu.VMEM((G, T, 1), F32),
                        pltpu.VMEM((blk, G * HEAD_DIM), F32), pltpu.VMEM((blk, G * HEAD_DIM), F32),
                        pltpu.VMEM((G, 1, blk), F32)],
        compiler_params=_params(),
    )(qn, kn, vb, do, lse, delta, f_tm, f_hm)


def _attn_delta(o, do, T, tb):
    W = N_HEADS * HEAD_DIM

    def body(ob, dob):
        return jnp.concatenate(
            _heads(lambda a, b: jnp.broadcast_to(jnp.sum(a * b, axis=1, keepdims=True), a.shape), ob, dob), axis=1)

    return _rowwise("attn_delta", body, T, tb, [(o, W, 0), (do, W, 0)], [], [(W, F32)], [])[0]


def _window_select(s, g, shift):
    picks = []
    for k in (1, 2, 4, 8):
        s = s + shift(s, k)
        picks.append(s)
    return jnp.where(g == 0, picks[0], jnp.where(g == 1, picks[1], jnp.where(g == 2, picks[2], picks[3])))


def _group_window(g):
    return jnp.where(g == 0, POOL_WINDOWS[0], jnp.where(g == 1, POOL_WINDOWS[1],
                     jnp.where(g == 2, POOL_WINDOWS[2], POOL_WINDOWS[3])))


def _pool_fwd(proj, ucol, pw, ps, T, tb):
    C = POOL_GROUP_DIM
    n_g = len(POOL_WINDOWS)

    def kern(uc_ref, up_ref, pw_ref, ps_ref, pooled_ref, out_ref):
        g, i = pl.program_id(0), pl.program_id(1)
        uc = uc_ref[...]
        t2 = (i - 1) * tb + lax.broadcasted_iota(jnp.int32, (2 * tb, C), 0)
        u2 = jnp.where(t2 >= 0, jnp.concatenate([up_ref[...], uc], axis=0), 0.0)
        sums = _window_select(u2, g, lambda s, k: pltpu.roll(s, k, 0))[tb:, :]
        count = jnp.minimum(t2[tb:, :] + 1, _group_window(g)).astype(F32)
        pooled = sums / count - uc
        pooled_ref[...] = pooled.astype(BF16)
        out_ref[...] = _dot(pooled, pw_ref[...], "NN") * ps_ref[...]

    ospec = pl.BlockSpec((tb, C), lambda g, i: (i, g))
    return pl.pallas_call(
        kern, name="pool_fwd", grid=(n_g, T // tb),
        in_specs=[pl.BlockSpec((tb, C), lambda g, i: (i, ucol + g)),
                  pl.BlockSpec((tb, C), lambda g, i: (jnp.maximum(i - 1, 0), ucol + g)),
                  pl.BlockSpec((None, C, C), lambda g, i: (g, 0, 0)),
                  pl.BlockSpec((1, C), lambda g, i: (0, g))],
        out_specs=[ospec, ospec],
        out_shape=[jax.ShapeDtypeStruct((T, n_g * C), BF16), jax.ShapeDtypeStruct((T, n_g * C), F32)],
        compiler_params=_params(),
    )(proj, proj, pw, ps)


def _pool_bwd(dmix_in, dcol, pooled, pw, ps, T, tb):
    C = POOL_GROUP_DIM
    n_g = len(POOL_WINDOWS)
    nb = T // tb

    def kern(dc_ref, dn_ref, pooled_ref, pw_ref, ps_ref, du_ref, dpw_ref, dps_ref):
        g, i = pl.program_id(0), pl.program_id(1)
        dc = dc_ref[...]
        scale = ps_ref[...]
        t2 = i * tb + lax.broadcasted_iota(jnp.int32, (2 * tb, C), 0)
        d2 = jnp.where(t2 < T, jnp.concatenate([dc, dn_ref[...]], axis=0) * scale, 0.0)
        dpooled2 = _dot(d2, pw_ref[...], "NT")
        count = jnp.minimum(t2 + 1, _group_window(g)).astype(F32)
        sums = _window_select(dpooled2 / count, g, lambda s, k: pltpu.roll(s, 2 * tb - k, 0))
        du_ref[...] = (sums[:tb, :] - dpooled2[:tb, :]).astype(BF16)
        pooled = pooled_ref[...]
        p = _dot(pooled, pw_ref[...], "NN")
        dps = jnp.sum(dc * p, axis=0, keepdims=True)
        dpw = _dot(pooled, d2[:tb, :], "TN")

        @pl.when(i == 0)
        def _():
            dps_ref[...] = dps
            dpw_ref[...] = dpw

        @pl.when(i > 0)
        def _():
            dps_ref[...] += dps
            dpw_ref[...] += dpw

    return pl.pallas_call(
        kern, name="pool_bwd", grid=(n_g, nb),
        in_specs=[pl.BlockSpec((tb, C), lambda g, i: (i, dcol + g)),
                  pl.BlockSpec((tb, C), lambda g, i: (jnp.minimum(i + 1, nb - 1), dcol + g)),
                  pl.BlockSpec((tb, C), lambda g, i: (i, g)),
                  pl.BlockSpec((None, C, C), lambda g, i: (g, 0, 0)),
                  pl.BlockSpec((1, C), lambda g, i: (0, g))],
        out_specs=[pl.BlockSpec((tb, C), lambda g, i: (i, g)),
                   pl.BlockSpec((None, C, C), lambda g, i: (g, 0, 0)),
                   pl.BlockSpec((1, C), lambda g, i: (0, g))],
        out_shape=[jax.ShapeDtypeStruct((T, n_g * C), BF16), jax.ShapeDtypeStruct((n_g, C, C), F32),
                   jax.ShapeDtypeStruct((1, n_g * C), F32)],
        compiler_params=_params(),
    )(dmix_in, dmix_in, pooled, pw, ps)


D_QKV = 3 * N_HEADS * HEAD_DIM
D_U = len(POOL_WINDOWS) * POOL_GROUP_DIM
F_PAD = 128
D_PROJ = D_QKV + D_U + F_PAD


def _perm_w_in(w):
    pad = jnp.zeros((w.shape[0], F_PAD - N_HEADS), w.dtype)
    return jnp.concatenate([w[:, :D_QKV], w[:, D_QKV + N_HEADS:], w[:, D_QKV:D_QKV + N_HEADS], pad], axis=1)


def _unperm_w_in(w):
    return jnp.concatenate([w[:, :D_QKV], w[:, D_QKV + D_U:D_QKV + D_U + N_HEADS], w[:, D_QKV:D_QKV + D_U]], axis=1)


def _mixer_fwd(x, norm_g, sh, sc, gate, w_in_p, b_pad, gq, gk, late_weights, ps, T, proj_comm, attn_comm):
    tb = min(256, T)
    blk = min(512, T)
    hm = _norm_mod_fwd("mix_norm_fwd", x, norm_g, sc, sh, T, tb)
    proj, got_proj = _mm("mix_proj", hm, w_in_p, "NN", F32, 512, D_PROJ // 3, 2048, comm=proj_comm)
    pw, w_out = late_weights(got_proj)
    qn, kn, vb = _qknorm_fwd(proj, gq, gk, T, tb)
    fcol = (D_QKV + D_U) // 128
    f_tm = _fgate_fwd(proj, fcol, b_pad, T)
    f_hm = f_tm[:, :N_HEADS].T
    (o, lse), got = _attn_fwd(qn, kn, vb, f_tm, f_hm, T, blk, comm=attn_comm)
    pooled, pool_o = _pool_fwd(proj, D_QKV // POOL_GROUP_DIM, pw, ps, T, tb)
    mix_in = jnp.concatenate([o.astype(BF16), pool_o.astype(BF16)], axis=1)
    mix = _mm("mix_out", mix_in, w_out, "NN", F32, 512, 2048, 2048)
    x_out = _residual("mix_res", x, mix, gate, 1.0, T, tb)
    return x_out, (x, hm, proj, qn, kn, vb, f_tm, f_hm, o, lse, pooled, mix_in, mix), pw, w_out, got


def _mixer_bwd(dx_out, saved, norm_g, sc, gate, w_in_p, b_pad, gq, gk, pw, ps, w_out, T, core, ride_sums):
    x, hm, proj, qn, kn, vb, f_tm, f_hm, o, lse, pooled, mix_in, mix = saved
    tb = min(256, T)
    blk = min(512, T)
    W = N_HEADS * HEAD_DIM
    D = x.shape[1]
    n_g = len(POOL_WINDOWS)
    dmix, dgate = _residual_bwd("mix_res_bwd", dx_out, mix, gate, 1.0, T, tb)
    dmix_in = _mm("mix_out_bwd", dmix, w_out, "NT", F32, 512, 2048, 2048)
    dw_out = _mm("mix_dw_out", mix_in, dmix, "TN", BF16, 512, 1024, T)
    delta = _attn_delta(o, dmix_in, T, tb)
    (dqn, dfq, dkn, dv, dfk), ride_got = _attn_bwd(qn, kn, vb, dmix_in, lse, delta, f_tm, f_hm, T, blk,
                                                   comm=_chip_comm(ride_sums))
    dq, dk, dgq, dgk = _qknorm_bwd(proj, dqn, dkn, gq, gk, T, tb)
    dF = jnp.pad(dfq[:, ::HEAD_DIM] + dfk.reshape(N_HEADS, T).T, ((0, 0), (0, F_PAD - N_HEADS)))
    fcol = (D_QKV + D_U) // 128
    dfl, dbf = _fgate_bwd(proj, fcol, b_pad, dF, T)
    du, dpw, dps = _pool_bwd(dmix_in, W // POOL_GROUP_DIM, pooled, pw, ps, T, tb)
    dproj = jnp.concatenate([dq, dk, dv.astype(BF16), du, dfl.astype(BF16)], axis=1)
    dw_in_p = _mm("mix_dw_in", hm, dproj, "TN", BF16, 512, D_PROJ // 3, T)
    pw_rows = POOL_GROUP_DIM // N_DEV
    slabs = [jnp.transpose(_unperm_w_in(dw_in_p).reshape(D, N_DEV, -1), (1, 0, 2)),
             jnp.transpose(dpw.astype(BF16).reshape(n_g, N_DEV, pw_rows, POOL_GROUP_DIM),
                           (1, 0, 2, 3)).reshape(N_DEV, n_g * pw_rows, POOL_GROUP_DIM),
             dw_out.reshape(N_DEV, -1, D)]
    _, sums = _reduce_level1("mix", slabs, core, [TILE_W_IN, TILE_POOL, TILE_MIX_OUT])
    dhm = _mm("mix_proj_bwd", dproj, w_in_p, "NT", F32, 512, 512, D_PROJ)
    dx, dsh, dsc, dng = _norm_mod_bwd("mix_norm_bwd", x, dhm, dx_out, norm_g, sc, T, tb)
    return dx, (dsh, dsc, dgate, dng), sums, dps, dgq, dgk, dbf, ride_got


def kernel(x, c, w_ada, b_ada, ffn1_norm_g, ffn1_w_in, ffn1_w_out, mix_norm_g, w_in, b_forget, q_norm_g, k_norm_g, pool_w, pool_scale, w_out, ffn2_norm_g, ffn2_w_in, ffn2_w_out, final_norm_g, loss_target, m_w_ada, m_b_ada, m_ffn1_norm_g, m_ffn1_w_in, m_ffn1_w_out, m_mix_norm_g, m_w_in, m_b_forget, m_q_norm_g, m_k_norm_g, m_pool_w, m_pool_scale, m_w_out, m_ffn2_norm_g, m_ffn2_w_in, m_ffn2_w_out, m_final_norm_g, v_w_ada, v_b_ada, v_ffn1_norm_g, v_ffn1_w_in, v_ffn1_w_out, v_mix_norm_g, v_w_in, v_b_forget, v_q_norm_g, v_k_norm_g, v_pool_w, v_pool_scale, v_w_out, v_ffn2_norm_g, v_ffn2_w_in, v_ffn2_w_out, v_final_norm_g):
    T, D = x.shape[1], x.shape[2]
    mx, my, mc = _mesh_pos()
    me = _flat(mx, my, mc)
    x0 = x[0]
    tgt = loss_target[0]
    tb = min(256, T)

    core = jnp.reshape(mc, (1,)).astype(jnp.int32)
    half = N_DEV // 2
    n_g = len(POOL_WINDOWS)
    pw_rows = POOL_GROUP_DIM // N_DEV

    def bf(w):
        return w.astype(BF16)

    n_loc = w_ada.shape[2]
    c_all = _standalone("gather_c", _gather_comm([c.reshape(8, D // 8)]))[0].reshape(N_DEV, D)
    b_loc = lax.dynamic_slice_in_dim(b_ada, me * n_loc, n_loc, axis=1)
    mod_loc = _ada_fwd(c_all, w_ada[0], b_loc, n_loc // 3)
    mod_all = _standalone("gather_mod", _gather_comm([mod_loc]))[0]
    mod = lax.dynamic_index_in_dim(mod_all, me, axis=1, keepdims=False).reshape(N_MOD, 1, D)
    sh1, sc1, g1, sh2, sc2, g2, sh3, sc3, g3 = [mod[k] for k in range(N_MOD)]
    b_pad = jnp.pad(b_forget, ((0, 0), (0, F_PAD - N_HEADS)))
    ps = pool_scale

    wi1 = _standalone("gather_ffn1_w_in", _gather_comm([bf(ffn1_w_in[0])]))[0]
    x1, sv1, wo1, (w_in_g,) = _ffn_fwd(
        "ffn1", x0, ffn1_norm_g, sh1, sc1, g1, wi1, lambda got: got[0].reshape(half, -1, D), T,
        up_comm=_gather_comm([bf(ffn1_w_out[0])], forward_at=0.7), down_comm=_gather_comm([bf(w_in[0])], forward_at=0.8))
    w_in_p = _perm_w_in(jnp.transpose(w_in_g, (1, 0, 2)).reshape(D, -1))

    def late_weights(got):
        pool_g, w_out_g = got
        pw = jnp.transpose(pool_g.reshape(N_DEV, n_g, pw_rows, POOL_GROUP_DIM),
                           (1, 0, 2, 3)).reshape(n_g, POOL_GROUP_DIM, POOL_GROUP_DIM)
        return pw, w_out_g.reshape(-1, D)

    x2, svm, pw_full, w_out_full, (wi2, wo2_g) = _mixer_fwd(
        x1, mix_norm_g, sh2, sc2, g2, w_in_p, b_pad, q_norm_g, k_norm_g, late_weights, ps, T,
        proj_comm=_gather_comm([bf(pool_w[0].reshape(-1, POOL_GROUP_DIM)), bf(w_out[0])], forward_at=0.6),
        attn_comm=_gather_comm([bf(ffn2_w_in[0]), bf(ffn2_w_out[0])], forward_at=0.8))
    x3, sv2, wo2, _ = _ffn_fwd("ffn2", x2, ffn2_norm_g, sh3, sc3, g3, wi2,
                               lambda got: wo2_g.reshape(half, -1, D), T)
    dx3, dgf, loss_l = _final_loss(x3, tgt, final_norm_g.reshape(1, D), T, tb)
    loss = lax.psum(loss_l[0, 0], ("x", "y", "c"))

    dx2, (dsh3, dsc3, dg3, dn3), dwi2_sum, dwo2, _ = _ffn_bwd(
        "ffn2", dx3, sv2, ffn2_norm_g, sc3, g3, wi2, wo2, T, core, defer_dwi=True)
    dx1, (dsh2, dsc2, dg2, dn2), mix_sums, dps, dgq, dgk, dbf, (dwi2,) = _mixer_bwd(
        dx2, svm, mix_norm_g, sc2, g2, w_in_p, b_pad, q_norm_g, k_norm_g, pw_full, ps, w_out_full, T, core,
        ride_sums=[dwi2_sum])
    dx0, (dsh1, dsc1, dg1, dn1), dwi1, dwo1, (dw_in_r, dpw_r, dw_out_r) = _ffn_bwd(
        "ffn1", dx1, sv1, ffn1_norm_g, sc1, g1, wi1, wo1, T, core, ride_sums=mix_sums)

    received = dict(ffn1_w_in=dwi1, ffn1_w_out=dwo1, w_in=dw_in_r, pool_w=dpw_r, w_out=dw_out_r,
                    ffn2_w_in=dwi2, ffn2_w_out=dwo2)
    moments = dict(ffn1_w_in=(m_ffn1_w_in, v_ffn1_w_in), ffn1_w_out=(m_ffn1_w_out, v_ffn1_w_out),
                   w_in=(m_w_in, v_w_in), pool_w=(m_pool_w, v_pool_w), w_out=(m_w_out, v_w_out),
                   ffn2_w_in=(m_ffn2_w_in, v_ffn2_w_in), ffn2_w_out=(m_ffn2_w_out, v_ffn2_w_out))
    weights = dict(ffn1_w_in=ffn1_w_in, ffn1_w_out=ffn1_w_out, w_in=w_in, pool_w=pool_w, w_out=w_out,
                   ffn2_w_in=ffn2_w_in, ffn2_w_out=ffn2_w_out)
    row_tiles = dict(ffn1_w_in=TILE_W_IN, ffn1_w_out=TILE_W_OUT, w_in=TILE_W_IN, pool_w=TILE_POOL,
                     w_out=TILE_MIX_OUT, ffn2_w_in=TILE_W_IN, ffn2_w_out=TILE_W_OUT)
    results = {}
    for k in received:
        shape = weights[k].shape
        two_d = received[k].shape[1:]
        mk, vk = moments[k]
        outs = _adamw("adamw_" + k, received[k], weights[k].reshape(two_d), mk.reshape(two_d),
                      vk.reshape(two_d), row_tiles[k][0])
        results[k] = [o.reshape(shape) for o in outs]

    dmod = jnp.concatenate([dsh1, dsc1, dg1, dsh2, dsc2, dg2, dsh3, dsc3, dg3], axis=1)
    small_names = ["b_ada", "ffn1_norm_g", "mix_norm_g", "ffn2_norm_g", "final_norm_g", "b_forget",
                   "q_norm_g", "k_norm_g", "pool_scale"]
    small_w = dict(b_ada=b_ada, ffn1_norm_g=ffn1_norm_g, mix_norm_g=mix_norm_g, ffn2_norm_g=ffn2_norm_g,
                   final_norm_g=final_norm_g, b_forget=b_forget, q_norm_g=q_norm_g, k_norm_g=k_norm_g,
                   pool_scale=pool_scale)
    small_m = dict(b_ada=m_b_ada, ffn1_norm_g=m_ffn1_norm_g, mix_norm_g=m_mix_norm_g, ffn2_norm_g=m_ffn2_norm_g,
                   final_norm_g=m_final_norm_g, b_forget=m_b_forget, q_norm_g=m_q_norm_g, k_norm_g=m_k_norm_g,
                   pool_scale=m_pool_scale)
    small_v = dict(b_ada=v_b_ada, ffn1_norm_g=v_ffn1_norm_g, mix_norm_g=v_mix_norm_g, ffn2_norm_g=v_ffn2_norm_g,
                   final_norm_g=v_final_norm_g, b_forget=v_b_forget, q_norm_g=v_q_norm_g, k_norm_g=v_k_norm_g,
                   pool_scale=v_pool_scale)
    small_g = dict(b_ada=dmod, ffn1_norm_g=dn1, mix_norm_g=dn2, ffn2_norm_g=dn3, final_norm_g=dgf,
                   b_forget=dbf[:, :N_HEADS], q_norm_g=dgq, k_norm_g=dgk, pool_scale=dps)
    sizes = [small_w[k].size for k in small_names]
    total = sum(sizes)
    lanes = 8 * 128
    padded = -(-total // lanes) * lanes

    def pack(d):
        flat = jnp.concatenate([d[k].reshape(-1) for k in small_names])
        return jnp.pad(flat, (0, padded - total)).reshape(8, padded // 8)

    small_parts = _standalone("gather_small_grads", _gather_comm([pack(small_g)]))[0]
    s_outs = _adamw("adamw_small", small_parts, pack(small_w), pack(small_m), pack(small_v), 8)
    offs = [0]
    for s in sizes:
        offs.append(offs[-1] + s)
    for idx, k in enumerate(small_names):
        results[k] = [o.reshape(-1)[offs[idx]:offs[idx + 1]].reshape(small_w[k].shape) for o in s_outs]

    dmod_all = small_parts.reshape(N_DEV, padded)[:, :N_MOD * D]
    dmod_loc = lax.dynamic_slice_in_dim(dmod_all, me * n_loc, n_loc, axis=1)
    g_ada = _ada_bwd(c_all, dmod_loc, n_loc // 3)
    a_outs = _adamw("adamw_w_ada", g_ada[None], w_ada[0], m_w_ada[0], v_w_ada[0], 128)
    results["w_ada"] = [o.reshape(w_ada.shape) for o in a_outs]

    order = ["w_ada", "b_ada", "ffn1_norm_g", "ffn1_w_in", "ffn1_w_out", "mix_norm_g", "w_in", "b_forget",
             "q_norm_g", "k_norm_g", "pool_w", "pool_scale", "w_out", "ffn2_norm_g", "ffn2_w_in", "ffn2_w_out",
             "final_norm_g"]
    out = [loss, dx0[None]]
    for part in range(4):
        out += [results[k][part] for k in order]
    return tuple(out)
```

```python
import jax
import jax.numpy as jnp
from jax import lax
from jax.experimental import pallas as pl
from jax.experimental.pallas import tpu as pltpu

F32 = jnp.float32
BF16 = jnp.bfloat16
MESH = pl.DeviceIdType.MESH
ANY = pl.BlockSpec(memory_space=pl.ANY)

N_DEV = 8
EPS = 1e-6
HEAD_DIM = 128
N_HEADS = 8
POOL_WINDOWS = (2, 4, 8, 16)
POOL_GROUP_DIM = 256
N_MOD = 9
ADAM_LR = 0.001
ADAM_B1 = 0.9
ADAM_B2 = 0.999
ADAM_EPS = 1e-08
ADAM_WD = 0.01
ADAM_STEP = 10
NEG = -1e30
VMEM_LIMIT_V7X = 56 * 1024 * 1024


def _params():
    return pltpu.CompilerParams(vmem_limit_bytes=VMEM_LIMIT_V7X)


def _sigmoid(z):
    return 1.0 / (1.0 + jnp.exp(-z))


def _rstd(x):
    return lax.rsqrt(jnp.mean(x * x, axis=-1, keepdims=True) + EPS)


def _mesh_pos():
    return lax.axis_index("x"), lax.axis_index("y"), lax.axis_index("c")


def _flat(px, py, pc):
    return 4 * px + 2 * py + pc


class _Comm:
    def __init__(self, ins, outs, sems, phases):
        self.ins, self.outs, self.sems, self.phases = list(ins), list(outs), list(sems), list(phases)


def _pallas(kern, *, comm=None, **kw):
    if comm is None:
        return pl.pallas_call(kern, **kw)
    grid = tuple(kw["grid"])
    single = not isinstance(kw["out_shape"], (list, tuple))
    out_shape = [kw["out_shape"]] if single else list(kw["out_shape"])
    out_specs = [kw["out_specs"]] if single else list(kw["out_specs"])
    in_specs = list(kw["in_specs"])
    scratch = list(kw.get("scratch_shapes", ()))
    n_in, n_out, n_scr = len(in_specs), len(out_shape), len(scratch)
    n_ci, n_co = len(comm.ins), len(comm.outs)
    strides, n_steps = [], 1
    for g in reversed(grid):
        strides.insert(0, n_steps)
        n_steps *= g

    def wrapped(*refs):
        ins, cins = refs[:n_in], refs[n_in:n_in + n_ci]
        base = n_in + n_ci
        outs, couts = refs[base:base + n_out], refs[base + n_out:base + n_out + n_co]
        base += n_out + n_co
        scr, sems = refs[base:base + n_scr], refs[base + n_scr:]
        step = sum(pl.program_id(d) * strides[d] for d in range(len(grid)))
        for frac, fn in comm.phases:
            if frac < 1.0:
                pl.when(step == int(round(frac * (n_steps - 1))))(lambda fn=fn: fn(cins, couts, sems))
        kern(*ins, *outs, *scr)
        for frac, fn in comm.phases:
            if frac >= 1.0:
                pl.when(step == n_steps - 1)(lambda fn=fn: fn(cins, couts, sems))

    kw = dict(kw, in_specs=in_specs + [ANY] * n_ci, out_specs=out_specs + [ANY] * n_co,
              out_shape=out_shape + comm.outs, scratch_shapes=scratch + comm.sems)
    call = pl.pallas_call(wrapped, **kw)

    def run(*args):
        res = call(*args, *comm.ins)
        main = res[0] if single else list(res[:n_out])
        return main, list(res[n_out:])

    return run


def _join(first, second):
    n_i, n_o, n_s = len(first.ins), len(first.outs), len(first.sems)

    def left(fn):
        return lambda ins, outs, sems: fn(ins[:n_i], outs[:n_o], sems[:n_s])

    def right(fn):
        return lambda ins, outs, sems: fn(ins[n_i:], outs[n_o:], sems[n_s:])

    phases = [(f, left(fn)) for f, fn in first.phases] + [(f, right(fn)) for f, fn in second.phases]
    return _Comm(first.ins + second.ins, first.outs + second.outs, first.sems + second.sems, phases)


def _hosted(comm, res):
    return res if comm is not None else (res, [])


def _standalone(name, comm):
    def kern():
        pass

    return _pallas(kern, comm=comm, name=name, grid=(1,), in_specs=[], out_specs=[], out_shape=[])()[1]


def _dma_sems(*shapes):
    return [pltpu.SemaphoreType.DMA(s) for s in shapes]


def _gather_comm(arrs, forward_at=0.5):
    n = len(arrs)

    def setup(outs, sems):
        send_sems, recv_sems, _ = sems
        x, y, c = _mesh_pos()
        chips = [(1 - x, y), (x, 1 - y), (1 - x, 1 - y)]

        def copy(a, k, block, to, src=None):
            dst = outs[a].at[_flat(*block)]
            return pltpu.make_async_remote_copy(
                src_ref=dst if src is None else src, dst_ref=dst,
                send_sem=send_sems.at[a, k], recv_sem=recv_sems.at[a, k],
                device_id=to, device_id_type=MESH)

        return (x, y, c), (x, y, 1 - c), chips, copy

    def local(ins, outs, sems, a, me):
        return pltpu.make_async_copy(ins[a], outs[a].at[_flat(*me)], sems[2].at[a])

    def send_own(ins, outs, sems):
        me, sibling, chips, copy = setup(outs, sems)
        for a in range(n):
            local(ins, outs, sems, a, me).start()
            copy(a, 0, me, sibling, src=ins[a]).start()
            for j, chip in enumerate(chips):
                copy(a, 1 + j, me, (*chip, me[2]), src=ins[a]).start()

    def forward(ins, outs, sems):
        me, sibling, chips, copy = setup(outs, sems)
        for a in range(n):
            for j, chip in enumerate(chips):
                copy(a, 1 + j, (*chip, me[2]), me).wait_recv()
                copy(a, 4 + j, (*chip, me[2]), sibling).start()

    def finish(ins, outs, sems):
        me, sibling, chips, copy = setup(outs, sems)
        for a in range(n):
            copy(a, 0, sibling, me).wait_recv()
            for j, chip in enumerate(chips):
                copy(a, 4 + j, (*chip, 1 - me[2]), me).wait_recv()
        for a in range(n):
            copy(a, 0, me, sibling, src=ins[a]).wait_send()
            for j, chip in enumerate(chips):
                copy(a, 1 + j, me, (*chip, me[2]), src=ins[a]).wait_send()
                copy(a, 4 + j, (*chip, me[2]), sibling).wait_send()
            local(ins, outs, sems, a, me).wait()

    return _Comm(arrs, [jax.ShapeDtypeStruct((N_DEV,) + a.shape, a.dtype) for a in arrs],
                 _dma_sems((n, 7), (n, 7), (n,)), [(0.0, send_own), (forward_at, forward), (1.0, finish)])


CHIPS = [(0, 0), (0, 1), (1, 0), (1, 1)]


def _sibling_comm(parts):
    n = len(parts)

    def copies(ins, outs, sems):
        x, y, c = _mesh_pos()
        return [pltpu.make_async_remote_copy(
                    src_ref=ins[a].at[_flat(qx, qy, 1 - c)], dst_ref=outs[a].at[q],
                    send_sem=sems[0].at[a, q], recv_sem=sems[1].at[a, q],
                    device_id=(x, y, 1 - c), device_id_type=MESH)
                for a in range(n) for q, (qx, qy) in enumerate(CHIPS)]

    def start(ins, outs, sems):
        for cp in copies(ins, outs, sems):
            cp.start()

    def finish(ins, outs, sems):
        for cp in copies(ins, outs, sems):
            cp.wait_recv()
        for cp in copies(ins, outs, sems):
            cp.wait_send()

    return _Comm(parts, [jax.ShapeDtypeStruct((4,) + p.shape[1:], p.dtype) for p in parts],
                 _dma_sems((n, 4), (n, 4)), [(0.0, start), (1.0, finish)])


def _chip_comm(sums):
    n = len(sums)
    flips = [(1, 0), (0, 1), (1, 1)]

    def own(ins, outs, sems):
        mine = 2 * lax.axis_index("x") + lax.axis_index("y")
        return [pltpu.make_async_copy(ins[a].at[mine], outs[a].at[mine], sems[2].at[a]) for a in range(n)]

    def copies(ins, outs, sems, arriving=False):
        x, y, c = _mesh_pos()
        mine = 2 * x + y
        remote = []
        for a in range(n):
            for k, (fx, fy) in enumerate(flips):
                qx, qy = x ^ fx, y ^ fy
                q = 2 * qx + qy
                remote.append(pltpu.make_async_remote_copy(
                    src_ref=ins[a].at[q], dst_ref=outs[a].at[q if arriving else mine],
                    send_sem=sems[0].at[a, k], recv_sem=sems[1].at[a, k],
                    device_id=(qx, qy, c), device_id_type=MESH))
        return remote

    def start(ins, outs, sems):
        for cp in own(ins, outs, sems) + copies(ins, outs, sems):
            cp.start()

    def finish(ins, outs, sems):
        for cp in copies(ins, outs, sems, arriving=True):
            cp.wait_recv()
        for cp in copies(ins, outs, sems):
            cp.wait_send()
        for cp in own(ins, outs, sems):
            cp.wait()

    return _Comm(sums, [jax.ShapeDtypeStruct(s.shape, s.dtype) for s in sums],
                 _dma_sems((n, 3), (n, 3), (n,)), [(0.0, start), (1.0, finish)])


def _pair_add(name, parts, got, core, tr):
    _, R, C = parts.shape
    assert R % tr == 0

    def kern(c_ref, p_ref, g_ref, o_ref):
        o_ref[...] = (p_ref[...].astype(F32) + g_ref[...].astype(F32)).astype(o_ref.dtype)

    blk = pl.BlockSpec((None, tr, C), lambda q, i, c_ref: (q, i, 0))
    return pl.pallas_call(
        kern, name=name,
        grid_spec=pltpu.PrefetchScalarGridSpec(
            num_scalar_prefetch=1, grid=(4, R // tr),
            in_specs=[pl.BlockSpec((None, tr, C), lambda q, i, c_ref: (2 * q + c_ref[0], i, 0)), blk],
            out_specs=blk),
        out_shape=jax.ShapeDtypeStruct((4, R, C), parts.dtype), compiler_params=_params(),
    )(core, parts, got)


def _rowwise(name, body, T, tb, rows, vecs, out_rows, out_accs):
    n_in = len(rows) + len(vecs)
    n_o, n_a = len(out_rows), len(out_accs)

    def kern(*refs):
        i = pl.program_id(0)
        res = body(*[r[...] for r in refs[:n_in]])
        if not isinstance(res, (tuple, list)):
            res = (res,)
        outs = refs[n_in:]
        for k in range(n_o):
            outs[k][...] = res[k].astype(outs[k].dtype)

        def accumulate(ref, val):
            @pl.when(i == 0)
            def _():
                ref[...] = val

            @pl.when(i > 0)
            def _():
                ref[...] += val

        for k in range(n_a):
            accumulate(outs[n_o + k], res[n_o + k])

    in_specs = [pl.BlockSpec((tb, w), lambda i, cb=cb: (i, cb)) for (_, w, cb) in rows]
    in_specs += [pl.BlockSpec((1, v.shape[1]), lambda i: (0, 0)) for v in vecs]
    out_specs = [pl.BlockSpec((tb, w), lambda i: (i, 0)) for (w, _) in out_rows]
    out_specs += [pl.BlockSpec((1, w), lambda i: (0, 0)) for w in out_accs]
    out_shape = [jax.ShapeDtypeStruct((T, w), dt) for (w, dt) in out_rows]
    out_shape += [jax.ShapeDtypeStruct((1, w), F32) for w in out_accs]
    res = pl.pallas_call(
        kern, name=name, grid=(T // tb,), in_specs=in_specs, out_specs=out_specs,
        out_shape=out_shape, compiler_params=_params(),
    )(*[r[0] for r in rows], *vecs)
    return res


def _dot(a, b, mode):
    dims = {"NN": ((1,), (0,)), "NT": ((1,), (1,)), "TN": ((0,), (0,))}[mode]
    return lax.dot_general(a.astype(BF16), b.astype(BF16), (dims, ((), ())),
                           preferred_element_type=F32)


def _mm(name, a, b, mode, out_dtype, tm, tn, tk, ga=False, gb=False, gmode=None, comm=None):
    G = (a.shape[0] if ga else b.shape[0]) if gmode else 1
    a2, b2 = a.shape[-2:], b.shape[-2:]
    if mode == "NN":
        (M, K), (_, N) = a2, b2
    elif mode == "NT":
        (M, K), (N, _) = a2, b2
    else:
        (K, M), (_, N) = a2, b2
    tm, tn, tk = min(tm, M), min(tn, N), min(tk, K)
    assert M % tm == 0 and N % tn == 0 and K % tk == 0, (name, M, N, K, tm, tn, tk)
    batch = gmode == "batch"
    n_gb, n_gs = (G if batch else 1), (G if gmode == "sum" else 1)
    nk = K // tk
    n_red = n_gs * nk

    def grp(g_b, g_s):
        return g_b if batch else g_s

    if mode == "TN":
        a_blk, a_idx = (tk, tm), lambda g_b, mi, ni, g_s, ki: (ki, mi)
    else:
        a_blk, a_idx = (tm, tk), lambda g_b, mi, ni, g_s, ki: (mi, ki)
    if mode == "NT":
        b_blk, b_idx = (tn, tk), lambda g_b, mi, ni, g_s, ki: (ni, ki)
    else:
        b_blk, b_idx = (tk, tn), lambda g_b, mi, ni, g_s, ki: (ki, ni)

    def with_group(blk, idx, has_group):
        if not has_group:
            return pl.BlockSpec(blk, idx)
        return pl.BlockSpec((None,) + blk, lambda g_b, mi, ni, g_s, ki: (grp(g_b, g_s),) + idx(g_b, mi, ni, g_s, ki))

    o_blk, o_idx = (tm, tn), lambda g_b, mi, ni, g_s, ki: (mi, ni)
    o_spec = with_group(o_blk, o_idx, batch)
    o_shape = ((G,) if batch else ()) + (M, N)

    def kern(a_ref, b_ref, o_ref, *scratch):
        part = _dot(a_ref[...], b_ref[...], mode)
        if n_red == 1:
            o_ref[...] = part.astype(o_ref.dtype)
            return
        acc = scratch[0]
        step = pl.program_id(3) * nk + pl.program_id(4)

        @pl.when(step == 0)
        def _():
            acc[...] = part

        @pl.when(step > 0)
        def _():
            acc[...] += part

        @pl.when(step == n_red - 1)
        def _():
            o_ref[...] = acc[...].astype(o_ref.dtype)

    return _pallas(
        kern, comm=comm, name=name, grid=(n_gb, M // tm, N // tn, n_gs, nk),
        in_specs=[with_group(a_blk, a_idx, ga), with_group(b_blk, b_idx, gb)],
        out_specs=o_spec, out_shape=jax.ShapeDtypeStruct(o_shape, out_dtype),
        scratch_shapes=[] if n_red == 1 else [pltpu.VMEM((tm, tn), F32)],
        compiler_params=_params(),
    )(a, b)


def _mm_groups(name, a, b, mode, tm, tn, residual=None, comm=None):
    G, M, K = a.shape
    N = b.shape[2] if mode == "NN" else b.shape[1]
    tm, tn = min(tm, M), min(tn, N)
    assert M % tm == 0 and N % tn == 0

    def kern(a_ref, b_ref, *rest):
        acc = _dot(a_ref[0], b_ref[0], mode)
        for g in range(1, G):
            acc = acc + _dot(a_ref[g], b_ref[g], mode)
        if residual is None:
            rest[0][...] = acc
        else:
            x_ref, g_ref, f_ref, o_ref = rest
            f_ref[...] = acc
            o_ref[...] = x_ref[...] + (residual[2] * g_ref[...]) * acc

    b_spec = (pl.BlockSpec((G, K, tn), lambda ni, mi: (0, 0, ni)) if mode == "NN"
              else pl.BlockSpec((G, tn, K), lambda ni, mi: (0, ni, 0)))
    o_spec = pl.BlockSpec((tm, tn), lambda ni, mi: (mi, ni))
    in_specs = [pl.BlockSpec((G, tm, K), lambda ni, mi: (0, mi, 0)), b_spec]
    args = [a, b]
    out = jax.ShapeDtypeStruct((M, N), F32)
    if residual is not None:
        in_specs += [o_spec, pl.BlockSpec((1, tn), lambda ni, mi: (0, ni))]
        args += [residual[0], residual[1]]
    return _pallas(
        kern, comm=comm, name=name, grid=(N // tn, M // tm), in_specs=in_specs,
        out_specs=o_spec if residual is None else [o_spec, o_spec],
        out_shape=out if residual is None else [out, out], compiler_params=_params(),
    )(*args)


def _adamw(name, parts, w, m, v, tr):
    G, R, C = parts.shape
    assert R % tr == 0
    bc1 = 1.0 - ADAM_B1 ** ADAM_STEP
    bc2 = 1.0 - ADAM_B2 ** ADAM_STEP

    def kern(p_ref, w_ref, m_ref, v_ref, g_out, d_out, m_out, v_out):
        g = p_ref[0].astype(F32)
        for s in range(1, G):
            g = g + p_ref[s].astype(F32)
        m2 = ADAM_B1 * m_ref[...] + (1.0 - ADAM_B1) * g
        v2 = ADAM_B2 * v_ref[...] + (1.0 - ADAM_B2) * (g * g)
        m_hat = m2 / bc1
        v_hat = v2 / bc2
        g_out[...] = g
        d_out[...] = -ADAM_LR * (m_hat / (jnp.sqrt(v_hat) + ADAM_EPS) + ADAM_WD * w_ref[...])
        m_out[...] = m2
        v_out[...] = v2

    blk = pl.BlockSpec((tr, C), lambda i: (i, 0))
    return pl.pallas_call(
        kern, name=name, grid=(R // tr,),
        in_specs=[pl.BlockSpec((G, tr, C), lambda i: (0, i, 0)), blk, blk, blk],
        out_specs=[blk] * 4, out_shape=[jax.ShapeDtypeStruct((R, C), F32)] * 4,
        compiler_params=_params(),
    )(parts, w, m, v)


def _ada_fwd(c_all, w_loc, b_loc, tn):
    B, D = c_all.shape
    N = w_loc.shape[1]

    def kern(c_ref, w_ref, b_ref, o_ref):
        cc = c_ref[...]
        act = cc * _sigmoid(cc)
        o_ref[...] = _dot(act, w_ref[...], "NN") + b_ref[...]

    return pl.pallas_call(
        kern, name="ada_fwd", grid=(N // tn,),
        in_specs=[pl.BlockSpec((B, D), lambda j: (0, 0)), pl.BlockSpec((D, tn), lambda j: (0, j)),
                  pl.BlockSpec((1, tn), lambda j: (0, j))],
        out_specs=pl.BlockSpec((B, tn), lambda j: (0, j)),
        out_shape=jax.ShapeDtypeStruct((B, N), F32), compiler_params=_params(),
    )(c_all, w_loc, b_loc)


def _ada_bwd(c_all, dmod_loc, tn):
    B, D = c_all.shape
    N = dmod_loc.shape[1]

    def kern(c_ref, d_ref, o_ref):
        cc = c_ref[...]
        act = cc * _sigmoid(cc)
        o_ref[...] = _dot(act, d_ref[...], "TN")

    return pl.pallas_call(
        kern, name="ada_bwd", grid=(N // tn,),
        in_specs=[pl.BlockSpec((B, D), lambda j: (0, 0)), pl.BlockSpec((B, tn), lambda j: (0, j))],
        out_specs=pl.BlockSpec((D, tn), lambda j: (0, j)),
        out_shape=jax.ShapeDtypeStruct((D, N), F32), compiler_params=_params(),
    )(c_all, dmod_loc)


def _norm_mod_fwd(name, x, g, sc, sh, T, tb):
    D = x.shape[1]

    def body(xb, gb, scb, shb):
        n = (xb * _rstd(xb)) * gb
        return n * (1.0 + scb) + shb

    return _rowwise(name, body, T, tb, [(x, D, 0)], [g, sc, sh], [(D, BF16)], [])[0]


def _norm_mod_bwd(name, x, dhm, dres, g, sc, T, tb):
    D = x.shape[1]

    def body(xb, db, rb, gb, scb):
        r = _rstd(xb)
        xh = xb * r
        n = xh * gb
        dn = db * (1.0 + scb)
        dxh = dn * gb
        dx = rb + r * (dxh - xh * jnp.mean(dxh * xh, axis=-1, keepdims=True))
        return (dx, jnp.sum(db, axis=0, keepdims=True), jnp.sum(db * n, axis=0, keepdims=True),
                jnp.sum(dn * xh, axis=0, keepdims=True))

    return _rowwise(name, body, T, tb, [(x, D, 0), (dhm, D, 0), (dres, D, 0)], [g, sc],
                    [(D, F32)], [D, D, D])


def _residual_bwd(name, dx, f, gate, coef, T, tb):
    D = dx.shape[1]

    def body(db, fb, gb):
        return (coef * gb) * db, jnp.sum((coef * fb) * db, axis=0, keepdims=True)

    return _rowwise(name, body, T, tb, [(dx, D, 0), (f, D, 0)], [gate], [(D, BF16)], [D])


def _final_loss(x, tgt, g, T, tb):
    D = x.shape[1]

    def body(xb, tb_, gb):
        r = _rstd(xb)
        xh = xb * r
        err = xh * gb - tb_
        loss = 0.5 * jnp.sum(jnp.mean(err * err, axis=-1, keepdims=True), axis=0, keepdims=True)
        dy = err * (1.0 / D)
        dxh = dy * gb
        dx = r * (dxh - xh * jnp.mean(dxh * xh, axis=-1, keepdims=True))
        return dx, jnp.sum(dy * xh, axis=0, keepdims=True), jnp.broadcast_to(loss, (1, 128))

    return _rowwise("final_loss", body, T, tb, [(x, D, 0), (tgt, D, 0)], [g], [(D, F32)], [D, 128])


def _ffn_up(name, hm, wi, T, tm, comm=None):
    D = hm.shape[1]
    Ws = wi.shape[2]
    half = wi.shape[0] // 2

    n_sub = 2 if tm % 32 == 0 else 1
    subs = [pl.ds(r * (tm // n_sub), tm // n_sub) for r in range(n_sub)]

    def kern(h_ref, wa_ref, wb_ref, a_ref, b_ref, hid_ref):
        wa, wb = wa_ref[...], wb_ref[...]
        ab = [(_dot(h_ref[rows, :], wa, "NN"), _dot(h_ref[rows, :], wb, "NN")) for rows in subs]
        for rows, (a, b) in zip(subs, ab):
            a_ref[rows, :] = a
            b_ref[rows, :] = b
            hid_ref[rows, :] = ((a * _sigmoid(a)) * b).astype(BF16)

    o_spec = pl.BlockSpec((None, tm, Ws), lambda g, i: (g, i, 0))
    return _pallas(
        kern, comm=comm, name=name, grid=(half, T // tm),
        in_specs=[pl.BlockSpec((tm, D), lambda g, i: (i, 0)),
                  pl.BlockSpec((None, D, Ws), lambda g, i: (g, 0, 0)),
                  pl.BlockSpec((None, D, Ws), lambda g, i: (g + half, 0, 0))],
        out_specs=[o_spec] * 3,
        out_shape=[jax.ShapeDtypeStruct((half, T, Ws), F32)] * 2 + [jax.ShapeDtypeStruct((half, T, Ws), BF16)],
        compiler_params=_params(),
    )(hm, wi, wi)


def _ffn_down_bwd(name, df, wo, a, b, T, tm, comm=None):
    D = df.shape[1]
    half, _, Ws = a.shape

    n_sub = 2 if tm % 32 == 0 else 1
    subs = [pl.ds(r * (tm // n_sub), tm // n_sub) for r in range(n_sub)]

    def kern(df_ref, wo_ref, a_ref, b_ref, dp_ref):
        wo_blk = wo_ref[...]
        dhid = [_dot(df_ref[rows, :], wo_blk, "NT") for rows in subs]
        for rows, dh in zip(subs, dhid):
            av = a_ref[rows, :]
            s = _sigmoid(av)
            silu = av * s
            dp_ref[0, rows, :] = (dh * b_ref[rows, :] * (s + silu * (1.0 - s))).astype(BF16)
            dp_ref[1, rows, :] = (dh * silu).astype(BF16)

    act = pl.BlockSpec((None, tm, Ws), lambda g, i: (g, i, 0))
    return _pallas(
        kern, comm=comm, name=name, grid=(half, T // tm),
        in_specs=[pl.BlockSpec((tm, D), lambda g, i: (i, 0)),
                  pl.BlockSpec((None, Ws, D), lambda g, i: (g, 0, 0)), act, act],
        out_specs=pl.BlockSpec((2, None, tm, Ws), lambda g, i: (0, g, i, 0)),
        out_shape=jax.ShapeDtypeStruct((2, half, T, Ws), BF16),
        compiler_params=_params(),
    )(df, wo, a, b)


def _ffn_fwd(tag, x, norm_g, sh, sc, gate, wi, wo_of, T, up_comm=None, down_comm=None):
    tb = min(256, T)
    hm = _norm_mod_fwd(tag + "_norm_fwd", x, norm_g, sc, sh, T, tb)
    (a, b, hid), got_up = _hosted(up_comm, _ffn_up(tag + "_up", hm, wi, T, min(512, T), comm=up_comm))
    wo = wo_of(got_up)
    (f, x_out), got_down = _hosted(down_comm, _mm_groups(tag + "_down", hid, wo, "NN", 512, 512,
                                                         residual=(x, gate, 0.5), comm=down_comm))
    return x_out, (x, hm, a, b, hid, f), wo, got_down


TILE_W_IN = (128, 512)
TILE_W_OUT = (16, 688)
TILE_MIX_OUT = (64, 256)
TILE_POOL = (128, 128)


def _reduce_level1(tag, parts, core, tiles, host=None):
    comm = _sibling_comm(parts)
    if host is None:
        res, got = None, _standalone(tag + "_sibling", comm)
    else:
        res, got = host(comm)
    sums = [_pair_add("%s_pair_add%d" % (tag, k), p, g, core, min(t[1], p.shape[1]))
            for k, (p, g, t) in enumerate(zip(parts, got, tiles))]
    return res, sums


def _ffn_bwd(tag, dx_out, saved, norm_g, sc, gate, wi, wo, T, core, ride_sums=None, defer_dwi=False):
    x, hm, a, b, hid, f = saved
    tb = min(256, T)
    D = x.shape[1]
    df, dgate = _residual_bwd(tag + "_res_bwd", dx_out, f, gate, 0.5, T, tb)
    dwo = _mm(tag + "_dwo", hid, df, "TN", BF16, 2048, 512, T, ga=True, gmode="batch").reshape(N_DEV, -1, D)
    n_ride = 0 if ride_sums is None else len(ride_sums)

    def down_bwd_call(comm):
        if n_ride:
            comm = _join(comm, _chip_comm(ride_sums))
        res, got = _ffn_down_bwd(tag + "_down_bwd", df, wo, a, b, T, min(512, T), comm=comm)
        return (res, got[len(got) - n_ride:]), got[:len(got) - n_ride]

    (dproj, ride_got), (dwo_sum,) = _reduce_level1(tag + "_dwo", [dwo], core, [TILE_W_OUT], host=down_bwd_call)
    dproj = dproj.reshape((2 * dproj.shape[1],) + dproj.shape[2:])
    dwi, (dwo_got,) = _mm(tag + "_dwi", hm, dproj, "TN", BF16, 512, 2048, T, gb=True, gmode="batch",
                          comm=_chip_comm([dwo_sum]))

    def dhm_call(comm):
        return _mm_groups(tag + "_dhm", dproj, wi, "NT", 512, 512, comm=comm)

    if defer_dwi:
        dhm, (dwi_out,) = _reduce_level1(tag + "_dwi", [dwi], core, [TILE_W_IN], host=dhm_call)
    else:
        _, (dwi_sum,) = _reduce_level1(tag + "_dwi", [dwi], core, [TILE_W_IN])
        dhm, (dwi_out,) = dhm_call(_chip_comm([dwi_sum]))
    dx, dsh, dsc, dng = _norm_mod_bwd(tag + "_norm_bwd", x, dhm, dx_out, norm_g, sc, T, tb)
    return dx, (dsh, dsc, dgate, dng), dwi_out, dwo_got, ride_got


def _heads(fn, *arrs):
    outs = [fn(*[a[:, h * HEAD_DIM:(h + 1) * HEAD_DIM] for a in arrs]) for h in range(N_HEADS)]
    return outs


def _qknorm_fwd(proj, gq, gk, T, tb):
    W = N_HEADS * HEAD_DIM

    def body(q, k, v, gqb, gkb):
        qn = jnp.concatenate(_heads(lambda t: (t * _rstd(t)) * gqb, q), axis=1)
        kn = jnp.concatenate(_heads(lambda t: (t * _rstd(t)) * gkb, k), axis=1)
        return qn, kn, v

    return _rowwise("qknorm_fwd", body, T, tb, [(proj, W, 0), (proj, W, 1), (proj, W, 2)], [gq, gk],
                    [(W, BF16)] * 3, [])


def _qknorm_bwd(proj, dqn, dkn, gq, gk, T, tb):
    W = N_HEADS * HEAD_DIM

    def one(t, dt, g):
        r = _rstd(t)
        th = t * r
        dth = dt * g
        d = r * (dth - th * jnp.mean(dth * th, axis=-1, keepdims=True))
        return d, jnp.sum(dt * th, axis=0, keepdims=True)

    def body(q, k, dq, dk, gqb, gkb):
        rq = _heads(lambda t, dt: one(t, dt, gqb), q, dq)
        rk = _heads(lambda t, dt: one(t, dt, gkb), k, dk)
        return (jnp.concatenate([r[0] for r in rq], axis=1), jnp.concatenate([r[0] for r in rk], axis=1),
                sum(r[1] for r in rq), sum(r[1] for r in rk))

    return _rowwise("qknorm_bwd", body, T, tb, [(proj, W, 0), (proj, W, 1), (dqn, W, 0), (dkn, W, 0)],
                    [gq, gk], [(W, BF16)] * 2, [HEAD_DIM, HEAD_DIM])


def _log_sigmoid(z):
    return jnp.minimum(z, 0.0) - jnp.log(1.0 + jnp.exp(-jnp.abs(z)))


def _fgate_fwd(proj, fcol, b_pad, T):
    nblk = T // 128

    def kern(f_ref, b_ref, o_ref):
        r = lax.broadcasted_iota(jnp.int32, (128, 128), 0)
        c = lax.broadcasted_iota(jnp.int32, (128, 128), 1)
        tri = (r >= c).astype(F32)
        carry = jnp.zeros((1, 128), F32)
        for k in range(nblk):
            rows = pl.ds(k * 128, 128)
            lf = _log_sigmoid(f_ref[rows, :] + b_ref[...])
            o_ref[rows, :] = jnp.dot(tri, lf, precision=lax.Precision.HIGHEST, preferred_element_type=F32) + carry
            carry = carry + jnp.sum(lf, axis=0, keepdims=True)

    return pl.pallas_call(
        kern, name="fgate_fwd", grid=(1,),
        in_specs=[pl.BlockSpec((T, 128), lambda i: (0, fcol)), pl.BlockSpec((1, 128), lambda i: (0, 0))],
        out_specs=pl.BlockSpec((T, 128), lambda i: (0, 0)),
        out_shape=jax.ShapeDtypeStruct((T, 128), F32), compiler_params=_params(),
    )(proj, b_pad)


def _fgate_bwd(proj, fcol, b_pad, dF, T):
    nblk = T // 128

    def kern(f_ref, b_ref, d_ref, o_ref, db_ref):
        r = lax.broadcasted_iota(jnp.int32, (128, 128), 0)
        c = lax.broadcasted_iota(jnp.int32, (128, 128), 1)
        tri = (c >= r).astype(F32)
        carry = jnp.zeros((1, 128), F32)
        db = jnp.zeros((1, 128), F32)
        for k in reversed(range(nblk)):
            rows = pl.ds(k * 128, 128)
            dblk = d_ref[rows, :]
            rc = jnp.dot(tri, dblk, precision=lax.Precision.HIGHEST, preferred_element_type=F32) + carry
            carry = carry + jnp.sum(dblk, axis=0, keepdims=True)
            z = f_ref[rows, :] + b_ref[...]
            dz = rc * (1.0 / (1.0 + jnp.exp(z)))
            o_ref[rows, :] = dz
            db = db + jnp.sum(dz, axis=0, keepdims=True)
        db_ref[...] = db

    return pl.pallas_call(
        kern, name="fgate_bwd", grid=(1,),
        in_specs=[pl.BlockSpec((T, 128), lambda i: (0, fcol)), pl.BlockSpec((1, 128), lambda i: (0, 0)),
                  pl.BlockSpec((T, 128), lambda i: (0, 0))],
        out_specs=[pl.BlockSpec((T, 128), lambda i: (0, 0)), pl.BlockSpec((1, 128), lambda i: (0, 0))],
        out_shape=[jax.ShapeDtypeStruct((T, 128), F32), jax.ShapeDtypeStruct((1, 128), F32)],
        compiler_params=_params(),
    )(proj, b_pad, dF)


LOG2E = 1.4426950408889634


def _gate_bias(ft, fh, h):
    lane = lax.broadcasted_iota(jnp.int32, ft.shape, 1)
    fq = jnp.sum(jnp.where(lane == h, ft, 0.0), axis=1, keepdims=True)
    f0 = jnp.max(fq, axis=0, keepdims=True)
    sub = lax.broadcasted_iota(jnp.int32, fh.shape, 0)
    fk = jnp.sum(jnp.where(sub == h, fh, 0.0), axis=0, keepdims=True)
    return (f0 - fk) * LOG2E


HEADS_PER_STEP = 2


def _tri_rows(s, nb):
    i = sum((s >= k * (k + 1) // 2).astype(jnp.int32) for k in range(1, nb))
    return i, s - (i * (i + 1)) // 2


def _tri_cols(s, nb):
    j = sum((s >= k * nb - (k * (k - 1)) // 2).astype(jnp.int32) for k in range(1, nb))
    return j, j + s - (j * nb - (j * (j - 1)) // 2)


def _causal_bias(blk):
    row = lax.broadcasted_iota(jnp.int32, (blk, blk), 0)
    col = lax.broadcasted_iota(jnp.int32, (blk, blk), 1)
    return jnp.where(row >= col, 0.0, NEG)


def _attn_fwd(qn, kn, vb, f_tm, f_hm, T, blk, comm=None):
    nb = T // blk
    scale = HEAD_DIM ** -0.5
    W = N_HEADS * HEAD_DIM
    G = HEADS_PER_STEP
    lanes = [slice(g * HEAD_DIM, (g + 1) * HEAD_DIM) for g in range(G)]

    def kern(q_ref, k_ref, v_ref, ft_ref, fh_ref, o_ref, lse_ref, m_scr, l_scr, acc_scr):
        hp = pl.program_id(0)
        i, j = _tri_rows(pl.program_id(1), nb)

        @pl.when(j == 0)
        def _():
            m_scr[...] = jnp.full_like(m_scr, NEG)
            l_scr[...] = jnp.zeros_like(l_scr)
            acc_scr[...] = jnp.zeros_like(acc_scr)

        def block(diagonal):
            ft, fh = ft_ref[...], fh_ref[...]
            s = [_dot(q_ref[:, sl], k_ref[:, sl], "NT") * (scale * LOG2E) + _gate_bias(ft, fh, hp * G + g)
                 for g, sl in enumerate(lanes)]
            if diagonal:
                mask = _causal_bias(blk)
                s = [sg + mask for sg in s]
            m_prev = [m_scr[g] for g in range(G)]
            m_new = [jnp.maximum(mp, jnp.max(sg, axis=1, keepdims=True)) for mp, sg in zip(m_prev, s)]
            alpha = [jnp.exp2(mp - mn) for mp, mn in zip(m_prev, m_new)]
            p = [jnp.exp2(sg - mn) for sg, mn in zip(s, m_new)]
            for g, sl in enumerate(lanes):
                l_scr[g] = alpha[g] * l_scr[g] + jnp.sum(p[g], axis=1, keepdims=True)
                acc_scr[:, sl] = alpha[g] * acc_scr[:, sl] + _dot(p[g], v_ref[:, sl], "NN")
                m_scr[g] = m_new[g]

        @pl.when(j < i)
        def _():
            block(False)

        @pl.when(j == i)
        def _():
            block(True)
            for g, sl in enumerate(lanes):
                l = l_scr[g]
                o_ref[:, sl] = acc_scr[:, sl] / l
                lse_ref[:, sl] = jnp.broadcast_to(m_scr[g] + jnp.log2(l), (blk, HEAD_DIM))

    qspec = pl.BlockSpec((blk, G * HEAD_DIM), lambda h, s: (_tri_rows(s, nb)[0], h))
    kspec = pl.BlockSpec((blk, G * HEAD_DIM), lambda h, s: (_tri_rows(s, nb)[1], h))
    return _pallas(
        kern, comm=comm, name="attn_fwd", grid=(N_HEADS // G, nb * (nb + 1) // 2),
        in_specs=[qspec, kspec, kspec,
                  pl.BlockSpec((blk, 128), lambda h, s: (_tri_rows(s, nb)[0], 0)),
                  pl.BlockSpec((N_HEADS, blk), lambda h, s: (0, _tri_rows(s, nb)[1]))],
        out_specs=[qspec, qspec],
        out_shape=[jax.ShapeDtypeStruct((T, W), F32)] * 2,
        scratch_shapes=[pltpu.VMEM((G, blk, 1), F32), pltpu.VMEM((G, blk, 1), F32),
                        pltpu.VMEM((blk, G * HEAD_DIM), F32)],
        compiler_params=_params(),
    )(qn, kn, vb, f_tm, f_hm)


def _attn_bwd(qn, kn, vb, do, lse, delta, f_tm, f_hm, T, blk, comm=None):
    nb = T // blk
    scale = HEAD_DIM ** -0.5
    W = N_HEADS * HEAD_DIM
    G = HEADS_PER_STEP
    lanes = [slice(g * HEAD_DIM, (g + 1) * HEAD_DIM) for g in range(G)]

    def kern(q_ref, k_ref, v_ref, do_ref, lse_ref, dl_ref, ft_ref, fh_ref,
             dq_ref, dfq_ref, dk_ref, dv_ref, df_ref, dq_scr, dfq_scr, dk_scr, dv_scr, df_scr):
        hp = pl.program_id(0)
        j, i = _tri_cols(pl.program_id(1), nb)

        @pl.when((j == 0) & (i == 0))
        def _():
            dq_scr[...] = jnp.zeros_like(dq_scr)
            dfq_scr[...] = jnp.zeros_like(dfq_scr)

        @pl.when(i == j)
        def _():
            dk_scr[...] = jnp.zeros_like(dk_scr)
            dv_scr[...] = jnp.zeros_like(dv_scr)
            df_scr[...] = jnp.zeros_like(df_scr)

        def block(diagonal):
            ft, fh = ft_ref[...], fh_ref[...]
            rows = pl.ds(pl.multiple_of(i * blk, blk), blk)
            q = [q_ref[:, sl] for sl in lanes]
            k = [k_ref[:, sl] for sl in lanes]
            dob = [do_ref[:, sl].astype(BF16) for sl in lanes]
            s = [_dot(q[g], k[g], "NT") * (scale * LOG2E) + _gate_bias(ft, fh, hp * G + g) for g in range(G)]
            if diagonal:
                mask = _causal_bias(blk)
                s = [sg + mask for sg in s]
            p = [jnp.exp2(s[g] - lse_ref[:, sl.start:sl.start + 1]) for g, sl in enumerate(lanes)]
            dp = [_dot(dob[g], v_ref[:, sl], "NT") for g, sl in enumerate(lanes)]
            ds = [p[g] * (dp[g] - dl_ref[:, sl.start:sl.start + 1]) for g, sl in enumerate(lanes)]
            dsb = [d.astype(BF16) for d in ds]
            for g, sl in enumerate(lanes):
                dv_scr[:, sl] += _dot(p[g], dob[g], "TN")
                dk_scr[:, sl] += _dot(dsb[g], q[g], "TN") * scale
                dq_scr[rows, sl] += _dot(dsb[g], k[g], "NN") * scale
                df_scr[g] += jnp.sum(ds[g], axis=0, keepdims=True)
                dfq_scr[g, rows, :] += jnp.sum(ds[g], axis=1, keepdims=True)

        @pl.when(i > j)
        def _():
            block(False)

        @pl.when(i == j)
        def _():
            block(True)

        @pl.when(i == nb - 1)
        def _():
            dk_ref[...] = dk_scr[...]
            dv_ref[...] = dv_scr[...]
            df_ref[...] = -df_scr[...]

        @pl.when((j == nb - 1) & (i == nb - 1))
        def _():
            dq_ref[...] = dq_scr[...]
            for g, sl in enumerate(lanes):
                dfq_ref[:, sl] = jnp.broadcast_to(dfq_scr[g], (T, HEAD_DIM))

    qspec = pl.BlockSpec((blk, G * HEAD_DIM), lambda h, s: (_tri_cols(s, nb)[1], h))
    full = pl.BlockSpec((T, G * HEAD_DIM), lambda h, s: (0, h))
    kspec = pl.BlockSpec((blk, G * HEAD_DIM), lambda h, s: (_tri_cols(s, nb)[0], h))
    return _pallas(
        kern, comm=comm, name="attn_bwd", grid=(N_HEADS // G, nb * (nb + 1) // 2),
        in_specs=[qspec, kspec, kspec, qspec, qspec, qspec,
                  pl.BlockSpec((blk, 128), lambda h, s: (_tri_cols(s, nb)[1], 0)),
                  pl.BlockSpec((N_HEADS, blk), lambda h, s: (0, _tri_cols(s, nb)[0]))],
        out_specs=[full, full, kspec, kspec, pl.BlockSpec((G, 1, blk), lambda h, s: (h, 0, _tri_cols(s, nb)[0]))],
        out_shape=[jax.ShapeDtypeStruct((T, W), F32)] * 4 + [jax.ShapeDtypeStruct((N_HEADS, 1, T), F32)],
        scratch_shapes=[pltpu.VMEM((T, G * HEAD_DIM), F32), pltpu.VMEM((G, T, 1), F32),
                        pltpu.VMEM((blk, G * HEAD_DIM), F32), pltpu.VMEM((blk, G * HEAD_DIM), F32),
                        pltpu.VMEM((G, 1, blk), F32)],
        compiler_params=_params(),
    )(qn, kn, vb, do, lse, delta, f_tm, f_hm)


def _attn_delta(o, do, T, tb):
    W = N_HEADS * HEAD_DIM

    def body(ob, dob):
        return jnp.concatenate(
            _heads(lambda a, b: jnp.broadcast_to(jnp.sum(a * b, axis=1, keepdims=True), a.shape), ob, dob), axis=1)

    return _rowwise("attn_delta", body, T, tb, [(o, W, 0), (do, W, 0)], [], [(W, F32)], [])[0]


def _window_select(s, g, shift):
    picks = []
    for k in (1, 2, 4, 8):
        s = s + shift(s, k)
        picks.append(s)
    return jnp.where(g == 0, picks[0], jnp.where(g == 1, picks[1], jnp.where(g == 2, picks[2], picks[3])))


def _group_window(g):
    return jnp.where(g == 0, POOL_WINDOWS[0], jnp.where(g == 1, POOL_WINDOWS[1],
                     jnp.where(g == 2, POOL_WINDOWS[2], POOL_WINDOWS[3])))


def _pool_fwd(proj, ucol, pw, ps, T, tb):
    C = POOL_GROUP_DIM
    n_g = len(POOL_WINDOWS)

    def kern(uc_ref, up_ref, pw_ref, ps_ref, pooled_ref, out_ref):
        g, i = pl.program_id(0), pl.program_id(1)
        uc = uc_ref[...]
        t2 = (i - 1) * tb + lax.broadcasted_iota(jnp.int32, (2 * tb, C), 0)
        u2 = jnp.where(t2 >= 0, jnp.concatenate([up_ref[...], uc], axis=0), 0.0)
        sums = _window_select(u2, g, lambda s, k: pltpu.roll(s, k, 0))[tb:, :]
        count = jnp.minimum(t2[tb:, :] + 1, _group_window(g)).astype(F32)
        pooled = sums / count - uc
        pooled_ref[...] = pooled.astype(BF16)
        out_ref[...] = _dot(pooled, pw_ref[...], "NN") * ps_ref[...]

    ospec = pl.BlockSpec((tb, C), lambda g, i: (i, g))
    return pl.pallas_call(
        kern, name="pool_fwd", grid=(n_g, T // tb),
        in_specs=[pl.BlockSpec((tb, C), lambda g, i: (i, ucol + g)),
                  pl.BlockSpec((tb, C), lambda g, i: (jnp.maximum(i - 1, 0), ucol + g)),
                  pl.BlockSpec((None, C, C), lambda g, i: (g, 0, 0)),
                  pl.BlockSpec((1, C), lambda g, i: (0, g))],
        out_specs=[ospec, ospec],
        out_shape=[jax.ShapeDtypeStruct((T, n_g * C), BF16), jax.ShapeDtypeStruct((T, n_g * C), F32)],
        compiler_params=_params(),
    )(proj, proj, pw, ps)


def _pool_bwd(dmix_in, dcol, pooled, pw, ps, T, tb):
    C = POOL_GROUP_DIM
    n_g = len(POOL_WINDOWS)
    nb = T // tb

    def kern(dc_ref, dn_ref, pooled_ref, pw_ref, ps_ref, du_ref, dpw_ref, dps_ref):
        g, i = pl.program_id(0), pl.program_id(1)
        dc = dc_ref[...]
        scale = ps_ref[...]
        t2 = i * tb + lax.broadcasted_iota(jnp.int32, (2 * tb, C), 0)
        d2 = jnp.where(t2 < T, jnp.concatenate([dc, dn_ref[...]], axis=0) * scale, 0.0)
        dpooled2 = _dot(d2, pw_ref[...], "NT")
        count = jnp.minimum(t2 + 1, _group_window(g)).astype(F32)
        sums = _window_select(dpooled2 / count, g, lambda s, k: pltpu.roll(s, 2 * tb - k, 0))
        du_ref[...] = (sums[:tb, :] - dpooled2[:tb, :]).astype(BF16)
        pooled = pooled_ref[...]
        p = _dot(pooled, pw_ref[...], "NN")
        dps = jnp.sum(dc * p, axis=0, keepdims=True)
        dpw = _dot(pooled, d2[:tb, :], "TN")

        @pl.when(i == 0)
        def _():
            dps_ref[...] = dps
            dpw_ref[...] = dpw

        @pl.when(i > 0)
        def _():
            dps_ref[...] += dps
            dpw_ref[...] += dpw

    return pl.pallas_call(
        kern, name="pool_bwd", grid=(n_g, nb),
        in_specs=[pl.BlockSpec((tb, C), lambda g, i: (i, dcol + g)),
                  pl.BlockSpec((tb, C), lambda g, i: (jnp.minimum(i + 1, nb - 1), dcol + g)),
                  pl.BlockSpec((tb, C), lambda g, i: (i, g)),
                  pl.BlockSpec((None, C, C), lambda g, i: (g, 0, 0)),
                  pl.BlockSpec((1, C), lambda g, i: (0, g))],
        out_specs=[pl.BlockSpec((tb, C), lambda g, i: (i, g)),
                   pl.BlockSpec((None, C, C), lambda g, i: (g, 0, 0)),
                   pl.BlockSpec((1, C), lambda g, i: (0, g))],
        out_shape=[jax.ShapeDtypeStruct((T, n_g * C), BF16), jax.ShapeDtypeStruct((n_g, C, C), F32),
                   jax.ShapeDtypeStruct((1, n_g * C), F32)],
        compiler_params=_params(),
    )(dmix_in, dmix_in, pooled, pw, ps)


D_QKV = 3 * N_HEADS * HEAD_DIM
D_U = len(POOL_WINDOWS) * POOL_GROUP_DIM
F_PAD = 128
D_PROJ = D_QKV + D_U + F_PAD


def _perm_w_in(w):
    pad = jnp.zeros((w.shape[0], F_PAD - N_HEADS), w.dtype)
    return jnp.concatenate([w[:, :D_QKV], w[:, D_QKV + N_HEADS:], w[:, D_QKV:D_QKV + N_HEADS], pad], axis=1)


def _unperm_w_in(w):
    return jnp.concatenate([w[:, :D_QKV], w[:, D_QKV + D_U:D_QKV + D_U + N_HEADS], w[:, D_QKV:D_QKV + D_U]], axis=1)


def _mixer_fwd(x, norm_g, sh, sc, gate, w_in_p, b_pad, gq, gk, late_weights, ps, T, proj_comm, attn_comm):
    tb = min(256, T)
    blk = min(512, T)
    hm = _norm_mod_fwd("mix_norm_fwd", x, norm_g, sc, sh, T, tb)
    proj, got_proj = _mm("mix_proj", hm, w_in_p, "NN", F32, 512, D_PROJ // 3, 2048, comm=proj_comm)
    pw, w_out = late_weights(got_proj)
    qn, kn, vb = _qknorm_fwd(proj, gq, gk, T, tb)
    fcol = (D_QKV + D_U) // 128
    f_tm = _fgate_fwd(proj, fcol, b_pad, T)
    f_hm = f_tm[:, :N_HEADS].T
    (o, lse), got = _attn_fwd(qn, kn, vb, f_tm, f_hm, T, blk, comm=attn_comm)
    pooled, pool_o = _pool_fwd(proj, D_QKV // POOL_GROUP_DIM, pw, ps, T, tb)
    mix_in = jnp.concatenate([o.astype(BF16), pool_o.astype(BF16)], axis=1)
    mix, x_out = _mm_groups("mix_out", mix_in[None], w_out[None], "NN", 512, 512, residual=(x, gate, 1.0))
    return x_out, (x, hm, proj, qn, kn, vb, f_tm, f_hm, o, lse, pooled, mix_in, mix), pw, w_out, got


def _mixer_bwd(dx_out, saved, norm_g, sc, gate, w_in_p, b_pad, gq, gk, pw, ps, w_out, T, core, ride_sums):
    x, hm, proj, qn, kn, vb, f_tm, f_hm, o, lse, pooled, mix_in, mix = saved
    tb = min(256, T)
    blk = min(512, T)
    W = N_HEADS * HEAD_DIM
    D = x.shape[1]
    n_g = len(POOL_WINDOWS)
    dmix, dgate = _residual_bwd("mix_res_bwd", dx_out, mix, gate, 1.0, T, tb)
    dmix_in = _mm("mix_out_bwd", dmix, w_out, "NT", F32, 512, 2048, 2048)
    dw_out = _mm("mix_dw_out", mix_in, dmix, "TN", BF16, 512, 1024, T)
    delta = _attn_delta(o, dmix_in, T, tb)
    (dqn, dfq, dkn, dv, dfk), ride_got = _attn_bwd(qn, kn, vb, dmix_in, lse, delta, f_tm, f_hm, T, blk,
                                                   comm=_chip_comm(ride_sums))
    dq, dk, dgq, dgk = _qknorm_bwd(proj, dqn, dkn, gq, gk, T, tb)
    dF = jnp.pad(dfq[:, ::HEAD_DIM] + dfk.reshape(N_HEADS, T).T, ((0, 0), (0, F_PAD - N_HEADS)))
    fcol = (D_QKV + D_U) // 128
    dfl, dbf = _fgate_bwd(proj, fcol, b_pad, dF, T)
    du, dpw, dps = _pool_bwd(dmix_in, W // POOL_GROUP_DIM, pooled, pw, ps, T, tb)
    dproj = jnp.concatenate([dq, dk, dv.astype(BF16), du, dfl.astype(BF16)], axis=1)
    dw_in_p = _mm("mix_dw_in", hm, dproj, "TN", BF16, 512, D_PROJ // 3, T)
    pw_rows = POOL_GROUP_DIM // N_DEV
    slabs = [jnp.transpose(_unperm_w_in(dw_in_p).reshape(D, N_DEV, -1), (1, 0, 2)),
             jnp.transpose(dpw.astype(BF16).reshape(n_g, N_DEV, pw_rows, POOL_GROUP_DIM),
                           (1, 0, 2, 3)).reshape(N_DEV, n_g * pw_rows, POOL_GROUP_DIM),
             dw_out.reshape(N_DEV, -1, D)]
    _, sums = _reduce_level1("mix", slabs, core, [TILE_W_IN, TILE_POOL, TILE_MIX_OUT])
    dhm = _mm("mix_proj_bwd", dproj, w_in_p, "NT", F32, 512, 512, D_PROJ)
    dx, dsh, dsc, dng = _norm_mod_bwd("mix_norm_bwd", x, dhm, dx_out, norm_g, sc, T, tb)
    return dx, (dsh, dsc, dgate, dng), sums, dps, dgq, dgk, dbf, ride_got


def kernel(x, c, w_ada, b_ada, ffn1_norm_g, ffn1_w_in, ffn1_w_out, mix_norm_g, w_in, b_forget, q_norm_g, k_norm_g, pool_w, pool_scale, w_out, ffn2_norm_g, ffn2_w_in, ffn2_w_out, final_norm_g, loss_target, m_w_ada, m_b_ada, m_ffn1_norm_g, m_ffn1_w_in, m_ffn1_w_out, m_mix_norm_g, m_w_in, m_b_forget, m_q_norm_g, m_k_norm_g, m_pool_w, m_pool_scale, m_w_out, m_ffn2_norm_g, m_ffn2_w_in, m_ffn2_w_out, m_final_norm_g, v_w_ada, v_b_ada, v_ffn1_norm_g, v_ffn1_w_in, v_ffn1_w_out, v_mix_norm_g, v_w_in, v_b_forget, v_q_norm_g, v_k_norm_g, v_pool_w, v_pool_scale, v_w_out, v_ffn2_norm_g, v_ffn2_w_in, v_ffn2_w_out, v_final_norm_g):
    T, D = x.shape[1], x.shape[2]
    mx, my, mc = _mesh_pos()
    me = _flat(mx, my, mc)
    x0 = x[0]
    tgt = loss_target[0]
    tb = min(256, T)

    core = jnp.reshape(mc, (1,)).astype(jnp.int32)
    half = N_DEV // 2
    n_g = len(POOL_WINDOWS)
    pw_rows = POOL_GROUP_DIM // N_DEV

    def bf(w):
        return w.astype(BF16)

    n_loc = w_ada.shape[2]
    c_all = _standalone("gather_c", _gather_comm([c.reshape(8, D // 8)]))[0].reshape(N_DEV, D)
    b_loc = lax.dynamic_slice_in_dim(b_ada, me * n_loc, n_loc, axis=1)
    mod_loc = _ada_fwd(c_all, w_ada[0], b_loc, n_loc // 3)
    mod_all = _standalone("gather_mod", _gather_comm([mod_loc]))[0]
    mod = lax.dynamic_index_in_dim(mod_all, me, axis=1, keepdims=False).reshape(N_MOD, 1, D)
    sh1, sc1, g1, sh2, sc2, g2, sh3, sc3, g3 = [mod[k] for k in range(N_MOD)]
    b_pad = jnp.pad(b_forget, ((0, 0), (0, F_PAD - N_HEADS)))
    ps = pool_scale

    wi1 = _standalone("gather_ffn1_w_in", _gather_comm([bf(ffn1_w_in[0])]))[0]
    x1, sv1, wo1, (w_in_g,) = _ffn_fwd(
        "ffn1", x0, ffn1_norm_g, sh1, sc1, g1, wi1, lambda got: got[0].reshape(half, -1, D), T,
        up_comm=_gather_comm([bf(ffn1_w_out[0])], forward_at=0.7), down_comm=_gather_comm([bf(w_in[0])], forward_at=0.8))
    w_in_p = _perm_w_in(jnp.transpose(w_in_g, (1, 0, 2)).reshape(D, -1))

    def late_weights(got):
        pool_g, w_out_g = got
        pw = jnp.transpose(pool_g.reshape(N_DEV, n_g, pw_rows, POOL_GROUP_DIM),
                           (1, 0, 2, 3)).reshape(n_g, POOL_GROUP_DIM, POOL_GROUP_DIM)
        return pw, w_out_g.reshape(-1, D)

    x2, svm, pw_full, w_out_full, (wi2,) = _mixer_fwd(
        x1, mix_norm_g, sh2, sc2, g2, w_in_p, b_pad, q_norm_g, k_norm_g, late_weights, ps, T,
        proj_comm=_gather_comm([bf(pool_w[0].reshape(-1, POOL_GROUP_DIM)), bf(w_out[0])], forward_at=0.6),
        attn_comm=_gather_comm([bf(ffn2_w_in[0])], forward_at=0.85))
    x3, sv2, wo2, _ = _ffn_fwd("ffn2", x2, ffn2_norm_g, sh3, sc3, g3, wi2,
                               lambda got: got[0].reshape(half, -1, D), T,
                               up_comm=_gather_comm([bf(ffn2_w_out[0])], forward_at=0.7))
    dx3, dgf, loss_l = _final_loss(x3, tgt, final_norm_g.reshape(1, D), T, tb)
    loss = lax.psum(loss_l[0, 0], ("x", "y", "c"))

    dx2, (dsh3, dsc3, dg3, dn3), dwi2_sum, dwo2, _ = _ffn_bwd(
        "ffn2", dx3, sv2, ffn2_norm_g, sc3, g3, wi2, wo2, T, core, defer_dwi=True)
    dx1, (dsh2, dsc2, dg2, dn2), mix_sums, dps, dgq, dgk, dbf, (dwi2,) = _mixer_bwd(
        dx2, svm, mix_norm_g, sc2, g2, w_in_p, b_pad, q_norm_g, k_norm_g, pw_full, ps, w_out_full, T, core,
        ride_sums=[dwi2_sum])
    dx0, (dsh1, dsc1, dg1, dn1), dwi1, dwo1, (dw_in_r, dpw_r, dw_out_r) = _ffn_bwd(
        "ffn1", dx1, sv1, ffn1_norm_g, sc1, g1, wi1, wo1, T, core, ride_sums=mix_sums)

    received = dict(ffn1_w_in=dwi1, ffn1_w_out=dwo1, w_in=dw_in_r, pool_w=dpw_r, w_out=dw_out_r,
                    ffn2_w_in=dwi2, ffn2_w_out=dwo2)
    moments = dict(ffn1_w_in=(m_ffn1_w_in, v_ffn1_w_in), ffn1_w_out=(m_ffn1_w_out, v_ffn1_w_out),
                   w_in=(m_w_in, v_w_in), pool_w=(m_pool_w, v_pool_w), w_out=(m_w_out, v_w_out),
                   ffn2_w_in=(m_ffn2_w_in, v_ffn2_w_in), ffn2_w_out=(m_ffn2_w_out, v_ffn2_w_out))
    weights = dict(ffn1_w_in=ffn1_w_in, ffn1_w_out=ffn1_w_out, w_in=w_in, pool_w=pool_w, w_out=w_out,
                   ffn2_w_in=ffn2_w_in, ffn2_w_out=ffn2_w_out)
    row_tiles = dict(ffn1_w_in=TILE_W_IN, ffn1_w_out=TILE_W_OUT, w_in=TILE_W_IN, pool_w=TILE_POOL,
                     w_out=TILE_MIX_OUT, ffn2_w_in=TILE_W_IN, ffn2_w_out=TILE_W_OUT)
    results = {}
    for k in received:
        shape = weights[k].shape
        two_d = received[k].shape[1:]
        mk, vk = moments[k]
        outs = _adamw("adamw_" + k, received[k], weights[k].reshape(two_d), mk.reshape(two_d),
                      vk.reshape(two_d), row_tiles[k][0])
        results[k] = [o.reshape(shape) for o in outs]

    dmod = jnp.concatenate([dsh1, dsc1, dg1, dsh2, dsc2, dg2, dsh3, dsc3, dg3], axis=1)
    small_names = ["b_ada", "ffn1_norm_g", "mix_norm_g", "ffn2_norm_g", "final_norm_g", "b_forget",
                   "q_norm_g", "k_norm_g", "pool_scale"]
    small_w = dict(b_ada=b_ada, ffn1_norm_g=ffn1_norm_g, mix_norm_g=mix_norm_g, ffn2_norm_g=ffn2_norm_g,
                   final_norm_g=final_norm_g, b_forget=b_forget, q_norm_g=q_norm_g, k_norm_g=k_norm_g,
                   pool_scale=pool_scale)
    small_m = dict(b_ada=m_b_ada, ffn1_norm_g=m_ffn1_norm_g, mix_norm_g=m_mix_norm_g, ffn2_norm_g=m_ffn2_norm_g,
                   final_norm_g=m_final_norm_g, b_forget=m_b_forget, q_norm_g=m_q_norm_g, k_norm_g=m_k_norm_g,
                   pool_scale=m_pool_scale)
    small_v = dict(b_ada=v_b_ada, ffn1_norm_g=v_ffn1_norm_g, mix_norm_g=v_mix_norm_g, ffn2_norm_g=v_ffn2_norm_g,
                   final_norm_g=v_final_norm_g, b_forget=v_b_forget, q_norm_g=v_q_norm_g, k_norm_g=v_k_norm_g,
                   pool_scale=v_pool_scale)
    small_g = dict(b_ada=dmod, ffn1_norm_g=dn1, mix_norm_g=dn2, ffn2_norm_g=dn3, final_norm_g=dgf,
                   b_forget=dbf[:, :N_HEADS], q_norm_g=dgq, k_norm_g=dgk, pool_scale=dps)
    sizes = [small_w[k].size for k in small_names]
    total = sum(sizes)
    lanes = 8 * 128
    padded = -(-total // lanes) * lanes

    def pack(d):
        flat = jnp.concatenate([d[k].reshape(-1) for k in small_names])
        return jnp.pad(flat, (0, padded - total)).reshape(8, padded // 8)

    small_parts = _standalone("gather_small_grads", _gather_comm([pack(small_g)]))[0]
    s_outs = _adamw("adamw_small", small_parts, pack(small_w), pack(small_m), pack(small_v), 8)
    offs = [0]
    for s in sizes:
        offs.append(offs[-1] + s)
    for idx, k in enumerate(small_names):
        results[k] = [o.reshape(-1)[offs[idx]:offs[idx + 1]].reshape(small_w[k].shape) for o in s_outs]

    dmod_all = small_parts.reshape(N_DEV, padded)[:, :N_MOD * D]
    dmod_loc = lax.dynamic_slice_in_dim(dmod_all, me * n_loc, n_loc, axis=1)
    g_ada = _ada_bwd(c_all, dmod_loc, n_loc // 3)
    a_outs = _adamw("adamw_w_ada", g_ada[None], w_ada[0], m_w_ada[0], v_w_ada[0], 128)
    results["w_ada"] = [o.reshape(w_ada.shape) for o in a_outs]

    order = ["w_ada", "b_ada", "ffn1_norm_g", "ffn1_w_in", "ffn1_w_out", "mix_norm_g", "w_in", "b_forget",
             "q_norm_g", "k_norm_g", "pool_w", "pool_scale", "w_out", "ffn2_norm_g", "ffn2_w_in", "ffn2_w_out",
             "final_norm_g"]
    out = [loss, dx0[None]]
    for part in range(4):
        out += [results[k][part] for k in order]
    return tuple(out)
```

```python
import jax
import jax.numpy as jnp
from jax import lax
from jax.experimental import pallas as pl
from jax.experimental.pallas import tpu as pltpu

F32 = jnp.float32
BF16 = jnp.bfloat16
MESH = pl.DeviceIdType.MESH
ANY = pl.BlockSpec(memory_space=pl.ANY)

N_DEV = 8
EPS = 1e-6
HEAD_DIM = 128
N_HEADS = 8
POOL_WINDOWS = (2, 4, 8, 16)
POOL_GROUP_DIM = 256
N_MOD = 9
ADAM_LR = 0.001
ADAM_B1 = 0.9
ADAM_B2 = 0.999
ADAM_EPS = 1e-08
ADAM_WD = 0.01
ADAM_STEP = 10
NEG = -1e30
VMEM_LIMIT_V7X = 56 * 1024 * 1024


def _params():
    return pltpu.CompilerParams(vmem_limit_bytes=VMEM_LIMIT_V7X)


def _sigmoid(z):
    return 1.0 / (1.0 + jnp.exp(-z))


def _rstd(x):
    return lax.rsqrt(jnp.mean(x * x, axis=-1, keepdims=True) + EPS)


def _mesh_pos():
    return lax.axis_index("x"), lax.axis_index("y"), lax.axis_index("c")


def _flat(px, py, pc):
    return 4 * px + 2 * py + pc


class _Comm:
    def __init__(self, ins, outs, sems, phases):
        self.ins, self.outs, self.sems, self.phases = list(ins), list(outs), list(sems), list(phases)


def _pallas(kern, *, comm=None, **kw):
    if comm is None:
        return pl.pallas_call(kern, **kw)
    grid = tuple(kw["grid"])
    single = not isinstance(kw["out_shape"], (list, tuple))
    out_shape = [kw["out_shape"]] if single else list(kw["out_shape"])
    out_specs = [kw["out_specs"]] if single else list(kw["out_specs"])
    in_specs = list(kw["in_specs"])
    scratch = list(kw.get("scratch_shapes", ()))
    n_in, n_out, n_scr = len(in_specs), len(out_shape), len(scratch)
    n_ci, n_co = len(comm.ins), len(comm.outs)
    strides, n_steps = [], 1
    for g in reversed(grid):
        strides.insert(0, n_steps)
        n_steps *= g

    def wrapped(*refs):
        ins, cins = refs[:n_in], refs[n_in:n_in + n_ci]
        base = n_in + n_ci
        outs, couts = refs[base:base + n_out], refs[base + n_out:base + n_out + n_co]
        base += n_out + n_co
        scr, sems = refs[base:base + n_scr], refs[base + n_scr:]
        step = sum(pl.program_id(d) * strides[d] for d in range(len(grid)))
        for frac, fn in comm.phases:
            if frac < 1.0:
                pl.when(step == int(round(frac * (n_steps - 1))))(lambda fn=fn: fn(cins, couts, sems))
        kern(*ins, *outs, *scr)
        for frac, fn in comm.phases:
            if frac >= 1.0:
                pl.when(step == n_steps - 1)(lambda fn=fn: fn(cins, couts, sems))

    kw = dict(kw, in_specs=in_specs + [ANY] * n_ci, out_specs=out_specs + [ANY] * n_co,
              out_shape=out_shape + comm.outs, scratch_shapes=scratch + comm.sems)
    call = pl.pallas_call(wrapped, **kw)

    def run(*args):
        res = call(*args, *comm.ins)
        main = res[0] if single else list(res[:n_out])
        return main, list(res[n_out:])

    return run


def _join(first, second):
    n_i, n_o, n_s = len(first.ins), len(first.outs), len(first.sems)

    def left(fn):
        return lambda ins, outs, sems: fn(ins[:n_i], outs[:n_o], sems[:n_s])

    def right(fn):
        return lambda ins, outs, sems: fn(ins[n_i:], outs[n_o:], sems[n_s:])

    phases = [(f, left(fn)) for f, fn in first.phases] + [(f, right(fn)) for f, fn in second.phases]
    return _Comm(first.ins + second.ins, first.outs + second.outs, first.sems + second.sems, phases)


def _hosted(comm, res):
    return res if comm is not None else (res, [])


def _standalone(name, comm):
    def kern():
        pass

    return _pallas(kern, comm=comm, name=name, grid=(1,), in_specs=[], out_specs=[], out_shape=[])()[1]


def _dma_sems(*shapes):
    return [pltpu.SemaphoreType.DMA(s) for s in shapes]


def _gather_comm(arrs, forward_at=0.5, sender_core=None):
    n = len(arrs)

    def senders(c, fn):
        if sender_core is None:
            fn()
        else:
            pl.when(c == sender_core)(fn)

    def receivers(c, fn):
        if sender_core is None:
            fn()
        else:
            pl.when(c != sender_core)(fn)

    def setup(outs, sems):
        send_sems, recv_sems, _ = sems
        x, y, c = _mesh_pos()
        chips = [(1 - x, y), (x, 1 - y), (1 - x, 1 - y)]

        def copy(a, k, block, to, src=None):
            dst = outs[a].at[_flat(*block)]
            return pltpu.make_async_remote_copy(
                src_ref=dst if src is None else src, dst_ref=dst,
                send_sem=send_sems.at[a, k], recv_sem=recv_sems.at[a, k],
                device_id=to, device_id_type=MESH)

        return (x, y, c), (x, y, 1 - c), chips, copy

    def local(ins, outs, sems, a, me):
        return pltpu.make_async_copy(ins[a], outs[a].at[_flat(*me)], sems[2].at[a])

    def send_own(ins, outs, sems):
        me, sibling, chips, copy = setup(outs, sems)

        def go():
            for a in range(n):
                local(ins, outs, sems, a, me).start()
                copy(a, 0, me, sibling, src=ins[a]).start()
                for j, chip in enumerate(chips):
                    copy(a, 1 + j, me, (*chip, me[2]), src=ins[a]).start()

        senders(me[2], go)

    def forward(ins, outs, sems):
        me, sibling, chips, copy = setup(outs, sems)

        def go():
            for a in range(n):
                for j, chip in enumerate(chips):
                    copy(a, 1 + j, (*chip, me[2]), me).wait_recv()
                    copy(a, 4 + j, (*chip, me[2]), sibling).start()

        senders(me[2], go)

    def finish(ins, outs, sems):
        me, sibling, chips, copy = setup(outs, sems)

        def from_sibling():
            for a in range(n):
                copy(a, 0, sibling, me).wait_recv()
                for j, chip in enumerate(chips):
                    copy(a, 4 + j, (*chip, 1 - me[2]), me).wait_recv()

        def drain():
            for a in range(n):
                copy(a, 0, me, sibling, src=ins[a]).wait_send()
                for j, chip in enumerate(chips):
                    copy(a, 1 + j, me, (*chip, me[2]), src=ins[a]).wait_send()
                    copy(a, 4 + j, (*chip, me[2]), sibling).wait_send()
                local(ins, outs, sems, a, me).wait()

        receivers(me[2], from_sibling)
        senders(me[2], drain)

    return _Comm(arrs, [jax.ShapeDtypeStruct((N_DEV,) + a.shape, a.dtype) for a in arrs],
                 _dma_sems((n, 7), (n, 7), (n,)), [(0.0, send_own), (forward_at, forward), (1.0, finish)])


def _merge_comm(even, odd):
    def copies(ins, outs, sems):
        return [pltpu.make_async_copy(ins[d % 2].at[d], outs[0].at[d], sems[0].at[d]) for d in range(N_DEV)]

    def start(ins, outs, sems):
        for cp in copies(ins, outs, sems):
            cp.start()

    def finish(ins, outs, sems):
        for cp in copies(ins, outs, sems):
            cp.wait()

    return _Comm([even, odd], [jax.ShapeDtypeStruct(even.shape, even.dtype)], _dma_sems((N_DEV,)),
                 [(0.0, start), (1.0, finish)])


CHIPS = [(0, 0), (0, 1), (1, 0), (1, 1)]


def _sibling_comm(parts):
    n = len(parts)

    def copies(ins, outs, sems):
        x, y, c = _mesh_pos()
        return [pltpu.make_async_remote_copy(
                    src_ref=ins[a].at[_flat(qx, qy, 1 - c)], dst_ref=outs[a].at[q],
                    send_sem=sems[0].at[a, q], recv_sem=sems[1].at[a, q],
                    device_id=(x, y, 1 - c), device_id_type=MESH)
                for a in range(n) for q, (qx, qy) in enumerate(CHIPS)]

    def start(ins, outs, sems):
        for cp in copies(ins, outs, sems):
            cp.start()

    def finish(ins, outs, sems):
        for cp in copies(ins, outs, sems):
            cp.wait_recv()
        for cp in copies(ins, outs, sems):
            cp.wait_send()

    return _Comm(parts, [jax.ShapeDtypeStruct((4,) + p.shape[1:], p.dtype) for p in parts],
                 _dma_sems((n, 4), (n, 4)), [(0.0, start), (1.0, finish)])


def _chip_comm(sums):
    n = len(sums)
    flips = [(1, 0), (0, 1), (1, 1)]

    def own(ins, outs, sems):
        mine = 2 * lax.axis_index("x") + lax.axis_index("y")
        return [pltpu.make_async_copy(ins[a].at[mine], outs[a].at[mine], sems[2].at[a]) for a in range(n)]

    def copies(ins, outs, sems, arriving=False):
        x, y, c = _mesh_pos()
        mine = 2 * x + y
        remote = []
        for a in range(n):
            for k, (fx, fy) in enumerate(flips):
                qx, qy = x ^ fx, y ^ fy
                q = 2 * qx + qy
                remote.append(pltpu.make_async_remote_copy(
                    src_ref=ins[a].at[q], dst_ref=outs[a].at[q if arriving else mine],
                    send_sem=sems[0].at[a, k], recv_sem=sems[1].at[a, k],
                    device_id=(qx, qy, c), device_id_type=MESH))
        return remote

    def start(ins, outs, sems):
        for cp in own(ins, outs, sems) + copies(ins, outs, sems):
            cp.start()

    def finish(ins, outs, sems):
        for cp in copies(ins, outs, sems, arriving=True):
            cp.wait_recv()
        for cp in copies(ins, outs, sems):
            cp.wait_send()
        for cp in own(ins, outs, sems):
            cp.wait()

    return _Comm(sums, [jax.ShapeDtypeStruct(s.shape, s.dtype) for s in sums],
                 _dma_sems((n, 3), (n, 3), (n,)), [(0.0, start), (1.0, finish)])


def _pair_add(name, parts, got, core, tr):
    _, R, C = parts.shape
    assert R % tr == 0

    def kern(c_ref, p_ref, g_ref, o_ref):
        o_ref[...] = (p_ref[...].astype(F32) + g_ref[...].astype(F32)).astype(o_ref.dtype)

    blk = pl.BlockSpec((None, tr, C), lambda q, i, c_ref: (q, i, 0))
    return pl.pallas_call(
        kern, name=name,
        grid_spec=pltpu.PrefetchScalarGridSpec(
            num_scalar_prefetch=1, grid=(4, R // tr),
            in_specs=[pl.BlockSpec((None, tr, C), lambda q, i, c_ref: (2 * q + c_ref[0], i, 0)), blk],
            out_specs=blk),
        out_shape=jax.ShapeDtypeStruct((4, R, C), parts.dtype), compiler_params=_params(),
    )(core, parts, got)


def _rowwise(name, body, T, tb, rows, vecs, out_rows, out_accs):
    n_in = len(rows) + len(vecs)
    n_o, n_a = len(out_rows), len(out_accs)

    def kern(*refs):
        i = pl.program_id(0)
        res = body(*[r[...] for r in refs[:n_in]])
        if not isinstance(res, (tuple, list)):
            res = (res,)
        outs = refs[n_in:]
        for k in range(n_o):
            outs[k][...] = res[k].astype(outs[k].dtype)

        def accumulate(ref, val):
            @pl.when(i == 0)
            def _():
                ref[...] = val

            @pl.when(i > 0)
            def _():
                ref[...] += val

        for k in range(n_a):
            accumulate(outs[n_o + k], res[n_o + k])

    in_specs = [pl.BlockSpec((tb, w), lambda i, cb=cb: (i, cb)) for (_, w, cb) in rows]
    in_specs += [pl.BlockSpec((1, v.shape[1]), lambda i: (0, 0)) for v in vecs]
    out_specs = [pl.BlockSpec((tb, w), lambda i: (i, 0)) for (w, _) in out_rows]
    out_specs += [pl.BlockSpec((1, w), lambda i: (0, 0)) for w in out_accs]
    out_shape = [jax.ShapeDtypeStruct((T, w), dt) for (w, dt) in out_rows]
    out_shape += [jax.ShapeDtypeStruct((1, w), F32) for w in out_accs]
    res = pl.pallas_call(
        kern, name=name, grid=(T // tb,), in_specs=in_specs, out_specs=out_specs,
        out_shape=out_shape, compiler_params=_params(),
    )(*[r[0] for r in rows], *vecs)
    return res


def _dot(a, b, mode):
    dims = {"NN": ((1,), (0,)), "NT": ((1,), (1,)), "TN": ((0,), (0,))}[mode]
    return lax.dot_general(a.astype(BF16), b.astype(BF16), (dims, ((), ())),
                           preferred_element_type=F32)


def _mm(name, a, b, mode, out_dtype, tm, tn, tk, ga=False, gb=False, gmode=None, comm=None):
    G = (a.shape[0] if ga else b.shape[0]) if gmode else 1
    a2, b2 = a.shape[-2:], b.shape[-2:]
    if mode == "NN":
        (M, K), (_, N) = a2, b2
    elif mode == "NT":
        (M, K), (N, _) = a2, b2
    else:
        (K, M), (_, N) = a2, b2
    tm, tn, tk = min(tm, M), min(tn, N), min(tk, K)
    assert M % tm == 0 and N % tn == 0 and K % tk == 0, (name, M, N, K, tm, tn, tk)
    batch = gmode == "batch"
    n_gb, n_gs = (G if batch else 1), (G if gmode == "sum" else 1)
    nk = K // tk
    n_red = n_gs * nk

    def grp(g_b, g_s):
        return g_b if batch else g_s

    if mode == "TN":
        a_blk, a_idx = (tk, tm), lambda g_b, mi, ni, g_s, ki: (ki, mi)
    else:
        a_blk, a_idx = (tm, tk), lambda g_b, mi, ni, g_s, ki: (mi, ki)
    if mode == "NT":
        b_blk, b_idx = (tn, tk), lambda g_b, mi, ni, g_s, ki: (ni, ki)
    else:
        b_blk, b_idx = (tk, tn), lambda g_b, mi, ni, g_s, ki: (ki, ni)

    def with_group(blk, idx, has_group):
        if not has_group:
            return pl.BlockSpec(blk, idx)
        return pl.BlockSpec((None,) + blk, lambda g_b, mi, ni, g_s, ki: (grp(g_b, g_s),) + idx(g_b, mi, ni, g_s, ki))

    o_blk, o_idx = (tm, tn), lambda g_b, mi, ni, g_s, ki: (mi, ni)
    o_spec = with_group(o_blk, o_idx, batch)
    o_shape = ((G,) if batch else ()) + (M, N)

    def kern(a_ref, b_ref, o_ref, *scratch):
        part = _dot(a_ref[...], b_ref[...], mode)
        if n_red == 1:
            o_ref[...] = part.astype(o_ref.dtype)
            return
        acc = scratch[0]
        step = pl.program_id(3) * nk + pl.program_id(4)

        @pl.when(step == 0)
        def _():
            acc[...] = part

        @pl.when(step > 0)
        def _():
            acc[...] += part

        @pl.when(step == n_red - 1)
        def _():
            o_ref[...] = acc[...].astype(o_ref.dtype)

    return _pallas(
        kern, comm=comm, name=name, grid=(n_gb, M // tm, N // tn, n_gs, nk),
        in_specs=[with_group(a_blk, a_idx, ga), with_group(b_blk, b_idx, gb)],
        out_specs=o_spec, out_shape=jax.ShapeDtypeStruct(o_shape, out_dtype),
        scratch_shapes=[] if n_red == 1 else [pltpu.VMEM((tm, tn), F32)],
        compiler_params=_params(),
    )(a, b)


def _mm_groups(name, a, b, mode, tm, tn, residual=None, comm=None):
    G, M, K = a.shape
    N = b.shape[2] if mode == "NN" else b.shape[1]
    tm, tn = min(tm, M), min(tn, N)
    assert M % tm == 0 and N % tn == 0

    def kern(a_ref, b_ref, *rest):
        acc = _dot(a_ref[0], b_ref[0], mode)
        for g in range(1, G):
            acc = acc + _dot(a_ref[g], b_ref[g], mode)
        if residual is None:
            rest[0][...] = acc
        else:
            x_ref, g_ref, f_ref, o_ref = rest
            f_ref[...] = acc
            o_ref[...] = x_ref[...] + (residual[2] * g_ref[...]) * acc

    b_spec = (pl.BlockSpec((G, K, tn), lambda ni, mi: (0, 0, ni)) if mode == "NN"
              else pl.BlockSpec((G, tn, K), lambda ni, mi: (0, ni, 0)))
    o_spec = pl.BlockSpec((tm, tn), lambda ni, mi: (mi, ni))
    in_specs = [pl.BlockSpec((G, tm, K), lambda ni, mi: (0, mi, 0)), b_spec]
    args = [a, b]
    out = jax.ShapeDtypeStruct((M, N), F32)
    if residual is not None:
        in_specs += [o_spec, pl.BlockSpec((1, tn), lambda ni, mi: (0, ni))]
        args += [residual[0], residual[1]]
    return _pallas(
        kern, comm=comm, name=name, grid=(N // tn, M // tm), in_specs=in_specs,
        out_specs=o_spec if residual is None else [o_spec, o_spec],
        out_shape=out if residual is None else [out, out], compiler_params=_params(),
    )(*args)


def _adamw(name, parts, w, m, v, tr):
    G, R, C = parts.shape
    assert R % tr == 0
    bc1 = 1.0 - ADAM_B1 ** ADAM_STEP
    bc2 = 1.0 - ADAM_B2 ** ADAM_STEP

    def kern(p_ref, w_ref, m_ref, v_ref, g_out, d_out, m_out, v_out):
        g = p_ref[0].astype(F32)
        for s in range(1, G):
            g = g + p_ref[s].astype(F32)
        m2 = ADAM_B1 * m_ref[...] + (1.0 - ADAM_B1) * g
        v2 = ADAM_B2 * v_ref[...] + (1.0 - ADAM_B2) * (g * g)
        m_hat = m2 / bc1
        v_hat = v2 / bc2
        g_out[...] = g
        d_out[...] = -ADAM_LR * (m_hat / (jnp.sqrt(v_hat) + ADAM_EPS) + ADAM_WD * w_ref[...])
        m_out[...] = m2
        v_out[...] = v2

    blk = pl.BlockSpec((tr, C), lambda i: (i, 0))
    return pl.pallas_call(
        kern, name=name, grid=(R // tr,),
        in_specs=[pl.BlockSpec((G, tr, C), lambda i: (0, i, 0)), blk, blk, blk],
        out_specs=[blk] * 4, out_shape=[jax.ShapeDtypeStruct((R, C), F32)] * 4,
        compiler_params=_params(),
    )(parts, w, m, v)


def _ada_fwd(c_all, w_loc, b_loc, tn):
    B, D = c_all.shape
    N = w_loc.shape[1]

    def kern(c_ref, w_ref, b_ref, o_ref):
        cc = c_ref[...]
        act = cc * _sigmoid(cc)
        o_ref[...] = _dot(act, w_ref[...], "NN") + b_ref[...]

    return pl.pallas_call(
        kern, name="ada_fwd", grid=(N // tn,),
        in_specs=[pl.BlockSpec((B, D), lambda j: (0, 0)), pl.BlockSpec((D, tn), lambda j: (0, j)),
                  pl.BlockSpec((1, tn), lambda j: (0, j))],
        out_specs=pl.BlockSpec((B, tn), lambda j: (0, j)),
        out_shape=jax.ShapeDtypeStruct((B, N), F32), compiler_params=_params(),
    )(c_all, w_loc, b_loc)


def _ada_bwd(c_all, dmod_loc, tn):
    B, D = c_all.shape
    N = dmod_loc.shape[1]

    def kern(c_ref, d_ref, o_ref):
        cc = c_ref[...]
        act = cc * _sigmoid(cc)
        o_ref[...] = _dot(act, d_ref[...], "TN")

    return pl.pallas_call(
        kern, name="ada_bwd", grid=(N // tn,),
        in_specs=[pl.BlockSpec((B, D), lambda j: (0, 0)), pl.BlockSpec((B, tn), lambda j: (0, j))],
        out_specs=pl.BlockSpec((D, tn), lambda j: (0, j)),
        out_shape=jax.ShapeDtypeStruct((D, N), F32), compiler_params=_params(),
    )(c_all, dmod_loc)


def _norm_mod_fwd(name, x, g, sc, sh, T, tb):
    D = x.shape[1]

    def body(xb, gb, scb, shb):
        n = (xb * _rstd(xb)) * gb
        return n * (1.0 + scb) + shb

    return _rowwise(name, body, T, tb, [(x, D, 0)], [g, sc, sh], [(D, BF16)], [])[0]


def _norm_mod_bwd(name, x, dhm, dres, g, sc, T, tb):
    D = x.shape[1]

    def body(xb, db, rb, gb, scb):
        r = _rstd(xb)
        xh = xb * r
        n = xh * gb
        dn = db * (1.0 + scb)
        dxh = dn * gb
        dx = rb + r * (dxh - xh * jnp.mean(dxh * xh, axis=-1, keepdims=True))
        return (dx, jnp.sum(db, axis=0, keepdims=True), jnp.sum(db * n, axis=0, keepdims=True),
                jnp.sum(dn * xh, axis=0, keepdims=True))

    return _rowwise(name, body, T, tb, [(x, D, 0), (dhm, D, 0), (dres, D, 0)], [g, sc],
                    [(D, F32)], [D, D, D])


def _residual_bwd(name, dx, f, gate, coef, T, tb):
    D = dx.shape[1]

    def body(db, fb, gb):
        return (coef * gb) * db, jnp.sum((coef * fb) * db, axis=0, keepdims=True)

    return _rowwise(name, body, T, tb, [(dx, D, 0), (f, D, 0)], [gate], [(D, BF16)], [D])


def _final_loss(x, tgt, g, T, tb):
    D = x.shape[1]

    def body(xb, tb_, gb):
        r = _rstd(xb)
        xh = xb * r
        err = xh * gb - tb_
        loss = 0.5 * jnp.sum(jnp.mean(err * err, axis=-1, keepdims=True), axis=0, keepdims=True)
        dy = err * (1.0 / D)
        dxh = dy * gb
        dx = r * (dxh - xh * jnp.mean(dxh * xh, axis=-1, keepdims=True))
        return dx, jnp.sum(dy * xh, axis=0, keepdims=True), jnp.broadcast_to(loss, (1, 128))

    return _rowwise("final_loss", body, T, tb, [(x, D, 0), (tgt, D, 0)], [g], [(D, F32)], [D, 128])


def _ffn_up(name, hm, wi, T, tm, comm=None, parity=None, prev=None):
    D = hm.shape[1]
    Ws = wi.shape[2]
    half = wi.shape[0] // 2
    n_groups = half if parity is None else half // 2

    def group(g):
        return g if parity is None else 2 * g + parity

    n_sub = 2 if tm % 32 == 0 else 1
    subs = [pl.ds(r * (tm // n_sub), tm // n_sub) for r in range(n_sub)]

    def kern(h_ref, wa_ref, wb_ref, *refs):
        a_ref, b_ref, hid_ref = refs[-3:]
        wa, wb = wa_ref[...], wb_ref[...]
        ab = [(_dot(h_ref[rows, :], wa, "NN"), _dot(h_ref[rows, :], wb, "NN")) for rows in subs]
        for rows, (a, b) in zip(subs, ab):
            a_ref[rows, :] = a
            b_ref[rows, :] = b
            hid_ref[rows, :] = ((a * _sigmoid(a)) * b).astype(BF16)

    o_spec = pl.BlockSpec((None, tm, Ws), lambda g, i: (group(g), i, 0))
    extra = {} if prev is None else dict(input_output_aliases={3: 0, 4: 1, 5: 2})
    return _pallas(
        kern, comm=comm, name=name, grid=(n_groups, T // tm),
        in_specs=[pl.BlockSpec((tm, D), lambda g, i: (i, 0)),
                  pl.BlockSpec((None, D, Ws), lambda g, i: (group(g), 0, 0)),
                  pl.BlockSpec((None, D, Ws), lambda g, i: (group(g) + half, 0, 0))]
                 + ([] if prev is None else [ANY] * 3),
        out_specs=[o_spec] * 3,
        out_shape=[jax.ShapeDtypeStruct((half, T, Ws), F32)] * 2 + [jax.ShapeDtypeStruct((half, T, Ws), BF16)],
        compiler_params=_params(), **extra,
    )(hm, wi, wi, *([] if prev is None else prev))


def _ffn_down_bwd(name, df, wo, a, b, T, tm, comm=None):
    D = df.shape[1]
    half, _, Ws = a.shape

    n_sub = 2 if tm % 32 == 0 else 1
    subs = [pl.ds(r * (tm // n_sub), tm // n_sub) for r in range(n_sub)]

    def kern(df_ref, wo_ref, a_ref, b_ref, dp_ref):
        wo_blk = wo_ref[...]
        dhid = [_dot(df_ref[rows, :], wo_blk, "NT") for rows in subs]
        for rows, dh in zip(subs, dhid):
            av = a_ref[rows, :]
            s = _sigmoid(av)
            silu = av * s
            dp_ref[0, rows, :] = (dh * b_ref[rows, :] * (s + silu * (1.0 - s))).astype(BF16)
            dp_ref[1, rows, :] = (dh * silu).astype(BF16)

    act = pl.BlockSpec((None, tm, Ws), lambda g, i: (g, i, 0))
    return _pallas(
        kern, comm=comm, name=name, grid=(half, T // tm),
        in_specs=[pl.BlockSpec((tm, D), lambda g, i: (i, 0)),
                  pl.BlockSpec((None, Ws, D), lambda g, i: (g, 0, 0)), act, act],
        out_specs=pl.BlockSpec((2, None, tm, Ws), lambda g, i: (0, g, i, 0)),
        out_shape=jax.ShapeDtypeStruct((2, half, T, Ws), BF16),
        compiler_params=_params(),
    )(df, wo, a, b)


def _ffn_fwd(tag, x, norm_g, sh, sc, gate, wi, wo_of, T, up_comm=None, down_comm=None, wi_odd_comm=None):
    tb = min(256, T)
    tm = min(512, T)
    hm = _norm_mod_fwd(tag + "_norm_fwd", x, norm_g, sc, sh, T, tb)
    if wi_odd_comm is None:
        (a, b, hid), got_up = _hosted(up_comm, _ffn_up(tag + "_up", hm, wi, T, tm, comm=up_comm))
    else:
        even, (wi_odd,) = _ffn_up(tag + "_up_even", hm, wi, T, tm, comm=wi_odd_comm, parity=0)
        (a, b, hid), got_up = _hosted(up_comm, _ffn_up(tag + "_up_odd", hm, wi_odd, T, tm, comm=up_comm,
                                                       parity=1, prev=even))
        merge = _merge_comm(wi, wi_odd)
        down_comm = merge if down_comm is None else _join(down_comm, merge)
    wo = wo_of(got_up)
    (f, x_out), got_down = _hosted(down_comm, _mm_groups(tag + "_down", hid, wo, "NN", 512, 512,
                                                         residual=(x, gate, 0.5), comm=down_comm))
    if wi_odd_comm is not None:
        wi, got_down = got_down[-1], got_down[:-1]
    return x_out, (x, hm, a, b, hid, f), wi, wo, got_down


TILE_W_IN = (128, 512)
TILE_W_OUT = (16, 688)
TILE_MIX_OUT = (64, 256)
TILE_POOL = (128, 128)


def _reduce_level1(tag, parts, core, tiles, host=None):
    comm = _sibling_comm(parts)
    if host is None:
        res, got = None, _standalone(tag + "_sibling", comm)
    else:
        res, got = host(comm)
    sums = [_pair_add("%s_pair_add%d" % (tag, k), p, g, core, min(t[1], p.shape[1]))
            for k, (p, g, t) in enumerate(zip(parts, got, tiles))]
    return res, sums


def _ffn_bwd(tag, dx_out, saved, norm_g, sc, gate, wi, wo, T, core, ride_sums=None, defer_dwi=False):
    x, hm, a, b, hid, f = saved
    tb = min(256, T)
    D = x.shape[1]
    df, dgate = _residual_bwd(tag + "_res_bwd", dx_out, f, gate, 0.5, T, tb)
    dwo = _mm(tag + "_dwo", hid, df, "TN", BF16, 2048, 512, T, ga=True, gmode="batch").reshape(N_DEV, -1, D)
    n_ride = 0 if ride_sums is None else len(ride_sums)

    def down_bwd_call(comm):
        if n_ride:
            comm = _join(comm, _chip_comm(ride_sums))
        res, got = _ffn_down_bwd(tag + "_down_bwd", df, wo, a, b, T, min(512, T), comm=comm)
        return (res, got[len(got) - n_ride:]), got[:len(got) - n_ride]

    (dproj, ride_got), (dwo_sum,) = _reduce_level1(tag + "_dwo", [dwo], core, [TILE_W_OUT], host=down_bwd_call)
    dproj = dproj.reshape((2 * dproj.shape[1],) + dproj.shape[2:])
    dwi, (dwo_got,) = _mm(tag + "_dwi", hm, dproj, "TN", BF16, 512, 2048, T, gb=True, gmode="batch",
                          comm=_chip_comm([dwo_sum]))

    def dhm_call(comm):
        return _mm_groups(tag + "_dhm", dproj, wi, "NT", 512, 512, comm=comm)

    if defer_dwi:
        dhm, (dwi_out,) = _reduce_level1(tag + "_dwi", [dwi], core, [TILE_W_IN], host=dhm_call)
    else:
        _, (dwi_sum,) = _reduce_level1(tag + "_dwi", [dwi], core, [TILE_W_IN])
        dhm, (dwi_out,) = dhm_call(_chip_comm([dwi_sum]))
    dx, dsh, dsc, dng = _norm_mod_bwd(tag + "_norm_bwd", x, dhm, dx_out, norm_g, sc, T, tb)
    return dx, (dsh, dsc, dgate, dng), dwi_out, dwo_got, ride_got


def _heads(fn, *arrs):
    outs = [fn(*[a[:, h * HEAD_DIM:(h + 1) * HEAD_DIM] for a in arrs]) for h in range(N_HEADS)]
    return outs


def _qknorm_fwd(proj, gq, gk, T, tb):
    W = N_HEADS * HEAD_DIM

    def body(q, k, v, gqb, gkb):
        qn = jnp.concatenate(_heads(lambda t: (t * _rstd(t)) * gqb, q), axis=1)
        kn = jnp.concatenate(_heads(lambda t: (t * _rstd(t)) * gkb, k), axis=1)
        return qn, kn, v

    return _rowwise("qknorm_fwd", body, T, tb, [(proj, W, 0), (proj, W, 1), (proj, W, 2)], [gq, gk],
                    [(W, BF16)] * 3, [])


def _qknorm_bwd(proj, dqn, dkn, gq, gk, T, tb):
    W = N_HEADS * HEAD_DIM

    def one(t, dt, g):
        r = _rstd(t)
        th = t * r
        dth = dt * g
        d = r * (dth - th * jnp.mean(dth * th, axis=-1, keepdims=True))
        return d, jnp.sum(dt * th, axis=0, keepdims=True)

    def body(q, k, dq, dk, gqb, gkb):
        rq = _heads(lambda t, dt: one(t, dt, gqb), q, dq)
        rk = _heads(lambda t, dt: one(t, dt, gkb), k, dk)
        return (jnp.concatenate([r[0] for r in rq], axis=1), jnp.concatenate([r[0] for r in rk], axis=1),
                sum(r[1] for r in rq), sum(r[1] for r in rk))

    return _rowwise("qknorm_bwd", body, T, tb, [(proj, W, 0), (proj, W, 1), (dqn, W, 0), (dkn, W, 0)],
                    [gq, gk], [(W, BF16)] * 2, [HEAD_DIM, HEAD_DIM])


def _log_sigmoid(z):
    return jnp.minimum(z, 0.0) - jnp.log(1.0 + jnp.exp(-jnp.abs(z)))


def _fgate_fwd(proj, fcol, b_pad, T):
    nblk = T // 128

    def kern(f_ref, b_ref, o_ref):
        r = lax.broadcasted_iota(jnp.int32, (128, 128), 0)
        c = lax.broadcasted_iota(jnp.int32, (128, 128), 1)
        tri = (r >= c).astype(F32)
        carry = jnp.zeros((1, 128), F32)
        for k in range(nblk):
            rows = pl.ds(k * 128, 128)
            lf = _log_sigmoid(f_ref[rows, :] + b_ref[...])
            o_ref[rows, :] = jnp.dot(tri, lf, precision=lax.Precision.HIGHEST, preferred_element_type=F32) + carry
            carry = carry + jnp.sum(lf, axis=0, keepdims=True)

    return pl.pallas_call(
        kern, name="fgate_fwd", grid=(1,),
        in_specs=[pl.BlockSpec((T, 128), lambda i: (0, fcol)), pl.BlockSpec((1, 128), lambda i: (0, 0))],
        out_specs=pl.BlockSpec((T, 128), lambda i: (0, 0)),
        out_shape=jax.ShapeDtypeStruct((T, 128), F32), compiler_params=_params(),
    )(proj, b_pad)


def _fgate_bwd(proj, fcol, b_pad, dF, T):
    nblk = T // 128

    def kern(f_ref, b_ref, d_ref, o_ref, db_ref):
        r = lax.broadcasted_iota(jnp.int32, (128, 128), 0)
        c = lax.broadcasted_iota(jnp.int32, (128, 128), 1)
        tri = (c >= r).astype(F32)
        carry = jnp.zeros((1, 128), F32)
        db = jnp.zeros((1, 128), F32)
        for k in reversed(range(nblk)):
            rows = pl.ds(k * 128, 128)
            dblk = d_ref[rows, :]
            rc = jnp.dot(tri, dblk, precision=lax.Precision.HIGHEST, preferred_element_type=F32) + carry
            carry = carry + jnp.sum(dblk, axis=0, keepdims=True)
            z = f_ref[rows, :] + b_ref[...]
            dz = rc * (1.0 / (1.0 + jnp.exp(z)))
            o_ref[rows, :] = dz
            db = db + jnp.sum(dz, axis=0, keepdims=True)
        db_ref[...] = db

    return pl.pallas_call(
        kern, name="fgate_bwd", grid=(1,),
        in_specs=[pl.BlockSpec((T, 128), lambda i: (0, fcol)), pl.BlockSpec((1, 128), lambda i: (0, 0)),
                  pl.BlockSpec((T, 128), lambda i: (0, 0))],
        out_specs=[pl.BlockSpec((T, 128), lambda i: (0, 0)), pl.BlockSpec((1, 128), lambda i: (0, 0))],
        out_shape=[jax.ShapeDtypeStruct((T, 128), F32), jax.ShapeDtypeStruct((1, 128), F32)],
        compiler_params=_params(),
    )(proj, b_pad, dF)


LOG2E = 1.4426950408889634


def _gate_bias(ft, fh, h):
    lane = lax.broadcasted_iota(jnp.int32, ft.shape, 1)
    fq = jnp.sum(jnp.where(lane == h, ft, 0.0), axis=1, keepdims=True)
    f0 = jnp.max(fq, axis=0, keepdims=True)
    sub = lax.broadcasted_iota(jnp.int32, fh.shape, 0)
    fk = jnp.sum(jnp.where(sub == h, fh, 0.0), axis=0, keepdims=True)
    return (f0 - fk) * LOG2E


HEADS_PER_STEP = 2


def _tri_rows(s, nb):
    i = sum((s >= k * (k + 1) // 2).astype(jnp.int32) for k in range(1, nb))
    return i, s - (i * (i + 1)) // 2


def _tri_cols(s, nb):
    j = sum((s >= k * nb - (k * (k - 1)) // 2).astype(jnp.int32) for k in range(1, nb))
    return j, j + s - (j * nb - (j * (j - 1)) // 2)


def _causal_bias(blk):
    row = lax.broadcasted_iota(jnp.int32, (blk, blk), 0)
    col = lax.broadcasted_iota(jnp.int32, (blk, blk), 1)
    return jnp.where(row >= col, 0.0, NEG)


def _attn_fwd(qn, kn, vb, f_tm, f_hm, T, blk, comm=None):
    nb = T // blk
    scale = HEAD_DIM ** -0.5
    W = N_HEADS * HEAD_DIM
    G = HEADS_PER_STEP
    lanes = [slice(g * HEAD_DIM, (g + 1) * HEAD_DIM) for g in range(G)]

    def kern(q_ref, k_ref, v_ref, ft_ref, fh_ref, o_ref, lse_ref, m_scr, l_scr, acc_scr):
        hp = pl.program_id(0)
        i, j = _tri_rows(pl.program_id(1), nb)

        @pl.when(j == 0)
        def _():
            m_scr[...] = jnp.full_like(m_scr, NEG)
            l_scr[...] = jnp.zeros_like(l_scr)
            acc_scr[...] = jnp.zeros_like(acc_scr)

        def block(diagonal):
            ft, fh = ft_ref[...], fh_ref[...]
            s = [_dot(q_ref[:, sl], k_ref[:, sl], "NT") * (scale * LOG2E) + _gate_bias(ft, fh, hp * G + g)
                 for g, sl in enumerate(lanes)]
            if diagonal:
                mask = _causal_bias(blk)
                s = [sg + mask for sg in s]
            m_prev = [m_scr[g] for g in range(G)]
            m_new = [jnp.maximum(mp, jnp.max(sg, axis=1, keepdims=True)) for mp, sg in zip(m_prev, s)]
            alpha = [jnp.exp2(mp - mn) for mp, mn in zip(m_prev, m_new)]
            p = [jnp.exp2(sg - mn) for sg, mn in zip(s, m_new)]
            for g, sl in enumerate(lanes):
                l_scr[g] = alpha[g] * l_scr[g] + jnp.sum(p[g], axis=1, keepdims=True)
                acc_scr[:, sl] = alpha[g] * acc_scr[:, sl] + _dot(p[g], v_ref[:, sl], "NN")
                m_scr[g] = m_new[g]

        @pl.when(j < i)
        def _():
            block(False)

        @pl.when(j == i)
        def _():
            block(True)
            for g, sl in enumerate(lanes):
                l = l_scr[g]
                o_ref[:, sl] = acc_scr[:, sl] / l
                lse_ref[:, sl] = jnp.broadcast_to(m_scr[g] + jnp.log2(l), (blk, HEAD_DIM))

    qspec = pl.BlockSpec((blk, G * HEAD_DIM), lambda h, s: (_tri_rows(s, nb)[0], h))
    kspec = pl.BlockSpec((blk, G * HEAD_DIM), lambda h, s: (_tri_rows(s, nb)[1], h))
    return _pallas(
        kern, comm=comm, name="attn_fwd", grid=(N_HEADS // G, nb * (nb + 1) // 2),
        in_specs=[qspec, kspec, kspec,
                  pl.BlockSpec((blk, 128), lambda h, s: (_tri_rows(s, nb)[0], 0)),
                  pl.BlockSpec((N_HEADS, blk), lambda h, s: (0, _tri_rows(s, nb)[1]))],
        out_specs=[qspec, qspec],
        out_shape=[jax.ShapeDtypeStruct((T, W), F32)] * 2,
        scratch_shapes=[pltpu.VMEM((G, blk, 1), F32), pltpu.VMEM((G, blk, 1), F32),
                        pltpu.VMEM((blk, G * HEAD_DIM), F32)],
        compiler_params=_params(),
    )(qn, kn, vb, f_tm, f_hm)


def _attn_bwd(qn, kn, vb, do, lse, delta, f_tm, f_hm, T, blk, comm=None):
    nb = T // blk
    scale = HEAD_DIM ** -0.5
    W = N_HEADS * HEAD_DIM
    G = HEADS_PER_STEP
    lanes = [slice(g * HEAD_DIM, (g + 1) * HEAD_DIM) for g in range(G)]

    def kern(q_ref, k_ref, v_ref, do_ref, lse_ref, dl_ref, ft_ref, fh_ref,
             dq_ref, dfq_ref, dk_ref, dv_ref, df_ref, dq_scr, dfq_scr, dk_scr, dv_scr, df_scr):
        hp = pl.program_id(0)
        j, i = _tri_cols(pl.program_id(1), nb)

        @pl.when((j == 0) & (i == 0))
        def _():
            dq_scr[...] = jnp.zeros_like(dq_scr)
            dfq_scr[...] = jnp.zeros_like(dfq_scr)

        @pl.when(i == j)
        def _():
            dk_scr[...] = jnp.zeros_like(dk_scr)
            dv_scr[...] = jnp.zeros_like(dv_scr)
            df_scr[...] = jnp.zeros_like(df_scr)

        def block(diagonal):
            ft, fh = ft_ref[...], fh_ref[...]
            rows = pl.ds(pl.multiple_of(i * blk, blk), blk)
            q = [q_ref[:, sl] for sl in lanes]
            k = [k_ref[:, sl] for sl in lanes]
            dob = [do_ref[:, sl].astype(BF16) for sl in lanes]
            s = [_dot(q[g], k[g], "NT") * (scale * LOG2E) + _gate_bias(ft, fh, hp * G + g) for g in range(G)]
            if diagonal:
                mask = _causal_bias(blk)
                s = [sg + mask for sg in s]
            p = [jnp.exp2(s[g] - lse_ref[:, sl.start:sl.start + 1]) for g, sl in enumerate(lanes)]
            dp = [_dot(dob[g], v_ref[:, sl], "NT") for g, sl in enumerate(lanes)]
            ds = [p[g] * (dp[g] - dl_ref[:, sl.start:sl.start + 1]) for g, sl in enumerate(lanes)]
            dsb = [d.astype(BF16) for d in ds]
            for g, sl in enumerate(lanes):
                dv_scr[:, sl] += _dot(p[g], dob[g], "TN")
                dk_scr[:, sl] += _dot(dsb[g], q[g], "TN") * scale
                dq_scr[rows, sl] += _dot(dsb[g], k[g], "NN") * scale
                df_scr[g] += jnp.sum(ds[g], axis=0, keepdims=True)
                dfq_scr[g, rows, :] += jnp.sum(ds[g], axis=1, keepdims=True)

        @pl.when(i > j)
        def _():
            block(False)

        @pl.when(i == j)
        def _():
            block(True)

        @pl.when(i == nb - 1)
        def _():
            dk_ref[...] = dk_scr[...]
            dv_ref[...] = dv_scr[...]
            df_ref[...] = -df_scr[...]

        @pl.when((j == nb - 1) & (i == nb - 1))
        def _():
            dq_ref[...] = dq_scr[...]
            for g, sl in enumerate(lanes):
                dfq_ref[:, sl] = jnp.broadcast_to(dfq_scr[g], (T, HEAD_DIM))

    qspec = pl.BlockSpec((blk, G * HEAD_DIM), lambda h, s: (_tri_cols(s, nb)[1], h))
    full = pl.BlockSpec((T, G * HEAD_DIM), lambda h, s: (0, h))
    kspec = pl.BlockSpec((blk, G * HEAD_DIM), lambda h, s: (_tri_cols(s, nb)[0], h))
    return _pallas(
        kern, comm=comm, name="attn_bwd", grid=(N_HEADS // G, nb * (nb + 1) // 2),
        in_specs=[qspec, kspec, kspec, qspec, qspec, qspec,
                  pl.BlockSpec((blk, 128), lambda h, s: (_tri_cols(s, nb)[1], 0)),
                  pl.BlockSpec((N_HEADS, blk), lambda h, s: (0, _tri_cols(s, nb)[0]))],
        out_specs=[full, full, kspec, kspec, pl.BlockSpec((G, 1, blk), lambda h, s: (h, 0, _tri_cols(s, nb)[0]))],
        out_shape=[jax.ShapeDtypeStruct((T, W), F32)] * 4 + [jax.ShapeDtypeStruct((N_HEADS, 1, T), F32)],
        scratch_shapes=[pltpu.VMEM((T, G * HEAD_DIM), F32), pltpu.VMEM((G, T, 1), F32),
                        pltpu.VMEM((blk, G * HEAD_DIM), F32), pltpu.VMEM((blk, G * HEAD_DIM), F32),
                        pltpu.VMEM((G, 1, blk), F32)],
        compiler_params=_params(),
    )(qn, kn, vb, do, lse, delta, f_tm, f_hm)


def _attn_delta(o, do, T, tb):
    W = N_HEADS * HEAD_DIM

    def body(ob, dob):
        return jnp.concatenate(
            _heads(lambda a, b: jnp.broadcast_to(jnp.sum(a * b, axis=1, keepdims=True), a.shape), ob, dob), axis=1)

    return _rowwise("attn_delta", body, T, tb, [(o, W, 0), (do, W, 0)], [], [(W, F32)], [])[0]


def _window_select(s, g, shift):
    picks = []
    for k in (1, 2, 4, 8):
        s = s + shift(s, k)
        picks.append(s)
    return jnp.where(g == 0, picks[0], jnp.where(g == 1, picks[1], jnp.where(g == 2, picks[2], picks[3])))


def _group_window(g):
    return jnp.where(g == 0, POOL_WINDOWS[0], jnp.where(g == 1, POOL_WINDOWS[1],
                     jnp.where(g == 2, POOL_WINDOWS[2], POOL_WINDOWS[3])))


def _pool_fwd(proj, ucol, pw, ps, T, tb):
    C = POOL_GROUP_DIM
    n_g = len(POOL_WINDOWS)

    def kern(uc_ref, up_ref, pw_ref, ps_ref, pooled_ref, out_ref):
        g, i = pl.program_id(0), pl.program_id(1)
        uc = uc_ref[...]
        t2 = (i - 1) * tb + lax.broadcasted_iota(jnp.int32, (2 * tb, C), 0)
        u2 = jnp.where(t2 >= 0, jnp.concatenate([up_ref[...], uc], axis=0), 0.0)
        sums = _window_select(u2, g, lambda s, k: pltpu.roll(s, k, 0))[tb:, :]
        count = jnp.minimum(t2[tb:, :] + 1, _group_window(g)).astype(F32)
        pooled = sums / count - uc
        pooled_ref[...] = pooled.astype(BF16)
        out_ref[...] = _dot(pooled, pw_ref[...], "NN") * ps_ref[...]

    ospec = pl.BlockSpec((tb, C), lambda g, i: (i, g))
    return pl.pallas_call(
        kern, name="pool_fwd", grid=(n_g, T // tb),
        in_specs=[pl.BlockSpec((tb, C), lambda g, i: (i, ucol + g)),
                  pl.BlockSpec((tb, C), lambda g, i: (jnp.maximum(i - 1, 0), ucol + g)),
                  pl.BlockSpec((None, C, C), lambda g, i: (g, 0, 0)),
                  pl.BlockSpec((1, C), lambda g, i: (0, g))],
        out_specs=[ospec, ospec],
        out_shape=[jax.ShapeDtypeStruct((T, n_g * C), BF16), jax.ShapeDtypeStruct((T, n_g * C), F32)],
        compiler_params=_params(),
    )(proj, proj, pw, ps)


def _pool_bwd(dmix_in, dcol, pooled, pw, ps, T, tb):
    C = POOL_GROUP_DIM
    n_g = len(POOL_WINDOWS)
    nb = T // tb

    def kern(dc_ref, dn_ref, pooled_ref, pw_ref, ps_ref, du_ref, dpw_ref, dps_ref):
        g, i = pl.program_id(0), pl.program_id(1)
        dc = dc_ref[...]
        scale = ps_ref[...]
        t2 = i * tb + lax.broadcasted_iota(jnp.int32, (2 * tb, C), 0)
        d2 = jnp.where(t2 < T, jnp.concatenate([dc, dn_ref[...]], axis=0) * scale, 0.0)
        dpooled2 = _dot(d2, pw_ref[...], "NT")
        count = jnp.minimum(t2 + 1, _group_window(g)).astype(F32)
        sums = _window_select(dpooled2 / count, g, lambda s, k: pltpu.roll(s, 2 * tb - k, 0))
        du_ref[...] = (sums[:tb, :] - dpooled2[:tb, :]).astype(BF16)
        pooled = pooled_ref[...]
        p = _dot(pooled, pw_ref[...], "NN")
        dps = jnp.sum(dc * p, axis=0, keepdims=True)
        dpw = _dot(pooled, d2[:tb, :], "TN")

        @pl.when(i == 0)
        def _():
            dps_ref[...] = dps
            dpw_ref[...] = dpw

        @pl.when(i > 0)
        def _():
            dps_ref[...] += dps
            dpw_ref[...] += dpw

    return pl.pallas_call(
        kern, name="pool_bwd", grid=(n_g, nb),
        in_specs=[pl.BlockSpec((tb, C), lambda g, i: (i, dcol + g)),
                  pl.BlockSpec((tb, C), lambda g, i: (jnp.minimum(i + 1, nb - 1), dcol + g)),
                  pl.BlockSpec((tb, C), lambda g, i: (i, g)),
                  pl.BlockSpec((None, C, C), lambda g, i: (g, 0, 0)),
                  pl.BlockSpec((1, C), lambda g, i: (0, g))],
        out_specs=[pl.BlockSpec((tb, C), lambda g, i: (i, g)),
                   pl.BlockSpec((None, C, C), lambda g, i: (g, 0, 0)),
                   pl.BlockSpec((1, C), lambda g, i: (0, g))],
        out_shape=[jax.ShapeDtypeStruct((T, n_g * C), BF16), jax.ShapeDtypeStruct((n_g, C, C), F32),
                   jax.ShapeDtypeStruct((1, n_g * C), F32)],
        compiler_params=_params(),
    )(dmix_in, dmix_in, pooled, pw, ps)


D_QKV = 3 * N_HEADS * HEAD_DIM
D_U = len(POOL_WINDOWS) * POOL_GROUP_DIM
F_PAD = 128
D_PROJ = D_QKV + D_U + F_PAD


def _perm_w_in(w):
    pad = jnp.zeros((w.shape[0], F_PAD - N_HEADS), w.dtype)
    return jnp.concatenate([w[:, :D_QKV], w[:, D_QKV + N_HEADS:], w[:, D_QKV:D_QKV + N_HEADS], pad], axis=1)


def _unperm_w_in(w):
    return jnp.concatenate([w[:, :D_QKV], w[:, D_QKV + D_U:D_QKV + D_U + N_HEADS], w[:, D_QKV:D_QKV + D_U]], axis=1)


def _mixer_fwd(x, norm_g, sh, sc, gate, w_in_p, b_pad, gq, gk, late_weights, ps, T, proj_comm, attn_comm):
    tb = min(256, T)
    blk = min(512, T)
    hm = _norm_mod_fwd("mix_norm_fwd", x, norm_g, sc, sh, T, tb)
    proj, got_proj = _mm("mix_proj", hm, w_in_p, "NN", F32, 512, D_PROJ // 3, 2048, comm=proj_comm)
    pw, w_out = late_weights(got_proj)
    qn, kn, vb = _qknorm_fwd(proj, gq, gk, T, tb)
    fcol = (D_QKV + D_U) // 128
    f_tm = _fgate_fwd(proj, fcol, b_pad, T)
    f_hm = f_tm[:, :N_HEADS].T
    (o, lse), got = _attn_fwd(qn, kn, vb, f_tm, f_hm, T, blk, comm=attn_comm)
    pooled, pool_o = _pool_fwd(proj, D_QKV // POOL_GROUP_DIM, pw, ps, T, tb)
    mix_in = jnp.concatenate([o.astype(BF16), pool_o.astype(BF16)], axis=1)
    mix, x_out = _mm_groups("mix_out", mix_in[None], w_out[None], "NN", 512, 512, residual=(x, gate, 1.0))
    return x_out, (x, hm, proj, qn, kn, vb, f_tm, f_hm, o, lse, pooled, mix_in, mix), pw, w_out, got


def _mixer_bwd(dx_out, saved, norm_g, sc, gate, w_in_p, b_pad, gq, gk, pw, ps, w_out, T, core, ride_sums):
    x, hm, proj, qn, kn, vb, f_tm, f_hm, o, lse, pooled, mix_in, mix = saved
    tb = min(256, T)
    blk = min(512, T)
    W = N_HEADS * HEAD_DIM
    D = x.shape[1]
    n_g = len(POOL_WINDOWS)
    dmix, dgate = _residual_bwd("mix_res_bwd", dx_out, mix, gate, 1.0, T, tb)
    dmix_in = _mm("mix_out_bwd", dmix, w_out, "NT", F32, 512, 2048, 2048)
    dw_out = _mm("mix_dw_out", mix_in, dmix, "TN", BF16, 512, 1024, T)
    delta = _attn_delta(o, dmix_in, T, tb)
    (dqn, dfq, dkn, dv, dfk), ride_got = _attn_bwd(qn, kn, vb, dmix_in, lse, delta, f_tm, f_hm, T, blk,
                                                   comm=_chip_comm(ride_sums))
    dq, dk, dgq, dgk = _qknorm_bwd(proj, dqn, dkn, gq, gk, T, tb)
    dF = jnp.pad(dfq[:, ::HEAD_DIM] + dfk.reshape(N_HEADS, T).T, ((0, 0), (0, F_PAD - N_HEADS)))
    fcol = (D_QKV + D_U) // 128
    dfl, dbf = _fgate_bwd(proj, fcol, b_pad, dF, T)
    du, dpw, dps = _pool_bwd(dmix_in, W // POOL_GROUP_DIM, pooled, pw, ps, T, tb)
    dproj = jnp.concatenate([dq, dk, dv.astype(BF16), du, dfl.astype(BF16)], axis=1)
    dw_in_p = _mm("mix_dw_in", hm, dproj, "TN", BF16, 512, D_PROJ // 3, T)
    pw_rows = POOL_GROUP_DIM // N_DEV
    slabs = [jnp.transpose(_unperm_w_in(dw_in_p).reshape(D, N_DEV, -1), (1, 0, 2)),
             jnp.transpose(dpw.astype(BF16).reshape(n_g, N_DEV, pw_rows, POOL_GROUP_DIM),
                           (1, 0, 2, 3)).reshape(N_DEV, n_g * pw_rows, POOL_GROUP_DIM),
             dw_out.reshape(N_DEV, -1, D)]
    _, sums = _reduce_level1("mix", slabs, core, [TILE_W_IN, TILE_POOL, TILE_MIX_OUT])
    dhm = _mm("mix_proj_bwd", dproj, w_in_p, "NT", F32, 512, 512, D_PROJ)
    dx, dsh, dsc, dng = _norm_mod_bwd("mix_norm_bwd", x, dhm, dx_out, norm_g, sc, T, tb)
    return dx, (dsh, dsc, dgate, dng), sums, dps, dgq, dgk, dbf, ride_got


def kernel(x, c, w_ada, b_ada, ffn1_norm_g, ffn1_w_in, ffn1_w_out, mix_norm_g, w_in, b_forget, q_norm_g, k_norm_g, pool_w, pool_scale, w_out, ffn2_norm_g, ffn2_w_in, ffn2_w_out, final_norm_g, loss_target, m_w_ada, m_b_ada, m_ffn1_norm_g, m_ffn1_w_in, m_ffn1_w_out, m_mix_norm_g, m_w_in, m_b_forget, m_q_norm_g, m_k_norm_g, m_pool_w, m_pool_scale, m_w_out, m_ffn2_norm_g, m_ffn2_w_in, m_ffn2_w_out, m_final_norm_g, v_w_ada, v_b_ada, v_ffn1_norm_g, v_ffn1_w_in, v_ffn1_w_out, v_mix_norm_g, v_w_in, v_b_forget, v_q_norm_g, v_k_norm_g, v_pool_w, v_pool_scale, v_w_out, v_ffn2_norm_g, v_ffn2_w_in, v_ffn2_w_out, v_final_norm_g):
    T, D = x.shape[1], x.shape[2]
    mx, my, mc = _mesh_pos()
    me = _flat(mx, my, mc)
    x0 = x[0]
    tgt = loss_target[0]
    tb = min(256, T)

    core = jnp.reshape(mc, (1,)).astype(jnp.int32)
    half = N_DEV // 2
    n_g = len(POOL_WINDOWS)
    pw_rows = POOL_GROUP_DIM // N_DEV

    def bf(w):
        return w.astype(BF16)

    n_loc = w_ada.shape[2]
    c_all = _standalone("gather_c", _gather_comm([c.reshape(8, D // 8)]))[0].reshape(N_DEV, D)
    b_loc = lax.dynamic_slice_in_dim(b_ada, me * n_loc, n_loc, axis=1)
    mod_loc = _ada_fwd(c_all, w_ada[0], b_loc, n_loc // 3)
    mod_all = _standalone("gather_mod", _gather_comm([mod_loc]))[0]
    mod = lax.dynamic_index_in_dim(mod_all, me, axis=1, keepdims=False).reshape(N_MOD, 1, D)
    sh1, sc1, g1, sh2, sc2, g2, sh3, sc3, g3 = [mod[k] for k in range(N_MOD)]
    b_pad = jnp.pad(b_forget, ((0, 0), (0, F_PAD - N_HEADS)))
    ps = pool_scale

    wi1_shard = bf(ffn1_w_in[0])
    wi1_even = _standalone("gather_ffn1_w_in_even", _gather_comm([wi1_shard], sender_core=0))[0]
    x1, sv1, wi1, wo1, (w_in_g,) = _ffn_fwd(
        "ffn1", x0, ffn1_norm_g, sh1, sc1, g1, wi1_even, lambda got: got[0].reshape(half, -1, D), T,
        up_comm=_gather_comm([bf(ffn1_w_out[0])], forward_at=0.7), down_comm=_gather_comm([bf(w_in[0])], forward_at=0.8),
        wi_odd_comm=_gather_comm([wi1_shard], forward_at=0.7, sender_core=1))
    w_in_p = _perm_w_in(jnp.transpose(w_in_g, (1, 0, 2)).reshape(D, -1))

    def late_weights(got):
        pool_g, w_out_g = got
        pw = jnp.transpose(pool_g.reshape(N_DEV, n_g, pw_rows, POOL_GROUP_DIM),
                           (1, 0, 2, 3)).reshape(n_g, POOL_GROUP_DIM, POOL_GROUP_DIM)
        return pw, w_out_g.reshape(-1, D)

    x2, svm, pw_full, w_out_full, (wi2,) = _mixer_fwd(
        x1, mix_norm_g, sh2, sc2, g2, w_in_p, b_pad, q_norm_g, k_norm_g, late_weights, ps, T,
        proj_comm=_gather_comm([bf(pool_w[0].reshape(-1, POOL_GROUP_DIM)), bf(w_out[0])], forward_at=0.6),
        attn_comm=_gather_comm([bf(ffn2_w_in[0])], forward_at=0.85))
    x3, sv2, wi2, wo2, _ = _ffn_fwd("ffn2", x2, ffn2_norm_g, sh3, sc3, g3, wi2,
                               lambda got: got[0].reshape(half, -1, D), T,
                               up_comm=_gather_comm([bf(ffn2_w_out[0])], forward_at=0.7))
    dx3, dgf, loss_l = _final_loss(x3, tgt, final_norm_g.reshape(1, D), T, tb)
    loss = lax.psum(loss_l[0, 0], ("x", "y", "c"))

    dx2, (dsh3, dsc3, dg3, dn3), dwi2_sum, dwo2, _ = _ffn_bwd(
        "ffn2", dx3, sv2, ffn2_norm_g, sc3, g3, wi2, wo2, T, core, defer_dwi=True)
    dx1, (dsh2, dsc2, dg2, dn2), mix_sums, dps, dgq, dgk, dbf, (dwi2,) = _mixer_bwd(
        dx2, svm, mix_norm_g, sc2, g2, w_in_p, b_pad, q_norm_g, k_norm_g, pw_full, ps, w_out_full, T, core,
        ride_sums=[dwi2_sum])
    dx0, (dsh1, dsc1, dg1, dn1), dwi1, dwo1, (dw_in_r, dpw_r, dw_out_r) = _ffn_bwd(
        "ffn1", dx1, sv1, ffn1_norm_g, sc1, g1, wi1, wo1, T, core, ride_sums=mix_sums)

    received = dict(ffn1_w_in=dwi1, ffn1_w_out=dwo1, w_in=dw_in_r, pool_w=dpw_r, w_out=dw_out_r,
                    ffn2_w_in=dwi2, ffn2_w_out=dwo2)
    moments = dict(ffn1_w_in=(m_ffn1_w_in, v_ffn1_w_in), ffn1_w_out=(m_ffn1_w_out, v_ffn1_w_out),
                   w_in=(m_w_in, v_w_in), pool_w=(m_pool_w, v_pool_w), w_out=(m_w_out, v_w_out),
                   ffn2_w_in=(m_ffn2_w_in, v_ffn2_w_in), ffn2_w_out=(m_ffn2_w_out, v_ffn2_w_out))
    weights = dict(ffn1_w_in=ffn1_w_in, ffn1_w_out=ffn1_w_out, w_in=w_in, pool_w=pool_w, w_out=w_out,
                   ffn2_w_in=ffn2_w_in, ffn2_w_out=ffn2_w_out)
    row_tiles = dict(ffn1_w_in=TILE_W_IN, ffn1_w_out=TILE_W_OUT, w_in=TILE_W_IN, pool_w=TILE_POOL,
                     w_out=TILE_MIX_OUT, ffn2_w_in=TILE_W_IN, ffn2_w_out=TILE_W_OUT)
    results = {}
    for k in received:
        shape = weights[k].shape
        two_d = received[k].shape[1:]
        mk, vk = moments[k]
        outs = _adamw("adamw_" + k, received[k], weights[k].reshape(two_d), mk.reshape(two_d),
                      vk.reshape(two_d), row_tiles[k][0])
        results[k] = [o.reshape(shape) for o in outs]

    dmod = jnp.concatenate([dsh1, dsc1, dg1, dsh2, dsc2, dg2, dsh3, dsc3, dg3], axis=1)
    small_names = ["b_ada", "ffn1_norm_g", "mix_norm_g", "ffn2_norm_g", "final_norm_g", "b_forget",
                   "q_norm_g", "k_norm_g", "pool_scale"]
    small_w = dict(b_ada=b_ada, ffn1_norm_g=ffn1_norm_g, mix_norm_g=mix_norm_g, ffn2_norm_g=ffn2_norm_g,
                   final_norm_g=final_norm_g, b_forget=b_forget, q_norm_g=q_norm_g, k_norm_g=k_norm_g,
                   pool_scale=pool_scale)
    small_m = dict(b_ada=m_b_ada, ffn1_norm_g=m_ffn1_norm_g, mix_norm_g=m_mix_norm_g, ffn2_norm_g=m_ffn2_norm_g,
                   final_norm_g=m_final_norm_g, b_forget=m_b_forget, q_norm_g=m_q_norm_g, k_norm_g=m_k_norm_g,
                   pool_scale=m_pool_scale)
    small_v = dict(b_ada=v_b_ada, ffn1_norm_g=v_ffn1_norm_g, mix_norm_g=v_mix_norm_g, ffn2_norm_g=v_ffn2_norm_g,
                   final_norm_g=v_final_norm_g, b_forget=v_b_forget, q_norm_g=v_q_norm_g, k_norm_g=v_k_norm_g,
                   pool_scale=v_pool_scale)
    small_g = dict(b_ada=dmod, ffn1_norm_g=dn1, mix_norm_g=dn2, ffn2_norm_g=dn3, final_norm_g=dgf,
                   b_forget=dbf[:, :N_HEADS], q_norm_g=dgq, k_norm_g=dgk, pool_scale=dps)
    sizes = [small_w[k].size for k in small_names]
    total = sum(sizes)
    lanes = 8 * 128
    padded = -(-total // lanes) * lanes

    def pack(d):
        flat = jnp.concatenate([d[k].reshape(-1) for k in small_names])
        return jnp.pad(flat, (0, padded - total)).reshape(8, padded // 8)

    small_parts = _standalone("gather_small_grads", _gather_comm([pack(small_g)]))[0]
    s_outs = _adamw("adamw_small", small_parts, pack(small_w), pack(small_m), pack(small_v), 8)
    offs = [0]
    for s in sizes:
        offs.append(offs[-1] + s)
    for idx, k in enumerate(small_names):
        results[k] = [o.reshape(-1)[offs[idx]:offs[idx + 1]].reshape(small_w[k].shape) for o in s_outs]

    dmod_all = small_parts.reshape(N_DEV, padded)[:, :N_MOD * D]
    dmod_loc = lax.dynamic_slice_in_dim(dmod_all, me * n_loc, n_loc, axis=1)
    g_ada = _ada_bwd(c_all, dmod_loc, n_loc // 3)
    a_outs = _adamw("adamw_w_ada", g_ada[None], w_ada[0], m_w_ada[0], v_w_ada[0], 128)
    results["w_ada"] = [o.reshape(w_ada.shape) for o in a_outs]

    order = ["w_ada", "b_ada", "ffn1_norm_g", "ffn1_w_in", "ffn1_w_out", "mix_norm_g", "w_in", "b_forget",
             "q_norm_g", "k_norm_g", "pool_w", "pool_scale", "w_out", "ffn2_norm_g", "ffn2_w_in", "ffn2_w_out",
             "final_norm_g"]
    out = [loss, dx0[None]]
    for part in range(4):
        out += [results[k][part] for k in order]
    return tuple(out)
```

```python
import jax
import jax.numpy as jnp
from jax import lax
from jax.experimental import pallas as pl
from jax.experimental.pallas import tpu as pltpu

F32 = jnp.float32
BF16 = jnp.bfloat16
MESH = pl.DeviceIdType.MESH
ANY = pl.BlockSpec(memory_space=pl.ANY)

N_DEV = 8
EPS = 1e-6
HEAD_DIM = 128
N_HEADS = 8
POOL_WINDOWS = (2, 4, 8, 16)
POOL_GROUP_DIM = 256
N_MOD = 9
ADAM_LR = 0.001
ADAM_B1 = 0.9
ADAM_B2 = 0.999
ADAM_EPS = 1e-08
ADAM_WD = 0.01
ADAM_STEP = 10
NEG = -1e30
VMEM_LIMIT_V7X = 56 * 1024 * 1024


def _params():
    return pltpu.CompilerParams(vmem_limit_bytes=VMEM_LIMIT_V7X)


def _sigmoid(z):
    return 1.0 / (1.0 + jnp.exp(-z))


def _rstd(x):
    return lax.rsqrt(jnp.mean(x * x, axis=-1, keepdims=True) + EPS)


def _mesh_pos():
    return lax.axis_index("x"), lax.axis_index("y"), lax.axis_index("c")


def _flat(px, py, pc):
    return 4 * px + 2 * py + pc


class _Comm:
    def __init__(self, ins, outs, sems, phases):
        self.ins, self.outs, self.sems, self.phases = list(ins), list(outs), list(sems), list(phases)


def _pallas(kern, *, comm=None, **kw):
    if comm is None:
        return pl.pallas_call(kern, **kw)
    grid = tuple(kw["grid"])
    single = not isinstance(kw["out_shape"], (list, tuple))
    out_shape = [kw["out_shape"]] if single else list(kw["out_shape"])
    out_specs = [kw["out_specs"]] if single else list(kw["out_specs"])
    in_specs = list(kw["in_specs"])
    scratch = list(kw.get("scratch_shapes", ()))
    n_in, n_out, n_scr = len(in_specs), len(out_shape), len(scratch)
    n_ci, n_co = len(comm.ins), len(comm.outs)
    strides, n_steps = [], 1
    for g in reversed(grid):
        strides.insert(0, n_steps)
        n_steps *= g

    def wrapped(*refs):
        ins, cins = refs[:n_in], refs[n_in:n_in + n_ci]
        base = n_in + n_ci
        outs, couts = refs[base:base + n_out], refs[base + n_out:base + n_out + n_co]
        base += n_out + n_co
        scr, sems = refs[base:base + n_scr], refs[base + n_scr:]
        step = sum(pl.program_id(d) * strides[d] for d in range(len(grid)))
        for frac, fn in comm.phases:
            if frac < 1.0:
                pl.when(step == int(round(frac * (n_steps - 1))))(lambda fn=fn: fn(cins, couts, sems))
        kern(*ins, *outs, *scr)
        for frac, fn in comm.phases:
            if frac >= 1.0:
                pl.when(step == n_steps - 1)(lambda fn=fn: fn(cins, couts, sems))

    kw = dict(kw, in_specs=in_specs + [ANY] * n_ci, out_specs=out_specs + [ANY] * n_co,
              out_shape=out_shape + comm.outs, scratch_shapes=scratch + comm.sems)
    call = pl.pallas_call(wrapped, **kw)

    def run(*args):
        res = call(*args, *comm.ins)
        main = res[0] if single else list(res[:n_out])
        return main, list(res[n_out:])

    return run


def _join(first, second):
    n_i, n_o, n_s = len(first.ins), len(first.outs), len(first.sems)

    def left(fn):
        return lambda ins, outs, sems: fn(ins[:n_i], outs[:n_o], sems[:n_s])

    def right(fn):
        return lambda ins, outs, sems: fn(ins[n_i:], outs[n_o:], sems[n_s:])

    phases = [(f, left(fn)) for f, fn in first.phases] + [(f, right(fn)) for f, fn in second.phases]
    return _Comm(first.ins + second.ins, first.outs + second.outs, first.sems + second.sems, phases)


def _hosted(comm, res):
    return res if comm is not None else (res, [])


def _standalone(name, comm):
    def kern():
        pass

    return _pallas(kern, comm=comm, name=name, grid=(1,), in_specs=[], out_specs=[], out_shape=[])()[1]


def _dma_sems(*shapes):
    return [pltpu.SemaphoreType.DMA(s) for s in shapes]


def _gather_comm(arrs, forward_at=0.5):
    n = len(arrs)

    def setup(outs, sems):
        send_sems, recv_sems, _ = sems
        x, y, c = _mesh_pos()
        chips = [(1 - x, y), (x, 1 - y), (1 - x, 1 - y)]

        def copy(a, k, block, to, src=None):
            dst = outs[a].at[_flat(*block)]
            return pltpu.make_async_remote_copy(
                src_ref=dst if src is None else src, dst_ref=dst,
                send_sem=send_sems.at[a, k], recv_sem=recv_sems.at[a, k],
                device_id=to, device_id_type=MESH)

        return (x, y, c), (x, y, 1 - c), chips, copy

    def local(ins, outs, sems, a, me):
        return pltpu.make_async_copy(ins[a], outs[a].at[_flat(*me)], sems[2].at[a])

    def send_own(ins, outs, sems):
        me, sibling, chips, copy = setup(outs, sems)
        for a in range(n):
            local(ins, outs, sems, a, me).start()
            copy(a, 0, me, sibling, src=ins[a]).start()
            for j, chip in enumerate(chips):
                copy(a, 1 + j, me, (*chip, me[2]), src=ins[a]).start()

    def forward(ins, outs, sems):
        me, sibling, chips, copy = setup(outs, sems)
        for a in range(n):
            for j, chip in enumerate(chips):
                copy(a, 1 + j, (*chip, me[2]), me).wait_recv()
                copy(a, 4 + j, (*chip, me[2]), sibling).start()

    def finish(ins, outs, sems):
        me, sibling, chips, copy = setup(outs, sems)
        for a in range(n):
            copy(a, 0, sibling, me).wait_recv()
            for j, chip in enumerate(chips):
                copy(a, 4 + j, (*chip, 1 - me[2]), me).wait_recv()
        for a in range(n):
            copy(a, 0, me, sibling, src=ins[a]).wait_send()
            for j, chip in enumerate(chips):
                copy(a, 1 + j, me, (*chip, me[2]), src=ins[a]).wait_send()
                copy(a, 4 + j, (*chip, me[2]), sibling).wait_send()
            local(ins, outs, sems, a, me).wait()

    return _Comm(arrs, [jax.ShapeDtypeStruct((N_DEV,) + a.shape, a.dtype) for a in arrs],
                 _dma_sems((n, 7), (n, 7), (n,)), [(0.0, send_own), (forward_at, forward), (1.0, finish)])


CHIPS = [(0, 0), (0, 1), (1, 0), (1, 1)]


def _sibling_comm(parts):
    n = len(parts)

    def copies(ins, outs, sems):
        x, y, c = _mesh_pos()
        return [pltpu.make_async_remote_copy(
                    src_ref=ins[a].at[_flat(qx, qy, 1 - c)], dst_ref=outs[a].at[q],
                    send_sem=sems[0].at[a, q], recv_sem=sems[1].at[a, q],
                    device_id=(x, y, 1 - c), device_id_type=MESH)
                for a in range(n) for q, (qx, qy) in enumerate(CHIPS)]

    def start(ins, outs, sems):
        for cp in copies(ins, outs, sems):
            cp.start()

    def finish(ins, outs, sems):
        for cp in copies(ins, outs, sems):
            cp.wait_recv()
        for cp in copies(ins, outs, sems):
            cp.wait_send()

    return _Comm(parts, [jax.ShapeDtypeStruct((4,) + p.shape[1:], p.dtype) for p in parts],
                 _dma_sems((n, 4), (n, 4)), [(0.0, start), (1.0, finish)])


def _chip_comm(sums):
    n = len(sums)
    flips = [(1, 0), (0, 1), (1, 1)]

    def own(ins, outs, sems):
        mine = 2 * lax.axis_index("x") + lax.axis_index("y")
        return [pltpu.make_async_copy(ins[a].at[mine], outs[a].at[mine], sems[2].at[a]) for a in range(n)]

    def copies(ins, outs, sems, arriving=False):
        x, y, c = _mesh_pos()
        mine = 2 * x + y
        remote = []
        for a in range(n):
            for k, (fx, fy) in enumerate(flips):
                qx, qy = x ^ fx, y ^ fy
                q = 2 * qx + qy
                remote.append(pltpu.make_async_remote_copy(
                    src_ref=ins[a].at[q], dst_ref=outs[a].at[q if arriving else mine],
                    send_sem=sems[0].at[a, k], recv_sem=sems[1].at[a, k],
                    device_id=(qx, qy, c), device_id_type=MESH))
        return remote

    def start(ins, outs, sems):
        for cp in own(ins, outs, sems) + copies(ins, outs, sems):
            cp.start()

    def finish(ins, outs, sems):
        for cp in copies(ins, outs, sems, arriving=True):
            cp.wait_recv()
        for cp in copies(ins, outs, sems):
            cp.wait_send()
        for cp in own(ins, outs, sems):
            cp.wait()

    return _Comm(sums, [jax.ShapeDtypeStruct(s.shape, s.dtype) for s in sums],
                 _dma_sems((n, 3), (n, 3), (n,)), [(0.0, start), (1.0, finish)])


def _pair_add(name, parts, got, core, tr):
    _, R, C = parts.shape
    assert R % tr == 0

    def kern(c_ref, p_ref, g_ref, o_ref):
        o_ref[...] = (p_ref[...].astype(F32) + g_ref[...].astype(F32)).astype(o_ref.dtype)

    blk = pl.BlockSpec((None, tr, C), lambda q, i, c_ref: (q, i, 0))
    return pl.pallas_call(
        kern, name=name,
        grid_spec=pltpu.PrefetchScalarGridSpec(
            num_scalar_prefetch=1, grid=(4, R // tr),
            in_specs=[pl.BlockSpec((None, tr, C), lambda q, i, c_ref: (2 * q + c_ref[0], i, 0)), blk],
            out_specs=blk),
        out_shape=jax.ShapeDtypeStruct((4, R, C), parts.dtype), compiler_params=_params(),
    )(core, parts, got)


def _rowwise(name, body, T, tb, rows, vecs, out_rows, out_accs):
    n_in = len(rows) + len(vecs)
    n_o, n_a = len(out_rows), len(out_accs)

    def kern(*refs):
        i = pl.program_id(0)
        res = body(*[r[...] for r in refs[:n_in]])
        if not isinstance(res, (tuple, list)):
            res = (res,)
        outs = refs[n_in:]
        for k in range(n_o):
            outs[k][...] = res[k].astype(outs[k].dtype)

        def accumulate(ref, val):
            @pl.when(i == 0)
            def _():
                ref[...] = val

            @pl.when(i > 0)
            def _():
                ref[...] += val

        for k in range(n_a):
            accumulate(outs[n_o + k], res[n_o + k])

    in_specs = [pl.BlockSpec((tb, w), lambda i, cb=cb: (i, cb)) for (_, w, cb) in rows]
    in_specs += [pl.BlockSpec((1, v.shape[1]), lambda i: (0, 0)) for v in vecs]
    out_specs = [pl.BlockSpec((tb, w), lambda i: (i, 0)) for (w, _) in out_rows]
    out_specs += [pl.BlockSpec((1, w), lambda i: (0, 0)) for w in out_accs]
    out_shape = [jax.ShapeDtypeStruct((T, w), dt) for (w, dt) in out_rows]
    out_shape += [jax.ShapeDtypeStruct((1, w), F32) for w in out_accs]
    res = pl.pallas_call(
        kern, name=name, grid=(T // tb,), in_specs=in_specs, out_specs=out_specs,
        out_shape=out_shape, compiler_params=_params(),
    )(*[r[0] for r in rows], *vecs)
    return res


def _dot(a, b, mode):
    dims = {"NN": ((1,), (0,)), "NT": ((1,), (1,)), "TN": ((0,), (0,))}[mode]
    return lax.dot_general(a.astype(BF16), b.astype(BF16), (dims, ((), ())),
                           preferred_element_type=F32)


def _mm(name, a, b, mode, out_dtype, tm, tn, tk, ga=False, gb=False, gmode=None, comm=None):
    G = (a.shape[0] if ga else b.shape[0]) if gmode else 1
    a2, b2 = a.shape[-2:], b.shape[-2:]
    if mode == "NN":
        (M, K), (_, N) = a2, b2
    elif mode == "NT":
        (M, K), (N, _) = a2, b2
    else:
        (K, M), (_, N) = a2, b2
    tm, tn, tk = min(tm, M), min(tn, N), min(tk, K)
    assert M % tm == 0 and N % tn == 0 and K % tk == 0, (name, M, N, K, tm, tn, tk)
    batch = gmode == "batch"
    n_gb, n_gs = (G if batch else 1), (G if gmode == "sum" else 1)
    nk = K // tk
    n_red = n_gs * nk

    def grp(g_b, g_s):
        return g_b if batch else g_s

    if mode == "TN":
        a_blk, a_idx = (tk, tm), lambda g_b, mi, ni, g_s, ki: (ki, mi)
    else:
        a_blk, a_idx = (tm, tk), lambda g_b, mi, ni, g_s, ki: (mi, ki)
    if mode == "NT":
        b_blk, b_idx = (tn, tk), lambda g_b, mi, ni, g_s, ki: (ni, ki)
    else:
        b_blk, b_idx = (tk, tn), lambda g_b, mi, ni, g_s, ki: (ki, ni)

    def with_group(blk, idx, has_group):
        if not has_group:
            return pl.BlockSpec(blk, idx)
        return pl.BlockSpec((None,) + blk, lambda g_b, mi, ni, g_s, ki: (grp(g_b, g_s),) + idx(g_b, mi, ni, g_s, ki))

    o_blk, o_idx = (tm, tn), lambda g_b, mi, ni, g_s, ki: (mi, ni)
    o_spec = with_group(o_blk, o_idx, batch)
    o_shape = ((G,) if batch else ()) + (M, N)

    def kern(a_ref, b_ref, o_ref, *scratch):
        part = _dot(a_ref[...], b_ref[...], mode)
        if n_red == 1:
            o_ref[...] = part.astype(o_ref.dtype)
            return
        acc = scratch[0]
        step = pl.program_id(3) * nk + pl.program_id(4)

        @pl.when(step == 0)
        def _():
            acc[...] = part

        @pl.when(step > 0)
        def _():
            acc[...] += part

        @pl.when(step == n_red - 1)
        def _():
            o_ref[...] = acc[...].astype(o_ref.dtype)

    return _pallas(
        kern, comm=comm, name=name, grid=(n_gb, M // tm, N // tn, n_gs, nk),
        in_specs=[with_group(a_blk, a_idx, ga), with_group(b_blk, b_idx, gb)],
        out_specs=o_spec, out_shape=jax.ShapeDtypeStruct(o_shape, out_dtype),
        scratch_shapes=[] if n_red == 1 else [pltpu.VMEM((tm, tn), F32)],
        compiler_params=_params(),
    )(a, b)


def _mm_groups(name, a, b, mode, tm, tn, residual=None, comm=None):
    G, M, K = a.shape
    N = b.shape[2] if mode == "NN" else b.shape[1]
    tm, tn = min(tm, M), min(tn, N)
    assert M % tm == 0 and N % tn == 0

    def kern(a_ref, b_ref, *rest):
        acc = _dot(a_ref[0], b_ref[0], mode)
        for g in range(1, G):
            acc = acc + _dot(a_ref[g], b_ref[g], mode)
        if residual is None:
            rest[0][...] = acc
        else:
            x_ref, g_ref, f_ref, o_ref = rest
            f_ref[...] = acc
            o_ref[...] = x_ref[...] + (residual[2] * g_ref[...]) * acc

    b_spec = (pl.BlockSpec((G, K, tn), lambda ni, mi: (0, 0, ni)) if mode == "NN"
              else pl.BlockSpec((G, tn, K), lambda ni, mi: (0, ni, 0)))
    o_spec = pl.BlockSpec((tm, tn), lambda ni, mi: (mi, ni))
    in_specs = [pl.BlockSpec((G, tm, K), lambda ni, mi: (0, mi, 0)), b_spec]
    args = [a, b]
    out = jax.ShapeDtypeStruct((M, N), F32)
    if residual is not None:
        in_specs += [o_spec, pl.BlockSpec((1, tn), lambda ni, mi: (0, ni))]
        args += [residual[0], residual[1]]
    return _pallas(
        kern, comm=comm, name=name, grid=(N // tn, M // tm), in_specs=in_specs,
        out_specs=o_spec if residual is None else [o_spec, o_spec],
        out_shape=out if residual is None else [out, out], compiler_params=_params(),
    )(*args)


def _adamw(name, parts, w, m, v, tr):
    G, R, C = parts.shape
    assert R % tr == 0
    bc1 = 1.0 - ADAM_B1 ** ADAM_STEP
    bc2 = 1.0 - ADAM_B2 ** ADAM_STEP

    def kern(p_ref, w_ref, m_ref, v_ref, g_out, d_out, m_out, v_out):
        g = p_ref[0].astype(F32)
        for s in range(1, G):
            g = g + p_ref[s].astype(F32)
        m2 = ADAM_B1 * m_ref[...] + (1.0 - ADAM_B1) * g
        v2 = ADAM_B2 * v_ref[...] + (1.0 - ADAM_B2) * (g * g)
        m_hat = m2 / bc1
        v_hat = v2 / bc2
        g_out[...] = g
        d_out[...] = -ADAM_LR * (m_hat / (jnp.sqrt(v_hat) + ADAM_EPS) + ADAM_WD * w_ref[...])
        m_out[...] = m2
        v_out[...] = v2

    blk = pl.BlockSpec((tr, C), lambda i: (i, 0))
    return pl.pallas_call(
        kern, name=name, grid=(R // tr,),
        in_specs=[pl.BlockSpec((G, tr, C), lambda i: (0, i, 0)), blk, blk, blk],
        out_specs=[blk] * 4, out_shape=[jax.ShapeDtypeStruct((R, C), F32)] * 4,
        compiler_params=_params(),
    )(parts, w, m, v)


def _ada_fwd(c_all, w_loc, b_loc, tn):
    B, D = c_all.shape
    N = w_loc.shape[1]

    def kern(c_ref, w_ref, b_ref, o_ref):
        cc = c_ref[...]
        act = cc * _sigmoid(cc)
        o_ref[...] = _dot(act, w_ref[...], "NN") + b_ref[...]

    return pl.pallas_call(
        kern, name="ada_fwd", grid=(N // tn,),
        in_specs=[pl.BlockSpec((B, D), lambda j: (0, 0)), pl.BlockSpec((D, tn), lambda j: (0, j)),
                  pl.BlockSpec((1, tn), lambda j: (0, j))],
        out_specs=pl.BlockSpec((B, tn), lambda j: (0, j)),
        out_shape=jax.ShapeDtypeStruct((B, N), F32), compiler_params=_params(),
    )(c_all, w_loc, b_loc)


def _ada_bwd(c_all, dmod_loc, tn):
    B, D = c_all.shape
    N = dmod_loc.shape[1]

    def kern(c_ref, d_ref, o_ref):
        cc = c_ref[...]
        act = cc * _sigmoid(cc)
        o_ref[...] = _dot(act, d_ref[...], "TN")

    return pl.pallas_call(
        kern, name="ada_bwd", grid=(N // tn,),
        in_specs=[pl.BlockSpec((B, D), lambda j: (0, 0)), pl.BlockSpec((B, tn), lambda j: (0, j))],
        out_specs=pl.BlockSpec((D, tn), lambda j: (0, j)),
        out_shape=jax.ShapeDtypeStruct((D, N), F32), compiler_params=_params(),
    )(c_all, dmod_loc)


def _norm_mod_fwd(name, x, g, sc, sh, T, tb):
    D = x.shape[1]

    def body(xb, gb, scb, shb):
        n = (xb * _rstd(xb)) * gb
        return n * (1.0 + scb) + shb

    return _rowwise(name, body, T, tb, [(x, D, 0)], [g, sc, sh], [(D, BF16)], [])[0]


def _norm_mod_bwd(name, x, dhm, dres, g, sc, T, tb):
    D = x.shape[1]

    def body(xb, db, rb, gb, scb):
        r = _rstd(xb)
        xh = xb * r
        n = xh * gb
        dn = db * (1.0 + scb)
        dxh = dn * gb
        dx = rb + r * (dxh - xh * jnp.mean(dxh * xh, axis=-1, keepdims=True))
        return (dx, jnp.sum(db, axis=0, keepdims=True), jnp.sum(db * n, axis=0, keepdims=True),
                jnp.sum(dn * xh, axis=0, keepdims=True))

    return _rowwise(name, body, T, tb, [(x, D, 0), (dhm, D, 0), (dres, D, 0)], [g, sc],
                    [(D, F32)], [D, D, D])


def _residual_bwd(name, dx, f, gate, coef, T, tb):
    D = dx.shape[1]

    def body(db, fb, gb):
        return (coef * gb) * db, jnp.sum((coef * fb) * db, axis=0, keepdims=True)

    return _rowwise(name, body, T, tb, [(dx, D, 0), (f, D, 0)], [gate], [(D, BF16)], [D])


def _final_loss(x, tgt, g, T, tb):
    D = x.shape[1]

    def body(xb, tb_, gb):
        r = _rstd(xb)
        xh = xb * r
        err = xh * gb - tb_
        loss = 0.5 * jnp.sum(jnp.mean(err * err, axis=-1, keepdims=True), axis=0, keepdims=True)
        dy = err * (1.0 / D)
        dxh = dy * gb
        dx = r * (dxh - xh * jnp.mean(dxh * xh, axis=-1, keepdims=True))
        return dx, jnp.sum(dy * xh, axis=0, keepdims=True), jnp.broadcast_to(loss, (1, 128))

    return _rowwise("final_loss", body, T, tb, [(x, D, 0), (tgt, D, 0)], [g], [(D, F32)], [D, 128])


def _ffn_up(name, hm, wi, T, tm, comm=None):
    D = hm.shape[1]
    Ws = wi.shape[1]
    half = wi.shape[0] // 2

    n_sub = 2 if tm % 32 == 0 else 1
    subs = [pl.ds(r * (tm // n_sub), tm // n_sub) for r in range(n_sub)]

    def kern(h_ref, wa_ref, wb_ref, a_ref, b_ref, hid_ref):
        wa, wb = wa_ref[...], wb_ref[...]
        ab = [(_dot(h_ref[rows, :], wa, "NT"), _dot(h_ref[rows, :], wb, "NT")) for rows in subs]
        for rows, (a, b) in zip(subs, ab):
            a_ref[rows, :] = a
            b_ref[rows, :] = b
            hid_ref[rows, :] = ((a * _sigmoid(a)) * b).astype(BF16)

    o_spec = pl.BlockSpec((None, tm, Ws), lambda g, i: (g, i, 0))
    return _pallas(
        kern, comm=comm, name=name, grid=(half, T // tm),
        in_specs=[pl.BlockSpec((tm, D), lambda g, i: (i, 0)),
                  pl.BlockSpec((None, Ws, D), lambda g, i: (g, 0, 0)),
                  pl.BlockSpec((None, Ws, D), lambda g, i: (g + half, 0, 0))],
        out_specs=[o_spec] * 3,
        out_shape=[jax.ShapeDtypeStruct((half, T, Ws), F32)] * 2 + [jax.ShapeDtypeStruct((half, T, Ws), BF16)],
        compiler_params=_params(),
    )(hm, wi, wi)


def _ffn_down_bwd(name, df, wo, a, b, T, tm, comm=None):
    D = df.shape[1]
    half, _, Ws = a.shape

    n_sub = 2 if tm % 32 == 0 else 1
    subs = [pl.ds(r * (tm // n_sub), tm // n_sub) for r in range(n_sub)]

    def kern(df_ref, wo_ref, a_ref, b_ref, dp_ref):
        wo_blk = wo_ref[...]
        dhid = [_dot(df_ref[rows, :], wo_blk, "NT") for rows in subs]
        for rows, dh in zip(subs, dhid):
            av = a_ref[rows, :]
            s = _sigmoid(av)
            silu = av * s
            dp_ref[0, rows, :] = (dh * b_ref[rows, :] * (s + silu * (1.0 - s))).astype(BF16)
            dp_ref[1, rows, :] = (dh * silu).astype(BF16)

    act = pl.BlockSpec((None, tm, Ws), lambda g, i: (g, i, 0))
    return _pallas(
        kern, comm=comm, name=name, grid=(half, T // tm),
        in_specs=[pl.BlockSpec((tm, D), lambda g, i: (i, 0)),
                  pl.BlockSpec((None, Ws, D), lambda g, i: (g, 0, 0)), act, act],
        out_specs=pl.BlockSpec((2, None, tm, Ws), lambda g, i: (0, g, i, 0)),
        out_shape=jax.ShapeDtypeStruct((2, half, T, Ws), BF16),
        compiler_params=_params(),
    )(df, wo, a, b)


def _ffn_fwd(tag, x, norm_g, sh, sc, gate, wi, wo_of, T, up_comm=None, down_comm=None):
    tb = min(256, T)
    hm = _norm_mod_fwd(tag + "_norm_fwd", x, norm_g, sc, sh, T, tb)
    (a, b, hid), got_up = _hosted(up_comm, _ffn_up(tag + "_up", hm, wi, T, min(512, T), comm=up_comm))
    wo = wo_of(got_up)
    (f, x_out), got_down = _hosted(down_comm, _mm_groups(tag + "_down", hid, wo, "NN", 512, 512,
                                                         residual=(x, gate, 0.5), comm=down_comm))
    return x_out, (x, hm, a, b, hid, f), wo, got_down


TILE_W_IN = (128, 512)
TILE_FFN_IN = (32, 688)
TILE_W_OUT = (16, 688)
TILE_MIX_OUT = (64, 256)
TILE_POOL = (128, 128)


def _reduce_level1(tag, parts, core, tiles, host=None):
    comm = _sibling_comm(parts)
    if host is None:
        res, got = None, _standalone(tag + "_sibling", comm)
    else:
        res, got = host(comm)
    sums = [_pair_add("%s_pair_add%d" % (tag, k), p, g, core, min(t[1], p.shape[1]))
            for k, (p, g, t) in enumerate(zip(parts, got, tiles))]
    return res, sums


def _ffn_bwd(tag, dx_out, saved, norm_g, sc, gate, wi, wo, T, core, ride_sums=None, defer_dwi=False):
    x, hm, a, b, hid, f = saved
    tb = min(256, T)
    D = x.shape[1]
    df, dgate = _residual_bwd(tag + "_res_bwd", dx_out, f, gate, 0.5, T, tb)
    dwo = _mm(tag + "_dwo", hid, df, "TN", BF16, 2048, 512, T, ga=True, gmode="batch").reshape(N_DEV, -1, D)
    n_ride = 0 if ride_sums is None else len(ride_sums)

    def down_bwd_call(comm):
        if n_ride:
            comm = _join(comm, _chip_comm(ride_sums))
        res, got = _ffn_down_bwd(tag + "_down_bwd", df, wo, a, b, T, min(512, T), comm=comm)
        return (res, got[len(got) - n_ride:]), got[:len(got) - n_ride]

    (dproj, ride_got), (dwo_sum,) = _reduce_level1(tag + "_dwo", [dwo], core, [TILE_W_OUT], host=down_bwd_call)
    dproj = dproj.reshape((2 * dproj.shape[1],) + dproj.shape[2:])
    dwi, (dwo_got,) = _mm(tag + "_dwi", dproj, hm, "TN", BF16, 2048, 512, T, ga=True, gmode="batch",
                          comm=_chip_comm([dwo_sum]))

    def dhm_call(comm):
        return _mm_groups(tag + "_dhm", dproj, wi, "NN", 512, 512, comm=comm)

    if defer_dwi:
        dhm, (dwi_out,) = _reduce_level1(tag + "_dwi", [dwi], core, [TILE_FFN_IN], host=dhm_call)
    else:
        _, (dwi_sum,) = _reduce_level1(tag + "_dwi", [dwi], core, [TILE_FFN_IN])
        dhm, (dwi_out,) = dhm_call(_chip_comm([dwi_sum]))
    dx, dsh, dsc, dng = _norm_mod_bwd(tag + "_norm_bwd", x, dhm, dx_out, norm_g, sc, T, tb)
    return dx, (dsh, dsc, dgate, dng), dwi_out, dwo_got, ride_got


def _heads(fn, *arrs):
    outs = [fn(*[a[:, h * HEAD_DIM:(h + 1) * HEAD_DIM] for a in arrs]) for h in range(N_HEADS)]
    return outs


def _qknorm_fwd(proj, gq, gk, T, tb):
    W = N_HEADS * HEAD_DIM

    def body(q, k, v, gqb, gkb):
        qn = jnp.concatenate(_heads(lambda t: (t * _rstd(t)) * gqb, q), axis=1)
        kn = jnp.concatenate(_heads(lambda t: (t * _rstd(t)) * gkb, k), axis=1)
        return qn, kn, v

    return _rowwise("qknorm_fwd", body, T, tb, [(proj, W, 0), (proj, W, 1), (proj, W, 2)], [gq, gk],
                    [(W, BF16)] * 3, [])


def _qknorm_bwd(proj, dqn, dkn, gq, gk, T, tb):
    W = N_HEADS * HEAD_DIM

    def one(t, dt, g):
        r = _rstd(t)
        th = t * r
        dth = dt * g
        d = r * (dth - th * jnp.mean(dth * th, axis=-1, keepdims=True))
        return d, jnp.sum(dt * th, axis=0, keepdims=True)

    def body(q, k, dq, dk, gqb, gkb):
        rq = _heads(lambda t, dt: one(t, dt, gqb), q, dq)
        rk = _heads(lambda t, dt: one(t, dt, gkb), k, dk)
        return (jnp.concatenate([r[0] for r in rq], axis=1), jnp.concatenate([r[0] for r in rk], axis=1),
                sum(r[1] for r in rq), sum(r[1] for r in rk))

    return _rowwise("qknorm_bwd", body, T, tb, [(proj, W, 0), (proj, W, 1), (dqn, W, 0), (dkn, W, 0)],
                    [gq, gk], [(W, BF16)] * 2, [HEAD_DIM, HEAD_DIM])


def _log_sigmoid(z):
    return jnp.minimum(z, 0.0) - jnp.log(1.0 + jnp.exp(-jnp.abs(z)))


def _fgate_fwd(proj, fcol, b_pad, T):
    nblk = T // 128

    def kern(f_ref, b_ref, o_ref):
        r = lax.broadcasted_iota(jnp.int32, (128, 128), 0)
        c = lax.broadcasted_iota(jnp.int32, (128, 128), 1)
        tri = (r >= c).astype(F32)
        carry = jnp.zeros((1, 128), F32)
        for k in range(nblk):
            rows = pl.ds(k * 128, 128)
            lf = _log_sigmoid(f_ref[rows, :] + b_ref[...])
            o_ref[rows, :] = jnp.dot(tri, lf, precision=lax.Precision.HIGHEST, preferred_element_type=F32) + carry
            carry = carry + jnp.sum(lf, axis=0, keepdims=True)

    return pl.pallas_call(
        kern, name="fgate_fwd", grid=(1,),
        in_specs=[pl.BlockSpec((T, 128), lambda i: (0, fcol)), pl.BlockSpec((1, 128), lambda i: (0, 0))],
        out_specs=pl.BlockSpec((T, 128), lambda i: (0, 0)),
        out_shape=jax.ShapeDtypeStruct((T, 128), F32), compiler_params=_params(),
    )(proj, b_pad)


def _fgate_bwd(proj, fcol, b_pad, dF, T):
    nblk = T // 128

    def kern(f_ref, b_ref, d_ref, o_ref, db_ref):
        r = lax.broadcasted_iota(jnp.int32, (128, 128), 0)
        c = lax.broadcasted_iota(jnp.int32, (128, 128), 1)
        tri = (c >= r).astype(F32)
        carry = jnp.zeros((1, 128), F32)
        db = jnp.zeros((1, 128), F32)
        for k in reversed(range(nblk)):
            rows = pl.ds(k * 128, 128)
            dblk = d_ref[rows, :]
            rc = jnp.dot(tri, dblk, precision=lax.Precision.HIGHEST, preferred_element_type=F32) + carry
            carry = carry + jnp.sum(dblk, axis=0, keepdims=True)
            z = f_ref[rows, :] + b_ref[...]
            dz = rc * (1.0 / (1.0 + jnp.exp(z)))
            o_ref[rows, :] = dz
            db = db + jnp.sum(dz, axis=0, keepdims=True)
        db_ref[...] = db

    return pl.pallas_call(
        kern, name="fgate_bwd", grid=(1,),
        in_specs=[pl.BlockSpec((T, 128), lambda i: (0, fcol)), pl.BlockSpec((1, 128), lambda i: (0, 0)),
                  pl.BlockSpec((T, 128), lambda i: (0, 0))],
        out_specs=[pl.BlockSpec((T, 128), lambda i: (0, 0)), pl.BlockSpec((1, 128), lambda i: (0, 0))],
        out_shape=[jax.ShapeDtypeStruct((T, 128), F32), jax.ShapeDtypeStruct((1, 128), F32)],
        compiler_params=_params(),
    )(proj, b_pad, dF)


LOG2E = 1.4426950408889634


def _gate_bias(ft, fh, h):
    lane = lax.broadcasted_iota(jnp.int32, ft.shape, 1)
    fq = jnp.sum(jnp.where(lane == h, ft, 0.0), axis=1, keepdims=True)
    f0 = jnp.max(fq, axis=0, keepdims=True)
    sub = lax.broadcasted_iota(jnp.int32, fh.shape, 0)
    fk = jnp.sum(jnp.where(sub == h, fh, 0.0), axis=0, keepdims=True)
    return (f0 - fk) * LOG2E


HEADS_PER_STEP = 2


def _tri_rows(s, nb):
    i = sum((s >= k * (k + 1) // 2).astype(jnp.int32) for k in range(1, nb))
    return i, s - (i * (i + 1)) // 2


def _tri_cols(s, nb):
    j = sum((s >= k * nb - (k * (k - 1)) // 2).astype(jnp.int32) for k in range(1, nb))
    return j, j + s - (j * nb - (j * (j - 1)) // 2)


def _causal_bias(blk):
    row = lax.broadcasted_iota(jnp.int32, (blk, blk), 0)
    col = lax.broadcasted_iota(jnp.int32, (blk, blk), 1)
    return jnp.where(row >= col, 0.0, NEG)


def _attn_fwd(qn, kn, vb, f_tm, f_hm, T, blk, comm=None):
    nb = T // blk
    scale = HEAD_DIM ** -0.5
    W = N_HEADS * HEAD_DIM
    G = HEADS_PER_STEP
    lanes = [slice(g * HEAD_DIM, (g + 1) * HEAD_DIM) for g in range(G)]

    def kern(q_ref, k_ref, v_ref, ft_ref, fh_ref, o_ref, lse_ref, m_scr, l_scr, acc_scr):
        hp = pl.program_id(0)
        i, j = _tri_rows(pl.program_id(1), nb)

        @pl.when(j == 0)
        def _():
            m_scr[...] = jnp.full_like(m_scr, NEG)
            l_scr[...] = jnp.zeros_like(l_scr)
            acc_scr[...] = jnp.zeros_like(acc_scr)

        def block(diagonal):
            ft, fh = ft_ref[...], fh_ref[...]
            s = [_dot(q_ref[:, sl], k_ref[:, sl], "NT") * (scale * LOG2E) + _gate_bias(ft, fh, hp * G + g)
                 for g, sl in enumerate(lanes)]
            if diagonal:
                mask = _causal_bias(blk)
                s = [sg + mask for sg in s]
            m_prev = [m_scr[g] for g in range(G)]
            m_new = [jnp.maximum(mp, jnp.max(sg, axis=1, keepdims=True)) for mp, sg in zip(m_prev, s)]
            alpha = [jnp.exp2(mp - mn) for mp, mn in zip(m_prev, m_new)]
            p = [jnp.exp2(sg - mn) for sg, mn in zip(s, m_new)]
            for g, sl in enumerate(lanes):
                l_scr[g] = alpha[g] * l_scr[g] + jnp.sum(p[g], axis=1, keepdims=True)
                acc_scr[:, sl] = alpha[g] * acc_scr[:, sl] + _dot(p[g], v_ref[:, sl], "NN")
                m_scr[g] = m_new[g]

        @pl.when(j < i)
        def _():
            block(False)

        @pl.when(j == i)
        def _():
            block(True)
            for g, sl in enumerate(lanes):
                l = l_scr[g]
                o_ref[:, sl] = acc_scr[:, sl] / l
                lse_ref[:, sl] = jnp.broadcast_to(m_scr[g] + jnp.log2(l), (blk, HEAD_DIM))

    qspec = pl.BlockSpec((blk, G * HEAD_DIM), lambda h, s: (_tri_rows(s, nb)[0], h))
    kspec = pl.BlockSpec((blk, G * HEAD_DIM), lambda h, s: (_tri_rows(s, nb)[1], h))
    return _pallas(
        kern, comm=comm, name="attn_fwd", grid=(N_HEADS // G, nb * (nb + 1) // 2),
        in_specs=[qspec, kspec, kspec,
                  pl.BlockSpec((blk, 128), lambda h, s: (_tri_rows(s, nb)[0], 0)),
                  pl.BlockSpec((N_HEADS, blk), lambda h, s: (0, _tri_rows(s, nb)[1]))],
        out_specs=[qspec, qspec],
        out_shape=[jax.ShapeDtypeStruct((T, W), F32)] * 2,
        scratch_shapes=[pltpu.VMEM((G, blk, 1), F32), pltpu.VMEM((G, blk, 1), F32),
                        pltpu.VMEM((blk, G * HEAD_DIM), F32)],
        compiler_params=_params(),
    )(qn, kn, vb, f_tm, f_hm)


def _attn_bwd(qn, kn, vb, do, lse, delta, f_tm, f_hm, T, blk, comm=None):
    nb = T // blk
    scale = HEAD_DIM ** -0.5
    W = N_HEADS * HEAD_DIM
    G = HEADS_PER_STEP
    lanes = [slice(g * HEAD_DIM, (g + 1) * HEAD_DIM) for g in range(G)]

    def kern(q_ref, k_ref, v_ref, do_ref, lse_ref, dl_ref, ft_ref, fh_ref,
             dq_ref, dfq_ref, dk_ref, dv_ref, df_ref, dq_scr, dfq_scr, dk_scr, dv_scr, df_scr):
        hp = pl.program_id(0)
        j, i = _tri_cols(pl.program_id(1), nb)

        @pl.when((j == 0) & (i == 0))
        def _():
            dq_scr[...] = jnp.zeros_like(dq_scr)
            dfq_scr[...] = jnp.zeros_like(dfq_scr)

        @pl.when(i == j)
        def _():
            dk_scr[...] = jnp.zeros_like(dk_scr)
            dv_scr[...] = jnp.zeros_like(dv_scr)
            df_scr[...] = jnp.zeros_like(df_scr)

        def block(diagonal):
            ft, fh = ft_ref[...], fh_ref[...]
            rows = pl.ds(pl.multiple_of(i * blk, blk), blk)
            q = [q_ref[:, sl] for sl in lanes]
            k = [k_ref[:, sl] for sl in lanes]
            dob = [do_ref[:, sl].astype(BF16) for sl in lanes]
            s = [_dot(q[g], k[g], "NT") * (scale * LOG2E) + _gate_bias(ft, fh, hp * G + g) for g in range(G)]
            if diagonal:
                mask = _causal_bias(blk)
                s = [sg + mask for sg in s]
            p = [jnp.exp2(s[g] - lse_ref[:, sl.start:sl.start + 1]) for g, sl in enumerate(lanes)]
            dp = [_dot(dob[g], v_ref[:, sl], "NT") for g, sl in enumerate(lanes)]
            ds = [p[g] * (dp[g] - dl_ref[:, sl.start:sl.start + 1]) for g, sl in enumerate(lanes)]
            dsb = [d.astype(BF16) for d in ds]
            for g, sl in enumerate(lanes):
                dv_scr[:, sl] += _dot(p[g], dob[g], "TN")
                dk_scr[:, sl] += _dot(dsb[g], q[g], "TN") * scale
                dq_scr[rows, sl] += _dot(dsb[g], k[g], "NN") * scale
                df_scr[g] += jnp.sum(ds[g], axis=0, keepdims=True)
                dfq_scr[g, rows, :] += jnp.sum(ds[g], axis=1, keepdims=True)

        @pl.when(i > j)
        def _():
            block(False)

        @pl.when(i == j)
        def _():
            block(True)

        @pl.when(i == nb - 1)
        def _():
            dk_ref[...] = dk_scr[...]
            dv_ref[...] = dv_scr[...]
            df_ref[...] = -df_scr[...]

        @pl.when((j == nb - 1) & (i == nb - 1))
        def _():
            dq_ref[...] = dq_scr[...]
            for g, sl in enumerate(lanes):
                dfq_ref[:, sl] = jnp.broadcast_to(dfq_scr[g], (T, HEAD_DIM))

    qspec = pl.BlockSpec((blk, G * HEAD_DIM), lambda h, s: (_tri_cols(s, nb)[1], h))
    full = pl.BlockSpec((T, G * HEAD_DIM), lambda h, s: (0, h))
    kspec = pl.BlockSpec((blk, G * HEAD_DIM), lambda h, s: (_tri_cols(s, nb)[0], h))
    return _pallas(
        kern, comm=comm, name="attn_bwd", grid=(N_HEADS // G, nb * (nb + 1) // 2),
        in_specs=[qspec, kspec, kspec, qspec, qspec, qspec,
                  pl.BlockSpec((blk, 128), lambda h, s: (_tri_cols(s, nb)[1], 0)),
                  pl.BlockSpec((N_HEADS, blk), lambda h, s: (0, _tri_cols(s, nb)[0]))],
        out_specs=[full, full, kspec, kspec, pl.BlockSpec((G, 1, blk), lambda h, s: (h, 0, _tri_cols(s, nb)[0]))],
        out_shape=[jax.ShapeDtypeStruct((T, W), F32)] * 4 + [jax.ShapeDtypeStruct((N_HEADS, 1, T), F32)],
        scratch_shapes=[pltpu.VMEM((T, G * HEAD_DIM), F32), pltpu.VMEM((G, T, 1), F32),
                        pltpu.VMEM((blk, G * HEAD_DIM), F32), pltpu.VMEM((blk, G * HEAD_DIM), F32),
                        pltpu.VMEM((G, 1, blk), F32)],
        compiler_params=_params(),
    )(qn, kn, vb, do, lse, delta, f_tm, f_hm)


def _attn_delta(o, do, T, tb):
    W = N_HEADS * HEAD_DIM

    def body(ob, dob):
        return jnp.concatenate(
            _heads(lambda a, b: jnp.broadcast_to(jnp.sum(a * b, axis=1, keepdims=True), a.shape), ob, dob), axis=1)

    return _rowwise("attn_delta", body, T, tb, [(o, W, 0), (do, W, 0)], [], [(W, F32)], [])[0]


def _window_select(s, g, shift):
    picks = []
    for k in (1, 2, 4, 8):
        s = s + shift(s, k)
        picks.append(s)
    return jnp.where(g == 0, picks[0], jnp.where(g == 1, picks[1], jnp.where(g == 2, picks[2], picks[3])))


def _group_window(g):
    return jnp.where(g == 0, POOL_WINDOWS[0], jnp.where(g == 1, POOL_WINDOWS[1],
                     jnp.where(g == 2, POOL_WINDOWS[2], POOL_WINDOWS[3])))


def _pool_fwd(proj, ucol, pw, ps, T, tb):
    C = POOL_GROUP_DIM
    n_g = len(POOL_WINDOWS)

    def kern(uc_ref, up_ref, pw_ref, ps_ref, pooled_ref, out_ref):
        g, i = pl.program_id(0), pl.program_id(1)
        uc = uc_ref[...]
        t2 = (i - 1) * tb + lax.broadcasted_iota(jnp.int32, (2 * tb, C), 0)
        u2 = jnp.where(t2 >= 0, jnp.concatenate([up_ref[...], uc], axis=0), 0.0)
        sums = _window_select(u2, g, lambda s, k: pltpu.roll(s, k, 0))[tb:, :]
        count = jnp.minimum(t2[tb:, :] + 1, _group_window(g)).astype(F32)
        pooled = sums / count - uc
        pooled_ref[...] = pooled.astype(BF16)
        out_ref[...] = _dot(pooled, pw_ref[...], "NN") * ps_ref[...]

    ospec = pl.BlockSpec((tb, C), lambda g, i: (i, g))
    return pl.pallas_call(
        kern, name="pool_fwd", grid=(n_g, T // tb),
        in_specs=[pl.BlockSpec((tb, C), lambda g, i: (i, ucol + g)),
                  pl.BlockSpec((tb, C), lambda g, i: (jnp.maximum(i - 1, 0), ucol + g)),
                  pl.BlockSpec((None, C, C), lambda g, i: (g, 0, 0)),
                  pl.BlockSpec((1, C), lambda g, i: (0, g))],
        out_specs=[ospec, ospec],
        out_shape=[jax.ShapeDtypeStruct((T, n_g * C), BF16), jax.ShapeDtypeStruct((T, n_g * C), F32)],
        compiler_params=_params(),
    )(proj, proj, pw, ps)


def _pool_bwd(dmix_in, dcol, pooled, pw, ps, T, tb):
    C = POOL_GROUP_DIM
    n_g = len(POOL_WINDOWS)
    nb = T // tb

    def kern(dc_ref, dn_ref, pooled_ref, pw_ref, ps_ref, du_ref, dpw_ref, dps_ref):
        g, i = pl.program_id(0), pl.program_id(1)
        dc = dc_ref[...]
        scale = ps_ref[...]
        t2 = i * tb + lax.broadcasted_iota(jnp.int32, (2 * tb, C), 0)
        d2 = jnp.where(t2 < T, jnp.concatenate([dc, dn_ref[...]], axis=0) * scale, 0.0)
        dpooled2 = _dot(d2, pw_ref[...], "NT")
        count = jnp.minimum(t2 + 1, _group_window(g)).astype(F32)
        sums = _window_select(dpooled2 / count, g, lambda s, k: pltpu.roll(s, 2 * tb - k, 0))
        du_ref[...] = (sums[:tb, :] - dpooled2[:tb, :]).astype(BF16)
        pooled = pooled_ref[...]
        p = _dot(pooled, pw_ref[...], "NN")
        dps = jnp.sum(dc * p, axis=0, keepdims=True)
        dpw = _dot(pooled, d2[:tb, :], "TN")

        @pl.when(i == 0)
        def _():
            dps_ref[...] = dps
            dpw_ref[...] = dpw

        @pl.when(i > 0)
        def _():
            dps_ref[...] += dps
            dpw_ref[...] += dpw

    return pl.pallas_call(
        kern, name="pool_bwd", grid=(n_g, nb),
        in_specs=[pl.BlockSpec((tb, C), lambda g, i: (i, dcol + g)),
                  pl.BlockSpec((tb, C), lambda g, i: (jnp.minimum(i + 1, nb - 1), dcol + g)),
                  pl.BlockSpec((tb, C), lambda g, i: (i, g)),
                  pl.BlockSpec((None, C, C), lambda g, i: (g, 0, 0)),
                  pl.BlockSpec((1, C), lambda g, i: (0, g))],
        out_specs=[pl.BlockSpec((tb, C), lambda g, i: (i, g)),
                   pl.BlockSpec((None, C, C), lambda g, i: (g, 0, 0)),
                   pl.BlockSpec((1, C), lambda g, i: (0, g))],
        out_shape=[jax.ShapeDtypeStruct((T, n_g * C), BF16), jax.ShapeDtypeStruct((n_g, C, C), F32),
                   jax.ShapeDtypeStruct((1, n_g * C), F32)],
        compiler_params=_params(),
    )(dmix_in, dmix_in, pooled, pw, ps)


D_QKV = 3 * N_HEADS * HEAD_DIM
D_U = len(POOL_WINDOWS) * POOL_GROUP_DIM
F_PAD = 128
D_PROJ = D_QKV + D_U + F_PAD


def _perm_w_in(w):
    pad = jnp.zeros((w.shape[0], F_PAD - N_HEADS), w.dtype)
    return jnp.concatenate([w[:, :D_QKV], w[:, D_QKV + N_HEADS:], w[:, D_QKV:D_QKV + N_HEADS], pad], axis=1)


def _unperm_w_in(w):
    return jnp.concatenate([w[:, :D_QKV], w[:, D_QKV + D_U:D_QKV + D_U + N_HEADS], w[:, D_QKV:D_QKV + D_U]], axis=1)


def _mixer_fwd(x, norm_g, sh, sc, gate, w_in_p, b_pad, gq, gk, late_weights, ps, T, proj_comm, attn_comm):
    tb = min(256, T)
    blk = min(512, T)
    hm = _norm_mod_fwd("mix_norm_fwd", x, norm_g, sc, sh, T, tb)
    proj, got_proj = _mm("mix_proj", hm, w_in_p, "NN", F32, 512, D_PROJ // 3, 2048, comm=proj_comm)
    pw, w_out = late_weights(got_proj)
    qn, kn, vb = _qknorm_fwd(proj, gq, gk, T, tb)
    fcol = (D_QKV + D_U) // 128
    f_tm = _fgate_fwd(proj, fcol, b_pad, T)
    f_hm = f_tm[:, :N_HEADS].T
    (o, lse), got = _attn_fwd(qn, kn, vb, f_tm, f_hm, T, blk, comm=attn_comm)
    pooled, pool_o = _pool_fwd(proj, D_QKV // POOL_GROUP_DIM, pw, ps, T, tb)
    mix_in = jnp.concatenate([o.astype(BF16), pool_o.astype(BF16)], axis=1)
    mix, x_out = _mm_groups("mix_out", mix_in[None], w_out[None], "NN", 512, 512, residual=(x, gate, 1.0))
    return x_out, (x, hm, proj, qn, kn, vb, f_tm, f_hm, o, lse, pooled, mix_in, mix), pw, w_out, got


def _mixer_bwd(dx_out, saved, norm_g, sc, gate, w_in_p, b_pad, gq, gk, pw, ps, w_out, T, core, ride_sums):
    x, hm, proj, qn, kn, vb, f_tm, f_hm, o, lse, pooled, mix_in, mix = saved
    tb = min(256, T)
    blk = min(512, T)
    W = N_HEADS * HEAD_DIM
    D = x.shape[1]
    n_g = len(POOL_WINDOWS)
    dmix, dgate = _residual_bwd("mix_res_bwd", dx_out, mix, gate, 1.0, T, tb)
    dmix_in = _mm("mix_out_bwd", dmix, w_out, "NT", F32, 512, 2048, 2048)
    dw_out = _mm("mix_dw_out", mix_in, dmix, "TN", BF16, 512, 1024, T)
    delta = _attn_delta(o, dmix_in, T, tb)
    (dqn, dfq, dkn, dv, dfk), ride_got = _attn_bwd(qn, kn, vb, dmix_in, lse, delta, f_tm, f_hm, T, blk,
                                                   comm=_chip_comm(ride_sums))
    dq, dk, dgq, dgk = _qknorm_bwd(proj, dqn, dkn, gq, gk, T, tb)
    dF = jnp.pad(dfq[:, ::HEAD_DIM] + dfk.reshape(N_HEADS, T).T, ((0, 0), (0, F_PAD - N_HEADS)))
    fcol = (D_QKV + D_U) // 128
    dfl, dbf = _fgate_bwd(proj, fcol, b_pad, dF, T)
    du, dpw, dps = _pool_bwd(dmix_in, W // POOL_GROUP_DIM, pooled, pw, ps, T, tb)
    dproj = jnp.concatenate([dq, dk, dv.astype(BF16), du, dfl.astype(BF16)], axis=1)
    dw_in_p = _mm("mix_dw_in", hm, dproj, "TN", BF16, 512, D_PROJ // 3, T)
    pw_rows = POOL_GROUP_DIM // N_DEV
    slabs = [jnp.transpose(_unperm_w_in(dw_in_p).reshape(D, N_DEV, -1), (1, 0, 2)),
             jnp.transpose(dpw.astype(BF16).reshape(n_g, N_DEV, pw_rows, POOL_GROUP_DIM),
                           (1, 0, 2, 3)).reshape(N_DEV, n_g * pw_rows, POOL_GROUP_DIM),
             dw_out.reshape(N_DEV, -1, D)]
    _, sums = _reduce_level1("mix", slabs, core, [TILE_W_IN, TILE_POOL, TILE_MIX_OUT])
    dhm = _mm("mix_proj_bwd", dproj, w_in_p, "NT", F32, 512, 512, D_PROJ)
    dx, dsh, dsc, dng = _norm_mod_bwd("mix_norm_bwd", x, dhm, dx_out, norm_g, sc, T, tb)
    return dx, (dsh, dsc, dgate, dng), sums, dps, dgq, dgk, dbf, ride_got


def kernel(x, c, w_ada, b_ada, ffn1_norm_g, ffn1_w_in, ffn1_w_out, mix_norm_g, w_in, b_forget, q_norm_g, k_norm_g, pool_w, pool_scale, w_out, ffn2_norm_g, ffn2_w_in, ffn2_w_out, final_norm_g, loss_target, m_w_ada, m_b_ada, m_ffn1_norm_g, m_ffn1_w_in, m_ffn1_w_out, m_mix_norm_g, m_w_in, m_b_forget, m_q_norm_g, m_k_norm_g, m_pool_w, m_pool_scale, m_w_out, m_ffn2_norm_g, m_ffn2_w_in, m_ffn2_w_out, m_final_norm_g, v_w_ada, v_b_ada, v_ffn1_norm_g, v_ffn1_w_in, v_ffn1_w_out, v_mix_norm_g, v_w_in, v_b_forget, v_q_norm_g, v_k_norm_g, v_pool_w, v_pool_scale, v_w_out, v_ffn2_norm_g, v_ffn2_w_in, v_ffn2_w_out, v_final_norm_g):
    T, D = x.shape[1], x.shape[2]
    mx, my, mc = _mesh_pos()
    me = _flat(mx, my, mc)
    x0 = x[0]
    tgt = loss_target[0]
    tb = min(256, T)

    core = jnp.reshape(mc, (1,)).astype(jnp.int32)
    half = N_DEV // 2
    n_g = len(POOL_WINDOWS)
    pw_rows = POOL_GROUP_DIM // N_DEV

    def bf(w):
        return w.astype(BF16)

    n_loc = w_ada.shape[2]
    c_all = _standalone("gather_c", _gather_comm([c.reshape(8, D // 8)]))[0].reshape(N_DEV, D)
    b_loc = lax.dynamic_slice_in_dim(b_ada, me * n_loc, n_loc, axis=1)
    mod_loc = _ada_fwd(c_all, w_ada[0], b_loc, n_loc // 3)
    mod_all = _standalone("gather_mod", _gather_comm([mod_loc]))[0]
    mod = lax.dynamic_index_in_dim(mod_all, me, axis=1, keepdims=False).reshape(N_MOD, 1, D)
    sh1, sc1, g1, sh2, sc2, g2, sh3, sc3, g3 = [mod[k] for k in range(N_MOD)]
    b_pad = jnp.pad(b_forget, ((0, 0), (0, F_PAD - N_HEADS)))
    ps = pool_scale

    def shard_t(w):
        return jnp.swapaxes(w[0], 0, 1)

    wi1 = _standalone("gather_ffn1_w_in", _gather_comm([bf(shard_t(ffn1_w_in))]))[0]
    x1, sv1, wo1, (w_in_g,) = _ffn_fwd(
        "ffn1", x0, ffn1_norm_g, sh1, sc1, g1, wi1, lambda got: got[0].reshape(half, -1, D), T,
        up_comm=_gather_comm([bf(ffn1_w_out[0])], forward_at=0.7), down_comm=_gather_comm([bf(w_in[0])], forward_at=0.8))
    w_in_p = _perm_w_in(jnp.transpose(w_in_g, (1, 0, 2)).reshape(D, -1))

    def late_weights(got):
        pool_g, w_out_g = got
        pw = jnp.transpose(pool_g.reshape(N_DEV, n_g, pw_rows, POOL_GROUP_DIM),
                           (1, 0, 2, 3)).reshape(n_g, POOL_GROUP_DIM, POOL_GROUP_DIM)
        return pw, w_out_g.reshape(-1, D)

    x2, svm, pw_full, w_out_full, (wi2,) = _mixer_fwd(
        x1, mix_norm_g, sh2, sc2, g2, w_in_p, b_pad, q_norm_g, k_norm_g, late_weights, ps, T,
        proj_comm=_gather_comm([bf(pool_w[0].reshape(-1, POOL_GROUP_DIM)), bf(w_out[0])], forward_at=0.6),
        attn_comm=_gather_comm([bf(shard_t(ffn2_w_in))], forward_at=0.97))
    x3, sv2, wo2, _ = _ffn_fwd("ffn2", x2, ffn2_norm_g, sh3, sc3, g3, wi2,
                               lambda got: got[0].reshape(half, -1, D), T,
                               up_comm=_gather_comm([bf(ffn2_w_out[0])], forward_at=0.7))
    dx3, dgf, loss_l = _final_loss(x3, tgt, final_norm_g.reshape(1, D), T, tb)
    loss = lax.psum(loss_l[0, 0], ("x", "y", "c"))

    dx2, (dsh3, dsc3, dg3, dn3), dwi2_sum, dwo2, _ = _ffn_bwd(
        "ffn2", dx3, sv2, ffn2_norm_g, sc3, g3, wi2, wo2, T, core, defer_dwi=True)
    dx1, (dsh2, dsc2, dg2, dn2), mix_sums, dps, dgq, dgk, dbf, (dwi2,) = _mixer_bwd(
        dx2, svm, mix_norm_g, sc2, g2, w_in_p, b_pad, q_norm_g, k_norm_g, pw_full, ps, w_out_full, T, core,
        ride_sums=[dwi2_sum])
    dx0, (dsh1, dsc1, dg1, dn1), dwi1, dwo1, (dw_in_r, dpw_r, dw_out_r) = _ffn_bwd(
        "ffn1", dx1, sv1, ffn1_norm_g, sc1, g1, wi1, wo1, T, core, ride_sums=mix_sums)

    received = dict(ffn1_w_in=dwi1, ffn1_w_out=dwo1, w_in=dw_in_r, pool_w=dpw_r, w_out=dw_out_r,
                    ffn2_w_in=dwi2, ffn2_w_out=dwo2)
    moments = dict(ffn1_w_in=(m_ffn1_w_in, v_ffn1_w_in), ffn1_w_out=(m_ffn1_w_out, v_ffn1_w_out),
                   w_in=(m_w_in, v_w_in), pool_w=(m_pool_w, v_pool_w), w_out=(m_w_out, v_w_out),
                   ffn2_w_in=(m_ffn2_w_in, v_ffn2_w_in), ffn2_w_out=(m_ffn2_w_out, v_ffn2_w_out))
    weights = dict(ffn1_w_in=ffn1_w_in, ffn1_w_out=ffn1_w_out, w_in=w_in, pool_w=pool_w, w_out=w_out,
                   ffn2_w_in=ffn2_w_in, ffn2_w_out=ffn2_w_out)
    row_tiles = dict(ffn1_w_in=TILE_FFN_IN, ffn1_w_out=TILE_W_OUT, w_in=TILE_W_IN, pool_w=TILE_POOL,
                     w_out=TILE_MIX_OUT, ffn2_w_in=TILE_FFN_IN, ffn2_w_out=TILE_W_OUT)
    results = {}
    for k in received:
        shape = weights[k].shape
        two_d = received[k].shape[1:]
        mk, vk = moments[k]
        if row_tiles[k] is TILE_FFN_IN:
            outs = _adamw("adamw_" + k, received[k], shard_t(weights[k]), shard_t(mk), shard_t(vk), row_tiles[k][0])
            results[k] = [jnp.swapaxes(o, 0, 1)[None] for o in outs]
        else:
            outs = _adamw("adamw_" + k, received[k], weights[k].reshape(two_d), mk.reshape(two_d),
                          vk.reshape(two_d), row_tiles[k][0])
            results[k] = [o.reshape(shape) for o in outs]

    dmod = jnp.concatenate([dsh1, dsc1, dg1, dsh2, dsc2, dg2, dsh3, dsc3, dg3], axis=1)
    small_names = ["b_ada", "ffn1_norm_g", "mix_norm_g", "ffn2_norm_g", "final_norm_g", "b_forget",
                   "q_norm_g", "k_norm_g", "pool_scale"]
    small_w = dict(b_ada=b_ada, ffn1_norm_g=ffn1_norm_g, mix_norm_g=mix_norm_g, ffn2_norm_g=ffn2_norm_g,
                   final_norm_g=final_norm_g, b_forget=b_forget, q_norm_g=q_norm_g, k_norm_g=k_norm_g,
                   pool_scale=pool_scale)
    small_m = dict(b_ada=m_b_ada, ffn1_norm_g=m_ffn1_norm_g, mix_norm_g=m_mix_norm_g, ffn2_norm_g=m_ffn2_norm_g,
                   final_norm_g=m_final_norm_g, b_forget=m_b_forget, q_norm_g=m_q_norm_g, k_norm_g=m_k_norm_g,
                   pool_scale=m_pool_scale)
    small_v = dict(b_ada=v_b_ada, ffn1_norm_g=v_ffn1_norm_g, mix_norm_g=v_mix_norm_g, ffn2_norm_g=v_ffn2_norm_g,
                   final_norm_g=v_final_norm_g, b_forget=v_b_forget, q_norm_g=v_q_norm_g, k_norm_g=v_k_norm_g,
                   pool_scale=v_pool_scale)
    small_g = dict(b_ada=dmod, ffn1_norm_g=dn1, mix_norm_g=dn2, ffn2_norm_g=dn3, final_norm_g=dgf,
                   b_forget=dbf[:, :N_HEADS], q_norm_g=dgq, k_norm_g=dgk, pool_scale=dps)
    sizes = [small_w[k].size for k in small_names]
    total = sum(sizes)
    lanes = 8 * 128
    padded = -(-total // lanes) * lanes

    def pack(d):
        flat = jnp.concatenate([d[k].reshape(-1) for k in small_names])
        return jnp.pad(flat, (0, padded - total)).reshape(8, padded // 8)

    small_parts = _standalone("gather_small_grads", _gather_comm([pack(small_g)]))[0]
    s_outs = _adamw("adamw_small", small_parts, pack(small_w), pack(small_m), pack(small_v), 8)
    offs = [0]
    for s in sizes:
        offs.append(offs[-1] + s)
    for idx, k in enumerate(small_names):
        results[k] = [o.reshape(-1)[offs[idx]:offs[idx + 1]].reshape(small_w[k].shape) for o in s_outs]

    dmod_all = small_parts.reshape(N_DEV, padded)[:, :N_MOD * D]
    dmod_loc = lax.dynamic_slice_in_dim(dmod_all, me * n_loc, n_loc, axis=1)
    g_ada = _ada_bwd(c_all, dmod_loc, n_loc // 3)
    a_outs = _adamw("adamw_w_ada", g_ada[None], w_ada[0], m_w_ada[0], v_w_ada[0], 128)
    results["w_ada"] = [o.reshape(w_ada.shape) for o in a_outs]

    order = ["w_ada", "b_ada", "ffn1_norm_g", "ffn1_w_in", "ffn1_w_out", "mix_norm_g", "w_in", "b_forget",
             "q_norm_g", "k_norm_g", "pool_w", "pool_scale", "w_out", "ffn2_norm_g", "ffn2_w_in", "ffn2_w_out",
             "final_norm_g"]
    out = [loss, dx0[None]]
    for part in range(4):
        out += [results[k][part] for k in order]
    return tuple(out)
```

```python
import jax
import jax.numpy as jnp
from jax import lax
from jax.experimental import pallas as pl
from jax.experimental.pallas import tpu as pltpu

F32 = jnp.float32
BF16 = jnp.bfloat16
MESH = pl.DeviceIdType.MESH
ANY = pl.BlockSpec(memory_space=pl.ANY)

N_DEV = 8
EPS = 1e-6
HEAD_DIM = 128
N_HEADS = 8
POOL_WINDOWS = (2, 4, 8, 16)
POOL_GROUP_DIM = 256
N_MOD = 9
ADAM_LR = 0.001
ADAM_B1 = 0.9
ADAM_B2 = 0.999
ADAM_EPS = 1e-08
ADAM_WD = 0.01
ADAM_STEP = 10
NEG = -1e30
VMEM_LIMIT_V7X = 56 * 1024 * 1024


def _params():
    return pltpu.CompilerParams(vmem_limit_bytes=VMEM_LIMIT_V7X)


def _sigmoid(z):
    return 1.0 / (1.0 + jnp.exp(-z))


def _rstd(x):
    return lax.rsqrt(jnp.mean(x * x, axis=-1, keepdims=True) + EPS)


def _mesh_pos():
    return lax.axis_index("x"), lax.axis_index("y"), lax.axis_index("c")


def _flat(px, py, pc):
    return 4 * px + 2 * py + pc


class _Comm:
    def __init__(self, ins, outs, sems, phases):
        self.ins, self.outs, self.sems, self.phases = list(ins), list(outs), list(sems), list(phases)


def _pallas(kern, *, comm=None, **kw):
    if comm is None:
        return pl.pallas_call(kern, **kw)
    grid = tuple(kw["grid"])
    single = not isinstance(kw["out_shape"], (list, tuple))
    out_shape = [kw["out_shape"]] if single else list(kw["out_shape"])
    out_specs = [kw["out_specs"]] if single else list(kw["out_specs"])
    in_specs = list(kw["in_specs"])
    scratch = list(kw.get("scratch_shapes", ()))
    n_in, n_out, n_scr = len(in_specs), len(out_shape), len(scratch)
    n_ci, n_co = len(comm.ins), len(comm.outs)
    strides, n_steps = [], 1
    for g in reversed(grid):
        strides.insert(0, n_steps)
        n_steps *= g

    def wrapped(*refs):
        ins, cins = refs[:n_in], refs[n_in:n_in + n_ci]
        base = n_in + n_ci
        outs, couts = refs[base:base + n_out], refs[base + n_out:base + n_out + n_co]
        base += n_out + n_co
        scr, sems = refs[base:base + n_scr], refs[base + n_scr:]
        step = sum(pl.program_id(d) * strides[d] for d in range(len(grid)))
        for frac, fn in comm.phases:
            if frac < 1.0:
                pl.when(step == int(round(frac * (n_steps - 1))))(lambda fn=fn: fn(cins, couts, sems))
        kern(*ins, *outs, *scr)
        for frac, fn in comm.phases:
            if frac >= 1.0:
                pl.when(step == n_steps - 1)(lambda fn=fn: fn(cins, couts, sems))

    kw = dict(kw, in_specs=in_specs + [ANY] * n_ci, out_specs=out_specs + [ANY] * n_co,
              out_shape=out_shape + comm.outs, scratch_shapes=scratch + comm.sems)
    call = pl.pallas_call(wrapped, **kw)

    def run(*args):
        res = call(*args, *comm.ins)
        main = res[0] if single else list(res[:n_out])
        return main, list(res[n_out:])

    return run


def _join(first, second):
    n_i, n_o, n_s = len(first.ins), len(first.outs), len(first.sems)

    def left(fn):
        return lambda ins, outs, sems: fn(ins[:n_i], outs[:n_o], sems[:n_s])

    def right(fn):
        return lambda ins, outs, sems: fn(ins[n_i:], outs[n_o:], sems[n_s:])

    phases = [(f, left(fn)) for f, fn in first.phases] + [(f, right(fn)) for f, fn in second.phases]
    return _Comm(first.ins + second.ins, first.outs + second.outs, first.sems + second.sems, phases)


def _hosted(comm, res):
    return res if comm is not None else (res, [])


def _standalone(name, comm):
    def kern():
        pass

    return _pallas(kern, comm=comm, name=name, grid=(1,), in_specs=[], out_specs=[], out_shape=[])()[1]


def _dma_sems(*shapes):
    return [pltpu.SemaphoreType.DMA(s) for s in shapes]


def _gather_comm(arrs, forward_at=0.5):
    n = len(arrs)

    def setup(outs, sems):
        send_sems, recv_sems, _ = sems
        x, y, c = _mesh_pos()
        chips = [(1 - x, y), (x, 1 - y), (1 - x, 1 - y)]

        def copy(a, k, block, to, src=None):
            dst = outs[a].at[_flat(*block)]
            return pltpu.make_async_remote_copy(
                src_ref=dst if src is None else src, dst_ref=dst,
                send_sem=send_sems.at[a, k], recv_sem=recv_sems.at[a, k],
                device_id=to, device_id_type=MESH)

        return (x, y, c), (x, y, 1 - c), chips, copy

    def local(ins, outs, sems, a, me):
        return pltpu.make_async_copy(ins[a], outs[a].at[_flat(*me)], sems[2].at[a])

    def send_own(ins, outs, sems):
        me, sibling, chips, copy = setup(outs, sems)
        for a in range(n):
            local(ins, outs, sems, a, me).start()
            copy(a, 0, me, sibling, src=ins[a]).start()
            for j, chip in enumerate(chips):
                copy(a, 1 + j, me, (*chip, me[2]), src=ins[a]).start()

    def forward(ins, outs, sems):
        me, sibling, chips, copy = setup(outs, sems)
        for a in range(n):
            for j, chip in enumerate(chips):
                copy(a, 1 + j, (*chip, me[2]), me).wait_recv()
                copy(a, 4 + j, (*chip, me[2]), sibling).start()

    def finish(ins, outs, sems):
        me, sibling, chips, copy = setup(outs, sems)
        for a in range(n):
            copy(a, 0, sibling, me).wait_recv()
            for j, chip in enumerate(chips):
                copy(a, 4 + j, (*chip, 1 - me[2]), me).wait_recv()
        for a in range(n):
            copy(a, 0, me, sibling, src=ins[a]).wait_send()
            for j, chip in enumerate(chips):
                copy(a, 1 + j, me, (*chip, me[2]), src=ins[a]).wait_send()
                copy(a, 4 + j, (*chip, me[2]), sibling).wait_send()
            local(ins, outs, sems, a, me).wait()

    return _Comm(arrs, [jax.ShapeDtypeStruct((N_DEV,) + a.shape, a.dtype) for a in arrs],
                 _dma_sems((n, 7), (n, 7), (n,)), [(0.0, send_own), (forward_at, forward), (1.0, finish)])


CHIPS = [(0, 0), (0, 1), (1, 0), (1, 1)]


def _sibling_comm(parts):
    n = len(parts)

    def copies(ins, outs, sems):
        x, y, c = _mesh_pos()
        return [pltpu.make_async_remote_copy(
                    src_ref=ins[a].at[_flat(qx, qy, 1 - c)], dst_ref=outs[a].at[q],
                    send_sem=sems[0].at[a, q], recv_sem=sems[1].at[a, q],
                    device_id=(x, y, 1 - c), device_id_type=MESH)
                for a in range(n) for q, (qx, qy) in enumerate(CHIPS)]

    def start(ins, outs, sems):
        for cp in copies(ins, outs, sems):
            cp.start()

    def finish(ins, outs, sems):
        for cp in copies(ins, outs, sems):
            cp.wait_recv()
        for cp in copies(ins, outs, sems):
            cp.wait_send()

    return _Comm(parts, [jax.ShapeDtypeStruct((4,) + p.shape[1:], p.dtype) for p in parts],
                 _dma_sems((n, 4), (n, 4)), [(0.0, start), (1.0, finish)])


def _chip_comm(sums):
    n = len(sums)
    flips = [(1, 0), (0, 1), (1, 1)]

    def own(ins, outs, sems):
        mine = 2 * lax.axis_index("x") + lax.axis_index("y")
        return [pltpu.make_async_copy(ins[a].at[mine], outs[a].at[mine], sems[2].at[a]) for a in range(n)]

    def copies(ins, outs, sems, arriving=False):
        x, y, c = _mesh_pos()
        mine = 2 * x + y
        remote = []
        for a in range(n):
            for k, (fx, fy) in enumerate(flips):
                qx, qy = x ^ fx, y ^ fy
                q = 2 * qx + qy
                remote.append(pltpu.make_async_remote_copy(
                    src_ref=ins[a].at[q], dst_ref=outs[a].at[q if arriving else mine],
                    send_sem=sems[0].at[a, k], recv_sem=sems[1].at[a, k],
                    device_id=(qx, qy, c), device_id_type=MESH))
        return remote

    def start(ins, outs, sems):
        for cp in own(ins, outs, sems) + copies(ins, outs, sems):
            cp.start()

    def finish(ins, outs, sems):
        for cp in copies(ins, outs, sems, arriving=True):
            cp.wait_recv()
        for cp in copies(ins, outs, sems):
            cp.wait_send()
        for cp in own(ins, outs, sems):
            cp.wait()

    return _Comm(sums, [jax.ShapeDtypeStruct(s.shape, s.dtype) for s in sums],
                 _dma_sems((n, 3), (n, 3), (n,)), [(0.0, start), (1.0, finish)])


def _pair_add(name, parts, got, core, tr):
    _, R, C = parts.shape
    assert R % tr == 0

    def kern(c_ref, p_ref, g_ref, o_ref):
        o_ref[...] = (p_ref[...].astype(F32) + g_ref[...].astype(F32)).astype(o_ref.dtype)

    blk = pl.BlockSpec((None, tr, C), lambda q, i, c_ref: (q, i, 0))
    return pl.pallas_call(
        kern, name=name,
        grid_spec=pltpu.PrefetchScalarGridSpec(
            num_scalar_prefetch=1, grid=(4, R // tr),
            in_specs=[pl.BlockSpec((None, tr, C), lambda q, i, c_ref: (2 * q + c_ref[0], i, 0)), blk],
            out_specs=blk),
        out_shape=jax.ShapeDtypeStruct((4, R, C), parts.dtype), compiler_params=_params(),
    )(core, parts, got)


def _rowwise(name, body, T, tb, rows, vecs, out_rows, out_accs):
    n_in = len(rows) + len(vecs)
    n_o, n_a = len(out_rows), len(out_accs)

    def kern(*refs):
        i = pl.program_id(0)
        res = body(*[r[...] for r in refs[:n_in]])
        if not isinstance(res, (tuple, list)):
            res = (res,)
        outs = refs[n_in:]
        for k in range(n_o):
            outs[k][...] = res[k].astype(outs[k].dtype)

        def accumulate(ref, val):
            @pl.when(i == 0)
            def _():
                ref[...] = val

            @pl.when(i > 0)
            def _():
                ref[...] += val

        for k in range(n_a):
            accumulate(outs[n_o + k], res[n_o + k])

    in_specs = [pl.BlockSpec((tb, w), lambda i, cb=cb: (i, cb)) for (_, w, cb) in rows]
    in_specs += [pl.BlockSpec((1, v.shape[1]), lambda i: (0, 0)) for v in vecs]
    out_specs = [pl.BlockSpec((tb, w), lambda i: (i, 0)) for (w, _) in out_rows]
    out_specs += [pl.BlockSpec((1, w), lambda i: (0, 0)) for w in out_accs]
    out_shape = [jax.ShapeDtypeStruct((T, w), dt) for (w, dt) in out_rows]
    out_shape += [jax.ShapeDtypeStruct((1, w), F32) for w in out_accs]
    res = pl.pallas_call(
        kern, name=name, grid=(T // tb,), in_specs=in_specs, out_specs=out_specs,
        out_shape=out_shape, compiler_params=_params(),
    )(*[r[0] for r in rows], *vecs)
    return res


def _dot(a, b, mode):
    dims = {"NN": ((1,), (0,)), "NT": ((1,), (1,)), "TN": ((0,), (0,))}[mode]
    return lax.dot_general(a.astype(BF16), b.astype(BF16), (dims, ((), ())),
                           preferred_element_type=F32)


def _mm(name, a, b, mode, out_dtype, tm, tn, tk, ga=False, gb=False, gmode=None, comm=None):
    G = (a.shape[0] if ga else b.shape[0]) if gmode else 1
    a2, b2 = a.shape[-2:], b.shape[-2:]
    if mode == "NN":
        (M, K), (_, N) = a2, b2
    elif mode == "NT":
        (M, K), (N, _) = a2, b2
    else:
        (K, M), (_, N) = a2, b2
    tm, tn, tk = min(tm, M), min(tn, N), min(tk, K)
    assert M % tm == 0 and N % tn == 0 and K % tk == 0, (name, M, N, K, tm, tn, tk)
    batch = gmode == "batch"
    n_gb, n_gs = (G if batch else 1), (G if gmode == "sum" else 1)
    nk = K // tk
    n_red = n_gs * nk

    def grp(g_b, g_s):
        return g_b if batch else g_s

    if mode == "TN":
        a_blk, a_idx = (tk, tm), lambda g_b, mi, ni, g_s, ki: (ki, mi)
    else:
        a_blk, a_idx = (tm, tk), lambda g_b, mi, ni, g_s, ki: (mi, ki)
    if mode == "NT":
        b_blk, b_idx = (tn, tk), lambda g_b, mi, ni, g_s, ki: (ni, ki)
    else:
        b_blk, b_idx = (tk, tn), lambda g_b, mi, ni, g_s, ki: (ki, ni)

    def with_group(blk, idx, has_group):
        if not has_group:
            return pl.BlockSpec(blk, idx)
        return pl.BlockSpec((None,) + blk, lambda g_b, mi, ni, g_s, ki: (grp(g_b, g_s),) + idx(g_b, mi, ni, g_s, ki))

    o_blk, o_idx = (tm, tn), lambda g_b, mi, ni, g_s, ki: (mi, ni)
    o_spec = with_group(o_blk, o_idx, batch)
    o_shape = ((G,) if batch else ()) + (M, N)

    def kern(a_ref, b_ref, o_ref, *scratch):
        part = _dot(a_ref[...], b_ref[...], mode)
        if n_red == 1:
            o_ref[...] = part.astype(o_ref.dtype)
            return
        acc = scratch[0]
        step = pl.program_id(3) * nk + pl.program_id(4)

        @pl.when(step == 0)
        def _():
            acc[...] = part

        @pl.when(step > 0)
        def _():
            acc[...] += part

        @pl.when(step == n_red - 1)
        def _():
            o_ref[...] = acc[...].astype(o_ref.dtype)

    return _pallas(
        kern, comm=comm, name=name, grid=(n_gb, M // tm, N // tn, n_gs, nk),
        in_specs=[with_group(a_blk, a_idx, ga), with_group(b_blk, b_idx, gb)],
        out_specs=o_spec, out_shape=jax.ShapeDtypeStruct(o_shape, out_dtype),
        scratch_shapes=[] if n_red == 1 else [pltpu.VMEM((tm, tn), F32)],
        compiler_params=_params(),
    )(a, b)


def _mm_groups(name, a, b, mode, tm, tn, residual=None, comm=None):
    G, M, K = a.shape
    N = b.shape[2] if mode == "NN" else b.shape[1]
    tm, tn = min(tm, M), min(tn, N)
    assert M % tm == 0 and N % tn == 0

    def kern(a_ref, b_ref, *rest):
        acc = _dot(a_ref[0], b_ref[0], mode)
        for g in range(1, G):
            acc = acc + _dot(a_ref[g], b_ref[g], mode)
        if residual is None:
            rest[0][...] = acc
        else:
            x_ref, g_ref, f_ref, o_ref = rest
            f_ref[...] = acc
            o_ref[...] = x_ref[...] + (residual[2] * g_ref[...]) * acc

    b_spec = (pl.BlockSpec((G, K, tn), lambda ni, mi: (0, 0, ni)) if mode == "NN"
              else pl.BlockSpec((G, tn, K), lambda ni, mi: (0, ni, 0)))
    o_spec = pl.BlockSpec((tm, tn), lambda ni, mi: (mi, ni))
    in_specs = [pl.BlockSpec((G, tm, K), lambda ni, mi: (0, mi, 0)), b_spec]
    args = [a, b]
    out = jax.ShapeDtypeStruct((M, N), F32)
    if residual is not None:
        in_specs += [o_spec, pl.BlockSpec((1, tn), lambda ni, mi: (0, ni))]
        args += [residual[0], residual[1]]
    return _pallas(
        kern, comm=comm, name=name, grid=(N // tn, M // tm), in_specs=in_specs,
        out_specs=o_spec if residual is None else [o_spec, o_spec],
        out_shape=out if residual is None else [out, out], compiler_params=_params(),
    )(*args)


def _adamw(name, parts, w, m, v, tr, tc=None):
    G, R, C = parts.shape
    tc = C if tc is None else tc
    assert R % tr == 0 and C % tc == 0
    bc1 = 1.0 - ADAM_B1 ** ADAM_STEP
    bc2 = 1.0 - ADAM_B2 ** ADAM_STEP

    def kern(p_ref, w_ref, m_ref, v_ref, g_out, d_out, m_out, v_out):
        g = p_ref[0].astype(F32)
        for s in range(1, G):
            g = g + p_ref[s].astype(F32)
        m2 = ADAM_B1 * m_ref[...] + (1.0 - ADAM_B1) * g
        v2 = ADAM_B2 * v_ref[...] + (1.0 - ADAM_B2) * (g * g)
        m_hat = m2 / bc1
        v_hat = v2 / bc2
        g_out[...] = g
        d_out[...] = -ADAM_LR * (m_hat / (jnp.sqrt(v_hat) + ADAM_EPS) + ADAM_WD * w_ref[...])
        m_out[...] = m2
        v_out[...] = v2

    blk = pl.BlockSpec((tr, tc), lambda i, j: (i, j))
    return pl.pallas_call(
        kern, name=name, grid=(R // tr, C // tc),
        in_specs=[pl.BlockSpec((G, tr, tc), lambda i, j: (0, i, j)), blk, blk, blk],
        out_specs=[blk] * 4, out_shape=[jax.ShapeDtypeStruct((R, C), F32)] * 4,
        compiler_params=_params(),
    )(parts, w, m, v)


def _ada_fwd(c_all, w_loc, b_loc, tn):
    B, D = c_all.shape
    N = w_loc.shape[1]

    def kern(c_ref, w_ref, b_ref, o_ref):
        cc = c_ref[...]
        act = cc * _sigmoid(cc)
        o_ref[...] = _dot(act, w_ref[...], "NN") + b_ref[...]

    return pl.pallas_call(
        kern, name="ada_fwd", grid=(N // tn,),
        in_specs=[pl.BlockSpec((B, D), lambda j: (0, 0)), pl.BlockSpec((D, tn), lambda j: (0, j)),
                  pl.BlockSpec((1, tn), lambda j: (0, j))],
        out_specs=pl.BlockSpec((B, tn), lambda j: (0, j)),
        out_shape=jax.ShapeDtypeStruct((B, N), F32), compiler_params=_params(),
    )(c_all, w_loc, b_loc)


def _ada_bwd(c_all, dmod_loc, tn):
    B, D = c_all.shape
    N = dmod_loc.shape[1]

    def kern(c_ref, d_ref, o_ref):
        cc = c_ref[...]
        act = cc * _sigmoid(cc)
        o_ref[...] = _dot(act, d_ref[...], "TN")

    return pl.pallas_call(
        kern, name="ada_bwd", grid=(N // tn,),
        in_specs=[pl.BlockSpec((B, D), lambda j: (0, 0)), pl.BlockSpec((B, tn), lambda j: (0, j))],
        out_specs=pl.BlockSpec((D, tn), lambda j: (0, j)),
        out_shape=jax.ShapeDtypeStruct((D, N), F32), compiler_params=_params(),
    )(c_all, dmod_loc)


def _norm_mod_fwd(name, x, g, sc, sh, T, tb):
    D = x.shape[1]

    def body(xb, gb, scb, shb):
        n = (xb * _rstd(xb)) * gb
        return n * (1.0 + scb) + shb

    return _rowwise(name, body, T, tb, [(x, D, 0)], [g, sc, sh], [(D, BF16)], [])[0]


def _norm_mod_bwd(name, x, dhm, dres, g, sc, T, tb):
    D = x.shape[1]

    def body(xb, db, rb, gb, scb):
        r = _rstd(xb)
        xh = xb * r
        n = xh * gb
        dn = db * (1.0 + scb)
        dxh = dn * gb
        dx = rb + r * (dxh - xh * jnp.mean(dxh * xh, axis=-1, keepdims=True))
        return (dx, jnp.sum(db, axis=0, keepdims=True), jnp.sum(db * n, axis=0, keepdims=True),
                jnp.sum(dn * xh, axis=0, keepdims=True))

    return _rowwise(name, body, T, tb, [(x, D, 0), (dhm, D, 0), (dres, D, 0)], [g, sc],
                    [(D, F32)], [D, D, D])


def _residual_bwd(name, dx, f, gate, coef, T, tb):
    D = dx.shape[1]

    def body(db, fb, gb):
        return (coef * gb) * db, jnp.sum((coef * fb) * db, axis=0, keepdims=True)

    return _rowwise(name, body, T, tb, [(dx, D, 0), (f, D, 0)], [gate], [(D, BF16)], [D])


def _final_loss(x, tgt, g, T, tb):
    D = x.shape[1]

    def body(xb, tb_, gb):
        r = _rstd(xb)
        xh = xb * r
        err = xh * gb - tb_
        loss = 0.5 * jnp.sum(jnp.mean(err * err, axis=-1, keepdims=True), axis=0, keepdims=True)
        dy = err * (1.0 / D)
        dxh = dy * gb
        dx = r * (dxh - xh * jnp.mean(dxh * xh, axis=-1, keepdims=True))
        return dx, jnp.sum(dy * xh, axis=0, keepdims=True), jnp.broadcast_to(loss, (1, 128))

    return _rowwise("final_loss", body, T, tb, [(x, D, 0), (tgt, D, 0)], [g], [(D, F32)], [D, 128])


def _ffn_up(name, hm, wi, T, tm, comm=None):
    D = hm.shape[1]
    Ws = wi.shape[1]
    half = wi.shape[0] // 2

    n_sub = 2 if tm % 32 == 0 else 1
    subs = [pl.ds(r * (tm // n_sub), tm // n_sub) for r in range(n_sub)]

    def kern(h_ref, wa_ref, wb_ref, a_ref, b_ref, hid_ref):
        wa, wb = wa_ref[...], wb_ref[...]
        ab = [(_dot(h_ref[rows, :], wa, "NT"), _dot(h_ref[rows, :], wb, "NT")) for rows in subs]
        for rows, (a, b) in zip(subs, ab):
            a_ref[rows, :] = a
            b_ref[rows, :] = b
            hid_ref[rows, :] = ((a * _sigmoid(a)) * b).astype(BF16)

    o_spec = pl.BlockSpec((None, tm, Ws), lambda g, i: (g, i, 0))
    return _pallas(
        kern, comm=comm, name=name, grid=(half, T // tm),
        in_specs=[pl.BlockSpec((tm, D), lambda g, i: (i, 0)),
                  pl.BlockSpec((None, Ws, D), lambda g, i: (g, 0, 0)),
                  pl.BlockSpec((None, Ws, D), lambda g, i: (g + half, 0, 0))],
        out_specs=[o_spec] * 3,
        out_shape=[jax.ShapeDtypeStruct((half, T, Ws), F32)] * 2 + [jax.ShapeDtypeStruct((half, T, Ws), BF16)],
        compiler_params=_params(),
    )(hm, wi, wi)


def _ffn_down_bwd(name, df, wo, a, b, T, tm, comm=None):
    D = df.shape[1]
    half, _, Ws = a.shape

    n_sub = 2 if tm % 32 == 0 else 1
    subs = [pl.ds(r * (tm // n_sub), tm // n_sub) for r in range(n_sub)]

    def kern(df_ref, wo_ref, a_ref, b_ref, dp_ref):
        wo_blk = wo_ref[...]
        dhid = [_dot(df_ref[rows, :], wo_blk, "NT") for rows in subs]
        for rows, dh in zip(subs, dhid):
            av = a_ref[rows, :]
            s = _sigmoid(av)
            silu = av * s
            dp_ref[0, rows, :] = (dh * b_ref[rows, :] * (s + silu * (1.0 - s))).astype(BF16)
            dp_ref[1, rows, :] = (dh * silu).astype(BF16)

    act = pl.BlockSpec((None, tm, Ws), lambda g, i: (g, i, 0))
    return _pallas(
        kern, comm=comm, name=name, grid=(half, T // tm),
        in_specs=[pl.BlockSpec((tm, D), lambda g, i: (i, 0)),
                  pl.BlockSpec((None, Ws, D), lambda g, i: (g, 0, 0)), act, act],
        out_specs=pl.BlockSpec((2, None, tm, Ws), lambda g, i: (0, g, i, 0)),
        out_shape=jax.ShapeDtypeStruct((2, half, T, Ws), BF16),
        compiler_params=_params(),
    )(df, wo, a, b)


def _ffn_fwd(tag, x, norm_g, sh, sc, gate, wi, wo_of, T, up_comm=None, down_comm=None):
    tb = min(256, T)
    hm = _norm_mod_fwd(tag + "_norm_fwd", x, norm_g, sc, sh, T, tb)
    (a, b, hid), got_up = _hosted(up_comm, _ffn_up(tag + "_up", hm, wi, T, min(512, T), comm=up_comm))
    wo = wo_of(got_up)
    (f, x_out), got_down = _hosted(down_comm, _mm_groups(tag + "_down", hid, wo, "NN", 512, 512,
                                                         residual=(x, gate, 0.5), comm=down_comm))
    return x_out, (x, hm, a, b, hid, f), wo, got_down


TILE_W_IN = (513, 513)
TILE_FFN_IN = (688, 688)
TILE_W_OUT = (16, 688)
TILE_MIX_OUT = (64, 256)
TILE_POOL = (128, 128)


def _reduce_level1(tag, parts, core, tiles, host=None):
    comm = _sibling_comm(parts)
    if host is None:
        res, got = None, _standalone(tag + "_sibling", comm)
    else:
        res, got = host(comm)
    sums = [_pair_add("%s_pair_add%d" % (tag, k), p, g, core, min(t[1], p.shape[1]))
            for k, (p, g, t) in enumerate(zip(parts, got, tiles))]
    return res, sums


def _ffn_bwd(tag, dx_out, saved, norm_g, sc, gate, wi, wo, T, core, ride_sums=None, defer_dwi=False):
    x, hm, a, b, hid, f = saved
    tb = min(256, T)
    D = x.shape[1]
    df, dgate = _residual_bwd(tag + "_res_bwd", dx_out, f, gate, 0.5, T, tb)
    dwo = _mm(tag + "_dwo", hid, df, "TN", BF16, 2048, 512, T, ga=True, gmode="batch").reshape(N_DEV, -1, D)
    n_ride = 0 if ride_sums is None else len(ride_sums)

    def down_bwd_call(comm):
        if n_ride:
            comm = _join(comm, _chip_comm(ride_sums))
        res, got = _ffn_down_bwd(tag + "_down_bwd", df, wo, a, b, T, min(512, T), comm=comm)
        return (res, got[len(got) - n_ride:]), got[:len(got) - n_ride]

    (dproj, ride_got), (dwo_sum,) = _reduce_level1(tag + "_dwo", [dwo], core, [TILE_W_OUT], host=down_bwd_call)
    dproj = dproj.reshape((2 * dproj.shape[1],) + dproj.shape[2:])
    dwi, (dwo_got,) = _mm(tag + "_dwi", dproj, hm, "TN", BF16, 2048, 512, T, ga=True, gmode="batch",
                          comm=_chip_comm([dwo_sum]))

    def dhm_call(comm):
        return _mm_groups(tag + "_dhm", dproj, wi, "NN", 512, 512, comm=comm)

    if defer_dwi:
        dhm, (dwi_out,) = _reduce_level1(tag + "_dwi", [dwi], core, [TILE_FFN_IN], host=dhm_call)
    else:
        _, (dwi_sum,) = _reduce_level1(tag + "_dwi", [dwi], core, [TILE_FFN_IN])
        dhm, (dwi_out,) = dhm_call(_chip_comm([dwi_sum]))
    dx, dsh, dsc, dng = _norm_mod_bwd(tag + "_norm_bwd", x, dhm, dx_out, norm_g, sc, T, tb)
    return dx, (dsh, dsc, dgate, dng), dwi_out, dwo_got, ride_got


def _heads(fn, *arrs):
    outs = [fn(*[a[:, h * HEAD_DIM:(h + 1) * HEAD_DIM] for a in arrs]) for h in range(N_HEADS)]
    return outs


def _qknorm_fwd(proj, gq, gk, T, tb):
    W = N_HEADS * HEAD_DIM

    def body(q, k, v, gqb, gkb):
        qn = jnp.concatenate(_heads(lambda t: (t * _rstd(t)) * gqb, q), axis=1)
        kn = jnp.concatenate(_heads(lambda t: (t * _rstd(t)) * gkb, k), axis=1)
        return qn, kn, v

    return _rowwise("qknorm_fwd", body, T, tb, [(proj, W, 0), (proj, W, 1), (proj, W, 2)], [gq, gk],
                    [(W, BF16)] * 3, [])


def _qknorm_bwd(proj, dqn, dkn, gq, gk, T, tb):
    W = N_HEADS * HEAD_DIM

    def one(t, dt, g):
        r = _rstd(t)
        th = t * r
        dth = dt * g
        d = r * (dth - th * jnp.mean(dth * th, axis=-1, keepdims=True))
        return d, jnp.sum(dt * th, axis=0, keepdims=True)

    def body(q, k, dq, dk, gqb, gkb):
        rq = _heads(lambda t, dt: one(t, dt, gqb), q, dq)
        rk = _heads(lambda t, dt: one(t, dt, gkb), k, dk)
        return (jnp.concatenate([r[0] for r in rq], axis=1), jnp.concatenate([r[0] for r in rk], axis=1),
                sum(r[1] for r in rq), sum(r[1] for r in rk))

    return _rowwise("qknorm_bwd", body, T, tb, [(proj, W, 0), (proj, W, 1), (dqn, W, 0), (dkn, W, 0)],
                    [gq, gk], [(W, BF16)] * 2, [HEAD_DIM, HEAD_DIM])


def _log_sigmoid(z):
    return jnp.minimum(z, 0.0) - jnp.log(1.0 + jnp.exp(-jnp.abs(z)))


def _fgate_fwd(proj, fcol, b_pad, T):
    nblk = T // 128

    def kern(f_ref, b_ref, o_ref):
        r = lax.broadcasted_iota(jnp.int32, (128, 128), 0)
        c = lax.broadcasted_iota(jnp.int32, (128, 128), 1)
        tri = (r >= c).astype(F32)
        carry = jnp.zeros((1, 128), F32)
        for k in range(nblk):
            rows = pl.ds(k * 128, 128)
            lf = _log_sigmoid(f_ref[rows, :] + b_ref[...])
            o_ref[rows, :] = jnp.dot(tri, lf, precision=lax.Precision.HIGHEST, preferred_element_type=F32) + carry
            carry = carry + jnp.sum(lf, axis=0, keepdims=True)

    return pl.pallas_call(
        kern, name="fgate_fwd", grid=(1,),
        in_specs=[pl.BlockSpec((T, 128), lambda i: (0, fcol)), pl.BlockSpec((1, 128), lambda i: (0, 0))],
        out_specs=pl.BlockSpec((T, 128), lambda i: (0, 0)),
        out_shape=jax.ShapeDtypeStruct((T, 128), F32), compiler_params=_params(),
    )(proj, b_pad)


def _fgate_bwd(proj, fcol, b_pad, dF, T):
    nblk = T // 128

    def kern(f_ref, b_ref, d_ref, o_ref, db_ref):
        r = lax.broadcasted_iota(jnp.int32, (128, 128), 0)
        c = lax.broadcasted_iota(jnp.int32, (128, 128), 1)
        tri = (c >= r).astype(F32)
        carry = jnp.zeros((1, 128), F32)
        db = jnp.zeros((1, 128), F32)
        for k in reversed(range(nblk)):
            rows = pl.ds(k * 128, 128)
            dblk = d_ref[rows, :]
            rc = jnp.dot(tri, dblk, precision=lax.Precision.HIGHEST, preferred_element_type=F32) + carry
            carry = carry + jnp.sum(dblk, axis=0, keepdims=True)
            z = f_ref[rows, :] + b_ref[...]
            dz = rc * (1.0 / (1.0 + jnp.exp(z)))
            o_ref[rows, :] = dz
            db = db + jnp.sum(dz, axis=0, keepdims=True)
        db_ref[...] = db

    return pl.pallas_call(
        kern, name="fgate_bwd", grid=(1,),
        in_specs=[pl.BlockSpec((T, 128), lambda i: (0, fcol)), pl.BlockSpec((1, 128), lambda i: (0, 0)),
                  pl.BlockSpec((T, 128), lambda i: (0, 0))],
        out_specs=[pl.BlockSpec((T, 128), lambda i: (0, 0)), pl.BlockSpec((1, 128), lambda i: (0, 0))],
        out_shape=[jax.ShapeDtypeStruct((T, 128), F32), jax.ShapeDtypeStruct((1, 128), F32)],
        compiler_params=_params(),
    )(proj, b_pad, dF)


LOG2E = 1.4426950408889634


def _gate_bias(ft, fh, h):
    lane = lax.broadcasted_iota(jnp.int32, ft.shape, 1)
    fq = jnp.sum(jnp.where(lane == h, ft, 0.0), axis=1, keepdims=True)
    f0 = jnp.max(fq, axis=0, keepdims=True)
    sub = lax.broadcasted_iota(jnp.int32, fh.shape, 0)
    fk = jnp.sum(jnp.where(sub == h, fh, 0.0), axis=0, keepdims=True)
    return (f0 - fk) * LOG2E


HEADS_PER_STEP = 2


def _tri_rows(s, nb):
    i = sum((s >= k * (k + 1) // 2).astype(jnp.int32) for k in range(1, nb))
    return i, s - (i * (i + 1)) // 2


def _tri_cols(s, nb):
    j = sum((s >= k * nb - (k * (k - 1)) // 2).astype(jnp.int32) for k in range(1, nb))
    return j, j + s - (j * nb - (j * (j - 1)) // 2)


def _causal_bias(blk):
    row = lax.broadcasted_iota(jnp.int32, (blk, blk), 0)
    col = lax.broadcasted_iota(jnp.int32, (blk, blk), 1)
    return jnp.where(row >= col, 0.0, NEG)


def _attn_fwd(qn, kn, vb, f_tm, f_hm, T, blk, comm=None):
    nb = T // blk
    scale = HEAD_DIM ** -0.5
    W = N_HEADS * HEAD_DIM
    G = HEADS_PER_STEP
    lanes = [slice(g * HEAD_DIM, (g + 1) * HEAD_DIM) for g in range(G)]

    def kern(q_ref, k_ref, v_ref, ft_ref, fh_ref, o_ref, lse_ref, m_scr, l_scr, acc_scr):
        hp = pl.program_id(0)
        i, j = _tri_rows(pl.program_id(1), nb)

        @pl.when(j == 0)
        def _():
            m_scr[...] = jnp.full_like(m_scr, NEG)
            l_scr[...] = jnp.zeros_like(l_scr)
            acc_scr[...] = jnp.zeros_like(acc_scr)

        def block(diagonal):
            ft, fh = ft_ref[...], fh_ref[...]
            s = [_dot(q_ref[:, sl], k_ref[:, sl], "NT") * (scale * LOG2E) + _gate_bias(ft, fh, hp * G + g)
                 for g, sl in enumerate(lanes)]
            if diagonal:
                mask = _causal_bias(blk)
                s = [sg + mask for sg in s]
            m_prev = [m_scr[g] for g in range(G)]
            m_new = [jnp.maximum(mp, jnp.max(sg, axis=1, keepdims=True)) for mp, sg in zip(m_prev, s)]
            alpha = [jnp.exp2(mp - mn) for mp, mn in zip(m_prev, m_new)]
            p = [jnp.exp2(sg - mn) for sg, mn in zip(s, m_new)]
            for g, sl in enumerate(lanes):
                l_scr[g] = alpha[g] * l_scr[g] + jnp.sum(p[g], axis=1, keepdims=True)
                acc_scr[:, sl] = alpha[g] * acc_scr[:, sl] + _dot(p[g], v_ref[:, sl], "NN")
                m_scr[g] = m_new[g]

        @pl.when(j < i)
        def _():
            block(False)

        @pl.when(j == i)
        def _():
            block(True)
            for g, sl in enumerate(lanes):
                l = l_scr[g]
                o_ref[:, sl] = acc_scr[:, sl] / l
                lse_ref[:, sl] = jnp.broadcast_to(m_scr[g] + jnp.log2(l), (blk, HEAD_DIM))

    qspec = pl.BlockSpec((blk, G * HEAD_DIM), lambda h, s: (_tri_rows(s, nb)[0], h))
    kspec = pl.BlockSpec((blk, G * HEAD_DIM), lambda h, s: (_tri_rows(s, nb)[1], h))
    return _pallas(
        kern, comm=comm, name="attn_fwd", grid=(N_HEADS // G, nb * (nb + 1) // 2),
        in_specs=[qspec, kspec, kspec,
                  pl.BlockSpec((blk, 128), lambda h, s: (_tri_rows(s, nb)[0], 0)),
                  pl.BlockSpec((N_HEADS, blk), lambda h, s: (0, _tri_rows(s, nb)[1]))],
        out_specs=[qspec, qspec],
        out_shape=[jax.ShapeDtypeStruct((T, W), F32)] * 2,
        scratch_shapes=[pltpu.VMEM((G, blk, 1), F32), pltpu.VMEM((G, blk, 1), F32),
                        pltpu.VMEM((blk, G * HEAD_DIM), F32)],
        compiler_params=_params(),
    )(qn, kn, vb, f_tm, f_hm)


def _attn_bwd(qn, kn, vb, do, lse, delta, f_tm, f_hm, T, blk, comm=None):
    nb = T // blk
    scale = HEAD_DIM ** -0.5
    W = N_HEADS * HEAD_DIM
    G = HEADS_PER_STEP
    lanes = [slice(g * HEAD_DIM, (g + 1) * HEAD_DIM) for g in range(G)]

    def kern(q_ref, k_ref, v_ref, do_ref, lse_ref, dl_ref, ft_ref, fh_ref,
             dq_ref, dfq_ref, dk_ref, dv_ref, df_ref, dq_scr, dfq_scr, dk_scr, dv_scr, df_scr):
        hp = pl.program_id(0)
        j, i = _tri_cols(pl.program_id(1), nb)

        @pl.when((j == 0) & (i == 0))
        def _():
            dq_scr[...] = jnp.zeros_like(dq_scr)
            dfq_scr[...] = jnp.zeros_like(dfq_scr)

        @pl.when(i == j)
        def _():
            dk_scr[...] = jnp.zeros_like(dk_scr)
            dv_scr[...] = jnp.zeros_like(dv_scr)
            df_scr[...] = jnp.zeros_like(df_scr)

        def block(diagonal):
            ft, fh = ft_ref[...], fh_ref[...]
            rows = pl.ds(pl.multiple_of(i * blk, blk), blk)
            q = [q_ref[:, sl] for sl in lanes]
            k = [k_ref[:, sl] for sl in lanes]
            dob = [do_ref[:, sl].astype(BF16) for sl in lanes]
            s = [_dot(q[g], k[g], "NT") * (scale * LOG2E) + _gate_bias(ft, fh, hp * G + g) for g in range(G)]
            if diagonal:
                mask = _causal_bias(blk)
                s = [sg + mask for sg in s]
            p = [jnp.exp2(s[g] - lse_ref[:, sl.start:sl.start + 1]) for g, sl in enumerate(lanes)]
            dp = [_dot(dob[g], v_ref[:, sl], "NT") for g, sl in enumerate(lanes)]
            ds = [p[g] * (dp[g] - dl_ref[:, sl.start:sl.start + 1]) for g, sl in enumerate(lanes)]
            dsb = [d.astype(BF16) for d in ds]
            for g, sl in enumerate(lanes):
                dv_scr[:, sl] += _dot(p[g], dob[g], "TN")
                dk_scr[:, sl] += _dot(dsb[g], q[g], "TN") * scale
                dq_scr[rows, sl] += _dot(dsb[g], k[g], "NN") * scale
                df_scr[g] += jnp.sum(ds[g], axis=0, keepdims=True)
                dfq_scr[g, rows, :] += jnp.sum(ds[g], axis=1, keepdims=True)

        @pl.when(i > j)
        def _():
            block(False)

        @pl.when(i == j)
        def _():
            block(True)

        @pl.when(i == nb - 1)
        def _():
            dk_ref[...] = dk_scr[...]
            dv_ref[...] = dv_scr[...]
            df_ref[...] = -df_scr[...]

        @pl.when((j == nb - 1) & (i == nb - 1))
        def _():
            dq_ref[...] = dq_scr[...]
            for g, sl in enumerate(lanes):
                dfq_ref[:, sl] = jnp.broadcast_to(dfq_scr[g], (T, HEAD_DIM))

    qspec = pl.BlockSpec((blk, G * HEAD_DIM), lambda h, s: (_tri_cols(s, nb)[1], h))
    full = pl.BlockSpec((T, G * HEAD_DIM), lambda h, s: (0, h))
    kspec = pl.BlockSpec((blk, G * HEAD_DIM), lambda h, s: (_tri_cols(s, nb)[0], h))
    return _pallas(
        kern, comm=comm, name="attn_bwd", grid=(N_HEADS // G, nb * (nb + 1) // 2),
        in_specs=[qspec, kspec, kspec, qspec, qspec, qspec,
                  pl.BlockSpec((blk, 128), lambda h, s: (_tri_cols(s, nb)[1], 0)),
                  pl.BlockSpec((N_HEADS, blk), lambda h, s: (0, _tri_cols(s, nb)[0]))],
        out_specs=[full, full, kspec, kspec, pl.BlockSpec((G, 1, blk), lambda h, s: (h, 0, _tri_cols(s, nb)[0]))],
        out_shape=[jax.ShapeDtypeStruct((T, W), F32)] * 4 + [jax.ShapeDtypeStruct((N_HEADS, 1, T), F32)],
        scratch_shapes=[pltpu.VMEM((T, G * HEAD_DIM), F32), pltpu.VMEM((G, T, 1), F32),
                        pltpu.VMEM((blk, G * HEAD_DIM), F32), pltpu.VMEM((blk, G * HEAD_DIM), F32),
                        pltpu.VMEM((G, 1, blk), F32)],
        compiler_params=_params(),
    )(qn, kn, vb, do, lse, delta, f_tm, f_hm)


def _attn_delta(o, do, T, tb):
    W = N_HEADS * HEAD_DIM

    def body(ob, dob):
        return jnp.concatenate(
            _heads(lambda a, b: jnp.broadcast_to(jnp.sum(a * b, axis=1, keepdims=True), a.shape), ob, dob), axis=1)

    return _rowwise("attn_delta", body, T, tb, [(o, W, 0), (do, W, 0)], [], [(W, F32)], [])[0]


def _window_select(s, g, shift):
    picks = []
    for k in (1, 2, 4, 8):
        s = s + shift(s, k)
        picks.append(s)
    return jnp.where(g == 0, picks[0], jnp.where(g == 1, picks[1], jnp.where(g == 2, picks[2], picks[3])))


def _group_window(g):
    return jnp.where(g == 0, POOL_WINDOWS[0], jnp.where(g == 1, POOL_WINDOWS[1],
                     jnp.where(g == 2, POOL_WINDOWS[2], POOL_WINDOWS[3])))


def _pool_fwd(proj, ucol, pw, ps, T, tb):
    C = POOL_GROUP_DIM
    n_g = len(POOL_WINDOWS)

    def kern(uc_ref, up_ref, pw_ref, ps_ref, pooled_ref, out_ref):
        g, i = pl.program_id(0), pl.program_id(1)
        uc = uc_ref[...]
        t2 = (i - 1) * tb + lax.broadcasted_iota(jnp.int32, (2 * tb, C), 0)
        u2 = jnp.where(t2 >= 0, jnp.concatenate([up_ref[...], uc], axis=0), 0.0)
        sums = _window_select(u2, g, lambda s, k: pltpu.roll(s, k, 0))[tb:, :]
        count = jnp.minimum(t2[tb:, :] + 1, _group_window(g)).astype(F32)
        pooled = sums / count - uc
        pooled_ref[...] = pooled.astype(BF16)
        out_ref[...] = _dot(pooled, pw_ref[...], "NN") * ps_ref[...]

    ospec = pl.BlockSpec((tb, C), lambda g, i: (i, g))
    return pl.pallas_call(
        kern, name="pool_fwd", grid=(n_g, T // tb),
        in_specs=[pl.BlockSpec((tb, C), lambda g, i: (i, ucol + g)),
                  pl.BlockSpec((tb, C), lambda g, i: (jnp.maximum(i - 1, 0), ucol + g)),
                  pl.BlockSpec((None, C, C), lambda g, i: (g, 0, 0)),
                  pl.BlockSpec((1, C), lambda g, i: (0, g))],
        out_specs=[ospec, ospec],
        out_shape=[jax.ShapeDtypeStruct((T, n_g * C), BF16), jax.ShapeDtypeStruct((T, n_g * C), F32)],
        compiler_params=_params(),
    )(proj, proj, pw, ps)


def _pool_bwd(dmix_in, dcol, pooled, pw, ps, T, tb):
    C = POOL_GROUP_DIM
    n_g = len(POOL_WINDOWS)
    nb = T // tb

    def kern(dc_ref, dn_ref, pooled_ref, pw_ref, ps_ref, du_ref, dpw_ref, dps_ref):
        g, i = pl.program_id(0), pl.program_id(1)
        dc = dc_ref[...]
        scale = ps_ref[...]
        t2 = i * tb + lax.broadcasted_iota(jnp.int32, (2 * tb, C), 0)
        d2 = jnp.where(t2 < T, jnp.concatenate([dc, dn_ref[...]], axis=0) * scale, 0.0)
        dpooled2 = _dot(d2, pw_ref[...], "NT")
        count = jnp.minimum(t2 + 1, _group_window(g)).astype(F32)
        sums = _window_select(dpooled2 / count, g, lambda s, k: pltpu.roll(s, 2 * tb - k, 0))
        du_ref[...] = (sums[:tb, :] - dpooled2[:tb, :]).astype(BF16)
        pooled = pooled_ref[...]
        p = _dot(pooled, pw_ref[...], "NN")
        dps = jnp.sum(dc * p, axis=0, keepdims=True)
        dpw = _dot(pooled, d2[:tb, :], "TN")

        @pl.when(i == 0)
        def _():
            dps_ref[...] = dps
            dpw_ref[...] = dpw

        @pl.when(i > 0)
        def _():
            dps_ref[...] += dps
            dpw_ref[...] += dpw

    return pl.pallas_call(
        kern, name="pool_bwd", grid=(n_g, nb),
        in_specs=[pl.BlockSpec((tb, C), lambda g, i: (i, dcol + g)),
                  pl.BlockSpec((tb, C), lambda g, i: (jnp.minimum(i + 1, nb - 1), dcol + g)),
                  pl.BlockSpec((tb, C), lambda g, i: (i, g)),
                  pl.BlockSpec((None, C, C), lambda g, i: (g, 0, 0)),
                  pl.BlockSpec((1, C), lambda g, i: (0, g))],
        out_specs=[pl.BlockSpec((tb, C), lambda g, i: (i, g)),
                   pl.BlockSpec((None, C, C), lambda g, i: (g, 0, 0)),
                   pl.BlockSpec((1, C), lambda g, i: (0, g))],
        out_shape=[jax.ShapeDtypeStruct((T, n_g * C), BF16), jax.ShapeDtypeStruct((n_g, C, C), F32),
                   jax.ShapeDtypeStruct((1, n_g * C), F32)],
        compiler_params=_params(),
    )(dmix_in, dmix_in, pooled, pw, ps)


D_QKV = 3 * N_HEADS * HEAD_DIM
D_U = len(POOL_WINDOWS) * POOL_GROUP_DIM
F_PAD = 128
D_PROJ = D_QKV + D_U + F_PAD


def _perm_w_in(w):
    pad = jnp.zeros((F_PAD - N_HEADS, w.shape[1]), w.dtype)
    return jnp.concatenate([w[:D_QKV], w[D_QKV + N_HEADS:], w[D_QKV:D_QKV + N_HEADS], pad], axis=0)


def _unperm_w_in(w):
    return jnp.concatenate([w[:D_QKV], w[D_QKV + D_U:D_QKV + D_U + N_HEADS], w[D_QKV:D_QKV + D_U]], axis=0)


def _mixer_fwd(x, norm_g, sh, sc, gate, w_in_p, b_pad, gq, gk, late_weights, ps, T, proj_comm, attn_comm):
    tb = min(256, T)
    blk = min(512, T)
    hm = _norm_mod_fwd("mix_norm_fwd", x, norm_g, sc, sh, T, tb)
    proj, got_proj = _mm("mix_proj", hm, w_in_p, "NT", F32, 512, D_PROJ // 3, 2048, comm=proj_comm)
    pw, w_out = late_weights(got_proj)
    qn, kn, vb = _qknorm_fwd(proj, gq, gk, T, tb)
    fcol = (D_QKV + D_U) // 128
    f_tm = _fgate_fwd(proj, fcol, b_pad, T)
    f_hm = f_tm[:, :N_HEADS].T
    (o, lse), got = _attn_fwd(qn, kn, vb, f_tm, f_hm, T, blk, comm=attn_comm)
    pooled, pool_o = _pool_fwd(proj, D_QKV // POOL_GROUP_DIM, pw, ps, T, tb)
    mix_in = jnp.concatenate([o.astype(BF16), pool_o.astype(BF16)], axis=1)
    mix, x_out = _mm_groups("mix_out", mix_in[None], w_out[None], "NN", 512, 512, residual=(x, gate, 1.0))
    return x_out, (x, hm, proj, qn, kn, vb, f_tm, f_hm, o, lse, pooled, mix_in, mix), pw, w_out, got


def _mixer_bwd(dx_out, saved, norm_g, sc, gate, w_in_p, b_pad, gq, gk, pw, ps, w_out, T, core, ride_sums):
    x, hm, proj, qn, kn, vb, f_tm, f_hm, o, lse, pooled, mix_in, mix = saved
    tb = min(256, T)
    blk = min(512, T)
    W = N_HEADS * HEAD_DIM
    D = x.shape[1]
    n_g = len(POOL_WINDOWS)
    dmix, dgate = _residual_bwd("mix_res_bwd", dx_out, mix, gate, 1.0, T, tb)
    dmix_in = _mm("mix_out_bwd", dmix, w_out, "NT", F32, 512, 2048, 2048)
    dw_out = _mm("mix_dw_out", mix_in, dmix, "TN", BF16, 512, 1024, T)
    delta = _attn_delta(o, dmix_in, T, tb)
    (dqn, dfq, dkn, dv, dfk), ride_got = _attn_bwd(qn, kn, vb, dmix_in, lse, delta, f_tm, f_hm, T, blk,
                                                   comm=_chip_comm(ride_sums))
    dq, dk, dgq, dgk = _qknorm_bwd(proj, dqn, dkn, gq, gk, T, tb)
    dF = jnp.pad(dfq[:, ::HEAD_DIM] + dfk.reshape(N_HEADS, T).T, ((0, 0), (0, F_PAD - N_HEADS)))
    fcol = (D_QKV + D_U) // 128
    dfl, dbf = _fgate_bwd(proj, fcol, b_pad, dF, T)
    du, dpw, dps = _pool_bwd(dmix_in, W // POOL_GROUP_DIM, pooled, pw, ps, T, tb)
    dproj = jnp.concatenate([dq, dk, dv.astype(BF16), du, dfl.astype(BF16)], axis=1)
    dw_in_p = _mm("mix_dw_in", dproj, hm, "TN", BF16, D_PROJ // 3, 512, T)
    pw_rows = POOL_GROUP_DIM // N_DEV
    slabs = [_unperm_w_in(dw_in_p).reshape(N_DEV, -1, D),
             jnp.transpose(dpw.astype(BF16).reshape(n_g, N_DEV, pw_rows, POOL_GROUP_DIM),
                           (1, 0, 2, 3)).reshape(N_DEV, n_g * pw_rows, POOL_GROUP_DIM),
             dw_out.reshape(N_DEV, -1, D)]
    _, sums = _reduce_level1("mix", slabs, core, [TILE_W_IN, TILE_POOL, TILE_MIX_OUT])
    dhm = _mm("mix_proj_bwd", dproj, w_in_p, "NN", F32, 512, 512, D_PROJ)
    dx, dsh, dsc, dng = _norm_mod_bwd("mix_norm_bwd", x, dhm, dx_out, norm_g, sc, T, tb)
    return dx, (dsh, dsc, dgate, dng), sums, dps, dgq, dgk, dbf, ride_got


def kernel(x, c, w_ada, b_ada, ffn1_norm_g, ffn1_w_in, ffn1_w_out, mix_norm_g, w_in, b_forget, q_norm_g, k_norm_g, pool_w, pool_scale, w_out, ffn2_norm_g, ffn2_w_in, ffn2_w_out, final_norm_g, loss_target, m_w_ada, m_b_ada, m_ffn1_norm_g, m_ffn1_w_in, m_ffn1_w_out, m_mix_norm_g, m_w_in, m_b_forget, m_q_norm_g, m_k_norm_g, m_pool_w, m_pool_scale, m_w_out, m_ffn2_norm_g, m_ffn2_w_in, m_ffn2_w_out, m_final_norm_g, v_w_ada, v_b_ada, v_ffn1_norm_g, v_ffn1_w_in, v_ffn1_w_out, v_mix_norm_g, v_w_in, v_b_forget, v_q_norm_g, v_k_norm_g, v_pool_w, v_pool_scale, v_w_out, v_ffn2_norm_g, v_ffn2_w_in, v_ffn2_w_out, v_final_norm_g):
    T, D = x.shape[1], x.shape[2]
    mx, my, mc = _mesh_pos()
    me = _flat(mx, my, mc)
    x0 = x[0]
    tgt = loss_target[0]
    tb = min(256, T)

    core = jnp.reshape(mc, (1,)).astype(jnp.int32)
    half = N_DEV // 2
    n_g = len(POOL_WINDOWS)
    pw_rows = POOL_GROUP_DIM // N_DEV

    def bf(w):
        return w.astype(BF16)

    n_loc = w_ada.shape[2]
    c_all = _standalone("gather_c", _gather_comm([c.reshape(8, D // 8)]))[0].reshape(N_DEV, D)
    b_loc = lax.dynamic_slice_in_dim(b_ada, me * n_loc, n_loc, axis=1)
    mod_loc = _ada_fwd(c_all, w_ada[0], b_loc, n_loc // 3)
    mod_all = _standalone("gather_mod", _gather_comm([mod_loc]))[0]
    mod = lax.dynamic_index_in_dim(mod_all, me, axis=1, keepdims=False).reshape(N_MOD, 1, D)
    sh1, sc1, g1, sh2, sc2, g2, sh3, sc3, g3 = [mod[k] for k in range(N_MOD)]
    b_pad = jnp.pad(b_forget, ((0, 0), (0, F_PAD - N_HEADS)))
    ps = pool_scale

    def shard_t(w):
        return jnp.swapaxes(w[0], 0, 1)

    wi1 = _standalone("gather_ffn1_w_in", _gather_comm([bf(shard_t(ffn1_w_in))]))[0]
    x1, sv1, wo1, (w_in_g,) = _ffn_fwd(
        "ffn1", x0, ffn1_norm_g, sh1, sc1, g1, wi1, lambda got: got[0].reshape(half, -1, D), T,
        up_comm=_gather_comm([bf(ffn1_w_out[0])], forward_at=0.7),
        down_comm=_gather_comm([bf(shard_t(w_in))], forward_at=0.8))
    w_in_p = _perm_w_in(w_in_g.reshape(-1, D))

    def late_weights(got):
        pool_g, w_out_g = got
        pw = jnp.transpose(pool_g.reshape(N_DEV, n_g, pw_rows, POOL_GROUP_DIM),
                           (1, 0, 2, 3)).reshape(n_g, POOL_GROUP_DIM, POOL_GROUP_DIM)
        return pw, w_out_g.reshape(-1, D)

    x2, svm, pw_full, w_out_full, (wi2,) = _mixer_fwd(
        x1, mix_norm_g, sh2, sc2, g2, w_in_p, b_pad, q_norm_g, k_norm_g, late_weights, ps, T,
        proj_comm=_gather_comm([bf(pool_w[0].reshape(-1, POOL_GROUP_DIM)), bf(w_out[0])], forward_at=0.6),
        attn_comm=_gather_comm([bf(shard_t(ffn2_w_in))], forward_at=0.97))
    x3, sv2, wo2, _ = _ffn_fwd("ffn2", x2, ffn2_norm_g, sh3, sc3, g3, wi2,
                               lambda got: got[0].reshape(half, -1, D), T,
                               up_comm=_gather_comm([bf(ffn2_w_out[0])], forward_at=0.7))
    dx3, dgf, loss_l = _final_loss(x3, tgt, final_norm_g.reshape(1, D), T, tb)
    loss = lax.psum(loss_l[0, 0], ("x", "y", "c"))

    dx2, (dsh3, dsc3, dg3, dn3), dwi2_sum, dwo2, _ = _ffn_bwd(
        "ffn2", dx3, sv2, ffn2_norm_g, sc3, g3, wi2, wo2, T, core, defer_dwi=True)
    dx1, (dsh2, dsc2, dg2, dn2), mix_sums, dps, dgq, dgk, dbf, (dwi2,) = _mixer_bwd(
        dx2, svm, mix_norm_g, sc2, g2, w_in_p, b_pad, q_norm_g, k_norm_g, pw_full, ps, w_out_full, T, core,
        ride_sums=[dwi2_sum])
    dx0, (dsh1, dsc1, dg1, dn1), dwi1, dwo1, (dw_in_r, dpw_r, dw_out_r) = _ffn_bwd(
        "ffn1", dx1, sv1, ffn1_norm_g, sc1, g1, wi1, wo1, T, core, ride_sums=mix_sums)

    received = dict(ffn1_w_in=dwi1, ffn1_w_out=dwo1, w_in=dw_in_r, pool_w=dpw_r, w_out=dw_out_r,
                    ffn2_w_in=dwi2, ffn2_w_out=dwo2)
    moments = dict(ffn1_w_in=(m_ffn1_w_in, v_ffn1_w_in), ffn1_w_out=(m_ffn1_w_out, v_ffn1_w_out),
                   w_in=(m_w_in, v_w_in), pool_w=(m_pool_w, v_pool_w), w_out=(m_w_out, v_w_out),
                   ffn2_w_in=(m_ffn2_w_in, v_ffn2_w_in), ffn2_w_out=(m_ffn2_w_out, v_ffn2_w_out))
    weights = dict(ffn1_w_in=ffn1_w_in, ffn1_w_out=ffn1_w_out, w_in=w_in, pool_w=pool_w, w_out=w_out,
                   ffn2_w_in=ffn2_w_in, ffn2_w_out=ffn2_w_out)
    row_tiles = dict(ffn1_w_in=TILE_FFN_IN, ffn1_w_out=TILE_W_OUT, w_in=TILE_W_IN, pool_w=TILE_POOL,
                     w_out=TILE_MIX_OUT, ffn2_w_in=TILE_FFN_IN, ffn2_w_out=TILE_W_OUT)
    results = {}
    for k in received:
        shape = weights[k].shape
        two_d = received[k].shape[1:]
        mk, vk = moments[k]
        if row_tiles[k] in (TILE_FFN_IN, TILE_W_IN):
            outs = _adamw("adamw_" + k, received[k], shard_t(weights[k]), shard_t(mk), shard_t(vk), row_tiles[k][0],
                          tc=512)
            results[k] = [jnp.swapaxes(o, 0, 1)[None] for o in outs]
        else:
            outs = _adamw("adamw_" + k, received[k], weights[k].reshape(two_d), mk.reshape(two_d),
                          vk.reshape(two_d), row_tiles[k][0])
            results[k] = [o.reshape(shape) for o in outs]

    dmod = jnp.concatenate([dsh1, dsc1, dg1, dsh2, dsc2, dg2, dsh3, dsc3, dg3], axis=1)
    small_names = ["b_ada", "ffn1_norm_g", "mix_norm_g", "ffn2_norm_g", "final_norm_g", "b_forget",
                   "q_norm_g", "k_norm_g", "pool_scale"]
    small_w = dict(b_ada=b_ada, ffn1_norm_g=ffn1_norm_g, mix_norm_g=mix_norm_g, ffn2_norm_g=ffn2_norm_g,
                   final_norm_g=final_norm_g, b_forget=b_forget, q_norm_g=q_norm_g, k_norm_g=k_norm_g,
                   pool_scale=pool_scale)
    small_m = dict(b_ada=m_b_ada, ffn1_norm_g=m_ffn1_norm_g, mix_norm_g=m_mix_norm_g, ffn2_norm_g=m_ffn2_norm_g,
                   final_norm_g=m_final_norm_g, b_forget=m_b_forget, q_norm_g=m_q_norm_g, k_norm_g=m_k_norm_g,
                   pool_scale=m_pool_scale)
    small_v = dict(b_ada=v_b_ada, ffn1_norm_g=v_ffn1_norm_g, mix_norm_g=v_mix_norm_g, ffn2_norm_g=v_ffn2_norm_g,
                   final_norm_g=v_final_norm_g, b_forget=v_b_forget, q_norm_g=v_q_norm_g, k_norm_g=v_k_norm_g,
                   pool_scale=v_pool_scale)
    small_g = dict(b_ada=dmod, ffn1_norm_g=dn1, mix_norm_g=dn2, ffn2_norm_g=dn3, final_norm_g=dgf,
                   b_forget=dbf[:, :N_HEADS], q_norm_g=dgq, k_norm_g=dgk, pool_scale=dps)
    sizes = [small_w[k].size for k in small_names]
    total = sum(sizes)
    lanes = 8 * 128
    padded = -(-total // lanes) * lanes

    def pack(d):
        flat = jnp.concatenate([d[k].reshape(-1) for k in small_names])
        return jnp.pad(flat, (0, padded - total)).reshape(8, padded // 8)

    small_parts = _standalone("gather_small_grads", _gather_comm([pack(small_g)]))[0]
    s_outs = _adamw("adamw_small", small_parts, pack(small_w), pack(small_m), pack(small_v), 8)
    offs = [0]
    for s in sizes:
        offs.append(offs[-1] + s)
    for idx, k in enumerate(small_names):
        results[k] = [o.reshape(-1)[offs[idx]:offs[idx + 1]].reshape(small_w[k].shape) for o in s_outs]

    dmod_all = small_parts.reshape(N_DEV, padded)[:, :N_MOD * D]
    dmod_loc = lax.dynamic_slice_in_dim(dmod_all, me * n_loc, n_loc, axis=1)
    g_ada = _ada_bwd(c_all, dmod_loc, n_loc // 3)
    a_outs = _adamw("adamw_w_ada", g_ada[None], w_ada[0], m_w_ada[0], v_w_ada[0], 128)
    results["w_ada"] = [o.reshape(w_ada.shape) for o in a_outs]

    order = ["w_ada", "b_ada", "ffn1_norm_g", "ffn1_w_in", "ffn1_w_out", "mix_norm_g", "w_in", "b_forget",
             "q_norm_g", "k_norm_g", "pool_w", "pool_scale", "w_out", "ffn2_norm_g", "ffn2_w_in", "ffn2_w_out",
             "final_norm_g"]
    out = [loss, dx0[None]]
    for part in range(4):
        out += [results[k][part] for k in order]
    return tuple(out)
```

```python
import jax
import jax.numpy as jnp
from jax import lax
from jax.experimental import pallas as pl
from jax.experimental.pallas import tpu as pltpu

F32 = jnp.float32
BF16 = jnp.bfloat16
MESH = pl.DeviceIdType.MESH
ANY = pl.BlockSpec(memory_space=pl.ANY)

N_DEV = 8
EPS = 1e-6
HEAD_DIM = 128
N_HEADS = 8
POOL_WINDOWS = (2, 4, 8, 16)
POOL_GROUP_DIM = 256
N_MOD = 9
ADAM_LR = 0.001
ADAM_B1 = 0.9
ADAM_B2 = 0.999
ADAM_EPS = 1e-08
ADAM_WD = 0.01
ADAM_STEP = 10
NEG = -1e30
VMEM_LIMIT_V7X = 56 * 1024 * 1024


def _params():
    return pltpu.CompilerParams(vmem_limit_bytes=VMEM_LIMIT_V7X)


def _sigmoid(z):
    return 1.0 / (1.0 + jnp.exp(-z))


def _rstd(x):
    return lax.rsqrt(jnp.mean(x * x, axis=-1, keepdims=True) + EPS)


def _mesh_pos():
    return lax.axis_index("x"), lax.axis_index("y"), lax.axis_index("c")


def _flat(px, py, pc):
    return 4 * px + 2 * py + pc


class _Comm:
    def __init__(self, ins, outs, sems, phases):
        self.ins, self.outs, self.sems, self.phases = list(ins), list(outs), list(sems), list(phases)


def _pallas(kern, *, comm=None, **kw):
    if comm is None:
        return pl.pallas_call(kern, **kw)
    grid = tuple(kw["grid"])
    single = not isinstance(kw["out_shape"], (list, tuple))
    out_shape = [kw["out_shape"]] if single else list(kw["out_shape"])
    out_specs = [kw["out_specs"]] if single else list(kw["out_specs"])
    in_specs = list(kw["in_specs"])
    scratch = list(kw.get("scratch_shapes", ()))
    n_in, n_out, n_scr = len(in_specs), len(out_shape), len(scratch)
    n_ci, n_co = len(comm.ins), len(comm.outs)
    strides, n_steps = [], 1
    for g in reversed(grid):
        strides.insert(0, n_steps)
        n_steps *= g

    def wrapped(*refs):
        ins, cins = refs[:n_in], refs[n_in:n_in + n_ci]
        base = n_in + n_ci
        outs, couts = refs[base:base + n_out], refs[base + n_out:base + n_out + n_co]
        base += n_out + n_co
        scr, sems = refs[base:base + n_scr], refs[base + n_scr:]
        step = sum(pl.program_id(d) * strides[d] for d in range(len(grid)))
        for frac, fn in comm.phases:
            if frac < 1.0:
                pl.when(step == int(round(frac * (n_steps - 1))))(lambda fn=fn: fn(cins, couts, sems))
        kern(*ins, *outs, *scr)
        for frac, fn in comm.phases:
            if frac >= 1.0:
                pl.when(step == n_steps - 1)(lambda fn=fn: fn(cins, couts, sems))

    kw = dict(kw, in_specs=in_specs + [ANY] * n_ci, out_specs=out_specs + [ANY] * n_co,
              out_shape=out_shape + comm.outs, scratch_shapes=scratch + comm.sems)
    call = pl.pallas_call(wrapped, **kw)

    def run(*args):
        res = call(*args, *comm.ins)
        main = res[0] if single else list(res[:n_out])
        return main, list(res[n_out:])

    return run


def _join(first, second):
    n_i, n_o, n_s = len(first.ins), len(first.outs), len(first.sems)

    def left(fn):
        return lambda ins, outs, sems: fn(ins[:n_i], outs[:n_o], sems[:n_s])

    def right(fn):
        return lambda ins, outs, sems: fn(ins[n_i:], outs[n_o:], sems[n_s:])

    phases = [(f, left(fn)) for f, fn in first.phases] + [(f, right(fn)) for f, fn in second.phases]
    return _Comm(first.ins + second.ins, first.outs + second.outs, first.sems + second.sems, phases)


def _hosted(comm, res):
    return res if comm is not None else (res, [])


def _standalone(name, comm):
    def kern():
        pass

    return _pallas(kern, comm=comm, name=name, grid=(1,), in_specs=[], out_specs=[], out_shape=[])()[1]


def _dma_sems(*shapes):
    return [pltpu.SemaphoreType.DMA(s) for s in shapes]


def _gather_comm(arrs, forward_at=0.5):
    n = len(arrs)

    def setup(outs, sems):
        send_sems, recv_sems, _ = sems
        x, y, c = _mesh_pos()
        chips = [(1 - x, y), (x, 1 - y), (1 - x, 1 - y)]

        def copy(a, k, block, to, src=None):
            dst = outs[a].at[_flat(*block)]
            return pltpu.make_async_remote_copy(
                src_ref=dst if src is None else src, dst_ref=dst,
                send_sem=send_sems.at[a, k], recv_sem=recv_sems.at[a, k],
                device_id=to, device_id_type=MESH)

        return (x, y, c), (x, y, 1 - c), chips, copy

    def local(ins, outs, sems, a, me):
        return pltpu.make_async_copy(ins[a], outs[a].at[_flat(*me)], sems[2].at[a])

    def send_own(ins, outs, sems):
        me, sibling, chips, copy = setup(outs, sems)
        for a in range(n):
            local(ins, outs, sems, a, me).start()
            copy(a, 0, me, sibling, src=ins[a]).start()
            for j, chip in enumerate(chips):
                copy(a, 1 + j, me, (*chip, me[2]), src=ins[a]).start()

    def forward(ins, outs, sems):
        me, sibling, chips, copy = setup(outs, sems)
        for a in range(n):
            for j, chip in enumerate(chips):
                copy(a, 1 + j, (*chip, me[2]), me).wait_recv()
                copy(a, 4 + j, (*chip, me[2]), sibling).start()

    def finish(ins, outs, sems):
        me, sibling, chips, copy = setup(outs, sems)
        for a in range(n):
            copy(a, 0, sibling, me).wait_recv()
            for j, chip in enumerate(chips):
                copy(a, 4 + j, (*chip, 1 - me[2]), me).wait_recv()
        for a in range(n):
            copy(a, 0, me, sibling, src=ins[a]).wait_send()
            for j, chip in enumerate(chips):
                copy(a, 1 + j, me, (*chip, me[2]), src=ins[a]).wait_send()
                copy(a, 4 + j, (*chip, me[2]), sibling).wait_send()
            local(ins, outs, sems, a, me).wait()

    return _Comm(arrs, [jax.ShapeDtypeStruct((N_DEV,) + a.shape, a.dtype) for a in arrs],
                 _dma_sems((n, 7), (n, 7), (n,)), [(0.0, send_own), (forward_at, forward), (1.0, finish)])


CHIPS = [(0, 0), (0, 1), (1, 0), (1, 1)]


def _sibling_comm(parts):
    n = len(parts)

    def copies(ins, outs, sems):
        x, y, c = _mesh_pos()
        return [pltpu.make_async_remote_copy(
                    src_ref=ins[a].at[_flat(qx, qy, 1 - c)], dst_ref=outs[a].at[q],
                    send_sem=sems[0].at[a, q], recv_sem=sems[1].at[a, q],
                    device_id=(x, y, 1 - c), device_id_type=MESH)
                for a in range(n) for q, (qx, qy) in enumerate(CHIPS)]

    def start(ins, outs, sems):
        for cp in copies(ins, outs, sems):
            cp.start()

    def finish(ins, outs, sems):
        for cp in copies(ins, outs, sems):
            cp.wait_recv()
        for cp in copies(ins, outs, sems):
            cp.wait_send()

    return _Comm(parts, [jax.ShapeDtypeStruct((4,) + p.shape[1:], p.dtype) for p in parts],
                 _dma_sems((n, 4), (n, 4)), [(0.0, start), (1.0, finish)])


def _chip_comm(sums):
    n = len(sums)
    flips = [(1, 0), (0, 1), (1, 1)]

    def own(ins, outs, sems):
        mine = 2 * lax.axis_index("x") + lax.axis_index("y")
        return [pltpu.make_async_copy(ins[a].at[mine], outs[a].at[mine], sems[2].at[a]) for a in range(n)]

    def copies(ins, outs, sems, arriving=False):
        x, y, c = _mesh_pos()
        mine = 2 * x + y
        remote = []
        for a in range(n):
            for k, (fx, fy) in enumerate(flips):
                qx, qy = x ^ fx, y ^ fy
                q = 2 * qx + qy
                remote.append(pltpu.make_async_remote_copy(
                    src_ref=ins[a].at[q], dst_ref=outs[a].at[q if arriving else mine],
                    send_sem=sems[0].at[a, k], recv_sem=sems[1].at[a, k],
                    device_id=(qx, qy, c), device_id_type=MESH))
        return remote

    def start(ins, outs, sems):
        for cp in own(ins, outs, sems) + copies(ins, outs, sems):
            cp.start()

    def finish(ins, outs, sems):
        for cp in copies(ins, outs, sems, arriving=True):
            cp.wait_recv()
        for cp in copies(ins, outs, sems):
            cp.wait_send()
        for cp in own(ins, outs, sems):
            cp.wait()

    return _Comm(sums, [jax.ShapeDtypeStruct(s.shape, s.dtype) for s in sums],
                 _dma_sems((n, 3), (n, 3), (n,)), [(0.0, start), (1.0, finish)])


def _pair_add(name, parts, got, core, tr):
    _, R, C = parts.shape
    assert R % tr == 0

    def kern(c_ref, p_ref, g_ref, o_ref):
        o_ref[...] = (p_ref[...].astype(F32) + g_ref[...].astype(F32)).astype(o_ref.dtype)

    blk = pl.BlockSpec((None, tr, C), lambda q, i, c_ref: (q, i, 0))
    return pl.pallas_call(
        kern, name=name,
        grid_spec=pltpu.PrefetchScalarGridSpec(
            num_scalar_prefetch=1, grid=(4, R // tr),
            in_specs=[pl.BlockSpec((None, tr, C), lambda q, i, c_ref: (2 * q + c_ref[0], i, 0)), blk],
            out_specs=blk),
        out_shape=jax.ShapeDtypeStruct((4, R, C), parts.dtype), compiler_params=_params(),
    )(core, parts, got)


def _rowwise(name, body, T, tb, rows, vecs, out_rows, out_accs):
    n_in = len(rows) + len(vecs)
    n_o, n_a = len(out_rows), len(out_accs)

    def kern(*refs):
        i = pl.program_id(0)
        res = body(*[r[...] for r in refs[:n_in]])
        if not isinstance(res, (tuple, list)):
            res = (res,)
        outs = refs[n_in:]
        for k in range(n_o):
            outs[k][...] = res[k].astype(outs[k].dtype)

        def accumulate(ref, val):
            @pl.when(i == 0)
            def _():
                ref[...] = val

            @pl.when(i > 0)
            def _():
                ref[...] += val

        for k in range(n_a):
            accumulate(outs[n_o + k], res[n_o + k])

    in_specs = [pl.BlockSpec((tb, w), lambda i, cb=cb: (i, cb)) for (_, w, cb) in rows]
    in_specs += [pl.BlockSpec((1, v.shape[1]), lambda i: (0, 0)) for v in vecs]
    out_specs = [pl.BlockSpec((tb, w), lambda i: (i, 0)) for (w, _) in out_rows]
    out_specs += [pl.BlockSpec((1, w), lambda i: (0, 0)) for w in out_accs]
    out_shape = [jax.ShapeDtypeStruct((T, w), dt) for (w, dt) in out_rows]
    out_shape += [jax.ShapeDtypeStruct((1, w), F32) for w in out_accs]
    res = pl.pallas_call(
        kern, name=name, grid=(T // tb,), in_specs=in_specs, out_specs=out_specs,
        out_shape=out_shape, compiler_params=_params(),
    )(*[r[0] for r in rows], *vecs)
    return res


def _dot(a, b, mode):
    dims = {"NN": ((1,), (0,)), "NT": ((1,), (1,)), "TN": ((0,), (0,))}[mode]
    return lax.dot_general(a.astype(BF16), b.astype(BF16), (dims, ((), ())),
                           preferred_element_type=F32)


def _mm(name, a, b, mode, out_dtype, tm, tn, tk, ga=False, gb=False, gmode=None, comm=None):
    G = (a.shape[0] if ga else b.shape[0]) if gmode else 1
    a2, b2 = a.shape[-2:], b.shape[-2:]
    if mode == "NN":
        (M, K), (_, N) = a2, b2
    elif mode == "NT":
        (M, K), (N, _) = a2, b2
    else:
        (K, M), (_, N) = a2, b2
    tm, tn, tk = min(tm, M), min(tn, N), min(tk, K)
    assert M % tm == 0 and N % tn == 0 and K % tk == 0, (name, M, N, K, tm, tn, tk)
    batch = gmode == "batch"
    n_gb, n_gs = (G if batch else 1), (G if gmode == "sum" else 1)
    nk = K // tk
    n_red = n_gs * nk

    def grp(g_b, g_s):
        return g_b if batch else g_s

    if mode == "TN":
        a_blk, a_idx = (tk, tm), lambda g_b, mi, ni, g_s, ki: (ki, mi)
    else:
        a_blk, a_idx = (tm, tk), lambda g_b, mi, ni, g_s, ki: (mi, ki)
    if mode == "NT":
        b_blk, b_idx = (tn, tk), lambda g_b, mi, ni, g_s, ki: (ni, ki)
    else:
        b_blk, b_idx = (tk, tn), lambda g_b, mi, ni, g_s, ki: (ki, ni)

    def with_group(blk, idx, has_group):
        if not has_group:
            return pl.BlockSpec(blk, idx)
        return pl.BlockSpec((None,) + blk, lambda g_b, mi, ni, g_s, ki: (grp(g_b, g_s),) + idx(g_b, mi, ni, g_s, ki))

    o_blk, o_idx = (tm, tn), lambda g_b, mi, ni, g_s, ki: (mi, ni)
    o_spec = with_group(o_blk, o_idx, batch)
    o_shape = ((G,) if batch else ()) + (M, N)

    def kern(a_ref, b_ref, o_ref, *scratch):
        part = _dot(a_ref[...], b_ref[...], mode)
        if n_red == 1:
            o_ref[...] = part.astype(o_ref.dtype)
            return
        acc = scratch[0]
        step = pl.program_id(3) * nk + pl.program_id(4)

        @pl.when(step == 0)
        def _():
            acc[...] = part

        @pl.when(step > 0)
        def _():
            acc[...] += part

        @pl.when(step == n_red - 1)
        def _():
            o_ref[...] = acc[...].astype(o_ref.dtype)

    return _pallas(
        kern, comm=comm, name=name, grid=(n_gb, M // tm, N // tn, n_gs, nk),
        in_specs=[with_group(a_blk, a_idx, ga), with_group(b_blk, b_idx, gb)],
        out_specs=o_spec, out_shape=jax.ShapeDtypeStruct(o_shape, out_dtype),
        scratch_shapes=[] if n_red == 1 else [pltpu.VMEM((tm, tn), F32)],
        compiler_params=_params(),
    )(a, b)


def _mm_groups(name, a, b, mode, tm, tn, residual=None, comm=None):
    G, M, K = a.shape
    N = b.shape[2] if mode == "NN" else b.shape[1]
    tm, tn = min(tm, M), min(tn, N)
    assert M % tm == 0 and N % tn == 0

    def kern(a_ref, b_ref, *rest):
        acc = _dot(a_ref[0], b_ref[0], mode)
        for g in range(1, G):
            acc = acc + _dot(a_ref[g], b_ref[g], mode)
        if residual is None:
            rest[0][...] = acc
        else:
            x_ref, g_ref, f_ref, o_ref = rest
            f_ref[...] = acc
            o_ref[...] = x_ref[...] + (residual[2] * g_ref[...]) * acc

    b_spec = (pl.BlockSpec((G, K, tn), lambda ni, mi: (0, 0, ni)) if mode == "NN"
              else pl.BlockSpec((G, tn, K), lambda ni, mi: (0, ni, 0)))
    o_spec = pl.BlockSpec((tm, tn), lambda ni, mi: (mi, ni))
    in_specs = [pl.BlockSpec((G, tm, K), lambda ni, mi: (0, mi, 0)), b_spec]
    args = [a, b]
    out = jax.ShapeDtypeStruct((M, N), F32)
    if residual is not None:
        in_specs += [o_spec, pl.BlockSpec((1, tn), lambda ni, mi: (0, ni))]
        args += [residual[0], residual[1]]
    return _pallas(
        kern, comm=comm, name=name, grid=(N // tn, M // tm), in_specs=in_specs,
        out_specs=o_spec if residual is None else [o_spec, o_spec],
        out_shape=out if residual is None else [out, out], compiler_params=_params(),
    )(*args)


def _adamw(name, parts, w, m, v, tr, tc=None):
    G, R, C = parts.shape
    tc = C if tc is None else tc
    assert R % tr == 0 and C % tc == 0
    bc1 = 1.0 - ADAM_B1 ** ADAM_STEP
    bc2 = 1.0 - ADAM_B2 ** ADAM_STEP

    def kern(p_ref, w_ref, m_ref, v_ref, g_out, d_out, m_out, v_out):
        g = p_ref[0].astype(F32)
        for s in range(1, G):
            g = g + p_ref[s].astype(F32)
        m2 = ADAM_B1 * m_ref[...] + (1.0 - ADAM_B1) * g
        v2 = ADAM_B2 * v_ref[...] + (1.0 - ADAM_B2) * (g * g)
        m_hat = m2 / bc1
        v_hat = v2 / bc2
        g_out[...] = g
        d_out[...] = -ADAM_LR * (m_hat / (jnp.sqrt(v_hat) + ADAM_EPS) + ADAM_WD * w_ref[...])
        m_out[...] = m2
        v_out[...] = v2

    blk = pl.BlockSpec((tr, tc), lambda i, j: (i, j))
    return pl.pallas_call(
        kern, name=name, grid=(R // tr, C // tc),
        in_specs=[pl.BlockSpec((G, tr, tc), lambda i, j: (0, i, j)), blk, blk, blk],
        out_specs=[blk] * 4, out_shape=[jax.ShapeDtypeStruct((R, C), F32)] * 4,
        compiler_params=_params(),
    )(parts, w, m, v)


def _ada_fwd(c_all, w_loc, b_loc, tn):
    B, D = c_all.shape
    N = w_loc.shape[1]

    def kern(c_ref, w_ref, b_ref, o_ref):
        cc = c_ref[...]
        act = cc * _sigmoid(cc)
        o_ref[...] = _dot(act, w_ref[...], "NN") + b_ref[...]

    return pl.pallas_call(
        kern, name="ada_fwd", grid=(N // tn,),
        in_specs=[pl.BlockSpec((B, D), lambda j: (0, 0)), pl.BlockSpec((D, tn), lambda j: (0, j)),
                  pl.BlockSpec((1, tn), lambda j: (0, j))],
        out_specs=pl.BlockSpec((B, tn), lambda j: (0, j)),
        out_shape=jax.ShapeDtypeStruct((B, N), F32), compiler_params=_params(),
    )(c_all, w_loc, b_loc)


def _ada_bwd(c_all, dmod_loc, tn):
    B, D = c_all.shape
    N = dmod_loc.shape[1]

    def kern(c_ref, d_ref, o_ref):
        cc = c_ref[...]
        act = cc * _sigmoid(cc)
        o_ref[...] = _dot(act, d_ref[...], "TN")

    return pl.pallas_call(
        kern, name="ada_bwd", grid=(N // tn,),
        in_specs=[pl.BlockSpec((B, D), lambda j: (0, 0)), pl.BlockSpec((B, tn), lambda j: (0, j))],
        out_specs=pl.BlockSpec((D, tn), lambda j: (0, j)),
        out_shape=jax.ShapeDtypeStruct((D, N), F32), compiler_params=_params(),
    )(c_all, dmod_loc)


def _norm_mod_fwd(name, x, g, sc, sh, T, tb):
    D = x.shape[1]

    def body(xb, gb, scb, shb):
        n = (xb * _rstd(xb)) * gb
        return n * (1.0 + scb) + shb

    return _rowwise(name, body, T, tb, [(x, D, 0)], [g, sc, sh], [(D, BF16)], [])[0]


def _branch_bwd(dx, fb, gateb, coef):
    return (coef * gateb) * dx, jnp.sum((coef * fb) * dx, axis=0, keepdims=True)


def _norm_mod_bwd(name, x, dhm, dres, g, sc, T, tb, below=None):
    D = x.shape[1]

    def body(xb, db, rb, *rest):
        gb, scb = rest[-2:] if below is None else rest[1:3]
        r = _rstd(xb)
        xh = xb * r
        n = xh * gb
        dn = db * (1.0 + scb)
        dxh = dn * gb
        dx = rb + r * (dxh - xh * jnp.mean(dxh * xh, axis=-1, keepdims=True))
        sums = (jnp.sum(db, axis=0, keepdims=True), jnp.sum(db * n, axis=0, keepdims=True),
                jnp.sum(dn * xh, axis=0, keepdims=True))
        if below is None:
            return (dx,) + sums
        df, dgate = _branch_bwd(dx, rest[0], rest[3], below[2])
        return (dx, df) + sums + (dgate,)

    rows = [(x, D, 0), (dhm, D, 0), (dres, D, 0)] + ([] if below is None else [(below[0], D, 0)])
    vecs = [g, sc] + ([] if below is None else [below[1]])
    return _rowwise(name, body, T, tb, rows, vecs, [(D, F32)] + ([] if below is None else [(D, BF16)]),
                    [D, D, D] + ([] if below is None else [D]))


def _final_loss(x, tgt, g, T, tb, below):
    D = x.shape[1]

    def body(xb, tb_, fb, gb, gateb):
        r = _rstd(xb)
        xh = xb * r
        err = xh * gb - tb_
        loss = 0.5 * jnp.sum(jnp.mean(err * err, axis=-1, keepdims=True), axis=0, keepdims=True)
        dy = err * (1.0 / D)
        dxh = dy * gb
        dx = r * (dxh - xh * jnp.mean(dxh * xh, axis=-1, keepdims=True))
        df, dgate = _branch_bwd(dx, fb, gateb, below[2])
        return dx, df, jnp.sum(dy * xh, axis=0, keepdims=True), jnp.broadcast_to(loss, (1, 128)), dgate

    return _rowwise("final_loss", body, T, tb, [(x, D, 0), (tgt, D, 0), (below[0], D, 0)], [g, below[1]],
                    [(D, F32), (D, BF16)], [D, 128, D])


def _ffn_up(name, hm, wi, T, tm, comm=None):
    D = hm.shape[1]
    Ws = wi.shape[1]
    half = wi.shape[0] // 2

    n_sub = 2 if tm % 32 == 0 else 1
    subs = [pl.ds(r * (tm // n_sub), tm // n_sub) for r in range(n_sub)]

    def kern(h_ref, wa_ref, wb_ref, a_ref, b_ref, hid_ref):
        wa, wb = wa_ref[...], wb_ref[...]
        ab = [(_dot(h_ref[rows, :], wa, "NT"), _dot(h_ref[rows, :], wb, "NT")) for rows in subs]
        for rows, (a, b) in zip(subs, ab):
            a_ref[rows, :] = a
            b_ref[rows, :] = b
            hid_ref[rows, :] = ((a * _sigmoid(a)) * b).astype(BF16)

    o_spec = pl.BlockSpec((None, tm, Ws), lambda g, i: (g, i, 0))
    return _pallas(
        kern, comm=comm, name=name, grid=(half, T // tm),
        in_specs=[pl.BlockSpec((tm, D), lambda g, i: (i, 0)),
                  pl.BlockSpec((None, Ws, D), lambda g, i: (g, 0, 0)),
                  pl.BlockSpec((None, Ws, D), lambda g, i: (g + half, 0, 0))],
        out_specs=[o_spec] * 3,
        out_shape=[jax.ShapeDtypeStruct((half, T, Ws), F32)] * 2 + [jax.ShapeDtypeStruct((half, T, Ws), BF16)],
        compiler_params=_params(),
    )(hm, wi, wi)


def _ffn_down_bwd(name, df, wo, a, b, T, tm, comm=None):
    D = df.shape[1]
    half, _, Ws = a.shape

    n_sub = 2 if tm % 32 == 0 else 1
    subs = [pl.ds(r * (tm // n_sub), tm // n_sub) for r in range(n_sub)]

    def kern(df_ref, wo_ref, a_ref, b_ref, dp_ref):
        wo_blk = wo_ref[...]
        dhid = [_dot(df_ref[rows, :], wo_blk, "NT") for rows in subs]
        for rows, dh in zip(subs, dhid):
            av = a_ref[rows, :]
            s = _sigmoid(av)
            silu = av * s
            dp_ref[0, rows, :] = (dh * b_ref[rows, :] * (s + silu * (1.0 - s))).astype(BF16)
            dp_ref[1, rows, :] = (dh * silu).astype(BF16)

    act = pl.BlockSpec((None, tm, Ws), lambda g, i: (g, i, 0))
    return _pallas(
        kern, comm=comm, name=name, grid=(half, T // tm),
        in_specs=[pl.BlockSpec((tm, D), lambda g, i: (i, 0)),
                  pl.BlockSpec((None, Ws, D), lambda g, i: (g, 0, 0)), act, act],
        out_specs=pl.BlockSpec((2, None, tm, Ws), lambda g, i: (0, g, i, 0)),
        out_shape=jax.ShapeDtypeStruct((2, half, T, Ws), BF16),
        compiler_params=_params(),
    )(df, wo, a, b)


def _ffn_fwd(tag, x, norm_g, sh, sc, gate, wi, wo_of, T, up_comm=None, down_comm=None):
    tb = min(256, T)
    hm = _norm_mod_fwd(tag + "_norm_fwd", x, norm_g, sc, sh, T, tb)
    (a, b, hid), got_up = _hosted(up_comm, _ffn_up(tag + "_up", hm, wi, T, min(512, T), comm=up_comm))
    wo = wo_of(got_up)
    (f, x_out), got_down = _hosted(down_comm, _mm_groups(tag + "_down", hid, wo, "NN", 512, 512,
                                                         residual=(x, gate, 0.5), comm=down_comm))
    return x_out, (x, hm, a, b, hid, f), wo, got_down


TILE_W_IN = (513, 513)
TILE_FFN_IN = (688, 688)
TILE_W_OUT = (16, 688)
TILE_MIX_OUT = (64, 256)
TILE_POOL = (128, 128)


def _reduce_level1(tag, parts, core, tiles, host=None):
    comm = _sibling_comm(parts)
    if host is None:
        res, got = None, _standalone(tag + "_sibling", comm)
    else:
        res, got = host(comm)
    sums = [_pair_add("%s_pair_add%d" % (tag, k), p, g, core, min(t[1], p.shape[1]))
            for k, (p, g, t) in enumerate(zip(parts, got, tiles))]
    return res, sums


def _ffn_bwd(tag, dx_out, branch, saved, norm_g, sc, wi, wo, T, core, ride_sums=None, defer_dwi=False,
             below=None):
    x, hm, a, b, hid, f = saved
    tb = min(256, T)
    D = x.shape[1]
    df, dgate = branch
    dwo = _mm(tag + "_dwo", hid, df, "TN", BF16, 2048, 512, T, ga=True, gmode="batch").reshape(N_DEV, -1, D)
    n_ride = 0 if ride_sums is None else len(ride_sums)

    def down_bwd_call(comm):
        if n_ride:
            comm = _join(comm, _chip_comm(ride_sums))
        res, got = _ffn_down_bwd(tag + "_down_bwd", df, wo, a, b, T, min(512, T), comm=comm)
        return (res, got[len(got) - n_ride:]), got[:len(got) - n_ride]

    (dproj, ride_got), (dwo_sum,) = _reduce_level1(tag + "_dwo", [dwo], core, [TILE_W_OUT], host=down_bwd_call)
    dproj = dproj.reshape((2 * dproj.shape[1],) + dproj.shape[2:])
    dwi, (dwo_got,) = _mm(tag + "_dwi", dproj, hm, "TN", BF16, 2048, 512, T, ga=True, gmode="batch",
                          comm=_chip_comm([dwo_sum]))

    def dhm_call(comm):
        return _mm_groups(tag + "_dhm", dproj, wi, "NN", 512, 512, comm=comm)

    if defer_dwi:
        dhm, (dwi_out,) = _reduce_level1(tag + "_dwi", [dwi], core, [TILE_FFN_IN], host=dhm_call)
    else:
        _, (dwi_sum,) = _reduce_level1(tag + "_dwi", [dwi], core, [TILE_FFN_IN])
        dhm, (dwi_out,) = dhm_call(_chip_comm([dwi_sum]))
    res = _norm_mod_bwd(tag + "_norm_bwd", x, dhm, dx_out, norm_g, sc, T, tb, below=below)
    dx, (dsh, dsc, dng) = res[0], res[-3:] if below is None else res[2:5]
    return dx, (dsh, dsc, dgate, dng), dwi_out, dwo_got, ride_got, None if below is None else (res[1], res[5])


def _heads(fn, *arrs):
    outs = [fn(*[a[:, h * HEAD_DIM:(h + 1) * HEAD_DIM] for a in arrs]) for h in range(N_HEADS)]
    return outs


def _qknorm_fwd(proj, gq, gk, T, tb):
    W = N_HEADS * HEAD_DIM

    def body(q, k, v, gqb, gkb):
        qn = jnp.concatenate(_heads(lambda t: (t * _rstd(t)) * gqb, q), axis=1)
        kn = jnp.concatenate(_heads(lambda t: (t * _rstd(t)) * gkb, k), axis=1)
        return qn, kn, v

    return _rowwise("qknorm_fwd", body, T, tb, [(proj, W, 0), (proj, W, 1), (proj, W, 2)], [gq, gk],
                    [(W, BF16)] * 3, [])


def _qknorm_bwd(proj, dqn, dkn, gq, gk, T, tb):
    W = N_HEADS * HEAD_DIM

    def one(t, dt, g):
        r = _rstd(t)
        th = t * r
        dth = dt * g
        d = r * (dth - th * jnp.mean(dth * th, axis=-1, keepdims=True))
        return d, jnp.sum(dt * th, axis=0, keepdims=True)

    def body(q, k, dq, dk, gqb, gkb):
        rq = _heads(lambda t, dt: one(t, dt, gqb), q, dq)
        rk = _heads(lambda t, dt: one(t, dt, gkb), k, dk)
        return (jnp.concatenate([r[0] for r in rq], axis=1), jnp.concatenate([r[0] for r in rk], axis=1),
                sum(r[1] for r in rq), sum(r[1] for r in rk))

    return _rowwise("qknorm_bwd", body, T, tb, [(proj, W, 0), (proj, W, 1), (dqn, W, 0), (dkn, W, 0)],
                    [gq, gk], [(W, BF16)] * 2, [HEAD_DIM, HEAD_DIM])


def _log_sigmoid(z):
    return jnp.minimum(z, 0.0) - jnp.log(1.0 + jnp.exp(-jnp.abs(z)))


def _fgate_fwd(proj, fcol, b_pad, T):
    nblk = T // 128

    def kern(f_ref, b_ref, o_ref):
        r = lax.broadcasted_iota(jnp.int32, (128, 128), 0)
        c = lax.broadcasted_iota(jnp.int32, (128, 128), 1)
        tri = (r >= c).astype(F32)
        carry = jnp.zeros((1, 128), F32)
        for k in range(nblk):
            rows = pl.ds(k * 128, 128)
            lf = _log_sigmoid(f_ref[rows, :] + b_ref[...])
            o_ref[rows, :] = jnp.dot(tri, lf, precision=lax.Precision.HIGHEST, preferred_element_type=F32) + carry
            carry = carry + jnp.sum(lf, axis=0, keepdims=True)

    return pl.pallas_call(
        kern, name="fgate_fwd", grid=(1,),
        in_specs=[pl.BlockSpec((T, 128), lambda i: (0, fcol)), pl.BlockSpec((1, 128), lambda i: (0, 0))],
        out_specs=pl.BlockSpec((T, 128), lambda i: (0, 0)),
        out_shape=jax.ShapeDtypeStruct((T, 128), F32), compiler_params=_params(),
    )(proj, b_pad)


def _fgate_bwd(proj, fcol, b_pad, dF, T):
    nblk = T // 128

    def kern(f_ref, b_ref, d_ref, o_ref, db_ref):
        r = lax.broadcasted_iota(jnp.int32, (128, 128), 0)
        c = lax.broadcasted_iota(jnp.int32, (128, 128), 1)
        tri = (c >= r).astype(F32)
        carry = jnp.zeros((1, 128), F32)
        db = jnp.zeros((1, 128), F32)
        for k in reversed(range(nblk)):
            rows = pl.ds(k * 128, 128)
            dblk = d_ref[rows, :]
            rc = jnp.dot(tri, dblk, precision=lax.Precision.HIGHEST, preferred_element_type=F32) + carry
            carry = carry + jnp.sum(dblk, axis=0, keepdims=True)
            z = f_ref[rows, :] + b_ref[...]
            dz = rc * (1.0 / (1.0 + jnp.exp(z)))
            o_ref[rows, :] = dz
            db = db + jnp.sum(dz, axis=0, keepdims=True)
        db_ref[...] = db

    return pl.pallas_call(
        kern, name="fgate_bwd", grid=(1,),
        in_specs=[pl.BlockSpec((T, 128), lambda i: (0, fcol)), pl.BlockSpec((1, 128), lambda i: (0, 0)),
                  pl.BlockSpec((T, 128), lambda i: (0, 0))],
        out_specs=[pl.BlockSpec((T, 128), lambda i: (0, 0)), pl.BlockSpec((1, 128), lambda i: (0, 0))],
        out_shape=[jax.ShapeDtypeStruct((T, 128), F32), jax.ShapeDtypeStruct((1, 128), F32)],
        compiler_params=_params(),
    )(proj, b_pad, dF)


LOG2E = 1.4426950408889634


def _gate_bias(ft, fh, h):
    lane = lax.broadcasted_iota(jnp.int32, ft.shape, 1)
    fq = jnp.sum(jnp.where(lane == h, ft, 0.0), axis=1, keepdims=True)
    f0 = jnp.max(fq, axis=0, keepdims=True)
    sub = lax.broadcasted_iota(jnp.int32, fh.shape, 0)
    fk = jnp.sum(jnp.where(sub == h, fh, 0.0), axis=0, keepdims=True)
    return (f0 - fk) * LOG2E


HEADS_PER_STEP = 2


def _tri_rows(s, nb):
    i = sum((s >= k * (k + 1) // 2).astype(jnp.int32) for k in range(1, nb))
    return i, s - (i * (i + 1)) // 2


def _tri_cols(s, nb):
    j = sum((s >= k * nb - (k * (k - 1)) // 2).astype(jnp.int32) for k in range(1, nb))
    return j, j + s - (j * nb - (j * (j - 1)) // 2)


def _causal_bias(blk):
    row = lax.broadcasted_iota(jnp.int32, (blk, blk), 0)
    col = lax.broadcasted_iota(jnp.int32, (blk, blk), 1)
    return jnp.where(row >= col, 0.0, NEG)


def _attn_fwd(qn, kn, vb, f_tm, f_hm, T, blk, comm=None):
    nb = T // blk
    scale = HEAD_DIM ** -0.5
    W = N_HEADS * HEAD_DIM
    G = HEADS_PER_STEP
    lanes = [slice(g * HEAD_DIM, (g + 1) * HEAD_DIM) for g in range(G)]

    def kern(q_ref, k_ref, v_ref, ft_ref, fh_ref, o_ref, lse_ref, m_scr, l_scr, acc_scr):
        hp = pl.program_id(0)
        i, j = _tri_rows(pl.program_id(1), nb)

        @pl.when(j == 0)
        def _():
            m_scr[...] = jnp.full_like(m_scr, NEG)
            l_scr[...] = jnp.zeros_like(l_scr)
            acc_scr[...] = jnp.zeros_like(acc_scr)

        def block(diagonal):
            ft, fh = ft_ref[...], fh_ref[...]
            s = [_dot(q_ref[:, sl], k_ref[:, sl], "NT") * (scale * LOG2E) + _gate_bias(ft, fh, hp * G + g)
                 for g, sl in enumerate(lanes)]
            if diagonal:
                mask = _causal_bias(blk)
                s = [sg + mask for sg in s]
            m_prev = [m_scr[g] for g in range(G)]
            m_new = [jnp.maximum(mp, jnp.max(sg, axis=1, keepdims=True)) for mp, sg in zip(m_prev, s)]
            alpha = [jnp.exp2(mp - mn) for mp, mn in zip(m_prev, m_new)]
            p = [jnp.exp2(sg - mn) for sg, mn in zip(s, m_new)]
            for g, sl in enumerate(lanes):
                l_scr[g] = alpha[g] * l_scr[g] + jnp.sum(p[g], axis=1, keepdims=True)
                acc_scr[:, sl] = alpha[g] * acc_scr[:, sl] + _dot(p[g], v_ref[:, sl], "NN")
                m_scr[g] = m_new[g]

        @pl.when(j < i)
        def _():
            block(False)

        @pl.when(j == i)
        def _():
            block(True)
            for g, sl in enumerate(lanes):
                l = l_scr[g]
                o_ref[:, sl] = acc_scr[:, sl] / l
                lse_ref[:, sl] = jnp.broadcast_to(m_scr[g] + jnp.log2(l), (blk, HEAD_DIM))

    qspec = pl.BlockSpec((blk, G * HEAD_DIM), lambda h, s: (_tri_rows(s, nb)[0], h))
    kspec = pl.BlockSpec((blk, G * HEAD_DIM), lambda h, s: (_tri_rows(s, nb)[1], h))
    return _pallas(
        kern, comm=comm, name="attn_fwd", grid=(N_HEADS // G, nb * (nb + 1) // 2),
        in_specs=[qspec, kspec, kspec,
                  pl.BlockSpec((blk, 128), lambda h, s: (_tri_rows(s, nb)[0], 0)),
                  pl.BlockSpec((N_HEADS, blk), lambda h, s: (0, _tri_rows(s, nb)[1]))],
        out_specs=[qspec, qspec],
        out_shape=[jax.ShapeDtypeStruct((T, W), F32)] * 2,
        scratch_shapes=[pltpu.VMEM((G, blk, 1), F32), pltpu.VMEM((G, blk, 1), F32),
                        pltpu.VMEM((blk, G * HEAD_DIM), F32)],
        compiler_params=_params(),
    )(qn, kn, vb, f_tm, f_hm)


def _attn_bwd(qn, kn, vb, do, lse, delta, f_tm, f_hm, T, blk, comm=None):
    nb = T // blk
    scale = HEAD_DIM ** -0.5
    W = N_HEADS * HEAD_DIM
    G = HEADS_PER_STEP
    lanes = [slice(g * HEAD_DIM, (g + 1) * HEAD_DIM) for g in range(G)]

    def kern(q_ref, k_ref, v_ref, do_ref, lse_ref, dl_ref, ft_ref, fh_ref,
             dq_ref, dfq_ref, dk_ref, dv_ref, df_ref, dq_scr, dfq_scr, dk_scr, dv_scr, df_scr):
        hp = pl.program_id(0)
        j, i = _tri_cols(pl.program_id(1), nb)

        @pl.when((j == 0) & (i == 0))
        def _():
            dq_scr[...] = jnp.zeros_like(dq_scr)
            dfq_scr[...] = jnp.zeros_like(dfq_scr)

        @pl.when(i == j)
        def _():
            dk_scr[...] = jnp.zeros_like(dk_scr)
            dv_scr[...] = jnp.zeros_like(dv_scr)
            df_scr[...] = jnp.zeros_like(df_scr)

        def block(diagonal):
            ft, fh = ft_ref[...], fh_ref[...]
            rows = pl.ds(pl.multiple_of(i * blk, blk), blk)
            q = [q_ref[:, sl] for sl in lanes]
            k = [k_ref[:, sl] for sl in lanes]
            dob = [do_ref[:, sl].astype(BF16) for sl in lanes]
            s = [_dot(q[g], k[g], "NT") * (scale * LOG2E) + _gate_bias(ft, fh, hp * G + g) for g in range(G)]
            if diagonal:
                mask = _causal_bias(blk)
                s = [sg + mask for sg in s]
            p = [jnp.exp2(s[g] - lse_ref[:, sl.start:sl.start + 1]) for g, sl in enumerate(lanes)]
            dp = [_dot(dob[g], v_ref[:, sl], "NT") for g, sl in enumerate(lanes)]
            ds = [p[g] * (dp[g] - dl_ref[:, sl.start:sl.start + 1]) for g, sl in enumerate(lanes)]
            dsb = [d.astype(BF16) for d in ds]
            for g, sl in enumerate(lanes):
                dv_scr[:, sl] += _dot(p[g], dob[g], "TN")
                dk_scr[:, sl] += _dot(dsb[g], q[g], "TN") * scale
                dq_scr[rows, sl] += _dot(dsb[g], k[g], "NN") * scale
                df_scr[g] += jnp.sum(ds[g], axis=0, keepdims=True)
                dfq_scr[g, rows, :] += jnp.sum(ds[g], axis=1, keepdims=True)

        @pl.when(i > j)
        def _():
            block(False)

        @pl.when(i == j)
        def _():
            block(True)

        @pl.when(i == nb - 1)
        def _():
            dk_ref[...] = dk_scr[...]
            dv_ref[...] = dv_scr[...]
            df_ref[...] = -df_scr[...]

        @pl.when((j == nb - 1) & (i == nb - 1))
        def _():
            dq_ref[...] = dq_scr[...]
            for g, sl in enumerate(lanes):
                dfq_ref[:, sl] = jnp.broadcast_to(dfq_scr[g], (T, HEAD_DIM))

    qspec = pl.BlockSpec((blk, G * HEAD_DIM), lambda h, s: (_tri_cols(s, nb)[1], h))
    full = pl.BlockSpec((T, G * HEAD_DIM), lambda h, s: (0, h))
    kspec = pl.BlockSpec((blk, G * HEAD_DIM), lambda h, s: (_tri_cols(s, nb)[0], h))
    return _pallas(
        kern, comm=comm, name="attn_bwd", grid=(N_HEADS // G, nb * (nb + 1) // 2),
        in_specs=[qspec, kspec, kspec, qspec, qspec, qspec,
                  pl.BlockSpec((blk, 128), lambda h, s: (_tri_cols(s, nb)[1], 0)),
                  pl.BlockSpec((N_HEADS, blk), lambda h, s: (0, _tri_cols(s, nb)[0]))],
        out_specs=[full, full, kspec, kspec, pl.BlockSpec((G, 1, blk), lambda h, s: (h, 0, _tri_cols(s, nb)[0]))],
        out_shape=[jax.ShapeDtypeStruct((T, W), F32)] * 4 + [jax.ShapeDtypeStruct((N_HEADS, 1, T), F32)],
        scratch_shapes=[pltpu.VMEM((T, G * HEAD_DIM), F32), pltpu.VMEM((G, T, 1), F32),
                        pltpu.VMEM((blk, G * HEAD_DIM), F32), pltpu.VMEM((blk, G * HEAD_DIM), F32),
                        pltpu.VMEM((G, 1, blk), F32)],
        compiler_params=_params(),
    )(qn, kn, vb, do, lse, delta, f_tm, f_hm)


def _attn_delta(o, do, T, tb):
    W = N_HEADS * HEAD_DIM

    def body(ob, dob):
        return jnp.concatenate(
            _heads(lambda a, b: jnp.broadcast_to(jnp.sum(a * b, axis=1, keepdims=True), a.shape), ob, dob), axis=1)

    return _rowwise("attn_delta", body, T, tb, [(o, W, 0), (do, W, 0)], [], [(W, F32)], [])[0]


def _window_select(s, g, shift):
    picks = []
    for k in (1, 2, 4, 8):
        s = s + shift(s, k)
        picks.append(s)
    return jnp.where(g == 0, picks[0], jnp.where(g == 1, picks[1], jnp.where(g == 2, picks[2], picks[3])))


def _group_window(g):
    return jnp.where(g == 0, POOL_WINDOWS[0], jnp.where(g == 1, POOL_WINDOWS[1],
                     jnp.where(g == 2, POOL_WINDOWS[2], POOL_WINDOWS[3])))


def _pool_fwd(proj, ucol, pw, ps, T, tb):
    C = POOL_GROUP_DIM
    n_g = len(POOL_WINDOWS)

    def kern(uc_ref, up_ref, pw_ref, ps_ref, pooled_ref, out_ref):
        g, i = pl.program_id(0), pl.program_id(1)
        uc = uc_ref[...]
        t2 = (i - 1) * tb + lax.broadcasted_iota(jnp.int32, (2 * tb, C), 0)
        u2 = jnp.where(t2 >= 0, jnp.concatenate([up_ref[...], uc], axis=0), 0.0)
        sums = _window_select(u2, g, lambda s, k: pltpu.roll(s, k, 0))[tb:, :]
        count = jnp.minimum(t2[tb:, :] + 1, _group_window(g)).astype(F32)
        pooled = sums / count - uc
        pooled_ref[...] = pooled.astype(BF16)
        out_ref[...] = _dot(pooled, pw_ref[...], "NN") * ps_ref[...]

    ospec = pl.BlockSpec((tb, C), lambda g, i: (i, g))
    return pl.pallas_call(
        kern, name="pool_fwd", grid=(n_g, T // tb),
        in_specs=[pl.BlockSpec((tb, C), lambda g, i: (i, ucol + g)),
                  pl.BlockSpec((tb, C), lambda g, i: (jnp.maximum(i - 1, 0), ucol + g)),
                  pl.BlockSpec((None, C, C), lambda g, i: (g, 0, 0)),
                  pl.BlockSpec((1, C), lambda g, i: (0, g))],
        out_specs=[ospec, ospec],
        out_shape=[jax.ShapeDtypeStruct((T, n_g * C), BF16), jax.ShapeDtypeStruct((T, n_g * C), F32)],
        compiler_params=_params(),
    )(proj, proj, pw, ps)


def _pool_bwd(dmix_in, dcol, pooled, pw, ps, T, tb):
    C = POOL_GROUP_DIM
    n_g = len(POOL_WINDOWS)
    nb = T // tb

    def kern(dc_ref, dn_ref, pooled_ref, pw_ref, ps_ref, du_ref, dpw_ref, dps_ref):
        g, i = pl.program_id(0), pl.program_id(1)
        dc = dc_ref[...]
        scale = ps_ref[...]
        t2 = i * tb + lax.broadcasted_iota(jnp.int32, (2 * tb, C), 0)
        d2 = jnp.where(t2 < T, jnp.concatenate([dc, dn_ref[...]], axis=0) * scale, 0.0)
        dpooled2 = _dot(d2, pw_ref[...], "NT")
        count = jnp.minimum(t2 + 1, _group_window(g)).astype(F32)
        sums = _window_select(dpooled2 / count, g, lambda s, k: pltpu.roll(s, 2 * tb - k, 0))
        du_ref[...] = (sums[:tb, :] - dpooled2[:tb, :]).astype(BF16)
        pooled = pooled_ref[...]
        p = _dot(pooled, pw_ref[...], "NN")
        dps = jnp.sum(dc * p, axis=0, keepdims=True)
        dpw = _dot(pooled, d2[:tb, :], "TN")

        @pl.when(i == 0)
        def _():
            dps_ref[...] = dps
            dpw_ref[...] = dpw

        @pl.when(i > 0)
        def _():
            dps_ref[...] += dps
            dpw_ref[...] += dpw

    return pl.pallas_call(
        kern, name="pool_bwd", grid=(n_g, nb),
        in_specs=[pl.BlockSpec((tb, C), lambda g, i: (i, dcol + g)),
                  pl.BlockSpec((tb, C), lambda g, i: (jnp.minimum(i + 1, nb - 1), dcol + g)),
                  pl.BlockSpec((tb, C), lambda g, i: (i, g)),
                  pl.BlockSpec((None, C, C), lambda g, i: (g, 0, 0)),
                  pl.BlockSpec((1, C), lambda g, i: (0, g))],
        out_specs=[pl.BlockSpec((tb, C), lambda g, i: (i, g)),
                   pl.BlockSpec((None, C, C), lambda g, i: (g, 0, 0)),
                   pl.BlockSpec((1, C), lambda g, i: (0, g))],
        out_shape=[jax.ShapeDtypeStruct((T, n_g * C), BF16), jax.ShapeDtypeStruct((n_g, C, C), F32),
                   jax.ShapeDtypeStruct((1, n_g * C), F32)],
        compiler_params=_params(),
    )(dmix_in, dmix_in, pooled, pw, ps)


D_QKV = 3 * N_HEADS * HEAD_DIM
D_U = len(POOL_WINDOWS) * POOL_GROUP_DIM
F_PAD = 128
D_PROJ = D_QKV + D_U + F_PAD


def _perm_w_in(w):
    pad = jnp.zeros((F_PAD - N_HEADS, w.shape[1]), w.dtype)
    return jnp.concatenate([w[:D_QKV], w[D_QKV + N_HEADS:], w[D_QKV:D_QKV + N_HEADS], pad], axis=0)


def _unperm_w_in(w):
    return jnp.concatenate([w[:D_QKV], w[D_QKV + D_U:D_QKV + D_U + N_HEADS], w[D_QKV:D_QKV + D_U]], axis=0)


def _mixer_fwd(x, norm_g, sh, sc, gate, w_in_p, b_pad, gq, gk, late_weights, ps, T, proj_comm, attn_comm):
    tb = min(256, T)
    blk = min(512, T)
    hm = _norm_mod_fwd("mix_norm_fwd", x, norm_g, sc, sh, T, tb)
    proj, got_proj = _mm("mix_proj", hm, w_in_p, "NT", F32, 512, D_PROJ // 3, 2048, comm=proj_comm)
    pw, w_out = late_weights(got_proj)
    qn, kn, vb = _qknorm_fwd(proj, gq, gk, T, tb)
    fcol = (D_QKV + D_U) // 128
    f_tm = _fgate_fwd(proj, fcol, b_pad, T)
    f_hm = f_tm[:, :N_HEADS].T
    (o, lse), got = _attn_fwd(qn, kn, vb, f_tm, f_hm, T, blk, comm=attn_comm)
    pooled, pool_o = _pool_fwd(proj, D_QKV // POOL_GROUP_DIM, pw, ps, T, tb)
    mix_in = jnp.concatenate([o.astype(BF16), pool_o.astype(BF16)], axis=1)
    mix, x_out = _mm_groups("mix_out", mix_in[None], w_out[None], "NN", 512, 512, residual=(x, gate, 1.0))
    return x_out, (x, hm, proj, qn, kn, vb, f_tm, f_hm, o, lse, pooled, mix_in, mix), pw, w_out, got


def _mixer_bwd(dx_out, branch, saved, norm_g, sc, w_in_p, b_pad, gq, gk, pw, ps, w_out, T, core, ride_sums, below):
    x, hm, proj, qn, kn, vb, f_tm, f_hm, o, lse, pooled, mix_in, mix = saved
    tb = min(256, T)
    blk = min(512, T)
    W = N_HEADS * HEAD_DIM
    D = x.shape[1]
    n_g = len(POOL_WINDOWS)
    dmix, dgate = branch
    dmix_in = _mm("mix_out_bwd", dmix, w_out, "NT", F32, 512, 2048, 2048)
    dw_out = _mm("mix_dw_out", mix_in, dmix, "TN", BF16, 512, 1024, T)
    delta = _attn_delta(o, dmix_in, T, tb)
    (dqn, dfq, dkn, dv, dfk), ride_got = _attn_bwd(qn, kn, vb, dmix_in, lse, delta, f_tm, f_hm, T, blk,
                                                   comm=_chip_comm(ride_sums))
    dq, dk, dgq, dgk = _qknorm_bwd(proj, dqn, dkn, gq, gk, T, tb)
    dF = jnp.pad(dfq[:, ::HEAD_DIM] + dfk.reshape(N_HEADS, T).T, ((0, 0), (0, F_PAD - N_HEADS)))
    fcol = (D_QKV + D_U) // 128
    dfl, dbf = _fgate_bwd(proj, fcol, b_pad, dF, T)
    du, dpw, dps = _pool_bwd(dmix_in, W // POOL_GROUP_DIM, pooled, pw, ps, T, tb)
    dproj = jnp.concatenate([dq, dk, dv.astype(BF16), du, dfl.astype(BF16)], axis=1)
    dw_in_p = _mm("mix_dw_in", dproj, hm, "TN", BF16, D_PROJ // 3, 512, T)
    pw_rows = POOL_GROUP_DIM // N_DEV
    slabs = [_unperm_w_in(dw_in_p).reshape(N_DEV, -1, D),
             jnp.transpose(dpw.astype(BF16).reshape(n_g, N_DEV, pw_rows, POOL_GROUP_DIM),
                           (1, 0, 2, 3)).reshape(N_DEV, n_g * pw_rows, POOL_GROUP_DIM),
             dw_out.reshape(N_DEV, -1, D)]
    dhm, sums = _reduce_level1(
        "mix", slabs, core, [TILE_W_IN, TILE_POOL, TILE_MIX_OUT],
        host=lambda comm: _mm("mix_proj_bwd", dproj, w_in_p, "NN", F32, 512, 512, D_PROJ, comm=comm))
    dx, df_below, dsh, dsc, dng, dgate_below = _norm_mod_bwd("mix_norm_bwd", x, dhm, dx_out, norm_g, sc, T, tb,
                                                             below=below)
    return dx, (dsh, dsc, dgate, dng), sums, dps, dgq, dgk, dbf, ride_got, (df_below, dgate_below)


def kernel(x, c, w_ada, b_ada, ffn1_norm_g, ffn1_w_in, ffn1_w_out, mix_norm_g, w_in, b_forget, q_norm_g, k_norm_g, pool_w, pool_scale, w_out, ffn2_norm_g, ffn2_w_in, ffn2_w_out, final_norm_g, loss_target, m_w_ada, m_b_ada, m_ffn1_norm_g, m_ffn1_w_in, m_ffn1_w_out, m_mix_norm_g, m_w_in, m_b_forget, m_q_norm_g, m_k_norm_g, m_pool_w, m_pool_scale, m_w_out, m_ffn2_norm_g, m_ffn2_w_in, m_ffn2_w_out, m_final_norm_g, v_w_ada, v_b_ada, v_ffn1_norm_g, v_ffn1_w_in, v_ffn1_w_out, v_mix_norm_g, v_w_in, v_b_forget, v_q_norm_g, v_k_norm_g, v_pool_w, v_pool_scale, v_w_out, v_ffn2_norm_g, v_ffn2_w_in, v_ffn2_w_out, v_final_norm_g):
    T, D = x.shape[1], x.shape[2]
    mx, my, mc = _mesh_pos()
    me = _flat(mx, my, mc)
    x0 = x[0]
    tgt = loss_target[0]
    tb = min(256, T)

    core = jnp.reshape(mc, (1,)).astype(jnp.int32)
    half = N_DEV // 2
    n_g = len(POOL_WINDOWS)
    pw_rows = POOL_GROUP_DIM // N_DEV

    def bf(w):
        return w.astype(BF16)

    n_loc = w_ada.shape[2]
    c_all = _standalone("gather_c", _gather_comm([c.reshape(8, D // 8)]))[0].reshape(N_DEV, D)
    b_loc = lax.dynamic_slice_in_dim(b_ada, me * n_loc, n_loc, axis=1)
    mod_loc = _ada_fwd(c_all, w_ada[0], b_loc, n_loc // 3)
    mod_all = _standalone("gather_mod", _gather_comm([mod_loc]))[0]
    mod = lax.dynamic_index_in_dim(mod_all, me, axis=1, keepdims=False).reshape(N_MOD, 1, D)
    sh1, sc1, g1, sh2, sc2, g2, sh3, sc3, g3 = [mod[k] for k in range(N_MOD)]
    b_pad = jnp.pad(b_forget, ((0, 0), (0, F_PAD - N_HEADS)))
    ps = pool_scale

    def shard_t(w):
        return jnp.swapaxes(w[0], 0, 1)

    wi1 = _standalone("gather_ffn1_w_in", _gather_comm([bf(shard_t(ffn1_w_in))]))[0]
    x1, sv1, wo1, (w_in_g,) = _ffn_fwd(
        "ffn1", x0, ffn1_norm_g, sh1, sc1, g1, wi1, lambda got: got[0].reshape(half, -1, D), T,
        up_comm=_gather_comm([bf(ffn1_w_out[0])], forward_at=0.7),
        down_comm=_gather_comm([bf(shard_t(w_in))], forward_at=0.8))
    w_in_p = _perm_w_in(w_in_g.reshape(-1, D))

    def late_weights(got):
        pool_g, w_out_g = got
        pw = jnp.transpose(pool_g.reshape(N_DEV, n_g, pw_rows, POOL_GROUP_DIM),
                           (1, 0, 2, 3)).reshape(n_g, POOL_GROUP_DIM, POOL_GROUP_DIM)
        return pw, w_out_g.reshape(-1, D)

    x2, svm, pw_full, w_out_full, (wi2,) = _mixer_fwd(
        x1, mix_norm_g, sh2, sc2, g2, w_in_p, b_pad, q_norm_g, k_norm_g, late_weights, ps, T,
        proj_comm=_gather_comm([bf(pool_w[0].reshape(-1, POOL_GROUP_DIM)), bf(w_out[0])], forward_at=0.6),
        attn_comm=_gather_comm([bf(shard_t(ffn2_w_in))], forward_at=0.9))
    x3, sv2, wo2, _ = _ffn_fwd("ffn2", x2, ffn2_norm_g, sh3, sc3, g3, wi2,
                               lambda got: got[0].reshape(half, -1, D), T,
                               up_comm=_gather_comm([bf(ffn2_w_out[0])], forward_at=0.7))
    dx3, df3, dgf, loss_l, dgate3 = _final_loss(x3, tgt, final_norm_g.reshape(1, D), T, tb, below=(sv2[5], g3, 0.5))
    loss = lax.psum(loss_l[0, 0], ("x", "y", "c"))

    dx2, (dsh3, dsc3, dg3, dn3), dwi2_sum, dwo2, _, branch2 = _ffn_bwd(
        "ffn2", dx3, (df3, dgate3), sv2, ffn2_norm_g, sc3, wi2, wo2, T, core, defer_dwi=True,
        below=(svm[12], g2, 1.0))
    dx1, (dsh2, dsc2, dg2, dn2), mix_sums, dps, dgq, dgk, dbf, (dwi2,), branch1 = _mixer_bwd(
        dx2, branch2, svm, mix_norm_g, sc2, w_in_p, b_pad, q_norm_g, k_norm_g, pw_full, ps, w_out_full, T, core,
        ride_sums=[dwi2_sum], below=(sv1[5], g1, 0.5))
    dx0, (dsh1, dsc1, dg1, dn1), dwi1, dwo1, (dw_in_r, dpw_r, dw_out_r), _ = _ffn_bwd(
        "ffn1", dx1, branch1, sv1, ffn1_norm_g, sc1, wi1, wo1, T, core, ride_sums=mix_sums)

    received = dict(ffn1_w_in=dwi1, ffn1_w_out=dwo1, w_in=dw_in_r, pool_w=dpw_r, w_out=dw_out_r,
                    ffn2_w_in=dwi2, ffn2_w_out=dwo2)
    moments = dict(ffn1_w_in=(m_ffn1_w_in, v_ffn1_w_in), ffn1_w_out=(m_ffn1_w_out, v_ffn1_w_out),
                   w_in=(m_w_in, v_w_in), pool_w=(m_pool_w, v_pool_w), w_out=(m_w_out, v_w_out),
                   ffn2_w_in=(m_ffn2_w_in, v_ffn2_w_in), ffn2_w_out=(m_ffn2_w_out, v_ffn2_w_out))
    weights = dict(ffn1_w_in=ffn1_w_in, ffn1_w_out=ffn1_w_out, w_in=w_in, pool_w=pool_w, w_out=w_out,
                   ffn2_w_in=ffn2_w_in, ffn2_w_out=ffn2_w_out)
    row_tiles = dict(ffn1_w_in=TILE_FFN_IN, ffn1_w_out=TILE_W_OUT, w_in=TILE_W_IN, pool_w=TILE_POOL,
                     w_out=TILE_MIX_OUT, ffn2_w_in=TILE_FFN_IN, ffn2_w_out=TILE_W_OUT)
    results = {}
    for k in received:
        shape = weights[k].shape
        two_d = received[k].shape[1:]
        mk, vk = moments[k]
        if row_tiles[k] in (TILE_FFN_IN, TILE_W_IN):
            outs = _adamw("adamw_" + k, received[k], shard_t(weights[k]), shard_t(mk), shard_t(vk), row_tiles[k][0],
                          tc=512)
            results[k] = [jnp.swapaxes(o, 0, 1)[None] for o in outs]
        else:
            outs = _adamw("adamw_" + k, received[k], weights[k].reshape(two_d), mk.reshape(two_d),
                          vk.reshape(two_d), row_tiles[k][0])
            results[k] = [o.reshape(shape) for o in outs]

    dmod = jnp.concatenate([dsh1, dsc1, dg1, dsh2, dsc2, dg2, dsh3, dsc3, dg3], axis=1)
    small_names = ["b_ada", "ffn1_norm_g", "mix_norm_g", "ffn2_norm_g", "final_norm_g", "b_forget",
                   "q_norm_g", "k_norm_g", "pool_scale"]
    small_w = dict(b_ada=b_ada, ffn1_norm_g=ffn1_norm_g, mix_norm_g=mix_norm_g, ffn2_norm_g=ffn2_norm_g,
                   final_norm_g=final_norm_g, b_forget=b_forget, q_norm_g=q_norm_g, k_norm_g=k_norm_g,
                   pool_scale=pool_scale)
    small_m = dict(b_ada=m_b_ada, ffn1_norm_g=m_ffn1_norm_g, mix_norm_g=m_mix_norm_g, ffn2_norm_g=m_ffn2_norm_g,
                   final_norm_g=m_final_norm_g, b_forget=m_b_forget, q_norm_g=m_q_norm_g, k_norm_g=m_k_norm_g,
                   pool_scale=m_pool_scale)
    small_v = dict(b_ada=v_b_ada, ffn1_norm_g=v_ffn1_norm_g, mix_norm_g=v_mix_norm_g, ffn2_norm_g=v_ffn2_norm_g,
                   final_norm_g=v_final_norm_g, b_forget=v_b_forget, q_norm_g=v_q_norm_g, k_norm_g=v_k_norm_g,
                   pool_scale=v_pool_scale)
    small_g = dict(b_ada=dmod, ffn1_norm_g=dn1, mix_norm_g=dn2, ffn2_norm_g=dn3, final_norm_g=dgf,
                   b_forget=dbf[:, :N_HEADS], q_norm_g=dgq, k_norm_g=dgk, pool_scale=dps)
    sizes = [small_w[k].size for k in small_names]
    total = sum(sizes)
    lanes = 8 * 128
    padded = -(-total // lanes) * lanes

    def pack(d):
        flat = jnp.concatenate([d[k].reshape(-1) for k in small_names])
        return jnp.pad(flat, (0, padded - total)).reshape(8, padded // 8)

    small_parts = _standalone("gather_small_grads", _gather_comm([pack(small_g)]))[0]
    s_outs = _adamw("adamw_small", small_parts, pack(small_w), pack(small_m), pack(small_v), 8)
    offs = [0]
    for s in sizes:
        offs.append(offs[-1] + s)
    for idx, k in enumerate(small_names):
        results[k] = [o.reshape(-1)[offs[idx]:offs[idx + 1]].reshape(small_w[k].shape) for o in s_outs]

    dmod_all = small_parts.reshape(N_DEV, padded)[:, :N_MOD * D]
    dmod_loc = lax.dynamic_slice_in_dim(dmod_all, me * n_loc, n_loc, axis=1)
    g_ada = _ada_bwd(c_all, dmod_loc, n_loc // 3)
    a_outs = _adamw("adamw_w_ada", g_ada[None], w_ada[0], m_w_ada[0], v_w_ada[0], 128)
    results["w_ada"] = [o.reshape(w_ada.shape) for o in a_outs]

    order = ["w_ada", "b_ada", "ffn1_norm_g", "ffn1_w_in", "ffn1_w_out", "mix_norm_g", "w_in", "b_forget",
             "q_norm_g", "k_norm_g", "pool_w", "pool_scale", "w_out", "ffn2_norm_g", "ffn2_w_in", "ffn2_w_out",
             "final_norm_g"]
    out = [loss, dx0[None]]
    for part in range(4):
        out += [results[k][part] for k in order]
    return tuple(out)
```

```python
import jax
import jax.numpy as jnp
from jax import lax
from jax.experimental import pallas as pl
from jax.experimental.pallas import tpu as pltpu

F32 = jnp.float32
BF16 = jnp.bfloat16
MESH = pl.DeviceIdType.MESH
ANY = pl.BlockSpec(memory_space=pl.ANY)

N_DEV = 8
EPS = 1e-6
HEAD_DIM = 128
N_HEADS = 8
POOL_WINDOWS = (2, 4, 8, 16)
POOL_GROUP_DIM = 256
N_MOD = 9
ADAM_LR = 0.001
ADAM_B1 = 0.9
ADAM_B2 = 0.999
ADAM_EPS = 1e-08
ADAM_WD = 0.01
ADAM_STEP = 10
NEG = -1e30
VMEM_LIMIT_V7X = 56 * 1024 * 1024


def _params():
    return pltpu.CompilerParams(vmem_limit_bytes=VMEM_LIMIT_V7X)


def _sigmoid(z):
    return 1.0 / (1.0 + jnp.exp(-z))


def _rstd(x):
    return lax.rsqrt(jnp.mean(x * x, axis=-1, keepdims=True) + EPS)


def _mesh_pos():
    return lax.axis_index("x"), lax.axis_index("y"), lax.axis_index("c")


def _flat(px, py, pc):
    return 4 * px + 2 * py + pc


class _Comm:
    def __init__(self, ins, outs, sems, phases):
        self.ins, self.outs, self.sems, self.phases = list(ins), list(outs), list(sems), list(phases)


def _pallas(kern, *, comm=None, **kw):
    if comm is None:
        return pl.pallas_call(kern, **kw)
    grid = tuple(kw["grid"])
    single = not isinstance(kw["out_shape"], (list, tuple))
    out_shape = [kw["out_shape"]] if single else list(kw["out_shape"])
    out_specs = [kw["out_specs"]] if single else list(kw["out_specs"])
    in_specs = list(kw["in_specs"])
    scratch = list(kw.get("scratch_shapes", ()))
    n_in, n_out, n_scr = len(in_specs), len(out_shape), len(scratch)
    n_ci, n_co = len(comm.ins), len(comm.outs)
    strides, n_steps = [], 1
    for g in reversed(grid):
        strides.insert(0, n_steps)
        n_steps *= g

    def wrapped(*refs):
        ins, cins = refs[:n_in], refs[n_in:n_in + n_ci]
        base = n_in + n_ci
        outs, couts = refs[base:base + n_out], refs[base + n_out:base + n_out + n_co]
        base += n_out + n_co
        scr, sems = refs[base:base + n_scr], refs[base + n_scr:]
        step = sum(pl.program_id(d) * strides[d] for d in range(len(grid)))
        for frac, fn in comm.phases:
            if frac < 1.0:
                pl.when(step == int(round(frac * (n_steps - 1))))(lambda fn=fn: fn(cins, couts, sems))
        kern(*ins, *outs, *scr)
        for frac, fn in comm.phases:
            if frac >= 1.0:
                pl.when(step == n_steps - 1)(lambda fn=fn: fn(cins, couts, sems))

    kw = dict(kw, in_specs=in_specs + [ANY] * n_ci, out_specs=out_specs + [ANY] * n_co,
              out_shape=out_shape + comm.outs, scratch_shapes=scratch + comm.sems)
    call = pl.pallas_call(wrapped, **kw)

    def run(*args):
        res = call(*args, *comm.ins)
        main = res[0] if single else list(res[:n_out])
        return main, list(res[n_out:])

    return run


def _join(first, second):
    n_i, n_o, n_s = len(first.ins), len(first.outs), len(first.sems)

    def left(fn):
        return lambda ins, outs, sems: fn(ins[:n_i], outs[:n_o], sems[:n_s])

    def right(fn):
        return lambda ins, outs, sems: fn(ins[n_i:], outs[n_o:], sems[n_s:])

    phases = [(f, left(fn)) for f, fn in first.phases] + [(f, right(fn)) for f, fn in second.phases]
    return _Comm(first.ins + second.ins, first.outs + second.outs, first.sems + second.sems, phases)


def _hosted(comm, res):
    return res if comm is not None else (res, [])


def _standalone(name, comm):
    def kern():
        pass

    return _pallas(kern, comm=comm, name=name, grid=(1,), in_specs=[], out_specs=[], out_shape=[])()[1]


def _dma_sems(*shapes):
    return [pltpu.SemaphoreType.DMA(s) for s in shapes]


def _gather_comm(arrs, forward_at=0.5):
    n = len(arrs)

    def setup(outs, sems):
        send_sems, recv_sems, _ = sems
        x, y, c = _mesh_pos()
        chips = [(1 - x, y), (x, 1 - y), (1 - x, 1 - y)]

        def copy(a, k, block, to, src=None):
            dst = outs[a].at[_flat(*block)]
            return pltpu.make_async_remote_copy(
                src_ref=dst if src is None else src, dst_ref=dst,
                send_sem=send_sems.at[a, k], recv_sem=recv_sems.at[a, k],
                device_id=to, device_id_type=MESH)

        return (x, y, c), (x, y, 1 - c), chips, copy

    def local(ins, outs, sems, a, me):
        return pltpu.make_async_copy(ins[a], outs[a].at[_flat(*me)], sems[2].at[a])

    def send_own(ins, outs, sems):
        me, sibling, chips, copy = setup(outs, sems)
        for a in range(n):
            local(ins, outs, sems, a, me).start()
            copy(a, 0, me, sibling, src=ins[a]).start()
            for j, chip in enumerate(chips):
                copy(a, 1 + j, me, (*chip, me[2]), src=ins[a]).start()

    def forward(ins, outs, sems):
        me, sibling, chips, copy = setup(outs, sems)
        for a in range(n):
            for j, chip in enumerate(chips):
                copy(a, 1 + j, (*chip, me[2]), me).wait_recv()
                copy(a, 4 + j, (*chip, me[2]), sibling).start()

    def finish(ins, outs, sems):
        me, sibling, chips, copy = setup(outs, sems)
        for a in range(n):
            copy(a, 0, sibling, me).wait_recv()
            for j, chip in enumerate(chips):
                copy(a, 4 + j, (*chip, 1 - me[2]), me).wait_recv()
        for a in range(n):
            copy(a, 0, me, sibling, src=ins[a]).wait_send()
            for j, chip in enumerate(chips):
                copy(a, 1 + j, me, (*chip, me[2]), src=ins[a]).wait_send()
                copy(a, 4 + j, (*chip, me[2]), sibling).wait_send()
            local(ins, outs, sems, a, me).wait()

    return _Comm(arrs, [jax.ShapeDtypeStruct((N_DEV,) + a.shape, a.dtype) for a in arrs],
                 _dma_sems((n, 7), (n, 7), (n,)), [(0.0, send_own), (forward_at, forward), (1.0, finish)])


CHIPS = [(0, 0), (0, 1), (1, 0), (1, 1)]


def _sibling_comm(parts):
    n = len(parts)

    def copies(ins, outs, sems):
        x, y, c = _mesh_pos()
        return [pltpu.make_async_remote_copy(
                    src_ref=ins[a].at[_flat(qx, qy, 1 - c)], dst_ref=outs[a].at[q],
                    send_sem=sems[0].at[a, q], recv_sem=sems[1].at[a, q],
                    device_id=(x, y, 1 - c), device_id_type=MESH)
                for a in range(n) for q, (qx, qy) in enumerate(CHIPS)]

    def start(ins, outs, sems):
        for cp in copies(ins, outs, sems):
            cp.start()

    def finish(ins, outs, sems):
        for cp in copies(ins, outs, sems):
            cp.wait_recv()
        for cp in copies(ins, outs, sems):
            cp.wait_send()

    return _Comm(parts, [jax.ShapeDtypeStruct((4,) + p.shape[1:], p.dtype) for p in parts],
                 _dma_sems((n, 4), (n, 4)), [(0.0, start), (1.0, finish)])


def _chip_comm(sums):
    n = len(sums)
    flips = [(1, 0), (0, 1), (1, 1)]

    def own(ins, outs, sems):
        mine = 2 * lax.axis_index("x") + lax.axis_index("y")
        return [pltpu.make_async_copy(ins[a].at[mine], outs[a].at[mine], sems[2].at[a]) for a in range(n)]

    def copies(ins, outs, sems, arriving=False):
        x, y, c = _mesh_pos()
        mine = 2 * x + y
        remote = []
        for a in range(n):
            for k, (fx, fy) in enumerate(flips):
                qx, qy = x ^ fx, y ^ fy
                q = 2 * qx + qy
                remote.append(pltpu.make_async_remote_copy(
                    src_ref=ins[a].at[q], dst_ref=outs[a].at[q if arriving else mine],
                    send_sem=sems[0].at[a, k], recv_sem=sems[1].at[a, k],
                    device_id=(qx, qy, c), device_id_type=MESH))
        return remote

    def start(ins, outs, sems):
        for cp in own(ins, outs, sems) + copies(ins, outs, sems):
            cp.start()

    def finish(ins, outs, sems):
        for cp in copies(ins, outs, sems, arriving=True):
            cp.wait_recv()
        for cp in copies(ins, outs, sems):
            cp.wait_send()
        for cp in own(ins, outs, sems):
            cp.wait()

    return _Comm(sums, [jax.ShapeDtypeStruct(s.shape, s.dtype) for s in sums],
                 _dma_sems((n, 3), (n, 3), (n,)), [(0.0, start), (1.0, finish)])


def _pair_add(name, parts, got, core, tr):
    _, R, C = parts.shape
    assert R % tr == 0

    def kern(c_ref, p_ref, g_ref, o_ref):
        o_ref[...] = (p_ref[...].astype(F32) + g_ref[...].astype(F32)).astype(o_ref.dtype)

    blk = pl.BlockSpec((None, tr, C), lambda q, i, c_ref: (q, i, 0))
    return pl.pallas_call(
        kern, name=name,
        grid_spec=pltpu.PrefetchScalarGridSpec(
            num_scalar_prefetch=1, grid=(4, R // tr),
            in_specs=[pl.BlockSpec((None, tr, C), lambda q, i, c_ref: (2 * q + c_ref[0], i, 0)), blk],
            out_specs=blk),
        out_shape=jax.ShapeDtypeStruct((4, R, C), parts.dtype), compiler_params=_params(),
    )(core, parts, got)


def _rowwise(name, body, T, tb, rows, vecs, out_rows, out_accs):
    n_in = len(rows) + len(vecs)
    n_o, n_a = len(out_rows), len(out_accs)

    def kern(*refs):
        i = pl.program_id(0)
        res = body(*[r[...] for r in refs[:n_in]])
        if not isinstance(res, (tuple, list)):
            res = (res,)
        outs = refs[n_in:]
        for k in range(n_o):
            outs[k][...] = res[k].astype(outs[k].dtype)

        def accumulate(ref, val):
            @pl.when(i == 0)
            def _():
                ref[...] = val

            @pl.when(i > 0)
            def _():
                ref[...] += val

        for k in range(n_a):
            accumulate(outs[n_o + k], res[n_o + k])

    in_specs = [pl.BlockSpec((tb, w), lambda i, cb=cb: (i, cb)) for (_, w, cb) in rows]
    in_specs += [pl.BlockSpec((1, v.shape[1]), lambda i: (0, 0)) for v in vecs]
    out_specs = [pl.BlockSpec((tb, w), lambda i: (i, 0)) for (w, _) in out_rows]
    out_specs += [pl.BlockSpec((1, w), lambda i: (0, 0)) for w in out_accs]
    out_shape = [jax.ShapeDtypeStruct((T, w), dt) for (w, dt) in out_rows]
    out_shape += [jax.ShapeDtypeStruct((1, w), F32) for w in out_accs]
    res = pl.pallas_call(
        kern, name=name, grid=(T // tb,), in_specs=in_specs, out_specs=out_specs,
        out_shape=out_shape, compiler_params=_params(),
    )(*[r[0] for r in rows], *vecs)
    return res


def _dot(a, b, mode):
    dims = {"NN": ((1,), (0,)), "NT": ((1,), (1,)), "TN": ((0,), (0,))}[mode]
    return lax.dot_general(a.astype(BF16), b.astype(BF16), (dims, ((), ())),
                           preferred_element_type=F32)


def _mm(name, a, b, mode, out_dtype, tm, tn, tk, ga=False, gb=False, gmode=None, comm=None):
    G = (a.shape[0] if ga else b.shape[0]) if gmode else 1
    a2, b2 = a.shape[-2:], b.shape[-2:]
    if mode == "NN":
        (M, K), (_, N) = a2, b2
    elif mode == "NT":
        (M, K), (N, _) = a2, b2
    else:
        (K, M), (_, N) = a2, b2
    tm, tn, tk = min(tm, M), min(tn, N), min(tk, K)
    assert M % tm == 0 and N % tn == 0 and K % tk == 0, (name, M, N, K, tm, tn, tk)
    batch = gmode == "batch"
    n_gb, n_gs = (G if batch else 1), (G if gmode == "sum" else 1)
    nk = K // tk
    n_red = n_gs * nk

    def grp(g_b, g_s):
        return g_b if batch else g_s

    if mode == "TN":
        a_blk, a_idx = (tk, tm), lambda g_b, mi, ni, g_s, ki: (ki, mi)
    else:
        a_blk, a_idx = (tm, tk), lambda g_b, mi, ni, g_s, ki: (mi, ki)
    if mode == "NT":
        b_blk, b_idx = (tn, tk), lambda g_b, mi, ni, g_s, ki: (ni, ki)
    else:
        b_blk, b_idx = (tk, tn), lambda g_b, mi, ni, g_s, ki: (ki, ni)

    def with_group(blk, idx, has_group):
        if not has_group:
            return pl.BlockSpec(blk, idx)
        return pl.BlockSpec((None,) + blk, lambda g_b, mi, ni, g_s, ki: (grp(g_b, g_s),) + idx(g_b, mi, ni, g_s, ki))

    o_blk, o_idx = (tm, tn), lambda g_b, mi, ni, g_s, ki: (mi, ni)
    o_spec = with_group(o_blk, o_idx, batch)
    o_shape = ((G,) if batch else ()) + (M, N)

    def kern(a_ref, b_ref, o_ref, *scratch):
        part = _dot(a_ref[...], b_ref[...], mode)
        if n_red == 1:
            o_ref[...] = part.astype(o_ref.dtype)
            return
        acc = scratch[0]
        step = pl.program_id(3) * nk + pl.program_id(4)

        @pl.when(step == 0)
        def _():
            acc[...] = part

        @pl.when(step > 0)
        def _():
            acc[...] += part

        @pl.when(step == n_red - 1)
        def _():
            o_ref[...] = acc[...].astype(o_ref.dtype)

    return _pallas(
        kern, comm=comm, name=name, grid=(n_gb, M // tm, N // tn, n_gs, nk),
        in_specs=[with_group(a_blk, a_idx, ga), with_group(b_blk, b_idx, gb)],
        out_specs=o_spec, out_shape=jax.ShapeDtypeStruct(o_shape, out_dtype),
        scratch_shapes=[] if n_red == 1 else [pltpu.VMEM((tm, tn), F32)],
        compiler_params=_params(),
    )(a, b)


def _mm_groups(name, a, b, mode, tm, tn, residual=None, comm=None):
    G, M, K = a.shape
    N = b.shape[2] if mode == "NN" else b.shape[1]
    tm, tn = min(tm, M), min(tn, N)
    assert M % tm == 0 and N % tn == 0

    def kern(a_ref, b_ref, *rest):
        acc = _dot(a_ref[0], b_ref[0], mode)
        for g in range(1, G):
            acc = acc + _dot(a_ref[g], b_ref[g], mode)
        if residual is None:
            rest[0][...] = acc
        else:
            x_ref, g_ref, f_ref, o_ref = rest
            f_ref[...] = acc
            o_ref[...] = x_ref[...] + (residual[2] * g_ref[...]) * acc

    b_spec = (pl.BlockSpec((G, K, tn), lambda ni, mi: (0, 0, ni)) if mode == "NN"
              else pl.BlockSpec((G, tn, K), lambda ni, mi: (0, ni, 0)))
    o_spec = pl.BlockSpec((tm, tn), lambda ni, mi: (mi, ni))
    in_specs = [pl.BlockSpec((G, tm, K), lambda ni, mi: (0, mi, 0)), b_spec]
    args = [a, b]
    out = jax.ShapeDtypeStruct((M, N), F32)
    if residual is not None:
        in_specs += [o_spec, pl.BlockSpec((1, tn), lambda ni, mi: (0, ni))]
        args += [residual[0], residual[1]]
    return _pallas(
        kern, comm=comm, name=name, grid=(N // tn, M // tm), in_specs=in_specs,
        out_specs=o_spec if residual is None else [o_spec, o_spec],
        out_shape=out if residual is None else [out, out], compiler_params=_params(),
    )(*args)


def _adamw(name, parts, w, m, v, tr, tc=None):
    G, R, C = parts.shape
    tc = C if tc is None else tc
    assert R % tr == 0 and C % tc == 0
    bc1 = 1.0 - ADAM_B1 ** ADAM_STEP
    bc2 = 1.0 - ADAM_B2 ** ADAM_STEP

    def kern(p_ref, w_ref, m_ref, v_ref, g_out, d_out, m_out, v_out):
        g = p_ref[0].astype(F32)
        for s in range(1, G):
            g = g + p_ref[s].astype(F32)
        m2 = ADAM_B1 * m_ref[...] + (1.0 - ADAM_B1) * g
        v2 = ADAM_B2 * v_ref[...] + (1.0 - ADAM_B2) * (g * g)
        m_hat = m2 / bc1
        v_hat = v2 / bc2
        g_out[...] = g
        d_out[...] = -ADAM_LR * (m_hat / (jnp.sqrt(v_hat) + ADAM_EPS) + ADAM_WD * w_ref[...])
        m_out[...] = m2
        v_out[...] = v2

    blk = pl.BlockSpec((tr, tc), lambda i, j: (i, j))
    return pl.pallas_call(
        kern, name=name, grid=(R // tr, C // tc),
        in_specs=[pl.BlockSpec((G, tr, tc), lambda i, j: (0, i, j)), blk, blk, blk],
        out_specs=[blk] * 4, out_shape=[jax.ShapeDtypeStruct((R, C), F32)] * 4,
        compiler_params=_params(),
    )(parts, w, m, v)


def _ada_fwd(c_all, w_loc, b_loc, tn):
    B, D = c_all.shape
    N = w_loc.shape[1]

    def kern(c_ref, w_ref, b_ref, o_ref):
        cc = c_ref[...]
        act = cc * _sigmoid(cc)
        o_ref[...] = _dot(act, w_ref[...], "NN") + b_ref[...]

    return pl.pallas_call(
        kern, name="ada_fwd", grid=(N // tn,),
        in_specs=[pl.BlockSpec((B, D), lambda j: (0, 0)), pl.BlockSpec((D, tn), lambda j: (0, j)),
                  pl.BlockSpec((1, tn), lambda j: (0, j))],
        out_specs=pl.BlockSpec((B, tn), lambda j: (0, j)),
        out_shape=jax.ShapeDtypeStruct((B, N), F32), compiler_params=_params(),
    )(c_all, w_loc, b_loc)


def _ada_bwd(c_all, dmod_loc, tn):
    B, D = c_all.shape
    N = dmod_loc.shape[1]

    def kern(c_ref, d_ref, o_ref):
        cc = c_ref[...]
        act = cc * _sigmoid(cc)
        o_ref[...] = _dot(act, d_ref[...], "TN")

    return pl.pallas_call(
        kern, name="ada_bwd", grid=(N // tn,),
        in_specs=[pl.BlockSpec((B, D), lambda j: (0, 0)), pl.BlockSpec((B, tn), lambda j: (0, j))],
        out_specs=pl.BlockSpec((D, tn), lambda j: (0, j)),
        out_shape=jax.ShapeDtypeStruct((D, N), F32), compiler_params=_params(),
    )(c_all, dmod_loc)


def _norm_mod_fwd(name, x, g, sc, sh, T, tb):
    D = x.shape[1]

    def body(xb, gb, scb, shb):
        n = (xb * _rstd(xb)) * gb
        return n * (1.0 + scb) + shb

    return _rowwise(name, body, T, tb, [(x, D, 0)], [g, sc, sh], [(D, BF16)], [])[0]


def _branch_bwd(dx, fb, gateb, coef):
    return (coef * gateb) * dx, jnp.sum((coef * fb) * dx, axis=0, keepdims=True)


def _norm_mod_bwd(name, x, dhm, dres, g, sc, T, tb, below=None):
    D = x.shape[1]

    def body(xb, db, rb, *rest):
        gb, scb = rest[-2:] if below is None else rest[1:3]
        r = _rstd(xb)
        xh = xb * r
        n = xh * gb
        dn = db * (1.0 + scb)
        dxh = dn * gb
        dx = rb + r * (dxh - xh * jnp.mean(dxh * xh, axis=-1, keepdims=True))
        sums = (jnp.sum(db, axis=0, keepdims=True), jnp.sum(db * n, axis=0, keepdims=True),
                jnp.sum(dn * xh, axis=0, keepdims=True))
        if below is None:
            return (dx,) + sums
        df, dgate = _branch_bwd(dx, rest[0], rest[3], below[2])
        return (dx, df) + sums + (dgate,)

    rows = [(x, D, 0), (dhm, D, 0), (dres, D, 0)] + ([] if below is None else [(below[0], D, 0)])
    vecs = [g, sc] + ([] if below is None else [below[1]])
    return _rowwise(name, body, T, tb, rows, vecs, [(D, F32)] + ([] if below is None else [(D, BF16)]),
                    [D, D, D] + ([] if below is None else [D]))


def _final_loss(x, tgt, g, T, tb, below):
    D = x.shape[1]

    def body(xb, tb_, fb, gb, gateb):
        r = _rstd(xb)
        xh = xb * r
        err = xh * gb - tb_
        loss = 0.5 * jnp.sum(jnp.mean(err * err, axis=-1, keepdims=True), axis=0, keepdims=True)
        dy = err * (1.0 / D)
        dxh = dy * gb
        dx = r * (dxh - xh * jnp.mean(dxh * xh, axis=-1, keepdims=True))
        df, dgate = _branch_bwd(dx, fb, gateb, below[2])
        return dx, df, jnp.sum(dy * xh, axis=0, keepdims=True), jnp.broadcast_to(loss, (1, 128)), dgate

    return _rowwise("final_loss", body, T, tb, [(x, D, 0), (tgt, D, 0), (below[0], D, 0)], [g, below[1]],
                    [(D, F32), (D, BF16)], [D, 128, D])


def _ffn_up(name, hm, wi, T, tm, comm=None):
    D = hm.shape[1]
    Ws = wi.shape[1]
    half = wi.shape[0] // 2

    n_sub = 2 if tm % 32 == 0 else 1
    subs = [pl.ds(r * (tm // n_sub), tm // n_sub) for r in range(n_sub)]

    def kern(h_ref, wa_ref, wb_ref, a_ref, b_ref, hid_ref):
        wa, wb = wa_ref[...], wb_ref[...]
        ab = [(_dot(h_ref[rows, :], wa, "NT"), _dot(h_ref[rows, :], wb, "NT")) for rows in subs]
        for rows, (a, b) in zip(subs, ab):
            a_ref[rows, :] = a.astype(BF16)
            b_ref[rows, :] = b.astype(BF16)
            hid_ref[rows, :] = ((a * _sigmoid(a)) * b).astype(BF16)

    o_spec = pl.BlockSpec((None, tm, Ws), lambda g, i: (g, i, 0))
    return _pallas(
        kern, comm=comm, name=name, grid=(half, T // tm),
        in_specs=[pl.BlockSpec((tm, D), lambda g, i: (i, 0)),
                  pl.BlockSpec((None, Ws, D), lambda g, i: (g, 0, 0)),
                  pl.BlockSpec((None, Ws, D), lambda g, i: (g + half, 0, 0))],
        out_specs=[o_spec] * 3,
        out_shape=[jax.ShapeDtypeStruct((half, T, Ws), BF16)] * 3,
        compiler_params=_params(),
    )(hm, wi, wi)


def _ffn_down_bwd(name, df, wo, a, b, T, tm, comm=None):
    D = df.shape[1]
    half, _, Ws = a.shape

    n_sub = 2 if tm % 32 == 0 else 1
    subs = [pl.ds(r * (tm // n_sub), tm // n_sub) for r in range(n_sub)]

    def kern(df_ref, wo_ref, a_ref, b_ref, dp_ref):
        wo_blk = wo_ref[...]
        dhid = [_dot(df_ref[rows, :], wo_blk, "NT") for rows in subs]
        for rows, dh in zip(subs, dhid):
            av = a_ref[rows, :].astype(F32)
            s = _sigmoid(av)
            silu = av * s
            dp_ref[0, rows, :] = (dh * b_ref[rows, :].astype(F32) * (s + silu * (1.0 - s))).astype(BF16)
            dp_ref[1, rows, :] = (dh * silu).astype(BF16)

    act = pl.BlockSpec((None, tm, Ws), lambda g, i: (g, i, 0))
    return _pallas(
        kern, comm=comm, name=name, grid=(half, T // tm),
        in_specs=[pl.BlockSpec((tm, D), lambda g, i: (i, 0)),
                  pl.BlockSpec((None, Ws, D), lambda g, i: (g, 0, 0)), act, act],
        out_specs=pl.BlockSpec((2, None, tm, Ws), lambda g, i: (0, g, i, 0)),
        out_shape=jax.ShapeDtypeStruct((2, half, T, Ws), BF16),
        compiler_params=_params(),
    )(df, wo, a, b)


def _ffn_fwd(tag, x, norm_g, sh, sc, gate, wi, wo_of, T, up_comm=None, down_comm=None):
    tb = min(256, T)
    hm = _norm_mod_fwd(tag + "_norm_fwd", x, norm_g, sc, sh, T, tb)
    (a, b, hid), got_up = _hosted(up_comm, _ffn_up(tag + "_up", hm, wi, T, min(512, T), comm=up_comm))
    wo = wo_of(got_up)
    (f, x_out), got_down = _hosted(down_comm, _mm_groups(tag + "_down", hid, wo, "NN", 512, 512,
                                                         residual=(x, gate, 0.5), comm=down_comm))
    return x_out, (x, hm, a, b, hid, f), wo, got_down


TILE_W_IN = (513, 513)
TILE_FFN_IN = (688, 688)
TILE_W_OUT = (16, 688)
TILE_MIX_OUT = (64, 256)
TILE_POOL = (128, 128)


def _reduce_level1(tag, parts, core, tiles, host=None):
    comm = _sibling_comm(parts)
    if host is None:
        res, got = None, _standalone(tag + "_sibling", comm)
    else:
        res, got = host(comm)
    sums = [_pair_add("%s_pair_add%d" % (tag, k), p, g, core, min(t[1], p.shape[1]))
            for k, (p, g, t) in enumerate(zip(parts, got, tiles))]
    return res, sums


def _ffn_bwd(tag, dx_out, branch, saved, norm_g, sc, wi, wo, T, core, ride_sums=None, defer_dwi=False,
             below=None):
    x, hm, a, b, hid, f = saved
    tb = min(256, T)
    D = x.shape[1]
    df, dgate = branch
    dwo = _mm(tag + "_dwo", hid, df, "TN", BF16, 2048, 512, T, ga=True, gmode="batch").reshape(N_DEV, -1, D)
    n_ride = 0 if ride_sums is None else len(ride_sums)

    def down_bwd_call(comm):
        if n_ride:
            comm = _join(comm, _chip_comm(ride_sums))
        res, got = _ffn_down_bwd(tag + "_down_bwd", df, wo, a, b, T, min(512, T), comm=comm)
        return (res, got[len(got) - n_ride:]), got[:len(got) - n_ride]

    (dproj, ride_got), (dwo_sum,) = _reduce_level1(tag + "_dwo", [dwo], core, [TILE_W_OUT], host=down_bwd_call)
    dproj = dproj.reshape((2 * dproj.shape[1],) + dproj.shape[2:])
    dwi, (dwo_got,) = _mm(tag + "_dwi", dproj, hm, "TN", BF16, 2048, 512, T, ga=True, gmode="batch",
                          comm=_chip_comm([dwo_sum]))

    def dhm_call(comm):
        return _mm_groups(tag + "_dhm", dproj, wi, "NN", 512, 512, comm=comm)

    if defer_dwi:
        dhm, (dwi_out,) = _reduce_level1(tag + "_dwi", [dwi], core, [TILE_FFN_IN], host=dhm_call)
    else:
        _, (dwi_sum,) = _reduce_level1(tag + "_dwi", [dwi], core, [TILE_FFN_IN])
        dhm, (dwi_out,) = dhm_call(_chip_comm([dwi_sum]))
    res = _norm_mod_bwd(tag + "_norm_bwd", x, dhm, dx_out, norm_g, sc, T, tb, below=below)
    dx, (dsh, dsc, dng) = res[0], res[-3:] if below is None else res[2:5]
    return dx, (dsh, dsc, dgate, dng), dwi_out, dwo_got, ride_got, None if below is None else (res[1], res[5])


def _heads(fn, *arrs):
    outs = [fn(*[a[:, h * HEAD_DIM:(h + 1) * HEAD_DIM] for a in arrs]) for h in range(N_HEADS)]
    return outs


def _qknorm_fwd(proj, gq, gk, T, tb):
    W = N_HEADS * HEAD_DIM

    def body(q, k, v, gqb, gkb):
        qn = jnp.concatenate(_heads(lambda t: (t * _rstd(t)) * gqb, q), axis=1)
        kn = jnp.concatenate(_heads(lambda t: (t * _rstd(t)) * gkb, k), axis=1)
        return qn, kn, v

    return _rowwise("qknorm_fwd", body, T, tb, [(proj, W, 0), (proj, W, 1), (proj, W, 2)], [gq, gk],
                    [(W, BF16)] * 3, [])


def _qknorm_bwd(proj, dqn, dkn, gq, gk, T, tb):
    W = N_HEADS * HEAD_DIM

    def one(t, dt, g):
        r = _rstd(t)
        th = t * r
        dth = dt * g
        d = r * (dth - th * jnp.mean(dth * th, axis=-1, keepdims=True))
        return d, jnp.sum(dt * th, axis=0, keepdims=True)

    def body(q, k, dq, dk, gqb, gkb):
        rq = _heads(lambda t, dt: one(t, dt, gqb), q, dq)
        rk = _heads(lambda t, dt: one(t, dt, gkb), k, dk)
        return (jnp.concatenate([r[0] for r in rq], axis=1), jnp.concatenate([r[0] for r in rk], axis=1),
                sum(r[1] for r in rq), sum(r[1] for r in rk))

    return _rowwise("qknorm_bwd", body, T, tb, [(proj, W, 0), (proj, W, 1), (dqn, W, 0), (dkn, W, 0)],
                    [gq, gk], [(W, BF16)] * 2, [HEAD_DIM, HEAD_DIM])


def _log_sigmoid(z):
    return jnp.minimum(z, 0.0) - jnp.log(1.0 + jnp.exp(-jnp.abs(z)))


def _fgate_fwd(proj, fcol, b_pad, T):
    nblk = T // 128

    def kern(f_ref, b_ref, o_ref):
        r = lax.broadcasted_iota(jnp.int32, (128, 128), 0)
        c = lax.broadcasted_iota(jnp.int32, (128, 128), 1)
        tri = (r >= c).astype(F32)
        carry = jnp.zeros((1, 128), F32)
        for k in range(nblk):
            rows = pl.ds(k * 128, 128)
            lf = _log_sigmoid(f_ref[rows, :] + b_ref[...])
            o_ref[rows, :] = jnp.dot(tri, lf, precision=lax.Precision.HIGHEST, preferred_element_type=F32) + carry
            carry = carry + jnp.sum(lf, axis=0, keepdims=True)

    return pl.pallas_call(
        kern, name="fgate_fwd", grid=(1,),
        in_specs=[pl.BlockSpec((T, 128), lambda i: (0, fcol)), pl.BlockSpec((1, 128), lambda i: (0, 0))],
        out_specs=pl.BlockSpec((T, 128), lambda i: (0, 0)),
        out_shape=jax.ShapeDtypeStruct((T, 128), F32), compiler_params=_params(),
    )(proj, b_pad)


def _fgate_bwd(proj, fcol, b_pad, dF, T):
    nblk = T // 128

    def kern(f_ref, b_ref, d_ref, o_ref, db_ref):
        r = lax.broadcasted_iota(jnp.int32, (128, 128), 0)
        c = lax.broadcasted_iota(jnp.int32, (128, 128), 1)
        tri = (c >= r).astype(F32)
        carry = jnp.zeros((1, 128), F32)
        db = jnp.zeros((1, 128), F32)
        for k in reversed(range(nblk)):
            rows = pl.ds(k * 128, 128)
            dblk = d_ref[rows, :]
            rc = jnp.dot(tri, dblk, precision=lax.Precision.HIGHEST, preferred_element_type=F32) + carry
            carry = carry + jnp.sum(dblk, axis=0, keepdims=True)
            z = f_ref[rows, :] + b_ref[...]
            dz = rc * (1.0 / (1.0 + jnp.exp(z)))
            o_ref[rows, :] = dz
            db = db + jnp.sum(dz, axis=0, keepdims=True)
        db_ref[...] = db

    return pl.pallas_call(
        kern, name="fgate_bwd", grid=(1,),
        in_specs=[pl.BlockSpec((T, 128), lambda i: (0, fcol)), pl.BlockSpec((1, 128), lambda i: (0, 0)),
                  pl.BlockSpec((T, 128), lambda i: (0, 0))],
        out_specs=[pl.BlockSpec((T, 128), lambda i: (0, 0)), pl.BlockSpec((1, 128), lambda i: (0, 0))],
        out_shape=[jax.ShapeDtypeStruct((T, 128), F32), jax.ShapeDtypeStruct((1, 128), F32)],
        compiler_params=_params(),
    )(proj, b_pad, dF)


LOG2E = 1.4426950408889634


def _gate_bias(ft, fh, h):
    lane = lax.broadcasted_iota(jnp.int32, ft.shape, 1)
    fq = jnp.sum(jnp.where(lane == h, ft, 0.0), axis=1, keepdims=True)
    f0 = jnp.max(fq, axis=0, keepdims=True)
    sub = lax.broadcasted_iota(jnp.int32, fh.shape, 0)
    fk = jnp.sum(jnp.where(sub == h, fh, 0.0), axis=0, keepdims=True)
    return (f0 - fk) * LOG2E


HEADS_PER_STEP = 2


def _tri_rows(s, nb):
    i = sum((s >= k * (k + 1) // 2).astype(jnp.int32) for k in range(1, nb))
    return i, s - (i * (i + 1)) // 2


def _tri_cols(s, nb):
    j = sum((s >= k * nb - (k * (k - 1)) // 2).astype(jnp.int32) for k in range(1, nb))
    return j, j + s - (j * nb - (j * (j - 1)) // 2)


def _causal_bias(blk):
    row = lax.broadcasted_iota(jnp.int32, (blk, blk), 0)
    col = lax.broadcasted_iota(jnp.int32, (blk, blk), 1)
    return jnp.where(row >= col, 0.0, NEG)


def _attn_fwd(qn, kn, vb, f_tm, f_hm, T, blk, comm=None):
    nb = T // blk
    scale = HEAD_DIM ** -0.5
    W = N_HEADS * HEAD_DIM
    G = HEADS_PER_STEP
    lanes = [slice(g * HEAD_DIM, (g + 1) * HEAD_DIM) for g in range(G)]

    def kern(q_ref, k_ref, v_ref, ft_ref, fh_ref, o_ref, lse_ref, m_scr, l_scr, acc_scr):
        hp = pl.program_id(0)
        i, j = _tri_rows(pl.program_id(1), nb)

        @pl.when(j == 0)
        def _():
            m_scr[...] = jnp.full_like(m_scr, NEG)
            l_scr[...] = jnp.zeros_like(l_scr)
            acc_scr[...] = jnp.zeros_like(acc_scr)

        def block(diagonal):
            ft, fh = ft_ref[...], fh_ref[...]
            s = [_dot(q_ref[:, sl], k_ref[:, sl], "NT") * (scale * LOG2E) + _gate_bias(ft, fh, hp * G + g)
                 for g, sl in enumerate(lanes)]
            if diagonal:
                mask = _causal_bias(blk)
                s = [sg + mask for sg in s]
            m_prev = [m_scr[g] for g in range(G)]
            m_new = [jnp.maximum(mp, jnp.max(sg, axis=1, keepdims=True)) for mp, sg in zip(m_prev, s)]
            alpha = [jnp.exp2(mp - mn) for mp, mn in zip(m_prev, m_new)]
            p = [jnp.exp2(sg - mn) for sg, mn in zip(s, m_new)]
            for g, sl in enumerate(lanes):
                l_scr[g] = alpha[g] * l_scr[g] + jnp.sum(p[g], axis=1, keepdims=True)
                acc_scr[:, sl] = alpha[g] * acc_scr[:, sl] + _dot(p[g], v_ref[:, sl], "NN")
                m_scr[g] = m_new[g]

        @pl.when(j < i)
        def _():
            block(False)

        @pl.when(j == i)
        def _():
            block(True)
            for g, sl in enumerate(lanes):
                l = l_scr[g]
                o_ref[:, sl] = acc_scr[:, sl] / l
                lse_ref[:, sl] = jnp.broadcast_to(m_scr[g] + jnp.log2(l), (blk, HEAD_DIM))

    qspec = pl.BlockSpec((blk, G * HEAD_DIM), lambda h, s: (_tri_rows(s, nb)[0], h))
    kspec = pl.BlockSpec((blk, G * HEAD_DIM), lambda h, s: (_tri_rows(s, nb)[1], h))
    return _pallas(
        kern, comm=comm, name="attn_fwd", grid=(N_HEADS // G, nb * (nb + 1) // 2),
        in_specs=[qspec, kspec, kspec,
                  pl.BlockSpec((blk, 128), lambda h, s: (_tri_rows(s, nb)[0], 0)),
                  pl.BlockSpec((N_HEADS, blk), lambda h, s: (0, _tri_rows(s, nb)[1]))],
        out_specs=[qspec, qspec],
        out_shape=[jax.ShapeDtypeStruct((T, W), F32)] * 2,
        scratch_shapes=[pltpu.VMEM((G, blk, 1), F32), pltpu.VMEM((G, blk, 1), F32),
                        pltpu.VMEM((blk, G * HEAD_DIM), F32)],
        compiler_params=_params(),
    )(qn, kn, vb, f_tm, f_hm)


def _attn_bwd(qn, kn, vb, do, lse, delta, f_tm, f_hm, T, blk, comm=None):
    nb = T // blk
    scale = HEAD_DIM ** -0.5
    W = N_HEADS * HEAD_DIM
    G = HEADS_PER_STEP
    lanes = [slice(g * HEAD_DIM, (g + 1) * HEAD_DIM) for g in range(G)]

    def kern(q_ref, k_ref, v_ref, do_ref, lse_ref, dl_ref, ft_ref, fh_ref,
             dq_ref, dfq_ref, dk_ref, dv_ref, df_ref, dq_scr, dfq_scr, dk_scr, dv_scr, df_scr):
        hp = pl.program_id(0)
        j, i = _tri_cols(pl.program_id(1), nb)

        @pl.when((j == 0) & (i == 0))
        def _():
            dq_scr[...] = jnp.zeros_like(dq_scr)
            dfq_scr[...] = jnp.zeros_like(dfq_scr)

        @pl.when(i == j)
        def _():
            dk_scr[...] = jnp.zeros_like(dk_scr)
            dv_scr[...] = jnp.zeros_like(dv_scr)
            df_scr[...] = jnp.zeros_like(df_scr)

        def block(diagonal):
            ft, fh = ft_ref[...], fh_ref[...]
            rows = pl.ds(pl.multiple_of(i * blk, blk), blk)
            q = [q_ref[:, sl] for sl in lanes]
            k = [k_ref[:, sl] for sl in lanes]
            dob = [do_ref[:, sl].astype(BF16) for sl in lanes]
            s = [_dot(q[g], k[g], "NT") * (scale * LOG2E) + _gate_bias(ft, fh, hp * G + g) for g in range(G)]
            if diagonal:
                mask = _causal_bias(blk)
                s = [sg + mask for sg in s]
            p = [jnp.exp2(s[g] - lse_ref[:, sl.start:sl.start + 1]) for g, sl in enumerate(lanes)]
            dp = [_dot(dob[g], v_ref[:, sl], "NT") for g, sl in enumerate(lanes)]
            ds = [p[g] * (dp[g] - dl_ref[:, sl.start:sl.start + 1]) for g, sl in enumerate(lanes)]
            dsb = [d.astype(BF16) for d in ds]
            for g, sl in enumerate(lanes):
                dv_scr[:, sl] += _dot(p[g], dob[g], "TN")
                dk_scr[:, sl] += _dot(dsb[g], q[g], "TN") * scale
                dq_scr[rows, sl] += _dot(dsb[g], k[g], "NN") * scale
                df_scr[g] += jnp.sum(ds[g], axis=0, keepdims=True)
                dfq_scr[g, rows, :] += jnp.sum(ds[g], axis=1, keepdims=True)

        @pl.when(i > j)
        def _():
            block(False)

        @pl.when(i == j)
        def _():
            block(True)

        @pl.when(i == nb - 1)
        def _():
            dk_ref[...] = dk_scr[...]
            dv_ref[...] = dv_scr[...]
            df_ref[...] = -df_scr[...]

        @pl.when((j == nb - 1) & (i == nb - 1))
        def _():
            dq_ref[...] = dq_scr[...]
            for g, sl in enumerate(lanes):
                dfq_ref[:, sl] = jnp.broadcast_to(dfq_scr[g], (T, HEAD_DIM))

    qspec = pl.BlockSpec((blk, G * HEAD_DIM), lambda h, s: (_tri_cols(s, nb)[1], h))
    full = pl.BlockSpec((T, G * HEAD_DIM), lambda h, s: (0, h))
    kspec = pl.BlockSpec((blk, G * HEAD_DIM), lambda h, s: (_tri_cols(s, nb)[0], h))
    return _pallas(
        kern, comm=comm, name="attn_bwd", grid=(N_HEADS // G, nb * (nb + 1) // 2),
        in_specs=[qspec, kspec, kspec, qspec, qspec, qspec,
                  pl.BlockSpec((blk, 128), lambda h, s: (_tri_cols(s, nb)[1], 0)),
                  pl.BlockSpec((N_HEADS, blk), lambda h, s: (0, _tri_cols(s, nb)[0]))],
        out_specs=[full, full, kspec, kspec, pl.BlockSpec((G, 1, blk), lambda h, s: (h, 0, _tri_cols(s, nb)[0]))],
        out_shape=[jax.ShapeDtypeStruct((T, W), F32)] * 4 + [jax.ShapeDtypeStruct((N_HEADS, 1, T), F32)],
        scratch_shapes=[pltpu.VMEM((T, G * HEAD_DIM), F32), pltpu.VMEM((G, T, 1), F32),
                        pltpu.VMEM((blk, G * HEAD_DIM), F32), pltpu.VMEM((blk, G * HEAD_DIM), F32),
                        pltpu.VMEM((G, 1, blk), F32)],
        compiler_params=_params(),
    )(qn, kn, vb, do, lse, delta, f_tm, f_hm)


def _attn_delta(o, do, T, tb):
    W = N_HEADS * HEAD_DIM

    def body(ob, dob):
        return jnp.concatenate(
            _heads(lambda a, b: jnp.broadcast_to(jnp.sum(a * b, axis=1, keepdims=True), a.shape), ob, dob), axis=1)

    return _rowwise("attn_delta", body, T, tb, [(o, W, 0), (do, W, 0)], [], [(W, F32)], [])[0]


def _window_select(s, g, shift):
    picks = []
    for k in (1, 2, 4, 8):
        s = s + shift(s, k)
        picks.append(s)
    return jnp.where(g == 0, picks[0], jnp.where(g == 1, picks[1], jnp.where(g == 2, picks[2], picks[3])))


def _group_window(g):
    return jnp.where(g == 0, POOL_WINDOWS[0], jnp.where(g == 1, POOL_WINDOWS[1],
                     jnp.where(g == 2, POOL_WINDOWS[2], POOL_WINDOWS[3])))


def _pool_fwd(proj, ucol, pw, ps, T, tb):
    C = POOL_GROUP_DIM
    n_g = len(POOL_WINDOWS)

    def kern(uc_ref, up_ref, pw_ref, ps_ref, pooled_ref, out_ref):
        g, i = pl.program_id(0), pl.program_id(1)
        uc = uc_ref[...]
        t2 = (i - 1) * tb + lax.broadcasted_iota(jnp.int32, (2 * tb, C), 0)
        u2 = jnp.where(t2 >= 0, jnp.concatenate([up_ref[...], uc], axis=0), 0.0)
        sums = _window_select(u2, g, lambda s, k: pltpu.roll(s, k, 0))[tb:, :]
        count = jnp.minimum(t2[tb:, :] + 1, _group_window(g)).astype(F32)
        pooled = sums / count - uc
        pooled_ref[...] = pooled.astype(BF16)
        out_ref[...] = _dot(pooled, pw_ref[...], "NN") * ps_ref[...]

    ospec = pl.BlockSpec((tb, C), lambda g, i: (i, g))
    return pl.pallas_call(
        kern, name="pool_fwd", grid=(n_g, T // tb),
        in_specs=[pl.BlockSpec((tb, C), lambda g, i: (i, ucol + g)),
                  pl.BlockSpec((tb, C), lambda g, i: (jnp.maximum(i - 1, 0), ucol + g)),
                  pl.BlockSpec((None, C, C), lambda g, i: (g, 0, 0)),
                  pl.BlockSpec((1, C), lambda g, i: (0, g))],
        out_specs=[ospec, ospec],
        out_shape=[jax.ShapeDtypeStruct((T, n_g * C), BF16), jax.ShapeDtypeStruct((T, n_g * C), F32)],
        compiler_params=_params(),
    )(proj, proj, pw, ps)


def _pool_bwd(dmix_in, dcol, pooled, pw, ps, T, tb):
    C = POOL_GROUP_DIM
    n_g = len(POOL_WINDOWS)
    nb = T // tb

    def kern(dc_ref, dn_ref, pooled_ref, pw_ref, ps_ref, du_ref, dpw_ref, dps_ref):
        g, i = pl.program_id(0), pl.program_id(1)
        dc = dc_ref[...]
        scale = ps_ref[...]
        t2 = i * tb + lax.broadcasted_iota(jnp.int32, (2 * tb, C), 0)
        d2 = jnp.where(t2 < T, jnp.concatenate([dc, dn_ref[...]], axis=0) * scale, 0.0)
        dpooled2 = _dot(d2, pw_ref[...], "NT")
        count = jnp.minimum(t2 + 1, _group_window(g)).astype(F32)
        sums = _window_select(dpooled2 / count, g, lambda s, k: pltpu.roll(s, 2 * tb - k, 0))
        du_ref[...] = (sums[:tb, :] - dpooled2[:tb, :]).astype(BF16)
        pooled = pooled_ref[...]
        p = _dot(pooled, pw_ref[...], "NN")
        dps = jnp.sum(dc * p, axis=0, keepdims=True)
        dpw = _dot(pooled, d2[:tb, :], "TN")

        @pl.when(i == 0)
        def _():
            dps_ref[...] = dps
            dpw_ref[...] = dpw

        @pl.when(i > 0)
        def _():
            dps_ref[...] += dps
            dpw_ref[...] += dpw

    return pl.pallas_call(
        kern, name="pool_bwd", grid=(n_g, nb),
        in_specs=[pl.BlockSpec((tb, C), lambda g, i: (i, dcol + g)),
                  pl.BlockSpec((tb, C), lambda g, i: (jnp.minimum(i + 1, nb - 1), dcol + g)),
                  pl.BlockSpec((tb, C), lambda g, i: (i, g)),
                  pl.BlockSpec((None, C, C), lambda g, i: (g, 0, 0)),
                  pl.BlockSpec((1, C), lambda g, i: (0, g))],
        out_specs=[pl.BlockSpec((tb, C), lambda g, i: (i, g)),
                   pl.BlockSpec((None, C, C), lambda g, i: (g, 0, 0)),
                   pl.BlockSpec((1, C), lambda g, i: (0, g))],
        out_shape=[jax.ShapeDtypeStruct((T, n_g * C), BF16), jax.ShapeDtypeStruct((n_g, C, C), F32),
                   jax.ShapeDtypeStruct((1, n_g * C), F32)],
        compiler_params=_params(),
    )(dmix_in, dmix_in, pooled, pw, ps)


D_QKV = 3 * N_HEADS * HEAD_DIM
D_U = len(POOL_WINDOWS) * POOL_GROUP_DIM
F_PAD = 128
D_PROJ = D_QKV + D_U + F_PAD


def _perm_w_in(w):
    pad = jnp.zeros((F_PAD - N_HEADS, w.shape[1]), w.dtype)
    return jnp.concatenate([w[:D_QKV], w[D_QKV + N_HEADS:], w[D_QKV:D_QKV + N_HEADS], pad], axis=0)


def _unperm_w_in(w):
    return jnp.concatenate([w[:D_QKV], w[D_QKV + D_U:D_QKV + D_U + N_HEADS], w[D_QKV:D_QKV + D_U]], axis=0)


def _mixer_fwd(x, norm_g, sh, sc, gate, w_in_p, b_pad, gq, gk, late_weights, ps, T, proj_comm, attn_comm):
    tb = min(256, T)
    blk = min(512, T)
    hm = _norm_mod_fwd("mix_norm_fwd", x, norm_g, sc, sh, T, tb)
    proj, got_proj = _mm("mix_proj", hm, w_in_p, "NT", F32, 512, D_PROJ // 3, 2048, comm=proj_comm)
    pw, w_out = late_weights(got_proj)
    qn, kn, vb = _qknorm_fwd(proj, gq, gk, T, tb)
    fcol = (D_QKV + D_U) // 128
    f_tm = _fgate_fwd(proj, fcol, b_pad, T)
    f_hm = f_tm[:, :N_HEADS].T
    (o, lse), got = _attn_fwd(qn, kn, vb, f_tm, f_hm, T, blk, comm=attn_comm)
    pooled, pool_o = _pool_fwd(proj, D_QKV // POOL_GROUP_DIM, pw, ps, T, tb)
    mix_in = jnp.concatenate([o.astype(BF16), pool_o.astype(BF16)], axis=1)
    mix, x_out = _mm_groups("mix_out", mix_in[None], w_out[None], "NN", 512, 512, residual=(x, gate, 1.0))
    return x_out, (x, hm, proj, qn, kn, vb, f_tm, f_hm, o, lse, pooled, mix_in, mix), pw, w_out, got


def _mixer_bwd(dx_out, branch, saved, norm_g, sc, w_in_p, b_pad, gq, gk, pw, ps, w_out, T, core, ride_sums, below):
    x, hm, proj, qn, kn, vb, f_tm, f_hm, o, lse, pooled, mix_in, mix = saved
    tb = min(256, T)
    blk = min(512, T)
    W = N_HEADS * HEAD_DIM
    D = x.shape[1]
    n_g = len(POOL_WINDOWS)
    dmix, dgate = branch
    dmix_in = _mm("mix_out_bwd", dmix, w_out, "NT", F32, 512, 2048, 2048)
    dw_out = _mm("mix_dw_out", mix_in, dmix, "TN", BF16, 512, 1024, T)
    delta = _attn_delta(o, dmix_in, T, tb)
    (dqn, dfq, dkn, dv, dfk), ride_got = _attn_bwd(qn, kn, vb, dmix_in, lse, delta, f_tm, f_hm, T, blk,
                                                   comm=_chip_comm(ride_sums))
    dq, dk, dgq, dgk = _qknorm_bwd(proj, dqn, dkn, gq, gk, T, tb)
    dF = jnp.pad(dfq[:, ::HEAD_DIM] + dfk.reshape(N_HEADS, T).T, ((0, 0), (0, F_PAD - N_HEADS)))
    fcol = (D_QKV + D_U) // 128
    dfl, dbf = _fgate_bwd(proj, fcol, b_pad, dF, T)
    du, dpw, dps = _pool_bwd(dmix_in, W // POOL_GROUP_DIM, pooled, pw, ps, T, tb)
    dproj = jnp.concatenate([dq, dk, dv.astype(BF16), du, dfl.astype(BF16)], axis=1)
    dw_in_p = _mm("mix_dw_in", dproj, hm, "TN", BF16, D_PROJ // 3, 512, T)
    pw_rows = POOL_GROUP_DIM // N_DEV
    slabs = [_unperm_w_in(dw_in_p).reshape(N_DEV, -1, D),
             jnp.transpose(dpw.astype(BF16).reshape(n_g, N_DEV, pw_rows, POOL_GROUP_DIM),
                           (1, 0, 2, 3)).reshape(N_DEV, n_g * pw_rows, POOL_GROUP_DIM),
             dw_out.reshape(N_DEV, -1, D)]
    dhm, sums = _reduce_level1(
        "mix", slabs, core, [TILE_W_IN, TILE_POOL, TILE_MIX_OUT],
        host=lambda comm: _mm("mix_proj_bwd", dproj, w_in_p, "NN", F32, 512, 512, D_PROJ, comm=comm))
    dx, df_below, dsh, dsc, dng, dgate_below = _norm_mod_bwd("mix_norm_bwd", x, dhm, dx_out, norm_g, sc, T, tb,
                                                             below=below)
    return dx, (dsh, dsc, dgate, dng), sums, dps, dgq, dgk, dbf, ride_got, (df_below, dgate_below)


def kernel(x, c, w_ada, b_ada, ffn1_norm_g, ffn1_w_in, ffn1_w_out, mix_norm_g, w_in, b_forget, q_norm_g, k_norm_g, pool_w, pool_scale, w_out, ffn2_norm_g, ffn2_w_in, ffn2_w_out, final_norm_g, loss_target, m_w_ada, m_b_ada, m_ffn1_norm_g, m_ffn1_w_in, m_ffn1_w_out, m_mix_norm_g, m_w_in, m_b_forget, m_q_norm_g, m_k_norm_g, m_pool_w, m_pool_scale, m_w_out, m_ffn2_norm_g, m_ffn2_w_in, m_ffn2_w_out, m_final_norm_g, v_w_ada, v_b_ada, v_ffn1_norm_g, v_ffn1_w_in, v_ffn1_w_out, v_mix_norm_g, v_w_in, v_b_forget, v_q_norm_g, v_k_norm_g, v_pool_w, v_pool_scale, v_w_out, v_ffn2_norm_g, v_ffn2_w_in, v_ffn2_w_out, v_final_norm_g):
    T, D = x.shape[1], x.shape[2]
    mx, my, mc = _mesh_pos()
    me = _flat(mx, my, mc)
    x0 = x[0]
    tgt = loss_target[0]
    tb = min(256, T)

    core = jnp.reshape(mc, (1,)).astype(jnp.int32)
    half = N_DEV // 2
    n_g = len(POOL_WINDOWS)
    pw_rows = POOL_GROUP_DIM // N_DEV

    def bf(w):
        return w.astype(BF16)

    n_loc = w_ada.shape[2]
    c_all = _standalone("gather_c", _gather_comm([c.reshape(8, D // 8)]))[0].reshape(N_DEV, D)
    b_loc = lax.dynamic_slice_in_dim(b_ada, me * n_loc, n_loc, axis=1)
    mod_loc = _ada_fwd(c_all, w_ada[0], b_loc, n_loc // 3)
    mod_all = _standalone("gather_mod", _gather_comm([mod_loc]))[0]
    mod = lax.dynamic_index_in_dim(mod_all, me, axis=1, keepdims=False).reshape(N_MOD, 1, D)
    sh1, sc1, g1, sh2, sc2, g2, sh3, sc3, g3 = [mod[k] for k in range(N_MOD)]
    b_pad = jnp.pad(b_forget, ((0, 0), (0, F_PAD - N_HEADS)))
    ps = pool_scale

    def shard_t(w):
        return jnp.swapaxes(w[0], 0, 1)

    wi1 = _standalone("gather_ffn1_w_in", _gather_comm([bf(shard_t(ffn1_w_in))]))[0]
    x1, sv1, wo1, (w_in_g,) = _ffn_fwd(
        "ffn1", x0, ffn1_norm_g, sh1, sc1, g1, wi1, lambda got: got[0].reshape(half, -1, D), T,
        up_comm=_gather_comm([bf(ffn1_w_out[0])], forward_at=0.7),
        down_comm=_gather_comm([bf(shard_t(w_in))], forward_at=0.8))
    w_in_p = _perm_w_in(w_in_g.reshape(-1, D))

    def late_weights(got):
        pool_g, w_out_g = got
        pw = jnp.transpose(pool_g.reshape(N_DEV, n_g, pw_rows, POOL_GROUP_DIM),
                           (1, 0, 2, 3)).reshape(n_g, POOL_GROUP_DIM, POOL_GROUP_DIM)
        return pw, w_out_g.reshape(-1, D)

    x2, svm, pw_full, w_out_full, (wi2,) = _mixer_fwd(
        x1, mix_norm_g, sh2, sc2, g2, w_in_p, b_pad, q_norm_g, k_norm_g, late_weights, ps, T,
        proj_comm=_gather_comm([bf(pool_w[0].reshape(-1, POOL_GROUP_DIM)), bf(w_out[0])], forward_at=0.6),
        attn_comm=_gather_comm([bf(shard_t(ffn2_w_in))], forward_at=0.9))
    x3, sv2, wo2, _ = _ffn_fwd("ffn2", x2, ffn2_norm_g, sh3, sc3, g3, wi2,
                               lambda got: got[0].reshape(half, -1, D), T,
                               up_comm=_gather_comm([bf(ffn2_w_out[0])], forward_at=0.7))
    dx3, df3, dgf, loss_l, dgate3 = _final_loss(x3, tgt, final_norm_g.reshape(1, D), T, tb, below=(sv2[5], g3, 0.5))
    loss = lax.psum(loss_l[0, 0], ("x", "y", "c"))

    dx2, (dsh3, dsc3, dg3, dn3), dwi2_sum, dwo2, _, branch2 = _ffn_bwd(
        "ffn2", dx3, (df3, dgate3), sv2, ffn2_norm_g, sc3, wi2, wo2, T, core, defer_dwi=True,
        below=(svm[12], g2, 1.0))
    dx1, (dsh2, dsc2, dg2, dn2), mix_sums, dps, dgq, dgk, dbf, (dwi2,), branch1 = _mixer_bwd(
        dx2, branch2, svm, mix_norm_g, sc2, w_in_p, b_pad, q_norm_g, k_norm_g, pw_full, ps, w_out_full, T, core,
        ride_sums=[dwi2_sum], below=(sv1[5], g1, 0.5))
    dx0, (dsh1, dsc1, dg1, dn1), dwi1, dwo1, (dw_in_r, dpw_r, dw_out_r), _ = _ffn_bwd(
        "ffn1", dx1, branch1, sv1, ffn1_norm_g, sc1, wi1, wo1, T, core, ride_sums=mix_sums)

    received = dict(ffn1_w_in=dwi1, ffn1_w_out=dwo1, w_in=dw_in_r, pool_w=dpw_r, w_out=dw_out_r,
                    ffn2_w_in=dwi2, ffn2_w_out=dwo2)
    moments = dict(ffn1_w_in=(m_ffn1_w_in, v_ffn1_w_in), ffn1_w_out=(m_ffn1_w_out, v_ffn1_w_out),
                   w_in=(m_w_in, v_w_in), pool_w=(m_pool_w, v_pool_w), w_out=(m_w_out, v_w_out),
                   ffn2_w_in=(m_ffn2_w_in, v_ffn2_w_in), ffn2_w_out=(m_ffn2_w_out, v_ffn2_w_out))
    weights = dict(ffn1_w_in=ffn1_w_in, ffn1_w_out=ffn1_w_out, w_in=w_in, pool_w=pool_w, w_out=w_out,
                   ffn2_w_in=ffn2_w_in, ffn2_w_out=ffn2_w_out)
    row_tiles = dict(ffn1_w_in=TILE_FFN_IN, ffn1_w_out=TILE_W_OUT, w_in=TILE_W_IN, pool_w=TILE_POOL,
                     w_out=TILE_MIX_OUT, ffn2_w_in=TILE_FFN_IN, ffn2_w_out=TILE_W_OUT)
    results = {}
    for k in received:
        shape = weights[k].shape
        two_d = received[k].shape[1:]
        mk, vk = moments[k]
        if row_tiles[k] in (TILE_FFN_IN, TILE_W_IN):
            outs = _adamw("adamw_" + k, received[k], shard_t(weights[k]), shard_t(mk), shard_t(vk), row_tiles[k][0],
                          tc=512)
            results[k] = [jnp.swapaxes(o, 0, 1)[None] for o in outs]
        else:
            outs = _adamw("adamw_" + k, received[k], weights[k].reshape(two_d), mk.reshape(two_d),
                          vk.reshape(two_d), row_tiles[k][0])
            results[k] = [o.reshape(shape) for o in outs]

    dmod = jnp.concatenate([dsh1, dsc1, dg1, dsh2, dsc2, dg2, dsh3, dsc3, dg3], axis=1)
    small_names = ["b_ada", "ffn1_norm_g", "mix_norm_g", "ffn2_norm_g", "final_norm_g", "b_forget",
                   "q_norm_g", "k_norm_g", "pool_scale"]
    small_w = dict(b_ada=b_ada, ffn1_norm_g=ffn1_norm_g, mix_norm_g=mix_norm_g, ffn2_norm_g=ffn2_norm_g,
                   final_norm_g=final_norm_g, b_forget=b_forget, q_norm_g=q_norm_g, k_norm_g=k_norm_g,
                   pool_scale=pool_scale)
    small_m = dict(b_ada=m_b_ada, ffn1_norm_g=m_ffn1_norm_g, mix_norm_g=m_mix_norm_g, ffn2_norm_g=m_ffn2_norm_g,
                   final_norm_g=m_final_norm_g, b_forget=m_b_forget, q_norm_g=m_q_norm_g, k_norm_g=m_k_norm_g,
                   pool_scale=m_pool_scale)
    small_v = dict(b_ada=v_b_ada, ffn1_norm_g=v_ffn1_norm_g, mix_norm_g=v_mix_norm_g, ffn2_norm_g=v_ffn2_norm_g,
                   final_norm_g=v_final_norm_g, b_forget=v_b_forget, q_norm_g=v_q_norm_g, k_norm_g=v_k_norm_g,
                   pool_scale=v_pool_scale)
    small_g = dict(b_ada=dmod, ffn1_norm_g=dn1, mix_norm_g=dn2, ffn2_norm_g=dn3, final_norm_g=dgf,
                   b_forget=dbf[:, :N_HEADS], q_norm_g=dgq, k_norm_g=dgk, pool_scale=dps)
    sizes = [small_w[k].size for k in small_names]
    total = sum(sizes)
    lanes = 8 * 128
    padded = -(-total // lanes) * lanes

    def pack(d):
        flat = jnp.concatenate([d[k].reshape(-1) for k in small_names])
        return jnp.pad(flat, (0, padded - total)).reshape(8, padded // 8)

    small_parts = _standalone("gather_small_grads", _gather_comm([pack(small_g)]))[0]
    s_outs = _adamw("adamw_small", small_parts, pack(small_w), pack(small_m), pack(small_v), 8)
    offs = [0]
    for s in sizes:
        offs.append(offs[-1] + s)
    for idx, k in enumerate(small_names):
        results[k] = [o.reshape(-1)[offs[idx]:offs[idx + 1]].reshape(small_w[k].shape) for o in s_outs]

    dmod_all = small_parts.reshape(N_DEV, padded)[:, :N_MOD * D]
    dmod_loc = lax.dynamic_slice_in_dim(dmod_all, me * n_loc, n_loc, axis=1)
    g_ada = _ada_bwd(c_all, dmod_loc, n_loc // 3)
    a_outs = _adamw("adamw_w_ada", g_ada[None], w_ada[0], m_w_ada[0], v_w_ada[0], 128)
    results["w_ada"] = [o.reshape(w_ada.shape) for o in a_outs]

    order = ["w_ada", "b_ada", "ffn1_norm_g", "ffn1_w_in", "ffn1_w_out", "mix_norm_g", "w_in", "b_forget",
             "q_norm_g", "k_norm_g", "pool_w", "pool_scale", "w_out", "ffn2_norm_g", "ffn2_w_in", "ffn2_w_out",
             "final_norm_g"]
    out = [loss, dx0[None]]
    for part in range(4):
        out += [results[k][part] for k in order]
    return tuple(out)
```

```python
import jax
import jax.numpy as jnp
from jax import lax
from jax.experimental import pallas as pl
from jax.experimental.pallas import tpu as pltpu

F32 = jnp.float32
BF16 = jnp.bfloat16
MESH = pl.DeviceIdType.MESH
ANY = pl.BlockSpec(memory_space=pl.ANY)

N_DEV = 8
EPS = 1e-6
HEAD_DIM = 128
N_HEADS = 8
POOL_WINDOWS = (2, 4, 8, 16)
POOL_GROUP_DIM = 256
N_MOD = 9
ADAM_LR = 0.001
ADAM_B1 = 0.9
ADAM_B2 = 0.999
ADAM_EPS = 1e-08
ADAM_WD = 0.01
ADAM_STEP = 10
NEG = -1e30
VMEM_LIMIT_V7X = 56 * 1024 * 1024


def _params():
    return pltpu.CompilerParams(vmem_limit_bytes=VMEM_LIMIT_V7X)


def _sigmoid(z):
    return 1.0 / (1.0 + jnp.exp(-z))


def _rstd(x):
    return lax.rsqrt(jnp.mean(x * x, axis=-1, keepdims=True) + EPS)


def _mesh_pos():
    return lax.axis_index("x"), lax.axis_index("y"), lax.axis_index("c")


def _flat(px, py, pc):
    return 4 * px + 2 * py + pc


class _Comm:
    def __init__(self, ins, outs, sems, phases):
        self.ins, self.outs, self.sems, self.phases = list(ins), list(outs), list(sems), list(phases)


def _pallas(kern, *, comm=None, **kw):
    if comm is None:
        return pl.pallas_call(kern, **kw)
    grid = tuple(kw["grid"])
    single = not isinstance(kw["out_shape"], (list, tuple))
    out_shape = [kw["out_shape"]] if single else list(kw["out_shape"])
    out_specs = [kw["out_specs"]] if single else list(kw["out_specs"])
    in_specs = list(kw["in_specs"])
    scratch = list(kw.get("scratch_shapes", ()))
    n_in, n_out, n_scr = len(in_specs), len(out_shape), len(scratch)
    n_ci, n_co = len(comm.ins), len(comm.outs)
    strides, n_steps = [], 1
    for g in reversed(grid):
        strides.insert(0, n_steps)
        n_steps *= g

    def wrapped(*refs):
        ins, cins = refs[:n_in], refs[n_in:n_in + n_ci]
        base = n_in + n_ci
        outs, couts = refs[base:base + n_out], refs[base + n_out:base + n_out + n_co]
        base += n_out + n_co
        scr, sems = refs[base:base + n_scr], refs[base + n_scr:]
        step = sum(pl.program_id(d) * strides[d] for d in range(len(grid)))
        for frac, fn in comm.phases:
            if frac < 1.0:
                pl.when(step == int(round(frac * (n_steps - 1))))(lambda fn=fn: fn(cins, couts, sems))
        kern(*ins, *outs, *scr)
        for frac, fn in comm.phases:
            if frac >= 1.0:
                pl.when(step == n_steps - 1)(lambda fn=fn: fn(cins, couts, sems))

    kw = dict(kw, in_specs=in_specs + [ANY] * n_ci, out_specs=out_specs + [ANY] * n_co,
              out_shape=out_shape + comm.outs, scratch_shapes=scratch + comm.sems)
    call = pl.pallas_call(wrapped, **kw)

    def run(*args):
        res = call(*args, *comm.ins)
        main = res[0] if single else list(res[:n_out])
        return main, list(res[n_out:])

    return run


def _join(first, second):
    n_i, n_o, n_s = len(first.ins), len(first.outs), len(first.sems)

    def left(fn):
        return lambda ins, outs, sems: fn(ins[:n_i], outs[:n_o], sems[:n_s])

    def right(fn):
        return lambda ins, outs, sems: fn(ins[n_i:], outs[n_o:], sems[n_s:])

    phases = [(f, left(fn)) for f, fn in first.phases] + [(f, right(fn)) for f, fn in second.phases]
    return _Comm(first.ins + second.ins, first.outs + second.outs, first.sems + second.sems, phases)


def _hosted(comm, res):
    return res if comm is not None else (res, [])


def _standalone(name, comm):
    def kern():
        pass

    return _pallas(kern, comm=comm, name=name, grid=(1,), in_specs=[], out_specs=[], out_shape=[])()[1]


def _dma_sems(*shapes):
    return [pltpu.SemaphoreType.DMA(s) for s in shapes]


def _gather_comm(arrs, forward_at=0.5):
    n = len(arrs)

    def setup(outs, sems):
        send_sems, recv_sems, _ = sems
        x, y, c = _mesh_pos()
        chips = [(1 - x, y), (x, 1 - y), (1 - x, 1 - y)]

        def copy(a, k, block, to, src=None):
            dst = outs[a].at[_flat(*block)]
            return pltpu.make_async_remote_copy(
                src_ref=dst if src is None else src, dst_ref=dst,
                send_sem=send_sems.at[a, k], recv_sem=recv_sems.at[a, k],
                device_id=to, device_id_type=MESH)

        return (x, y, c), (x, y, 1 - c), chips, copy

    def local(ins, outs, sems, a, me):
        return pltpu.make_async_copy(ins[a], outs[a].at[_flat(*me)], sems[2].at[a])

    def send_own(ins, outs, sems):
        me, sibling, chips, copy = setup(outs, sems)
        for a in range(n):
            local(ins, outs, sems, a, me).start()
            copy(a, 0, me, sibling, src=ins[a]).start()
            for j, chip in enumerate(chips):
                copy(a, 1 + j, me, (*chip, me[2]), src=ins[a]).start()

    def forward(ins, outs, sems):
        me, sibling, chips, copy = setup(outs, sems)
        for a in range(n):
            for j, chip in enumerate(chips):
                copy(a, 1 + j, (*chip, me[2]), me).wait_recv()
                copy(a, 4 + j, (*chip, me[2]), sibling).start()

    def finish(ins, outs, sems):
        me, sibling, chips, copy = setup(outs, sems)
        for a in range(n):
            copy(a, 0, sibling, me).wait_recv()
            for j, chip in enumerate(chips):
                copy(a, 4 + j, (*chip, 1 - me[2]), me).wait_recv()
        for a in range(n):
            copy(a, 0, me, sibling, src=ins[a]).wait_send()
            for j, chip in enumerate(chips):
                copy(a, 1 + j, me, (*chip, me[2]), src=ins[a]).wait_send()
                copy(a, 4 + j, (*chip, me[2]), sibling).wait_send()
            local(ins, outs, sems, a, me).wait()

    return _Comm(arrs, [jax.ShapeDtypeStruct((N_DEV,) + a.shape, a.dtype) for a in arrs],
                 _dma_sems((n, 7), (n, 7), (n,)), [(0.0, send_own), (forward_at, forward), (1.0, finish)])


CHIPS = [(0, 0), (0, 1), (1, 0), (1, 1)]


def _sibling_comm(parts):
    n = len(parts)

    def copies(ins, outs, sems):
        x, y, c = _mesh_pos()
        return [pltpu.make_async_remote_copy(
                    src_ref=ins[a].at[_flat(qx, qy, 1 - c)], dst_ref=outs[a].at[q],
                    send_sem=sems[0].at[a, q], recv_sem=sems[1].at[a, q],
                    device_id=(x, y, 1 - c), device_id_type=MESH)
                for a in range(n) for q, (qx, qy) in enumerate(CHIPS)]

    def start(ins, outs, sems):
        for cp in copies(ins, outs, sems):
            cp.start()

    def finish(ins, outs, sems):
        for cp in copies(ins, outs, sems):
            cp.wait_recv()
        for cp in copies(ins, outs, sems):
            cp.wait_send()

    return _Comm(parts, [jax.ShapeDtypeStruct((4,) + p.shape[1:], p.dtype) for p in parts],
                 _dma_sems((n, 4), (n, 4)), [(0.0, start), (1.0, finish)])


def _chip_comm(sums):
    n = len(sums)
    flips = [(1, 0), (0, 1), (1, 1)]

    def own(ins, outs, sems):
        mine = 2 * lax.axis_index("x") + lax.axis_index("y")
        return [pltpu.make_async_copy(ins[a].at[mine], outs[a].at[mine], sems[2].at[a]) for a in range(n)]

    def copies(ins, outs, sems, arriving=False):
        x, y, c = _mesh_pos()
        mine = 2 * x + y
        remote = []
        for a in range(n):
            for k, (fx, fy) in enumerate(flips):
                qx, qy = x ^ fx, y ^ fy
                q = 2 * qx + qy
                remote.append(pltpu.make_async_remote_copy(
                    src_ref=ins[a].at[q], dst_ref=outs[a].at[q if arriving else mine],
                    send_sem=sems[0].at[a, k], recv_sem=sems[1].at[a, k],
                    device_id=(qx, qy, c), device_id_type=MESH))
        return remote

    def start(ins, outs, sems):
        for cp in own(ins, outs, sems) + copies(ins, outs, sems):
            cp.start()

    def finish(ins, outs, sems):
        for cp in copies(ins, outs, sems, arriving=True):
            cp.wait_recv()
        for cp in copies(ins, outs, sems):
            cp.wait_send()
        for cp in own(ins, outs, sems):
            cp.wait()

    return _Comm(sums, [jax.ShapeDtypeStruct(s.shape, s.dtype) for s in sums],
                 _dma_sems((n, 3), (n, 3), (n,)), [(0.0, start), (1.0, finish)])


def _pair_add(name, parts, got, core, tr):
    _, R, C = parts.shape
    assert R % tr == 0

    def kern(c_ref, p_ref, g_ref, o_ref):
        o_ref[...] = (p_ref[...].astype(F32) + g_ref[...].astype(F32)).astype(o_ref.dtype)

    blk = pl.BlockSpec((None, tr, C), lambda q, i, c_ref: (q, i, 0))
    return pl.pallas_call(
        kern, name=name,
        grid_spec=pltpu.PrefetchScalarGridSpec(
            num_scalar_prefetch=1, grid=(4, R // tr),
            in_specs=[pl.BlockSpec((None, tr, C), lambda q, i, c_ref: (2 * q + c_ref[0], i, 0)), blk],
            out_specs=blk),
        out_shape=jax.ShapeDtypeStruct((4, R, C), parts.dtype), compiler_params=_params(),
    )(core, parts, got)


def _rowwise(name, body, T, tb, rows, vecs, out_rows, out_accs):
    n_in = len(rows) + len(vecs)
    n_o, n_a = len(out_rows), len(out_accs)

    def kern(*refs):
        i = pl.program_id(0)
        res = body(*[r[...] for r in refs[:n_in]])
        if not isinstance(res, (tuple, list)):
            res = (res,)
        outs = refs[n_in:]
        for k in range(n_o):
            outs[k][...] = res[k].astype(outs[k].dtype)

        def accumulate(ref, val):
            @pl.when(i == 0)
            def _():
                ref[...] = val

            @pl.when(i > 0)
            def _():
                ref[...] += val

        for k in range(n_a):
            accumulate(outs[n_o + k], res[n_o + k])

    in_specs = [pl.BlockSpec((tb, w), lambda i, cb=cb: (i, cb)) for (_, w, cb) in rows]
    in_specs += [pl.BlockSpec((1, v.shape[1]), lambda i: (0, 0)) for v in vecs]
    out_specs = [pl.BlockSpec((tb, w), lambda i: (i, 0)) for (w, _) in out_rows]
    out_specs += [pl.BlockSpec((1, w), lambda i: (0, 0)) for w in out_accs]
    out_shape = [jax.ShapeDtypeStruct((T, w), dt) for (w, dt) in out_rows]
    out_shape += [jax.ShapeDtypeStruct((1, w), F32) for w in out_accs]
    res = pl.pallas_call(
        kern, name=name, grid=(T // tb,), in_specs=in_specs, out_specs=out_specs,
        out_shape=out_shape, compiler_params=_params(),
    )(*[r[0] for r in rows], *vecs)
    return res


def _dot(a, b, mode):
    dims = {"NN": ((1,), (0,)), "NT": ((1,), (1,)), "TN": ((0,), (0,))}[mode]
    return lax.dot_general(a.astype(BF16), b.astype(BF16), (dims, ((), ())),
                           preferred_element_type=F32)


def _mm(name, a, b, mode, out_dtype, tm, tn, tk, ga=False, gb=False, gmode=None, comm=None):
    G = (a.shape[0] if ga else b.shape[0]) if gmode else 1
    a2, b2 = a.shape[-2:], b.shape[-2:]
    if mode == "NN":
        (M, K), (_, N) = a2, b2
    elif mode == "NT":
        (M, K), (N, _) = a2, b2
    else:
        (K, M), (_, N) = a2, b2
    tm, tn, tk = min(tm, M), min(tn, N), min(tk, K)
    assert M % tm == 0 and N % tn == 0 and K % tk == 0, (name, M, N, K, tm, tn, tk)
    batch = gmode == "batch"
    n_gb, n_gs = (G if batch else 1), (G if gmode == "sum" else 1)
    nk = K // tk
    n_red = n_gs * nk

    def grp(g_b, g_s):
        return g_b if batch else g_s

    if mode == "TN":
        a_blk, a_idx = (tk, tm), lambda g_b, mi, ni, g_s, ki: (ki, mi)
    else:
        a_blk, a_idx = (tm, tk), lambda g_b, mi, ni, g_s, ki: (mi, ki)
    if mode == "NT":
        b_blk, b_idx = (tn, tk), lambda g_b, mi, ni, g_s, ki: (ni, ki)
    else:
        b_blk, b_idx = (tk, tn), lambda g_b, mi, ni, g_s, ki: (ki, ni)

    def with_group(blk, idx, has_group):
        if not has_group:
            return pl.BlockSpec(blk, idx)
        return pl.BlockSpec((None,) + blk, lambda g_b, mi, ni, g_s, ki: (grp(g_b, g_s),) + idx(g_b, mi, ni, g_s, ki))

    o_blk, o_idx = (tm, tn), lambda g_b, mi, ni, g_s, ki: (mi, ni)
    o_spec = with_group(o_blk, o_idx, batch)
    o_shape = ((G,) if batch else ()) + (M, N)

    def kern(a_ref, b_ref, o_ref, *scratch):
        part = _dot(a_ref[...], b_ref[...], mode)
        if n_red == 1:
            o_ref[...] = part.astype(o_ref.dtype)
            return
        acc = scratch[0]
        step = pl.program_id(3) * nk + pl.program_id(4)

        @pl.when(step == 0)
        def _():
            acc[...] = part

        @pl.when(step > 0)
        def _():
            acc[...] += part

        @pl.when(step == n_red - 1)
        def _():
            o_ref[...] = acc[...].astype(o_ref.dtype)

    return _pallas(
        kern, comm=comm, name=name, grid=(n_gb, M // tm, N // tn, n_gs, nk),
        in_specs=[with_group(a_blk, a_idx, ga), with_group(b_blk, b_idx, gb)],
        out_specs=o_spec, out_shape=jax.ShapeDtypeStruct(o_shape, out_dtype),
        scratch_shapes=[] if n_red == 1 else [pltpu.VMEM((tm, tn), F32)],
        compiler_params=_params(),
    )(a, b)


def _mm_groups(name, a, b, mode, tm, tn, residual=None, comm=None):
    G, M, K = a.shape
    N = b.shape[2] if mode == "NN" else b.shape[1]
    tm, tn = min(tm, M), min(tn, N)
    assert M % tm == 0 and N % tn == 0

    def kern(a_ref, b_ref, *rest):
        acc = _dot(a_ref[0], b_ref[0], mode)
        for g in range(1, G):
            acc = acc + _dot(a_ref[g], b_ref[g], mode)
        if residual is None:
            rest[0][...] = acc
        else:
            x_ref, g_ref, f_ref, o_ref = rest
            f_ref[...] = acc
            o_ref[...] = x_ref[...] + (residual[2] * g_ref[...]) * acc

    b_spec = (pl.BlockSpec((G, K, tn), lambda ni, mi: (0, 0, ni)) if mode == "NN"
              else pl.BlockSpec((G, tn, K), lambda ni, mi: (0, ni, 0)))
    o_spec = pl.BlockSpec((tm, tn), lambda ni, mi: (mi, ni))
    in_specs = [pl.BlockSpec((G, tm, K), lambda ni, mi: (0, mi, 0)), b_spec]
    args = [a, b]
    out = jax.ShapeDtypeStruct((M, N), F32)
    if residual is not None:
        in_specs += [o_spec, pl.BlockSpec((1, tn), lambda ni, mi: (0, ni))]
        args += [residual[0], residual[1]]
    return _pallas(
        kern, comm=comm, name=name, grid=(N // tn, M // tm), in_specs=in_specs,
        out_specs=o_spec if residual is None else [o_spec, o_spec],
        out_shape=out if residual is None else [out, out], compiler_params=_params(),
    )(*args)


def _adamw(name, parts, w, m, v, tr, tc=None):
    G, R, C = parts.shape
    tc = C if tc is None else tc
    assert R % tr == 0 and C % tc == 0
    bc1 = 1.0 - ADAM_B1 ** ADAM_STEP
    bc2 = 1.0 - ADAM_B2 ** ADAM_STEP

    def kern(p_ref, w_ref, m_ref, v_ref, g_out, d_out, m_out, v_out):
        g = p_ref[0].astype(F32)
        for s in range(1, G):
            g = g + p_ref[s].astype(F32)
        m2 = ADAM_B1 * m_ref[...] + (1.0 - ADAM_B1) * g
        v2 = ADAM_B2 * v_ref[...] + (1.0 - ADAM_B2) * (g * g)
        m_hat = m2 / bc1
        v_hat = v2 / bc2
        g_out[...] = g
        d_out[...] = -ADAM_LR * (m_hat / (jnp.sqrt(v_hat) + ADAM_EPS) + ADAM_WD * w_ref[...])
        m_out[...] = m2
        v_out[...] = v2

    blk = pl.BlockSpec((tr, tc), lambda i, j: (i, j))
    return pl.pallas_call(
        kern, name=name, grid=(R // tr, C // tc),
        in_specs=[pl.BlockSpec((G, tr, tc), lambda i, j: (0, i, j)), blk, blk, blk],
        out_specs=[blk] * 4, out_shape=[jax.ShapeDtypeStruct((R, C), F32)] * 4,
        compiler_params=_params(),
    )(parts, w, m, v)


def _ada_fwd(c_all, w_loc, b_loc, tn):
    B, D = c_all.shape
    N = w_loc.shape[1]

    def kern(c_ref, w_ref, b_ref, o_ref):
        cc = c_ref[...]
        act = cc * _sigmoid(cc)
        o_ref[...] = _dot(act, w_ref[...], "NN") + b_ref[...]

    return pl.pallas_call(
        kern, name="ada_fwd", grid=(N // tn,),
        in_specs=[pl.BlockSpec((B, D), lambda j: (0, 0)), pl.BlockSpec((D, tn), lambda j: (0, j)),
                  pl.BlockSpec((1, tn), lambda j: (0, j))],
        out_specs=pl.BlockSpec((B, tn), lambda j: (0, j)),
        out_shape=jax.ShapeDtypeStruct((B, N), F32), compiler_params=_params(),
    )(c_all, w_loc, b_loc)


def _ada_bwd(c_all, dmod_loc, tn):
    B, D = c_all.shape
    N = dmod_loc.shape[1]

    def kern(c_ref, d_ref, o_ref):
        cc = c_ref[...]
        act = cc * _sigmoid(cc)
        o_ref[...] = _dot(act, d_ref[...], "TN")

    return pl.pallas_call(
        kern, name="ada_bwd", grid=(N // tn,),
        in_specs=[pl.BlockSpec((B, D), lambda j: (0, 0)), pl.BlockSpec((B, tn), lambda j: (0, j))],
        out_specs=pl.BlockSpec((D, tn), lambda j: (0, j)),
        out_shape=jax.ShapeDtypeStruct((D, N), F32), compiler_params=_params(),
    )(c_all, dmod_loc)


def _norm_mod_fwd(name, x, g, sc, sh, T, tb):
    D = x.shape[1]

    def body(xb, gb, scb, shb):
        n = (xb * _rstd(xb)) * gb
        return n * (1.0 + scb) + shb

    return _rowwise(name, body, T, tb, [(x, D, 0)], [g, sc, sh], [(D, BF16)], [])[0]


def _branch_bwd(dx, fb, gateb, coef):
    return (coef * gateb) * dx, jnp.sum((coef * fb) * dx, axis=0, keepdims=True)


def _norm_mod_bwd(name, x, dhm, dres, g, sc, T, tb, below=None):
    D = x.shape[1]

    def body(xb, db, rb, *rest):
        gb, scb = rest[-2:] if below is None else rest[1:3]
        r = _rstd(xb)
        xh = xb * r
        n = xh * gb
        dn = db * (1.0 + scb)
        dxh = dn * gb
        dx = rb + r * (dxh - xh * jnp.mean(dxh * xh, axis=-1, keepdims=True))
        sums = (jnp.sum(db, axis=0, keepdims=True), jnp.sum(db * n, axis=0, keepdims=True),
                jnp.sum(dn * xh, axis=0, keepdims=True))
        if below is None:
            return (dx,) + sums
        df, dgate = _branch_bwd(dx, rest[0], rest[3], below[2])
        return (dx, df) + sums + (dgate,)

    rows = [(x, D, 0), (dhm, D, 0), (dres, D, 0)] + ([] if below is None else [(below[0], D, 0)])
    vecs = [g, sc] + ([] if below is None else [below[1]])
    return _rowwise(name, body, T, tb, rows, vecs, [(D, F32)] + ([] if below is None else [(D, BF16)]),
                    [D, D, D] + ([] if below is None else [D]))


def _final_loss(x, tgt, g, T, tb, below):
    D = x.shape[1]

    def body(xb, tb_, fb, gb, gateb):
        r = _rstd(xb)
        xh = xb * r
        err = xh * gb - tb_
        loss = 0.5 * jnp.sum(jnp.mean(err * err, axis=-1, keepdims=True), axis=0, keepdims=True)
        dy = err * (1.0 / D)
        dxh = dy * gb
        dx = r * (dxh - xh * jnp.mean(dxh * xh, axis=-1, keepdims=True))
        df, dgate = _branch_bwd(dx, fb, gateb, below[2])
        return dx, df, jnp.sum(dy * xh, axis=0, keepdims=True), jnp.broadcast_to(loss, (1, 128)), dgate

    return _rowwise("final_loss", body, T, tb, [(x, D, 0), (tgt, D, 0), (below[0], D, 0)], [g, below[1]],
                    [(D, F32), (D, BF16)], [D, 128, D])


def _ffn_up(name, hm, wi, T, tm, comm=None):
    D = hm.shape[1]
    Ws = wi.shape[1]
    half = wi.shape[0] // 2

    n_sub = 2 if tm % 32 == 0 else 1
    subs = [pl.ds(r * (tm // n_sub), tm // n_sub) for r in range(n_sub)]

    def kern(h_ref, wa_ref, wb_ref, a_ref, b_ref, hid_ref):
        wa, wb = wa_ref[...], wb_ref[...]
        ab = [(_dot(h_ref[rows, :], wa, "NT"), _dot(h_ref[rows, :], wb, "NT")) for rows in subs]
        for rows, (a, b) in zip(subs, ab):
            a_ref[rows, :] = a.astype(BF16)
            b_ref[rows, :] = b.astype(BF16)
            hid_ref[rows, :] = ((a * _sigmoid(a)) * b).astype(BF16)

    o_spec = pl.BlockSpec((None, tm, Ws), lambda g, i: (g, i, 0))
    return _pallas(
        kern, comm=comm, name=name, grid=(half, T // tm),
        in_specs=[pl.BlockSpec((tm, D), lambda g, i: (i, 0)),
                  pl.BlockSpec((None, Ws, D), lambda g, i: (g, 0, 0)),
                  pl.BlockSpec((None, Ws, D), lambda g, i: (g + half, 0, 0))],
        out_specs=[o_spec] * 3,
        out_shape=[jax.ShapeDtypeStruct((half, T, Ws), BF16)] * 3,
        compiler_params=_params(),
    )(hm, wi, wi)


def _ffn_down_bwd(name, df, wo, a, b, T, tm, comm=None):
    D = df.shape[1]
    half, _, Ws = a.shape

    n_sub = 2 if tm % 32 == 0 else 1
    subs = [pl.ds(r * (tm // n_sub), tm // n_sub) for r in range(n_sub)]

    def kern(df_ref, wo_ref, a_ref, b_ref, dp_ref):
        wo_blk = wo_ref[...]
        dhid = [_dot(df_ref[rows, :], wo_blk, "NT") for rows in subs]
        for rows, dh in zip(subs, dhid):
            av = a_ref[rows, :].astype(F32)
            s = _sigmoid(av)
            silu = av * s
            dp_ref[0, rows, :] = (dh * b_ref[rows, :].astype(F32) * (s + silu * (1.0 - s))).astype(BF16)
            dp_ref[1, rows, :] = (dh * silu).astype(BF16)

    act = pl.BlockSpec((None, tm, Ws), lambda g, i: (g, i, 0))
    return _pallas(
        kern, comm=comm, name=name, grid=(half, T // tm),
        in_specs=[pl.BlockSpec((tm, D), lambda g, i: (i, 0)),
                  pl.BlockSpec((None, Ws, D), lambda g, i: (g, 0, 0)), act, act],
        out_specs=pl.BlockSpec((2, None, tm, Ws), lambda g, i: (0, g, i, 0)),
        out_shape=jax.ShapeDtypeStruct((2, half, T, Ws), BF16),
        compiler_params=_params(),
    )(df, wo, a, b)


def _ffn_fwd(tag, x, norm_g, sh, sc, gate, wi, wo_of, T, up_comm=None, down_comm=None):
    tb = min(256, T)
    hm = _norm_mod_fwd(tag + "_norm_fwd", x, norm_g, sc, sh, T, tb)
    (a, b, hid), got_up = _hosted(up_comm, _ffn_up(tag + "_up", hm, wi, T, min(512, T), comm=up_comm))
    wo = wo_of(got_up)
    (f, x_out), got_down = _hosted(down_comm, _mm_groups(tag + "_down", hid, wo, "NN", 512, 1024,
                                                         residual=(x, gate, 0.5), comm=down_comm))
    return x_out, (x, hm, a, b, hid, f), wo, got_down


TILE_W_IN = (513, 513)
TILE_FFN_IN = (688, 688)
TILE_W_OUT = (16, 688)
TILE_MIX_OUT = (64, 256)
TILE_POOL = (128, 128)


def _reduce_level1(tag, parts, core, tiles, host=None):
    comm = _sibling_comm(parts)
    if host is None:
        res, got = None, _standalone(tag + "_sibling", comm)
    else:
        res, got = host(comm)
    sums = [_pair_add("%s_pair_add%d" % (tag, k), p, g, core, min(t[1], p.shape[1]))
            for k, (p, g, t) in enumerate(zip(parts, got, tiles))]
    return res, sums


def _ffn_bwd(tag, dx_out, branch, saved, norm_g, sc, wi, wo, T, core, ride_sums=None, defer_dwi=False,
             below=None):
    x, hm, a, b, hid, f = saved
    tb = min(256, T)
    D = x.shape[1]
    df, dgate = branch
    dwo = _mm(tag + "_dwo", hid, df, "TN", BF16, 2048, 512, T, ga=True, gmode="batch").reshape(N_DEV, -1, D)
    n_ride = 0 if ride_sums is None else len(ride_sums)

    def down_bwd_call(comm):
        if n_ride:
            comm = _join(comm, _chip_comm(ride_sums))
        res, got = _ffn_down_bwd(tag + "_down_bwd", df, wo, a, b, T, min(512, T), comm=comm)
        return (res, got[len(got) - n_ride:]), got[:len(got) - n_ride]

    (dproj, ride_got), (dwo_sum,) = _reduce_level1(tag + "_dwo", [dwo], core, [TILE_W_OUT], host=down_bwd_call)
    dproj = dproj.reshape((2 * dproj.shape[1],) + dproj.shape[2:])
    dwi, (dwo_got,) = _mm(tag + "_dwi", dproj, hm, "TN", BF16, 2048, 512, T, ga=True, gmode="batch",
                          comm=_chip_comm([dwo_sum]))

    def dhm_call(comm):
        return _mm_groups(tag + "_dhm", dproj, wi, "NN", 512, 512, comm=comm)

    if defer_dwi:
        dhm, (dwi_out,) = _reduce_level1(tag + "_dwi", [dwi], core, [TILE_FFN_IN], host=dhm_call)
    else:
        _, (dwi_sum,) = _reduce_level1(tag + "_dwi", [dwi], core, [TILE_FFN_IN])
        dhm, (dwi_out,) = dhm_call(_chip_comm([dwi_sum]))
    res = _norm_mod_bwd(tag + "_norm_bwd", x, dhm, dx_out, norm_g, sc, T, tb, below=below)
    dx, (dsh, dsc, dng) = res[0], res[-3:] if below is None else res[2:5]
    return dx, (dsh, dsc, dgate, dng), dwi_out, dwo_got, ride_got, None if below is None else (res[1], res[5])


def _heads(fn, *arrs):
    outs = [fn(*[a[:, h * HEAD_DIM:(h + 1) * HEAD_DIM] for a in arrs]) for h in range(N_HEADS)]
    return outs


def _qknorm_fwd(proj, gq, gk, T, tb):
    W = N_HEADS * HEAD_DIM

    def body(q, k, v, gqb, gkb):
        qn = jnp.concatenate(_heads(lambda t: (t * _rstd(t)) * gqb, q), axis=1)
        kn = jnp.concatenate(_heads(lambda t: (t * _rstd(t)) * gkb, k), axis=1)
        return qn, kn, v

    return _rowwise("qknorm_fwd", body, T, tb, [(proj, W, 0), (proj, W, 1), (proj, W, 2)], [gq, gk],
                    [(W, BF16)] * 3, [])


def _qknorm_bwd(proj, dqn, dkn, gq, gk, T, tb):
    W = N_HEADS * HEAD_DIM

    def one(t, dt, g):
        r = _rstd(t)
        th = t * r
        dth = dt * g
        d = r * (dth - th * jnp.mean(dth * th, axis=-1, keepdims=True))
        return d, jnp.sum(dt * th, axis=0, keepdims=True)

    def body(q, k, dq, dk, gqb, gkb):
        rq = _heads(lambda t, dt: one(t, dt, gqb), q, dq)
        rk = _heads(lambda t, dt: one(t, dt, gkb), k, dk)
        return (jnp.concatenate([r[0] for r in rq], axis=1), jnp.concatenate([r[0] for r in rk], axis=1),
                sum(r[1] for r in rq), sum(r[1] for r in rk))

    return _rowwise("qknorm_bwd", body, T, tb, [(proj, W, 0), (proj, W, 1), (dqn, W, 0), (dkn, W, 0)],
                    [gq, gk], [(W, BF16)] * 2, [HEAD_DIM, HEAD_DIM])


def _log_sigmoid(z):
    return jnp.minimum(z, 0.0) - jnp.log(1.0 + jnp.exp(-jnp.abs(z)))


def _fgate_fwd(proj, fcol, b_pad, T):
    nblk = T // 128

    def kern(f_ref, b_ref, o_ref):
        r = lax.broadcasted_iota(jnp.int32, (128, 128), 0)
        c = lax.broadcasted_iota(jnp.int32, (128, 128), 1)
        tri = (r >= c).astype(F32)
        carry = jnp.zeros((1, 128), F32)
        for k in range(nblk):
            rows = pl.ds(k * 128, 128)
            lf = _log_sigmoid(f_ref[rows, :] + b_ref[...])
            o_ref[rows, :] = jnp.dot(tri, lf, precision=lax.Precision.HIGHEST, preferred_element_type=F32) + carry
            carry = carry + jnp.sum(lf, axis=0, keepdims=True)

    return pl.pallas_call(
        kern, name="fgate_fwd", grid=(1,),
        in_specs=[pl.BlockSpec((T, 128), lambda i: (0, fcol)), pl.BlockSpec((1, 128), lambda i: (0, 0))],
        out_specs=pl.BlockSpec((T, 128), lambda i: (0, 0)),
        out_shape=jax.ShapeDtypeStruct((T, 128), F32), compiler_params=_params(),
    )(proj, b_pad)


def _fgate_bwd(proj, fcol, b_pad, dF, T):
    nblk = T // 128

    def kern(f_ref, b_ref, d_ref, o_ref, db_ref):
        r = lax.broadcasted_iota(jnp.int32, (128, 128), 0)
        c = lax.broadcasted_iota(jnp.int32, (128, 128), 1)
        tri = (c >= r).astype(F32)
        carry = jnp.zeros((1, 128), F32)
        db = jnp.zeros((1, 128), F32)
        for k in reversed(range(nblk)):
            rows = pl.ds(k * 128, 128)
            dblk = d_ref[rows, :]
            rc = jnp.dot(tri, dblk, precision=lax.Precision.HIGHEST, preferred_element_type=F32) + carry
            carry = carry + jnp.sum(dblk, axis=0, keepdims=True)
            z = f_ref[rows, :] + b_ref[...]
            dz = rc * (1.0 / (1.0 + jnp.exp(z)))
            o_ref[rows, :] = dz
            db = db + jnp.sum(dz, axis=0, keepdims=True)
        db_ref[...] = db

    return pl.pallas_call(
        kern, name="fgate_bwd", grid=(1,),
        in_specs=[pl.BlockSpec((T, 128), lambda i: (0, fcol)), pl.BlockSpec((1, 128), lambda i: (0, 0)),
                  pl.BlockSpec((T, 128), lambda i: (0, 0))],
        out_specs=[pl.BlockSpec((T, 128), lambda i: (0, 0)), pl.BlockSpec((1, 128), lambda i: (0, 0))],
        out_shape=[jax.ShapeDtypeStruct((T, 128), F32), jax.ShapeDtypeStruct((1, 128), F32)],
        compiler_params=_params(),
    )(proj, b_pad, dF)


LOG2E = 1.4426950408889634


def _gate_bias(ft, fh, h):
    lane = lax.broadcasted_iota(jnp.int32, ft.shape, 1)
    fq = jnp.sum(jnp.where(lane == h, ft, 0.0), axis=1, keepdims=True)
    f0 = jnp.max(fq, axis=0, keepdims=True)
    sub = lax.broadcasted_iota(jnp.int32, fh.shape, 0)
    fk = jnp.sum(jnp.where(sub == h, fh, 0.0), axis=0, keepdims=True)
    return (f0 - fk) * LOG2E


HEADS_PER_STEP = 2


def _tri_rows(s, nb):
    i = sum((s >= k * (k + 1) // 2).astype(jnp.int32) for k in range(1, nb))
    return i, s - (i * (i + 1)) // 2


def _tri_cols(s, nb):
    j = sum((s >= k * nb - (k * (k - 1)) // 2).astype(jnp.int32) for k in range(1, nb))
    return j, j + s - (j * nb - (j * (j - 1)) // 2)


def _causal_bias(blk):
    row = lax.broadcasted_iota(jnp.int32, (blk, blk), 0)
    col = lax.broadcasted_iota(jnp.int32, (blk, blk), 1)
    return jnp.where(row >= col, 0.0, NEG)


def _attn_fwd(qn, kn, vb, f_tm, f_hm, T, blk, comm=None):
    nb = T // blk
    scale = HEAD_DIM ** -0.5
    W = N_HEADS * HEAD_DIM
    G = HEADS_PER_STEP
    lanes = [slice(g * HEAD_DIM, (g + 1) * HEAD_DIM) for g in range(G)]

    def kern(q_ref, k_ref, v_ref, ft_ref, fh_ref, o_ref, lse_ref, m_scr, l_scr, acc_scr):
        hp = pl.program_id(0)
        i, j = _tri_rows(pl.program_id(1), nb)

        @pl.when(j == 0)
        def _():
            m_scr[...] = jnp.full_like(m_scr, NEG)
            l_scr[...] = jnp.zeros_like(l_scr)
            acc_scr[...] = jnp.zeros_like(acc_scr)

        def block(diagonal):
            ft, fh = ft_ref[...], fh_ref[...]
            s = [_dot(q_ref[:, sl], k_ref[:, sl], "NT") * (scale * LOG2E) + _gate_bias(ft, fh, hp * G + g)
                 for g, sl in enumerate(lanes)]
            if diagonal:
                mask = _causal_bias(blk)
                s = [sg + mask for sg in s]
            m_prev = [m_scr[g] for g in range(G)]
            m_new = [jnp.maximum(mp, jnp.max(sg, axis=1, keepdims=True)) for mp, sg in zip(m_prev, s)]
            alpha = [jnp.exp2(mp - mn) for mp, mn in zip(m_prev, m_new)]
            p = [jnp.exp2(sg - mn) for sg, mn in zip(s, m_new)]
            for g, sl in enumerate(lanes):
                l_scr[g] = alpha[g] * l_scr[g] + jnp.sum(p[g], axis=1, keepdims=True)
                acc_scr[:, sl] = alpha[g] * acc_scr[:, sl] + _dot(p[g], v_ref[:, sl], "NN")
                m_scr[g] = m_new[g]

        @pl.when(j < i)
        def _():
            block(False)

        @pl.when(j == i)
        def _():
            block(True)
            for g, sl in enumerate(lanes):
                l = l_scr[g]
                o_ref[:, sl] = acc_scr[:, sl] / l
                lse_ref[:, sl] = jnp.broadcast_to(m_scr[g] + jnp.log2(l), (blk, HEAD_DIM))

    qspec = pl.BlockSpec((blk, G * HEAD_DIM), lambda h, s: (_tri_rows(s, nb)[0], h))
    kspec = pl.BlockSpec((blk, G * HEAD_DIM), lambda h, s: (_tri_rows(s, nb)[1], h))
    return _pallas(
        kern, comm=comm, name="attn_fwd", grid=(N_HEADS // G, nb * (nb + 1) // 2),
        in_specs=[qspec, kspec, kspec,
                  pl.BlockSpec((blk, 128), lambda h, s: (_tri_rows(s, nb)[0], 0)),
                  pl.BlockSpec((N_HEADS, blk), lambda h, s: (0, _tri_rows(s, nb)[1]))],
        out_specs=[qspec, qspec],
        out_shape=[jax.ShapeDtypeStruct((T, W), F32)] * 2,
        scratch_shapes=[pltpu.VMEM((G, blk, 1), F32), pltpu.VMEM((G, blk, 1), F32),
                        pltpu.VMEM((blk, G * HEAD_DIM), F32)],
        compiler_params=_params(),
    )(qn, kn, vb, f_tm, f_hm)


def _attn_bwd(qn, kn, vb, do, lse, delta, f_tm, f_hm, T, blk, comm=None):
    nb = T // blk
    scale = HEAD_DIM ** -0.5
    W = N_HEADS * HEAD_DIM
    G = HEADS_PER_STEP
    lanes = [slice(g * HEAD_DIM, (g + 1) * HEAD_DIM) for g in range(G)]

    def kern(q_ref, k_ref, v_ref, do_ref, lse_ref, dl_ref, ft_ref, fh_ref,
             dq_ref, dfq_ref, dk_ref, dv_ref, df_ref, dq_scr, dfq_scr, dk_scr, dv_scr, df_scr):
        hp = pl.program_id(0)
        j, i = _tri_cols(pl.program_id(1), nb)

        @pl.when((j == 0) & (i == 0))
        def _():
            dq_scr[...] = jnp.zeros_like(dq_scr)
            dfq_scr[...] = jnp.zeros_like(dfq_scr)

        @pl.when(i == j)
        def _():
            dk_scr[...] = jnp.zeros_like(dk_scr)
            dv_scr[...] = jnp.zeros_like(dv_scr)
            df_scr[...] = jnp.zeros_like(df_scr)

        def block(diagonal):
            ft, fh = ft_ref[...], fh_ref[...]
            rows = pl.ds(pl.multiple_of(i * blk, blk), blk)
            q = [q_ref[:, sl] for sl in lanes]
            k = [k_ref[:, sl] for sl in lanes]
            dob = [do_ref[:, sl].astype(BF16) for sl in lanes]
            s = [_dot(q[g], k[g], "NT") * (scale * LOG2E) + _gate_bias(ft, fh, hp * G + g) for g in range(G)]
            if diagonal:
                mask = _causal_bias(blk)
                s = [sg + mask for sg in s]
            p = [jnp.exp2(s[g] - lse_ref[:, sl.start:sl.start + 1]) for g, sl in enumerate(lanes)]
            dp = [_dot(dob[g], v_ref[:, sl], "NT") for g, sl in enumerate(lanes)]
            ds = [p[g] * (dp[g] - dl_ref[:, sl.start:sl.start + 1]) for g, sl in enumerate(lanes)]
            dsb = [d.astype(BF16) for d in ds]
            for g, sl in enumerate(lanes):
                dv_scr[:, sl] += _dot(p[g], dob[g], "TN")
                dk_scr[:, sl] += _dot(dsb[g], q[g], "TN") * scale
                dq_scr[rows, sl] += _dot(dsb[g], k[g], "NN") * scale
                df_scr[g] += jnp.sum(ds[g], axis=0, keepdims=True)
                dfq_scr[g, rows, :] += jnp.sum(ds[g], axis=1, keepdims=True)

        @pl.when(i > j)
        def _():
            block(False)

        @pl.when(i == j)
        def _():
            block(True)

        @pl.when(i == nb - 1)
        def _():
            dk_ref[...] = dk_scr[...]
            dv_ref[...] = dv_scr[...]
            df_ref[...] = -df_scr[...]

        @pl.when((j == nb - 1) & (i == nb - 1))
        def _():
            dq_ref[...] = dq_scr[...]
            for g, sl in enumerate(lanes):
                dfq_ref[:, sl] = jnp.broadcast_to(dfq_scr[g], (T, HEAD_DIM))

    qspec = pl.BlockSpec((blk, G * HEAD_DIM), lambda h, s: (_tri_cols(s, nb)[1], h))
    full = pl.BlockSpec((T, G * HEAD_DIM), lambda h, s: (0, h))
    kspec = pl.BlockSpec((blk, G * HEAD_DIM), lambda h, s: (_tri_cols(s, nb)[0], h))
    return _pallas(
        kern, comm=comm, name="attn_bwd", grid=(N_HEADS // G, nb * (nb + 1) // 2),
        in_specs=[qspec, kspec, kspec, qspec, qspec, qspec,
                  pl.BlockSpec((blk, 128), lambda h, s: (_tri_cols(s, nb)[1], 0)),
                  pl.BlockSpec((N_HEADS, blk), lambda h, s: (0, _tri_cols(s, nb)[0]))],
        out_specs=[full, full, kspec, kspec, pl.BlockSpec((G, 1, blk), lambda h, s: (h, 0, _tri_cols(s, nb)[0]))],
        out_shape=[jax.ShapeDtypeStruct((T, W), F32)] * 4 + [jax.ShapeDtypeStruct((N_HEADS, 1, T), F32)],
        scratch_shapes=[pltpu.VMEM((T, G * HEAD_DIM), F32), pltpu.VMEM((G, T, 1), F32),
                        pltpu.VMEM((blk, G * HEAD_DIM), F32), pltpu.VMEM((blk, G * HEAD_DIM), F32),
                        pltpu.VMEM((G, 1, blk), F32)],
        compiler_params=_params(),
    )(qn, kn, vb, do, lse, delta, f_tm, f_hm)


def _attn_delta(o, do, T, tb):
    W = N_HEADS * HEAD_DIM

    def body(ob, dob):
        return jnp.concatenate(
            _heads(lambda a, b: jnp.broadcast_to(jnp.sum(a * b, axis=1, keepdims=True), a.shape), ob, dob), axis=1)

    return _rowwise("attn_delta", body, T, tb, [(o, W, 0), (do, W, 0)], [], [(W, F32)], [])[0]


def _window_select(s, g, shift):
    picks = []
    for k in (1, 2, 4, 8):
        s = s + shift(s, k)
        picks.append(s)
    return jnp.where(g == 0, picks[0], jnp.where(g == 1, picks[1], jnp.where(g == 2, picks[2], picks[3])))


def _group_window(g):
    return jnp.where(g == 0, POOL_WINDOWS[0], jnp.where(g == 1, POOL_WINDOWS[1],
                     jnp.where(g == 2, POOL_WINDOWS[2], POOL_WINDOWS[3])))


def _pool_fwd(proj, ucol, pw, ps, T, tb):
    C = POOL_GROUP_DIM
    n_g = len(POOL_WINDOWS)

    def kern(uc_ref, up_ref, pw_ref, ps_ref, pooled_ref, out_ref):
        g, i = pl.program_id(0), pl.program_id(1)
        uc = uc_ref[...]
        t2 = (i - 1) * tb + lax.broadcasted_iota(jnp.int32, (2 * tb, C), 0)
        u2 = jnp.where(t2 >= 0, jnp.concatenate([up_ref[...], uc], axis=0), 0.0)
        sums = _window_select(u2, g, lambda s, k: pltpu.roll(s, k, 0))[tb:, :]
        count = jnp.minimum(t2[tb:, :] + 1, _group_window(g)).astype(F32)
        pooled = sums / count - uc
        pooled_ref[...] = pooled.astype(BF16)
        out_ref[...] = _dot(pooled, pw_ref[...], "NN") * ps_ref[...]

    ospec = pl.BlockSpec((tb, C), lambda g, i: (i, g))
    return pl.pallas_call(
        kern, name="pool_fwd", grid=(n_g, T // tb),
        in_specs=[pl.BlockSpec((tb, C), lambda g, i: (i, ucol + g)),
                  pl.BlockSpec((tb, C), lambda g, i: (jnp.maximum(i - 1, 0), ucol + g)),
                  pl.BlockSpec((None, C, C), lambda g, i: (g, 0, 0)),
                  pl.BlockSpec((1, C), lambda g, i: (0, g))],
        out_specs=[ospec, ospec],
        out_shape=[jax.ShapeDtypeStruct((T, n_g * C), BF16), jax.ShapeDtypeStruct((T, n_g * C), F32)],
        compiler_params=_params(),
    )(proj, proj, pw, ps)


def _pool_bwd(dmix_in, dcol, pooled, pw, ps, T, tb):
    C = POOL_GROUP_DIM
    n_g = len(POOL_WINDOWS)
    nb = T // tb

    def kern(dc_ref, dn_ref, pooled_ref, pw_ref, ps_ref, du_ref, dpw_ref, dps_ref):
        g, i = pl.program_id(0), pl.program_id(1)
        dc = dc_ref[...]
        scale = ps_ref[...]
        t2 = i * tb + lax.broadcasted_iota(jnp.int32, (2 * tb, C), 0)
        d2 = jnp.where(t2 < T, jnp.concatenate([dc, dn_ref[...]], axis=0) * scale, 0.0)
        dpooled2 = _dot(d2, pw_ref[...], "NT")
        count = jnp.minimum(t2 + 1, _group_window(g)).astype(F32)
        sums = _window_select(dpooled2 / count, g, lambda s, k: pltpu.roll(s, 2 * tb - k, 0))
        du_ref[...] = (sums[:tb, :] - dpooled2[:tb, :]).astype(BF16)
        pooled = pooled_ref[...]
        p = _dot(pooled, pw_ref[...], "NN")
        dps = jnp.sum(dc * p, axis=0, keepdims=True)
        dpw = _dot(pooled, d2[:tb, :], "TN")

        @pl.when(i == 0)
        def _():
            dps_ref[...] = dps
            dpw_ref[...] = dpw

        @pl.when(i > 0)
        def _():
            dps_ref[...] += dps
            dpw_ref[...] += dpw

    return pl.pallas_call(
        kern, name="pool_bwd", grid=(n_g, nb),
        in_specs=[pl.BlockSpec((tb, C), lambda g, i: (i, dcol + g)),
                  pl.BlockSpec((tb, C), lambda g, i: (jnp.minimum(i + 1, nb - 1), dcol + g)),
                  pl.BlockSpec((tb, C), lambda g, i: (i, g)),
                  pl.BlockSpec((None, C, C), lambda g, i: (g, 0, 0)),
                  pl.BlockSpec((1, C), lambda g, i: (0, g))],
        out_specs=[pl.BlockSpec((tb, C), lambda g, i: (i, g)),
                   pl.BlockSpec((None, C, C), lambda g, i: (g, 0, 0)),
                   pl.BlockSpec((1, C), lambda g, i: (0, g))],
        out_shape=[jax.ShapeDtypeStruct((T, n_g * C), BF16), jax.ShapeDtypeStruct((n_g, C, C), F32),
                   jax.ShapeDtypeStruct((1, n_g * C), F32)],
        compiler_params=_params(),
    )(dmix_in, dmix_in, pooled, pw, ps)


D_QKV = 3 * N_HEADS * HEAD_DIM
D_U = len(POOL_WINDOWS) * POOL_GROUP_DIM
F_PAD = 128
D_PROJ = D_QKV + D_U + F_PAD


def _perm_w_in(w):
    pad = jnp.zeros((F_PAD - N_HEADS, w.shape[1]), w.dtype)
    return jnp.concatenate([w[:D_QKV], w[D_QKV + N_HEADS:], w[D_QKV:D_QKV + N_HEADS], pad], axis=0)


def _unperm_w_in(w):
    return jnp.concatenate([w[:D_QKV], w[D_QKV + D_U:D_QKV + D_U + N_HEADS], w[D_QKV:D_QKV + D_U]], axis=0)


def _mixer_fwd(x, norm_g, sh, sc, gate, w_in_p, b_pad, gq, gk, late_weights, ps, T, proj_comm, attn_comm):
    tb = min(256, T)
    blk = min(512, T)
    hm = _norm_mod_fwd("mix_norm_fwd", x, norm_g, sc, sh, T, tb)
    proj, got_proj = _mm("mix_proj", hm, w_in_p, "NT", F32, 512, D_PROJ // 3, 2048, comm=proj_comm)
    pw, w_out = late_weights(got_proj)
    qn, kn, vb = _qknorm_fwd(proj, gq, gk, T, tb)
    fcol = (D_QKV + D_U) // 128
    f_tm = _fgate_fwd(proj, fcol, b_pad, T)
    f_hm = f_tm[:, :N_HEADS].T
    (o, lse), got = _attn_fwd(qn, kn, vb, f_tm, f_hm, T, blk, comm=attn_comm)
    pooled, pool_o = _pool_fwd(proj, D_QKV // POOL_GROUP_DIM, pw, ps, T, tb)
    mix_in = jnp.concatenate([o.astype(BF16), pool_o.astype(BF16)], axis=1)
    mix, x_out = _mm_groups("mix_out", mix_in[None], w_out[None], "NN", 512, 512, residual=(x, gate, 1.0))
    return x_out, (x, hm, proj, qn, kn, vb, f_tm, f_hm, o, lse, pooled, mix_in, mix), pw, w_out, got


def _mixer_bwd(dx_out, branch, saved, norm_g, sc, w_in_p, b_pad, gq, gk, pw, ps, w_out, T, core, ride_sums, below):
    x, hm, proj, qn, kn, vb, f_tm, f_hm, o, lse, pooled, mix_in, mix = saved
    tb = min(256, T)
    blk = min(512, T)
    W = N_HEADS * HEAD_DIM
    D = x.shape[1]
    n_g = len(POOL_WINDOWS)
    dmix, dgate = branch
    dmix_in = _mm("mix_out_bwd", dmix, w_out, "NT", F32, 512, 2048, 2048)
    dw_out = _mm("mix_dw_out", mix_in, dmix, "TN", BF16, 512, 1024, T)
    delta = _attn_delta(o, dmix_in, T, tb)
    (dqn, dfq, dkn, dv, dfk), ride_got = _attn_bwd(qn, kn, vb, dmix_in, lse, delta, f_tm, f_hm, T, blk,
                                                   comm=_chip_comm(ride_sums))
    dq, dk, dgq, dgk = _qknorm_bwd(proj, dqn, dkn, gq, gk, T, tb)
    dF = jnp.pad(dfq[:, ::HEAD_DIM] + dfk.reshape(N_HEADS, T).T, ((0, 0), (0, F_PAD - N_HEADS)))
    fcol = (D_QKV + D_U) // 128
    dfl, dbf = _fgate_bwd(proj, fcol, b_pad, dF, T)
    du, dpw, dps = _pool_bwd(dmix_in, W // POOL_GROUP_DIM, pooled, pw, ps, T, tb)
    dproj = jnp.concatenate([dq, dk, dv.astype(BF16), du, dfl.astype(BF16)], axis=1)
    dw_in_p = _mm("mix_dw_in", dproj, hm, "TN", BF16, D_PROJ // 3, 512, T)
    pw_rows = POOL_GROUP_DIM // N_DEV
    slabs = [_unperm_w_in(dw_in_p).reshape(N_DEV, -1, D),
             jnp.transpose(dpw.astype(BF16).reshape(n_g, N_DEV, pw_rows, POOL_GROUP_DIM),
                           (1, 0, 2, 3)).reshape(N_DEV, n_g * pw_rows, POOL_GROUP_DIM),
             dw_out.reshape(N_DEV, -1, D)]
    dhm, sums = _reduce_level1(
        "mix", slabs, core, [TILE_W_IN, TILE_POOL, TILE_MIX_OUT],
        host=lambda comm: _mm("mix_proj_bwd", dproj, w_in_p, "NN", F32, 512, 512, D_PROJ, comm=comm))
    dx, df_below, dsh, dsc, dng, dgate_below = _norm_mod_bwd("mix_norm_bwd", x, dhm, dx_out, norm_g, sc, T, tb,
                                                             below=below)
    return dx, (dsh, dsc, dgate, dng), sums, dps, dgq, dgk, dbf, ride_got, (df_below, dgate_below)


def kernel(x, c, w_ada, b_ada, ffn1_norm_g, ffn1_w_in, ffn1_w_out, mix_norm_g, w_in, b_forget, q_norm_g, k_norm_g, pool_w, pool_scale, w_out, ffn2_norm_g, ffn2_w_in, ffn2_w_out, final_norm_g, loss_target, m_w_ada, m_b_ada, m_ffn1_norm_g, m_ffn1_w_in, m_ffn1_w_out, m_mix_norm_g, m_w_in, m_b_forget, m_q_norm_g, m_k_norm_g, m_pool_w, m_pool_scale, m_w_out, m_ffn2_norm_g, m_ffn2_w_in, m_ffn2_w_out, m_final_norm_g, v_w_ada, v_b_ada, v_ffn1_norm_g, v_ffn1_w_in, v_ffn1_w_out, v_mix_norm_g, v_w_in, v_b_forget, v_q_norm_g, v_k_norm_g, v_pool_w, v_pool_scale, v_w_out, v_ffn2_norm_g, v_ffn2_w_in, v_ffn2_w_out, v_final_norm_g):
    T, D = x.shape[1], x.shape[2]
    mx, my, mc = _mesh_pos()
    me = _flat(mx, my, mc)
    x0 = x[0]
    tgt = loss_target[0]
    tb = min(256, T)

    core = jnp.reshape(mc, (1,)).astype(jnp.int32)
    half = N_DEV // 2
    n_g = len(POOL_WINDOWS)
    pw_rows = POOL_GROUP_DIM // N_DEV

    def bf(w):
        return w.astype(BF16)

    n_loc = w_ada.shape[2]
    c_all = _standalone("gather_c", _gather_comm([c.reshape(8, D // 8)]))[0].reshape(N_DEV, D)
    b_loc = lax.dynamic_slice_in_dim(b_ada, me * n_loc, n_loc, axis=1)
    mod_loc = _ada_fwd(c_all, w_ada[0], b_loc, n_loc // 3)
    mod_all = _standalone("gather_mod", _gather_comm([mod_loc]))[0]
    mod = lax.dynamic_index_in_dim(mod_all, me, axis=1, keepdims=False).reshape(N_MOD, 1, D)
    sh1, sc1, g1, sh2, sc2, g2, sh3, sc3, g3 = [mod[k] for k in range(N_MOD)]
    b_pad = jnp.pad(b_forget, ((0, 0), (0, F_PAD - N_HEADS)))
    ps = pool_scale

    def shard_t(w):
        return jnp.swapaxes(w[0], 0, 1)

    wi1 = _standalone("gather_ffn1_w_in", _gather_comm([bf(shard_t(ffn1_w_in))]))[0]
    x1, sv1, wo1, (w_in_g,) = _ffn_fwd(
        "ffn1", x0, ffn1_norm_g, sh1, sc1, g1, wi1, lambda got: got[0].reshape(half, -1, D), T,
        up_comm=_gather_comm([bf(ffn1_w_out[0])], forward_at=0.7),
        down_comm=_gather_comm([bf(shard_t(w_in))], forward_at=0.8))
    w_in_p = _perm_w_in(w_in_g.reshape(-1, D))

    def late_weights(got):
        pool_g, w_out_g = got
        pw = jnp.transpose(pool_g.reshape(N_DEV, n_g, pw_rows, POOL_GROUP_DIM),
                           (1, 0, 2, 3)).reshape(n_g, POOL_GROUP_DIM, POOL_GROUP_DIM)
        return pw, w_out_g.reshape(-1, D)

    x2, svm, pw_full, w_out_full, (wi2,) = _mixer_fwd(
        x1, mix_norm_g, sh2, sc2, g2, w_in_p, b_pad, q_norm_g, k_norm_g, late_weights, ps, T,
        proj_comm=_gather_comm([bf(pool_w[0].reshape(-1, POOL_GROUP_DIM)), bf(w_out[0])], forward_at=0.6),
        attn_comm=_gather_comm([bf(shard_t(ffn2_w_in))], forward_at=0.9))
    x3, sv2, wo2, _ = _ffn_fwd("ffn2", x2, ffn2_norm_g, sh3, sc3, g3, wi2,
                               lambda got: got[0].reshape(half, -1, D), T,
                               up_comm=_gather_comm([bf(ffn2_w_out[0])], forward_at=0.7))
    dx3, df3, dgf, loss_l, dgate3 = _final_loss(x3, tgt, final_norm_g.reshape(1, D), T, tb, below=(sv2[5], g3, 0.5))
    loss = lax.psum(loss_l[0, 0], ("x", "y", "c"))

    dx2, (dsh3, dsc3, dg3, dn3), dwi2_sum, dwo2, _, branch2 = _ffn_bwd(
        "ffn2", dx3, (df3, dgate3), sv2, ffn2_norm_g, sc3, wi2, wo2, T, core, defer_dwi=True,
        below=(svm[12], g2, 1.0))
    dx1, (dsh2, dsc2, dg2, dn2), mix_sums, dps, dgq, dgk, dbf, (dwi2,), branch1 = _mixer_bwd(
        dx2, branch2, svm, mix_norm_g, sc2, w_in_p, b_pad, q_norm_g, k_norm_g, pw_full, ps, w_out_full, T, core,
        ride_sums=[dwi2_sum], below=(sv1[5], g1, 0.5))
    dx0, (dsh1, dsc1, dg1, dn1), dwi1, dwo1, (dw_in_r, dpw_r, dw_out_r), _ = _ffn_bwd(
        "ffn1", dx1, branch1, sv1, ffn1_norm_g, sc1, wi1, wo1, T, core, ride_sums=mix_sums)

    received = dict(ffn1_w_in=dwi1, ffn1_w_out=dwo1, w_in=dw_in_r, pool_w=dpw_r, w_out=dw_out_r,
                    ffn2_w_in=dwi2, ffn2_w_out=dwo2)
    moments = dict(ffn1_w_in=(m_ffn1_w_in, v_ffn1_w_in), ffn1_w_out=(m_ffn1_w_out, v_ffn1_w_out),
                   w_in=(m_w_in, v_w_in), pool_w=(m_pool_w, v_pool_w), w_out=(m_w_out, v_w_out),
                   ffn2_w_in=(m_ffn2_w_in, v_ffn2_w_in), ffn2_w_out=(m_ffn2_w_out, v_ffn2_w_out))
    weights = dict(ffn1_w_in=ffn1_w_in, ffn1_w_out=ffn1_w_out, w_in=w_in, pool_w=pool_w, w_out=w_out,
                   ffn2_w_in=ffn2_w_in, ffn2_w_out=ffn2_w_out)
    row_tiles = dict(ffn1_w_in=TILE_FFN_IN, ffn1_w_out=TILE_W_OUT, w_in=TILE_W_IN, pool_w=TILE_POOL,
                     w_out=TILE_MIX_OUT, ffn2_w_in=TILE_FFN_IN, ffn2_w_out=TILE_W_OUT)
    results = {}
    for k in received:
        shape = weights[k].shape
        two_d = received[k].shape[1:]
        mk, vk = moments[k]
        if row_tiles[k] in (TILE_FFN_IN, TILE_W_IN):
            outs = _adamw("adamw_" + k, received[k], shard_t(weights[k]), shard_t(mk), shard_t(vk), row_tiles[k][0],
                          tc=512)
            results[k] = [jnp.swapaxes(o, 0, 1)[None] for o in outs]
        else:
            outs = _adamw("adamw_" + k, received[k], weights[k].reshape(two_d), mk.reshape(two_d),
                          vk.reshape(two_d), row_tiles[k][0])
            results[k] = [o.reshape(shape) for o in outs]

    dmod = jnp.concatenate([dsh1, dsc1, dg1, dsh2, dsc2, dg2, dsh3, dsc3, dg3], axis=1)
    small_names = ["b_ada", "ffn1_norm_g", "mix_norm_g", "ffn2_norm_g", "final_norm_g", "b_forget",
                   "q_norm_g", "k_norm_g", "pool_scale"]
    small_w = dict(b_ada=b_ada, ffn1_norm_g=ffn1_norm_g, mix_norm_g=mix_norm_g, ffn2_norm_g=ffn2_norm_g,
                   final_norm_g=final_norm_g, b_forget=b_forget, q_norm_g=q_norm_g, k_norm_g=k_norm_g,
                   pool_scale=pool_scale)
    small_m = dict(b_ada=m_b_ada, ffn1_norm_g=m_ffn1_norm_g, mix_norm_g=m_mix_norm_g, ffn2_norm_g=m_ffn2_norm_g,
                   final_norm_g=m_final_norm_g, b_forget=m_b_forget, q_norm_g=m_q_norm_g, k_norm_g=m_k_norm_g,
                   pool_scale=m_pool_scale)
    small_v = dict(b_ada=v_b_ada, ffn1_norm_g=v_ffn1_norm_g, mix_norm_g=v_mix_norm_g, ffn2_norm_g=v_ffn2_norm_g,
                   final_norm_g=v_final_norm_g, b_forget=v_b_forget, q_norm_g=v_q_norm_g, k_norm_g=v_k_norm_g,
                   pool_scale=v_pool_scale)
    small_g = dict(b_ada=dmod, ffn1_norm_g=dn1, mix_norm_g=dn2, ffn2_norm_g=dn3, final_norm_g=dgf,
                   b_forget=dbf[:, :N_HEADS], q_norm_g=dgq, k_norm_g=dgk, pool_scale=dps)
    sizes = [small_w[k].size for k in small_names]
    total = sum(sizes)
    lanes = 8 * 128
    padded = -(-total // lanes) * lanes

    def pack(d):
        flat = jnp.concatenate([d[k].reshape(-1) for k in small_names])
        return jnp.pad(flat, (0, padded - total)).reshape(8, padded // 8)

    small_parts = _standalone("gather_small_grads", _gather_comm([pack(small_g)]))[0]
    s_outs = _adamw("adamw_small", small_parts, pack(small_w), pack(small_m), pack(small_v), 8)
    offs = [0]
    for s in sizes:
        offs.append(offs[-1] + s)
    for idx, k in enumerate(small_names):
        results[k] = [o.reshape(-1)[offs[idx]:offs[idx + 1]].reshape(small_w[k].shape) for o in s_outs]

    dmod_all = small_parts.reshape(N_DEV, padded)[:, :N_MOD * D]
    dmod_loc = lax.dynamic_slice_in_dim(dmod_all, me * n_loc, n_loc, axis=1)
    g_ada = _ada_bwd(c_all, dmod_loc, n_loc // 3)
    a_outs = _adamw("adamw_w_ada", g_ada[None], w_ada[0], m_w_ada[0], v_w_ada[0], 128)
    results["w_ada"] = [o.reshape(w_ada.shape) for o in a_outs]

    order = ["w_ada", "b_ada", "ffn1_norm_g", "ffn1_w_in", "ffn1_w_out", "mix_norm_g", "w_in", "b_forget",
             "q_norm_g", "k_norm_g", "pool_w", "pool_scale", "w_out", "ffn2_norm_g", "ffn2_w_in", "ffn2_w_out",
             "final_norm_g"]
    out = [loss, dx0[None]]
    for part in range(4):
        out += [results[k][part] for k in order]
    return tuple(out)
```
